```python
import math
import jax
import jax.numpy as jnp
from jax import lax
import numpy as np


D_MODEL = 1024
BATCH = 8
SEQ = 2048
DEPTH = 1
DEC_BATCH = 128
DEC_SEQ = 1
PAST_LEN = 8192
PAGE_SIZE = 128

N_HEADS = 16
KV_HEADS = 4
HEAD_DIM = D_MODEL // N_HEADS
Q_PER_KV = N_HEADS // KV_HEADS
WINDOW = 128
ATTN_BLOCK = WINDOW
ROPE_THETA = 10000.0
Q_DIM = N_HEADS * HEAD_DIM
KV_DIM = KV_HEADS * HEAD_DIM
SSM_EXPAND = 2
D_INNER = SSM_EXPAND * D_MODEL
SSM_HEAD_DIM = 64
SSM_HEADS = D_INNER // SSM_HEAD_DIM
SSM_GROUPS = 4
HEADS_PER_GROUP = SSM_HEADS // SSM_GROUPS
D_STATE = 128
CONV_W = 4
CONV_DIM = D_INNER + 2 * SSM_GROUPS * D_STATE
SSD_CHUNK = 128
D_FF = 4 * D_MODEL
EPS = 1e-6
IN_SPLITS = (Q_DIM, Q_DIM + KV_DIM, Q_DIM + 2 * KV_DIM, Q_DIM + 2 * KV_DIM + D_INNER,
             Q_DIM + 2 * KV_DIM + D_INNER + CONV_DIM, Q_DIM + 2 * KV_DIM + D_INNER + CONV_DIM + SSM_HEADS)
IN_DIM = Q_DIM + 2 * KV_DIM + D_INNER + CONV_DIM + SSM_HEADS + 2 * D_MODEL

kernel_name = 'hybrid_swa_sink_ssd_decoder_step'


def rms_norm(x, g):
    xf = x.astype(jnp.float32)
    y = xf * lax.rsqrt(jnp.mean(xf * xf, axis=-1, keepdims=True) + EPS)
    return (y * g.astype(jnp.float32)).astype(x.dtype)


def rope(x, pos):
    half = HEAD_DIM // 2
    inv = ROPE_THETA ** (-jnp.arange(half, dtype=jnp.float32) / half)
    ang = pos.astype(jnp.float32)[:, None] * inv[None, :]
    cos = jnp.cos(ang)[None, :, None, :]
    sin = jnp.sin(ang)[None, :, None, :]
    xf = x.astype(jnp.float32)
    x1, x2 = xf[..., :half], xf[..., half:]
    return jnp.concatenate([x1 * cos - x2 * sin, x2 * cos + x1 * sin], axis=-1).astype(x.dtype)


def sink_attention(q, k, v, mask, sinks):
    s = jnp.einsum('...qkgd,...skd->...kgqs', q.astype(jnp.float32), k.astype(jnp.float32)) * (HEAD_DIM ** -0.5)
    s = jnp.where(mask[..., None, None, :, :], s, -jnp.inf)
    sink = jnp.broadcast_to(sinks.astype(jnp.float32).reshape(KV_HEADS, Q_PER_KV, 1, 1), s.shape[:-1] + (1,))
    p = jax.nn.softmax(jnp.concatenate([s, sink], axis=-1), axis=-1)[..., :-1]
    o = jnp.einsum('...kgqs,...skd->...qkgd', p, v.astype(jnp.float32))
    return o.astype(q.dtype)


def window_attention_prompt(q, k, v, sinks):
    b, t = q.shape[:2]
    nb = t // ATTN_BLOCK
    qb = q.reshape(b, nb, ATTN_BLOCK, KV_HEADS, Q_PER_KV, HEAD_DIM)
    pad = jnp.zeros((b, ATTN_BLOCK, KV_HEADS, HEAD_DIM), k.dtype)

    def band(a):
        prev = jnp.concatenate([pad, a], axis=1)[:, :t].reshape(b, nb, ATTN_BLOCK, KV_HEADS, HEAD_DIM)
        cur = a.reshape(b, nb, ATTN_BLOCK, KV_HEADS, HEAD_DIM)
        return jnp.concatenate([prev, cur], axis=2)

    kb, vb = band(k), band(v)
    blk = jnp.arange(nb)[:, None, None]
    qpos = blk * ATTN_BLOCK + jnp.arange(ATTN_BLOCK)[None, :, None]
    kpos = (blk - 1) * ATTN_BLOCK + jnp.arange(2 * ATTN_BLOCK)[None, None, :]
    mask = (kpos <= qpos) & (kpos > qpos - WINDOW) & (kpos >= 0)
    o = sink_attention(qb, kb, vb, mask, sinks)
    return o.reshape(b, t, Q_DIM)


def window_attention_decode(q, k, v, win_k, win_v, pos, sinks):
    b, t = q.shape[:2]
    w = win_k.shape[1]
    kk = jnp.concatenate([win_k, k], axis=1)
    vv = jnp.concatenate([win_v, v], axis=1)
    kpos = jnp.concatenate([pos[:1] - w + jnp.arange(w), pos])
    mask = (kpos[None, :] <= pos[:, None]) & (kpos[None, :] > pos[:, None] - WINDOW)
    qg = q.reshape(b, t, KV_HEADS, Q_PER_KV, HEAD_DIM)
    o = sink_attention(qg, kk, vv, mask, sinks)
    return o.reshape(b, t, Q_DIM), kk[:, -w:], vv[:, -w:]


def causal_conv(xbc, prev, w, bias):
    xp = jnp.concatenate([prev.astype(xbc.dtype), xbc], axis=1)
    y = lax.conv_general_dilated(xp, w[:, None, :].astype(xbc.dtype), window_strides=(1,), padding='VALID',
                                 dimension_numbers=('NWC', 'WIO', 'NWC'), feature_group_count=CONV_DIM)
    return jax.nn.silu(y + bias), xp[:, -(CONV_W - 1):]


def ssd_scan(x, dt, A, Bm, Cm, h0):
    b, T = x.shape[:2]
    L = SSD_CHUNK if T % SSD_CHUNK == 0 else T
    nc = T // L
    G, Hg, P, N = SSM_GROUPS, HEADS_PER_GROUP, SSM_HEAD_DIM, D_STATE
    x = x.reshape(b, nc, L, G, Hg, P)
    dt = dt.reshape(b, nc, L, G, Hg)
    Bm = Bm.reshape(b, nc, L, G, N)
    Cm = Cm.reshape(b, nc, L, G, N)
    cum = jnp.cumsum(dt * A.reshape(G, Hg), axis=2)
    diff = cum[:, :, :, None] - cum[:, :, None, :]
    causal = jnp.tril(jnp.ones((L, L), dtype=bool))[:, :, None, None]
    decay = jnp.exp(jnp.where(causal, diff, -jnp.inf))
    cb = jnp.einsum('bclgn,bcsgn->bclsg', Cm, Bm)
    w_intra = cb[..., None] * decay * dt[:, :, None]
    y = jnp.einsum('bclsgh,bcsghp->bclghp', w_intra, x)
    to_end = jnp.exp(cum[:, :, -1:] - cum) * dt
    st = jnp.einsum('bcsgn,bcsgh,bcsghp->bcghpn', Bm, to_end, x)
    chunk_decay = jnp.exp(cum[:, :, -1])

    def step(h, inp):
        s_c, d_c = inp
        return d_c[..., None, None] * h + s_c, h

    h_fin, h_enter = lax.scan(step, h0.reshape(b, G, Hg, P, N),
                              (jnp.moveaxis(st, 1, 0), jnp.moveaxis(chunk_decay, 1, 0)))
    h_enter = jnp.moveaxis(h_enter, 0, 1)
    y = y + jnp.einsum('bclgn,bclgh,bcghpn->bclghp', Cm, jnp.exp(cum), h_enter)
    return y.reshape(b, T, SSM_HEADS, P), h_fin.reshape(b, SSM_HEADS, P, N)


def hybrid_layer(x, pos, conv_prev, ssm_h0, win_k, win_v, w_buf,
                 norm_mix, w_in, q_norm, k_norm, attn_sinks, conv_w, conv_b, dt_bias, a_log, d_skip,
                 ssm_norm, w_attn_o, w_ssm_o, w_out, norm_mlp, w_up, w_down):
    b, t, _ = x.shape
    h = rms_norm(x, norm_mix)
    q, k, v, z, xbc, dt_raw, gates = jnp.split(h @ w_in, IN_SPLITS, axis=-1)
    q = rope(rms_norm(q.reshape(b, t, N_HEADS, HEAD_DIM), q_norm), pos)
    k = rope(rms_norm(k.reshape(b, t, KV_HEADS, HEAD_DIM), k_norm), pos)
    v = v.reshape(b, t, KV_HEADS, HEAD_DIM)
    if win_k is None:
        o_attn = window_attention_prompt(q, k, v, attn_sinks)
        new_k, new_v = k[:, -w_buf:], v[:, -w_buf:]
    else:
        o_attn, new_k, new_v = window_attention_decode(q, k, v, win_k, win_v, pos, attn_sinks)
    xc, conv_state = causal_conv(xbc, conv_prev, conv_w, conv_b)
    xs, Bm, Cm = jnp.split(xc, [D_INNER, D_INNER + SSM_GROUPS * D_STATE], axis=-1)
    dt = jax.nn.softplus(dt_raw.astype(jnp.float32) + dt_bias.astype(jnp.float32))
    A = -jnp.exp(a_log.astype(jnp.float32))
    xs_h = xs.reshape(b, t, SSM_HEADS, SSM_HEAD_DIM).astype(jnp.float32)
    y, h_fin = ssd_scan(xs_h, dt, A,
                        Bm.reshape(b, t, SSM_GROUPS, D_STATE).astype(jnp.float32),
                        Cm.reshape(b, t, SSM_GROUPS, D_STATE).astype(jnp.float32),
                        ssm_h0.astype(jnp.float32))
    y = (y + d_skip.astype(jnp.float32)[:, None] * xs_h).reshape(b, t, D_INNER)
    o_ssm = rms_norm(y * jax.nn.silu(z.astype(jnp.float32)), ssm_norm).astype(x.dtype)
    g_attn, g_ssm = jnp.split(jax.nn.sigmoid(gates), 2, axis=-1)
    mixed = g_attn * (o_attn @ w_attn_o) + g_ssm * (o_ssm @ w_ssm_o)
    x = x + mixed @ w_out
    u = rms_norm(x, norm_mlp) @ w_up
    x = x + jnp.square(jax.nn.relu(u)) @ w_down
    return x, new_k, new_v, conv_state, h_fin.astype(x.dtype)


def setup_inputs(seed: int = 0) -> dict:
    key = jax.random.key(seed)
    ks = jax.random.split(key, 32)
    w_buf = min(WINDOW, PAST_LEN)

    def nrm(k, shape, scale):
        return jax.random.normal(k, shape, jnp.float32) * scale

    def gain(k, n):
        return 1.0 + nrm(k, (DEPTH, n), 0.02)

    dt0 = jnp.exp(jax.random.uniform(ks[20], (DEPTH, SSM_HEADS)) * (math.log(0.1) - math.log(0.001)) + math.log(0.001))
    return {
        'x_prompt': nrm(ks[0], (BATCH, SEQ, D_MODEL), 1.0),
        'x_sample': nrm(ks[1], (DEC_BATCH, DEC_SEQ, D_MODEL), 1.0),
        'cache_k': nrm(ks[2], (DEPTH, DEC_BATCH, w_buf, KV_HEADS, HEAD_DIM), 1.0),
        'cache_v': nrm(ks[3], (DEPTH, DEC_BATCH, w_buf, KV_HEADS, HEAD_DIM), 1.0),
        'state_conv': nrm(ks[4], (DEPTH, DEC_BATCH, CONV_W - 1, CONV_DIM), 1.0),
        'state_ssm': nrm(ks[5], (DEPTH, DEC_BATCH, SSM_HEADS, SSM_HEAD_DIM, D_STATE), 0.1),
        'norm_mix': gain(ks[6], D_MODEL),
        'w_in': nrm(ks[7], (DEPTH, D_MODEL, IN_DIM), D_MODEL ** -0.5),
        'q_norm': gain(ks[8], HEAD_DIM),
        'k_norm': gain(ks[9], HEAD_DIM),
        'attn_sinks': nrm(ks[10], (DEPTH, N_HEADS), 0.5),
        'conv_w': nrm(ks[11], (DEPTH, CONV_W, CONV_DIM), 0.5),
        'conv_b': nrm(ks[12], (DEPTH, CONV_DIM), 0.02),
        'dt_bias': dt0 + jnp.log(-jnp.expm1(-dt0)),
        'a_log': jnp.log(jax.random.uniform(ks[13], (DEPTH, SSM_HEADS), minval=1.0, maxval=16.0)),
        'd_skip': 1.0 + nrm(ks[14], (DEPTH, SSM_HEADS), 0.02),
        'ssm_norm': gain(ks[15], D_INNER),
        'w_attn_o': nrm(ks[16], (DEPTH, Q_DIM, D_MODEL), Q_DIM ** -0.5),
        'w_ssm_o': nrm(ks[17], (DEPTH, D_INNER, D_MODEL), D_INNER ** -0.5),
        'w_out': nrm(ks[18], (DEPTH, D_MODEL, D_MODEL), D_MODEL ** -0.5),
        'norm_mlp': gain(ks[19], D_MODEL),
        'w_up': nrm(ks[21], (DEPTH, D_MODEL, D_FF), D_MODEL ** -0.5),
        'w_down': nrm(ks[22], (DEPTH, D_FF, D_MODEL), D_FF ** -0.5),
    }


def reference(x_prompt, x_sample, cache_k, cache_v, state_conv, state_ssm, norm_mix, w_in, q_norm, k_norm,
              attn_sinks, conv_w, conv_b, dt_bias, a_log, d_skip, ssm_norm, w_attn_o, w_ssm_o, w_out,
              norm_mlp, w_up, w_down):
    b_p, seq = x_prompt.shape[:2]
    dec_seq = x_sample.shape[1]
    w_buf = cache_k.shape[2]
    pos_p = jnp.arange(seq)
    pos_s = PAST_LEN + jnp.arange(dec_seq)
    yp, ys = x_prompt, x_sample
    kp_l, vp_l, cp_l, hp_l, ks_l, vs_l, cs_l, hs_l = [], [], [], [], [], [], [], []
    for l in range(DEPTH):
        lw = (norm_mix[l], w_in[l], q_norm[l], k_norm[l], attn_sinks[l], conv_w[l], conv_b[l], dt_bias[l],
              a_log[l], d_skip[l], ssm_norm[l], w_attn_o[l], w_ssm_o[l], w_out[l], norm_mlp[l], w_up[l], w_down[l])
        conv0 = jnp.zeros((b_p, CONV_W - 1, CONV_DIM), x_prompt.dtype)
        ssm0 = jnp.zeros((b_p, SSM_HEADS, SSM_HEAD_DIM, D_STATE), x_prompt.dtype)
        yp, kp, vp, cp, hp = hybrid_layer(yp, pos_p, conv0, ssm0, None, None, w_buf, *lw)
        ys, k_s, v_s, c_s, h_s = hybrid_layer(ys, pos_s, state_conv[l], state_ssm[l], cache_k[l], cache_v[l],
                                              w_buf, *lw)
        kp_l.append(kp); vp_l.append(vp); cp_l.append(cp); hp_l.append(hp)
        ks_l.append(k_s); vs_l.append(v_s); cs_l.append(c_s); hs_l.append(h_s)
    return (yp, ys, jnp.stack(kp_l), jnp.stack(vp_l), jnp.stack(cp_l), jnp.stack(hp_l),
            jnp.stack(ks_l), jnp.stack(vs_l), jnp.stack(cs_l), jnp.stack(hs_l))
```

```python
import functools

import numpy as np
import jax
import jax.numpy as jnp
from jax import lax
from jax.experimental import pallas as pl
from jax.experimental.pallas import tpu as pltpu

F32 = jnp.float32
BF16 = jnp.bfloat16

N_HEADS = 16
KV_HEADS = 4
HEAD_DIM = 64
Q_PER_KV = N_HEADS // KV_HEADS
WINDOW = 128
ROPE_THETA = 10000.0
Q_DIM = N_HEADS * HEAD_DIM
KV_DIM = KV_HEADS * HEAD_DIM
SSM_HEAD_DIM = 64
SSM_HEADS = 32
D_INNER = SSM_HEADS * SSM_HEAD_DIM
SSM_GROUPS = 4
HEADS_PER_GROUP = SSM_HEADS // SSM_GROUPS
D_STATE = 128
BC_DIM = SSM_GROUPS * D_STATE
CONV_W = 4
CONV_DIM = D_INNER + 2 * BC_DIM
SSD_CHUNK = 128
EPS = 1e-6
PAST_LEN = 8192

LANES = 128
SUBLANES = 8
VMEM_LIMIT = 48 * 1024 * 1024

BLK = 128
SLAB = KV_HEADS * HEAD_DIM
GROUP_LANES = HEADS_PER_GROUP * SSM_HEAD_DIM
DT_PAD = LANES


def _layout(d_model):
    segs = [("z", D_INNER), ("xs", D_INNER), ("ga", d_model), ("gs", d_model), ("q", Q_DIM),
            ("B", BC_DIM), ("C", BC_DIM), ("k", KV_DIM), ("v", KV_DIM), ("dt", DT_PAD)]
    lay, off = {}, 0
    for name, width in segs:
        assert off % width == 0, (name, off, width)
        lay[name] = (off, width)
        off += width
    return lay, off


def _col_tile(n, cap=3072):
    units = n // LANES
    best = 1
    for d in range(1, units + 1):
        if units % d == 0 and d * LANES <= cap:
            best = d
    return best * LANES


def _row_tile(m, cap):
    assert m % SUBLANES == 0
    best = SUBLANES
    for t in range(SUBLANES, min(m, cap) + 1, SUBLANES):
        if m % t == 0:
            best = t
    return best


def _cparams(sem):
    return pltpu.CompilerParams(dimension_semantics=sem, vmem_limit_bytes=VMEM_LIMIT)


def _sigmoid(x):
    return 1.0 / (1.0 + jnp.exp(-x))


def _silu(x):
    return x * _sigmoid(x)


def _softplus(x):
    return jnp.maximum(x, 0.0) + jnp.log1p(jnp.exp(-jnp.abs(x)))


def _split3(a):
    hi = a.astype(BF16)
    r1 = a - hi.astype(F32)
    mid = r1.astype(BF16)
    lo = (r1 - mid.astype(F32)).astype(BF16)
    return hi, mid, lo


def _expand_heads(a, e3):
    hi, mid, lo = _split3(a)
    return jnp.dot(jnp.concatenate([hi, mid, lo], axis=1), e3, preferred_element_type=F32)


def _head_ms(x, bd):
    sq = x * x
    hi = sq.astype(BF16)
    lo = (sq - hi.astype(F32)).astype(BF16)
    outs = []
    for s in range(x.shape[1] // SLAB):
        sl = slice(s * SLAB, (s + 1) * SLAB)
        outs.append(jnp.dot(hi[:, sl], bd, preferred_element_type=F32)
                    + jnp.dot(lo[:, sl], bd, preferred_element_type=F32))
    return outs[0] if len(outs) == 1 else jnp.concatenate(outs, axis=1)


def _head_norm_rope(x, g, cos, sin, bd):
    w = x.shape[1]
    xn = x * lax.rsqrt(_head_ms(x, bd) + EPS) * g
    lane = lax.broadcasted_iota(jnp.int32, xn.shape, 1)
    first_half = (lane % HEAD_DIM) < (HEAD_DIM // 2)
    rot = jnp.where(first_half, pltpu.roll(xn, w - HEAD_DIM // 2, 1), pltpu.roll(xn, HEAD_DIM // 2, 1))
    reps = w // LANES
    cos_t = cos if reps == 1 else jnp.concatenate([cos] * reps, axis=1)
    sin_t = sin if reps == 1 else jnp.concatenate([sin] * reps, axis=1)
    return xn * cos_t + rot * sin_t


def _in_proj_kernel(x_ref, g_ref, w_ref, o_ref):
    x = x_ref[...]
    ms = jnp.mean(x * x, axis=-1, keepdims=True)
    xn = (x * lax.rsqrt(ms + EPS) * g_ref[...]).astype(BF16)
    o_ref[...] = jnp.dot(xn, w_ref[...], preferred_element_type=F32)


def _in_proj(x, g, w):
    m, k = x.shape
    n = w.shape[1]
    tm = _row_tile(m, 512)
    tn = _col_tile(n)
    return pl.pallas_call(
        _in_proj_kernel,
        grid=(n // tn, m // tm),
        in_specs=[pl.BlockSpec((tm, k), lambda j, i: (i, 0)),
                  pl.BlockSpec((1, k), lambda j, i: (0, 0)),
                  pl.BlockSpec((k, tn), lambda j, i: (0, j))],
        out_specs=pl.BlockSpec((tm, tn), lambda j, i: (i, j)),
        out_shape=jax.ShapeDtypeStruct((m, n), F32),
        compiler_params=_cparams(("arbitrary", "arbitrary")),
        name="in_proj",
    )(x, g, w)


def _attn_prompt_kernel(q_ref, k_ref, v_ref, cos_ref, sin_ref, qg_ref, kg_ref, bd_ref, sink_ref,
                        o_ref, ko_ref, vo_ref, kprev, vprev):
    i = pl.program_id(1)

    @pl.when(i == 0)
    def _():
        kprev[...] = jnp.zeros_like(kprev)
        vprev[...] = jnp.zeros_like(vprev)

    cos, sin, bd = cos_ref[...], sin_ref[...], bd_ref[...]
    q = _head_norm_rope(q_ref[...], qg_ref[...], cos, sin, bd) * (HEAD_DIM ** -0.5)
    k = _head_norm_rope(k_ref[...], kg_ref[...], cos, sin, bd)
    v = v_ref[...]
    ko_ref[...] = k
    vo_ref[...] = v
    kb, vb = k.astype(BF16), v.astype(BF16)
    slot = i % 2
    k_all = jnp.concatenate([kprev[1 - slot], kb], axis=0)
    v_all = jnp.concatenate([vprev[1 - slot], vb], axis=0)
    kprev[slot] = kb
    vprev[slot] = vb

    r = lax.broadcasted_iota(jnp.int32, (BLK, 2 * BLK), 0)
    c = lax.broadcasted_iota(jnp.int32, (BLK, 2 * BLK), 1)
    first_key = jnp.where(i > 0, 0, BLK)
    mask = (c > r) & (c <= r + WINDOW) & (c >= first_key)
    grp = lax.broadcasted_iota(jnp.int32, (BLK, SLAB), 1) // HEAD_DIM
    for j in range(Q_PER_KV):
        qs = q[:, j * SLAB:(j + 1) * SLAB]
        acc = jnp.zeros((BLK, SLAB), F32)
        for p in range(KV_HEADS):
            sel = grp == p
            ql = jnp.where(sel, qs, 0.0).astype(BF16)
            s = lax.dot_general(ql, k_all, (((1,), (1,)), ((), ())), preferred_element_type=F32)
            s = jnp.where(mask, s, -jnp.inf)
            sink = sink_ref[j * KV_HEADS + p]
            mx = jnp.maximum(jnp.max(s, axis=-1, keepdims=True), sink)
            e = jnp.exp(s - mx)
            den = jnp.sum(e, axis=-1, keepdims=True) + jnp.exp(sink - mx)
            pv = jnp.dot(e.astype(BF16), v_all, preferred_element_type=F32)
            acc = jnp.where(sel, pv / den, acc)
        o_ref[:, j * SLAB:(j + 1) * SLAB] = acc


def _attn_prompt(p_act, lay, batch, seq, cos, sin, qg, kg, bd, sinks):
    nb = seq // BLK
    qc, kc, vc = lay["q"][0] // Q_DIM, lay["k"][0] // KV_DIM, lay["v"][0] // KV_DIM
    m = batch * seq
    return pl.pallas_call(
        _attn_prompt_kernel,
        grid=(batch, nb),
        in_specs=[pl.BlockSpec((BLK, Q_DIM), lambda b, i: (b * nb + i, qc)),
                  pl.BlockSpec((BLK, KV_DIM), lambda b, i: (b * nb + i, kc)),
                  pl.BlockSpec((BLK, KV_DIM), lambda b, i: (b * nb + i, vc)),
                  pl.BlockSpec((BLK, LANES), lambda b, i: (i, 0)),
                  pl.BlockSpec((BLK, LANES), lambda b, i: (i, 0)),
                  pl.BlockSpec((1, Q_DIM), lambda b, i: (0, 0)),
                  pl.BlockSpec((1, KV_DIM), lambda b, i: (0, 0)),
                  pl.BlockSpec((SLAB, SLAB), lambda b, i: (0, 0)),
                  pl.BlockSpec(memory_space=pltpu.SMEM)],
        out_specs=[pl.BlockSpec((BLK, Q_DIM), lambda b, i: (b * nb + i, 0)),
                   pl.BlockSpec((None, BLK, KV_DIM), lambda b, i: (b, 0, 0)),
                   pl.BlockSpec((None, BLK, KV_DIM), lambda b, i: (b, 0, 0))],
        out_shape=[jax.ShapeDtypeStruct((m, Q_DIM), F32),
                   jax.ShapeDtypeStruct((batch, BLK, KV_DIM), F32),
                   jax.ShapeDtypeStruct((batch, BLK, KV_DIM), F32)],
        scratch_shapes=[pltpu.VMEM((2, BLK, KV_DIM), BF16), pltpu.VMEM((2, BLK, KV_DIM), BF16)],
        compiler_params=_cparams(("arbitrary", "arbitrary")),
        name="attn_prompt",
    )(p_act, p_act, p_act, cos, sin, qg, kg, bd, sinks)


def _ssd_prompt_kernel(z_ref, xs_ref, b_ref, c_ref, dt_ref, cw_ref, cb_ref, dtb_ref, alog_ref, dskip_ref,
                       nw_ref, e3_ref, o_ref, tail_ref, hfin_ref, xpad, tails, st, ybuf):
    i = pl.program_id(1)
    last = pl.num_programs(1) - 1

    @pl.when(i == 0)
    def _():
        tails[...] = jnp.zeros_like(tails)
        st[...] = jnp.zeros_like(st)

    slot = i % 2
    xpad[0:SUBLANES, :] = tails[1 - slot]
    xpad[SUBLANES:SUBLANES + BLK, 0:D_INNER] = xs_ref[...]
    xpad[SUBLANES:SUBLANES + BLK, D_INNER:D_INNER + BC_DIM] = b_ref[...]
    xpad[SUBLANES:SUBLANES + BLK, D_INNER + BC_DIM:CONV_DIM] = c_ref[...]
    acc = cb_ref[...] + cw_ref[CONV_W - 1:CONV_W, :] * xpad[SUBLANES:SUBLANES + BLK, :]
    for t in range(1, CONV_W):
        acc = acc + cw_ref[CONV_W - 1 - t:CONV_W - t, :] * xpad[SUBLANES - t:SUBLANES - t + BLK, :]
    xc = _silu(acc)
    new_tail = xpad[BLK:BLK + SUBLANES, :]
    tail_ref[...] = new_tail
    tails[slot] = new_tail

    xs = xc[:, 0:D_INNER]
    bm = xc[:, D_INNER:D_INNER + BC_DIM].astype(BF16)
    cm = xc[:, D_INNER + BC_DIM:CONV_DIM].astype(BF16)

    e3 = e3_ref[...]
    dt = _softplus(dt_ref[...] + dtb_ref[...])
    dta = dt * (-jnp.exp(alog_ref[...]))
    row = lax.broadcasted_iota(jnp.int32, (BLK, BLK), 0)
    col = lax.broadcasted_iota(jnp.int32, (BLK, BLK), 1)
    causal = row >= col
    cum = jnp.dot(causal.astype(F32), dta, preferred_element_type=F32, precision=lax.Precision.HIGHEST)
    cum_t = cum.T
    ecum = jnp.exp(cum)
    to_end = jnp.exp(cum[BLK - 1:BLK, :] - cum) * dt
    dt_e = _expand_heads(dt, e3)
    ecum_e = _expand_heads(ecum, e3)
    to_end_e = _expand_heads(to_end, e3)
    xdt = (xs * dt_e).astype(BF16)
    xte = (xs * to_end_e).astype(BF16)
    lane = lax.broadcasted_iota(jnp.int32, (BLK, LANES), 1)
    first_head = lane < SSM_HEAD_DIM

    for g in range(SSM_GROUPS):
        gl = slice(g * GROUP_LANES, (g + 1) * GROUP_LANES)
        bg = bm[:, g * D_STATE:(g + 1) * D_STATE]
        cg = cm[:, g * D_STATE:(g + 1) * D_STATE]
        cbg = lax.dot_general(cg, bg, (((1,), (1,)), ((), ())), preferred_element_type=F32)
        st_g = st[1 - slot, :, gl]
        y_inter = jnp.dot(cg, st_g.astype(BF16), preferred_element_type=F32) * ecum_e[:, gl]
        for pr in range(HEADS_PER_GROUP // 2):
            h0 = g * HEADS_PER_GROUP + 2 * pr
            xd = xdt[:, h0 * SSM_HEAD_DIM:(h0 + 2) * SSM_HEAD_DIM]
            ys = []
            for h in (h0, h0 + 1):
                diff = cum[:, h:h + 1] - cum_t[h:h + 1, :]
                w = (jnp.exp(jnp.where(causal, diff, -jnp.inf)) * cbg).astype(BF16)
                ys.append(jnp.dot(w, xd, preferred_element_type=F32))
            lo = pr * LANES
            ybuf[:, h0 * SSM_HEAD_DIM:(h0 + 2) * SSM_HEAD_DIM] = (
                jnp.where(first_head, ys[0], ys[1]) + y_inter[:, lo:lo + LANES])
        upd = lax.dot_general(bg, xte[:, gl], (((0,), (0,)), ((), ())), preferred_element_type=F32)
        st[slot, :, gl] = st_g * ecum_e[BLK - 1:BLK, gl] + upd

    y = ybuf[...] + dskip_ref[...] * xs
    yz = y * _silu(z_ref[...])
    ms = jnp.mean(yz * yz, axis=-1, keepdims=True)
    o_ref[...] = yz * lax.rsqrt(ms + EPS) * nw_ref[...]

    @pl.when(i == last)
    def _():
        hfin_ref[...] = st[slot].T


def _ssd_prompt(p_act, lay, batch, seq, cw, cb, dtb, alog, dskip, nw, e3):
    nb = seq // BLK
    m = batch * seq
    zc, xc = lay["z"][0] // D_INNER, lay["xs"][0] // D_INNER
    bc, cc, dc = lay["B"][0] // BC_DIM, lay["C"][0] // BC_DIM, lay["dt"][0] // DT_PAD
    const = lambda b, i: (0, 0)
    return pl.pallas_call(
        _ssd_prompt_kernel,
        grid=(batch, nb),
        in_specs=[pl.BlockSpec((BLK, D_INNER), lambda b, i: (b * nb + i, zc)),
                  pl.BlockSpec((BLK, D_INNER), lambda b, i: (b * nb + i, xc)),
                  pl.BlockSpec((BLK, BC_DIM), lambda b, i: (b * nb + i, bc)),
                  pl.BlockSpec((BLK, BC_DIM), lambda b, i: (b * nb + i, cc)),
                  pl.BlockSpec((BLK, DT_PAD), lambda b, i: (b * nb + i, dc)),
                  pl.BlockSpec((CONV_W, CONV_DIM), const),
                  pl.BlockSpec((1, CONV_DIM), const),
                  pl.BlockSpec((1, DT_PAD), const),
                  pl.BlockSpec((1, DT_PAD), const),
                  pl.BlockSpec((1, D_INNER), const),
                  pl.BlockSpec((1, D_INNER), const),
                  pl.BlockSpec((3 * DT_PAD, D_INNER), const)],
        out_specs=[pl.BlockSpec((BLK, D_INNER), lambda b, i: (b * nb + i, 0)),
                   pl.BlockSpec((None, SUBLANES, CONV_DIM), lambda b, i: (b, 0, 0)),
                   pl.BlockSpec((None, D_INNER, D_STATE), lambda b, i: (b, 0, 0))],
        out_shape=[jax.ShapeDtypeStruct((m, D_INNER), F32),
                   jax.ShapeDtypeStruct((batch, SUBLANES, CONV_DIM), F32),
                   jax.ShapeDtypeStruct((batch, D_INNER, D_STATE), F32)],
        scratch_shapes=[pltpu.VMEM((BLK + SUBLANES, CONV_DIM), F32),
                        pltpu.VMEM((2, SUBLANES, CONV_DIM), F32),
                        pltpu.VMEM((2, D_STATE, D_INNER), F32),
                        pltpu.VMEM((BLK, D_INNER), F32)],
        compiler_params=_cparams(("arbitrary", "arbitrary")),
        name="ssd_prompt",
    )(p_act, p_act, p_act, p_act, p_act, cw, cb, dtb, alog, dskip, nw, e3)


def _decode_pre_kernel(q_ref, k_ref, xs_ref, b_ref, c_ref, dt_ref, cs_ref, cos_ref, sin_ref, qg_ref, kg_ref,
                       bd_ref, cw_ref, cb_ref, dtb_ref, alog_ref, e3_ref,
                       qo_ref, ko_ref, xc_ref, xdt_ref, dec_ref, cso_ref):
    cos, sin, bd = cos_ref[...], sin_ref[...], bd_ref[...]
    qo_ref[...] = _head_norm_rope(q_ref[...], qg_ref[...], cos, sin, bd) * (HEAD_DIM ** -0.5)
    ko_ref[...] = _head_norm_rope(k_ref[...], kg_ref[...], cos, sin, bd)

    segs = ((xs_ref, 0, D_INNER), (b_ref, D_INNER, BC_DIM), (c_ref, D_INNER + BC_DIM, BC_DIM))
    for ref, off, width in segs:
        new = ref[...]
        acc = cb_ref[:, off:off + width] + cw_ref[CONV_W - 1:CONV_W, off:off + width] * new
        for t in range(CONV_W - 1):
            lo = t * CONV_DIM + off
            acc = acc + cw_ref[t:t + 1, off:off + width] * cs_ref[:, lo:lo + width]
        xc_ref[:, off:off + width] = _silu(acc)
        for t in range(CONV_W - 2):
            cso_ref[:, t * CONV_DIM + off:t * CONV_DIM + off + width] = (
                cs_ref[:, (t + 1) * CONV_DIM + off:(t + 1) * CONV_DIM + off + width])
        lo = (CONV_W - 2) * CONV_DIM + off
        cso_ref[:, lo:lo + width] = new

    e3 = e3_ref[...]
    dt = _softplus(dt_ref[...] + dtb_ref[...])
    decay = jnp.exp(dt * (-jnp.exp(alog_ref[...])))
    xdt_ref[...] = xc_ref[:, 0:D_INNER] * _expand_heads(dt, e3)
    dec_ref[...] = _expand_heads(decay, e3)


def _decode_pre(p_act, lay, conv_state, cos, sin, qg, kg, bd, cw, cb, dtb, alog, e3):
    m = p_act.shape[0]
    full = lambda shape: pl.BlockSpec(shape, lambda i: (0, 0))

    def col(name):
        c = lay[name][0] // lay[name][1]
        return pl.BlockSpec((m, lay[name][1]), lambda i: (0, c))

    cs_w = (CONV_W - 1) * CONV_DIM
    return pl.pallas_call(
        _decode_pre_kernel,
        grid=(1,),
        in_specs=[col("q"), col("k"), col("xs"), col("B"), col("C"), col("dt"),
                  full((m, cs_w)), full((1, LANES)), full((1, LANES)), full((1, Q_DIM)), full((1, KV_DIM)),
                  full((SLAB, SLAB)), full((CONV_W, CONV_DIM)), full((1, CONV_DIM)), full((1, DT_PAD)),
                  full((1, DT_PAD)), full((3 * DT_PAD, D_INNER))],
        out_specs=[full((m, Q_DIM)), full((m, KV_DIM)), full((m, CONV_DIM)), full((m, D_INNER)),
                   full((m, D_INNER)), full((m, cs_w))],
        out_shape=[jax.ShapeDtypeStruct((m, Q_DIM), F32), jax.ShapeDtypeStruct((m, KV_DIM), F32),
                   jax.ShapeDtypeStruct((m, CONV_DIM), F32), jax.ShapeDtypeStruct((m, D_INNER), F32),
                   jax.ShapeDtypeStruct((m, D_INNER), F32), jax.ShapeDtypeStruct((m, cs_w), F32)],
        compiler_params=_cparams(("arbitrary",)),
        name="decode_pre",
    )(p_act, p_act, p_act, p_act, p_act, p_act, conv_state, cos, sin, qg, kg, bd, cw, cb, dtb, alog, e3)


DEC_ROWS = 8


def _attn_decode_kernel(q_ref, k_ref, v_ref, ck_ref, cv_ref, sink_ref, o_ref):
    w = ck_ref.shape[1]
    nq = N_HEADS
    r = lax.broadcasted_iota(jnp.int32, (nq, SLAB), 0)
    grp = lax.broadcasted_iota(jnp.int32, (nq, SLAB), 1) // HEAD_DIM
    own = grp == (r % KV_HEADS)
    krow = lax.broadcasted_iota(jnp.int32, (w, SLAB), 0)
    sink = sink_ref[...]
    for bl in range(DEC_ROWS):
        qrow = q_ref[bl:bl + 1, :]
        qm = jnp.zeros((nq, SLAB), F32)
        for j in range(Q_PER_KV):
            slab = jnp.broadcast_to(qrow[:, j * SLAB:(j + 1) * SLAB], (nq, SLAB))
            qm = jnp.where((r // KV_HEADS) == j, slab, qm)
        qm = jnp.where(own, qm, 0.0).astype(BF16)
        keys = jnp.where(krow == 0, k_ref[bl:bl + 1, :], ck_ref[bl]).astype(BF16)
        vals = jnp.where(krow == 0, v_ref[bl:bl + 1, :], cv_ref[bl]).astype(BF16)
        s = lax.dot_general(qm, keys, (((1,), (1,)), ((), ())), preferred_element_type=F32)
        mx = jnp.maximum(jnp.max(s, axis=-1, keepdims=True), sink[:, 0:1])
        e = jnp.exp(s - mx)
        den = jnp.sum(e, axis=-1, keepdims=True) + jnp.exp(sink[:, 0:1] - mx)
        pv = jnp.dot(e.astype(BF16), vals, preferred_element_type=F32) / den
        pv = jnp.where(own, pv, 0.0)
        for j in range(Q_PER_KV):
            o_ref[bl:bl + 1, j * SLAB:(j + 1) * SLAB] = jnp.sum(
                pv[j * KV_HEADS:(j + 1) * KV_HEADS, :], axis=0, keepdims=True)


def _attn_decode(q, k, p_act, lay, cache_k, cache_v, sink_rows):
    m = q.shape[0]
    w = cache_k.shape[1]
    vc = lay["v"][0] // KV_DIM
    return pl.pallas_call(
        _attn_decode_kernel,
        grid=(m // DEC_ROWS,),
        in_specs=[pl.BlockSpec((DEC_ROWS, Q_DIM), lambda i: (i, 0)),
                  pl.BlockSpec((DEC_ROWS, KV_DIM), lambda i: (i, 0)),
                  pl.BlockSpec((DEC_ROWS, KV_DIM), lambda i: (i, vc)),
                  pl.BlockSpec((DEC_ROWS, w, KV_DIM), lambda i: (i, 0, 0)),
                  pl.BlockSpec((DEC_ROWS, w, KV_DIM), lambda i: (i, 0, 0)),
                  pl.BlockSpec((N_HEADS, LANES), lambda i: (0, 0))],
        out_specs=pl.BlockSpec((DEC_ROWS, Q_DIM), lambda i: (i, 0)),
        out_shape=jax.ShapeDtypeStruct((m, Q_DIM), F32),
        compiler_params=_cparams(("arbitrary",)),
        name="attn_decode",
    )(q, k, p_act, cache_k, cache_v, sink_rows)


MM_ROWS = 16


def _ssm_decode_kernel(st_ref, xdt_ref, dec_ref, xc_ref, z_ref, dskip_ref, nw_ref, sto_ref, o_ref):
    h = st_ref[...]
    xdt = xdt_ref[...]
    xc = xc_ref[...]
    hi, mid, lo = _split3(dec_ref[...])
    r = lax.broadcasted_iota(jnp.int32, (MM_ROWS, D_INNER), 0)
    grp = lax.broadcasted_iota(jnp.int32, (MM_ROWS, D_INNER), 1) // GROUP_LANES
    bc = lambda a: jnp.broadcast_to(a, (MM_ROWS, D_INNER))
    lhs_t = jnp.where(r == grp, bc(xdt), 0.0)
    for t, piece in enumerate((hi, mid, lo)):
        lhs_t = jnp.where(r == SSM_GROUPS + t, bc(piece.astype(F32)), lhs_t)
    lhs_t = lhs_t.astype(BF16)
    rr = lax.broadcasted_iota(jnp.int32, (MM_ROWS, D_STATE), 0)
    b_rows = jnp.zeros((MM_ROWS, D_STATE), F32)
    c_rows = jnp.zeros((MM_ROWS, D_STATE), F32)
    for g in range(SSM_GROUPS):
        b_g = xc[:, D_INNER + g * D_STATE:D_INNER + (g + 1) * D_STATE]
        c_g = xc[:, D_INNER + BC_DIM + g * D_STATE:D_INNER + BC_DIM + (g + 1) * D_STATE]
        b_rows = jnp.where(rr == g, jnp.broadcast_to(b_g, (MM_ROWS, D_STATE)), b_rows)
        c_rows = jnp.where(rr == g, jnp.broadcast_to(c_g, (MM_ROWS, D_STATE)), c_rows)
    ones_rows = jnp.where((rr >= SSM_GROUPS) & (rr < SSM_GROUPS + 3), 1.0, 0.0)
    rhs = jnp.concatenate([b_rows, ones_rows], axis=1).astype(BF16)
    both = lax.dot_general(lhs_t, rhs, (((0,), (0,)), ((), ())), preferred_element_type=F32)
    h_new = both[:, D_STATE:] * h + both[:, :D_STATE]
    sto_ref[...] = h_new
    yg = lax.dot_general(c_rows.astype(BF16), h_new.astype(BF16), (((1,), (1,)), ((), ())),
                         preferred_element_type=F32)
    y = jnp.sum(jnp.where(r == grp, yg, 0.0), axis=0, keepdims=True)
    y = y + dskip_ref[...] * xc[:, 0:D_INNER]
    yz = y * _silu(z_ref[...])
    ms = jnp.mean(yz * yz, axis=-1, keepdims=True)
    o_ref[...] = yz * lax.rsqrt(ms + EPS) * nw_ref[...]


def _ssm_decode(state, xdt, dec, xc, p_act, lay, dskip, nw):
    m = state.shape[0]
    n = p_act.shape[1]
    zc = lay["z"][0] // D_INNER
    row = lambda width: pl.BlockSpec((None, 1, width), lambda b: (b, 0, 0))
    return pl.pallas_call(
        _ssm_decode_kernel,
        grid=(m,),
        in_specs=[pl.BlockSpec((None, D_INNER, D_STATE), lambda b: (b, 0, 0)),
                  row(D_INNER), row(D_INNER), row(CONV_DIM),
                  pl.BlockSpec((None, 1, D_INNER), lambda b: (b, 0, zc)),
                  pl.BlockSpec((1, D_INNER), lambda b: (0, 0)),
                  pl.BlockSpec((1, D_INNER), lambda b: (0, 0))],
        out_specs=[pl.BlockSpec((None, D_INNER, D_STATE), lambda b: (b, 0, 0)), row(D_INNER)],
        out_shape=[jax.ShapeDtypeStruct((m, D_INNER, D_STATE), F32),
                   jax.ShapeDtypeStruct((m, 1, D_INNER), F32)],
        compiler_params=_cparams(("arbitrary",)),
        name="ssm_decode",
    )(state, xdt.reshape(m, 1, D_INNER), dec.reshape(m, 1, D_INNER), xc.reshape(m, 1, CONV_DIM),
      p_act.reshape(m, 1, n), dskip, nw)


def _merge_kernel(x_ref, oa_ref, os_ref, ga_ref, gs_ref, wa_ref, ws_ref, wo_ref, o_ref):
    a = jnp.dot(oa_ref[...].astype(BF16), wa_ref[...], preferred_element_type=F32)
    s = jnp.dot(os_ref[...].astype(BF16), ws_ref[...], preferred_element_type=F32)
    mixed = _sigmoid(ga_ref[...]) * a + _sigmoid(gs_ref[...]) * s
    o_ref[...] = x_ref[...] + jnp.dot(mixed.astype(BF16), wo_ref[...], preferred_element_type=F32)


def _merge(x, o_attn, o_ssm, p_act, lay, wa, ws, wo):
    m, d = x.shape
    tm = _row_tile(m, 256)
    gac, gsc = lay["ga"][0] // d, lay["gs"][0] // d
    const = lambda i: (0, 0)
    return pl.pallas_call(
        _merge_kernel,
        grid=(m // tm,),
        in_specs=[pl.BlockSpec((tm, d), lambda i: (i, 0)),
                  pl.BlockSpec((tm, Q_DIM), lambda i: (i, 0)),
                  pl.BlockSpec((tm, D_INNER), lambda i: (i, 0)),
                  pl.BlockSpec((tm, d), lambda i: (i, gac)),
                  pl.BlockSpec((tm, d), lambda i: (i, gsc)),
                  pl.BlockSpec((Q_DIM, d), const),
                  pl.BlockSpec((D_INNER, d), const),
                  pl.BlockSpec((d, d), const)],
        out_specs=pl.BlockSpec((tm, d), lambda i: (i, 0)),
        out_shape=jax.ShapeDtypeStruct((m, d), F32),
        compiler_params=_cparams(("arbitrary",)),
        name="merge",
    )(x, o_attn, o_ssm, p_act, p_act, wa, ws, wo)


def _mlp_kernel(x_ref, g_ref, wu_ref, wd_ref, o_ref, *, ff_tile):
    x = x_ref[...]
    ms = jnp.mean(x * x, axis=-1, keepdims=True)
    xn = (x * lax.rsqrt(ms + EPS) * g_ref[...]).astype(BF16)
    acc = x
    for c in range(wu_ref.shape[1] // ff_tile):
        u = jnp.dot(xn, wu_ref[:, c * ff_tile:(c + 1) * ff_tile], preferred_element_type=F32)
        a = jnp.square(jnp.maximum(u, 0.0)).astype(BF16)
        acc = acc + jnp.dot(a, wd_ref[c * ff_tile:(c + 1) * ff_tile, :], preferred_element_type=F32)
    o_ref[...] = acc


def _mlp(x, g, wu, wd):
    m, d = x.shape
    ff = wu.shape[1]
    tm = _row_tile(m, 256)
    const = lambda i: (0, 0)
    return pl.pallas_call(
        functools.partial(_mlp_kernel, ff_tile=min(1024, ff)),
        grid=(m // tm,),
        in_specs=[pl.BlockSpec((tm, d), lambda i: (i, 0)),
                  pl.BlockSpec((1, d), const),
                  pl.BlockSpec((d, ff), const),
                  pl.BlockSpec((ff, d), const)],
        out_specs=pl.BlockSpec((tm, d), lambda i: (i, 0)),
        out_shape=jax.ShapeDtypeStruct((m, d), F32),
        compiler_params=_cparams(("arbitrary",)),
        name="mlp",
    )(x, g, wu, wd)


def _rope_tables(pos):
    half = HEAD_DIM // 2
    inv = ROPE_THETA ** (-jnp.arange(half, dtype=F32) / half)
    ang = pos.astype(F32)[:, None] * inv[None, :]
    cos, sin = jnp.cos(ang), jnp.sin(ang)
    reps = LANES // HEAD_DIM
    return (jnp.tile(jnp.concatenate([cos, cos], axis=1), (1, reps)),
            jnp.tile(jnp.concatenate([-sin, sin], axis=1), (1, reps)))


def _constants():
    heads = np.arange(DT_PAD)[:, None]
    cols = np.arange(D_INNER)[None, :] // SSM_HEAD_DIM
    e = (heads == cols).astype(np.float32)
    e3 = jnp.asarray(np.concatenate([e, e, e], axis=0), BF16)
    a = np.arange(SLAB)
    bd = jnp.asarray((a[:, None] // HEAD_DIM == a[None, :] // HEAD_DIM).astype(np.float32) / HEAD_DIM, BF16)
    perm = np.concatenate([np.arange(HEAD_DIM) + (p * Q_PER_KV + j) * HEAD_DIM
                           for j in range(Q_PER_KV) for p in range(KV_HEADS)])
    head_perm = np.array([p * Q_PER_KV + j for j in range(Q_PER_KV) for p in range(KV_HEADS)])
    return e3, bd, perm, head_perm


def _prep_layer(d_model, norm_mix, w_in, q_norm, k_norm, attn_sinks, conv_w, conv_b, dt_bias, a_log, d_skip,
                ssm_norm, w_attn_o, w_ssm_o, w_out, norm_mlp, w_up, w_down):
    e3, bd, perm, head_perm = _constants()
    o_q, o_k, o_v, o_z, o_xbc, o_dt = (0, Q_DIM, Q_DIM + KV_DIM, Q_DIM + 2 * KV_DIM,
                                       Q_DIM + 2 * KV_DIM + D_INNER, Q_DIM + 2 * KV_DIM + D_INNER + CONV_DIM)
    o_g = o_dt + SSM_HEADS
    cut = lambda lo, width: w_in[:, lo:lo + width]
    pieces = {"z": cut(o_z, D_INNER), "xs": cut(o_xbc, D_INNER), "ga": cut(o_g, d_model),
              "gs": cut(o_g + d_model, d_model), "q": cut(o_q, Q_DIM)[:, perm],
              "B": cut(o_xbc + D_INNER, BC_DIM), "C": cut(o_xbc + D_INNER + BC_DIM, BC_DIM),
              "k": cut(o_k, KV_DIM), "v": cut(o_v, KV_DIM),
              "dt": jnp.pad(cut(o_dt, SSM_HEADS), ((0, 0), (0, DT_PAD - SSM_HEADS)))}
    lay, _ = _layout(d_model)
    w_p = jnp.concatenate([pieces[name] for name in lay], axis=1).astype(BF16)
    pad_heads = lambda a: jnp.pad(a, (0, DT_PAD - SSM_HEADS))[None, :]
    sink_p = attn_sinks[head_perm]
    return dict(
        lay=lay, e3=e3, bd=bd, w_p=w_p,
        norm_mix=norm_mix[None, :], norm_mlp=norm_mlp[None, :],
        qg=jnp.tile(q_norm, N_HEADS)[None, :], kg=jnp.tile(k_norm, KV_HEADS)[None, :],
        sink_p=sink_p, sink_rows=jnp.broadcast_to(sink_p[:, None], (N_HEADS, LANES)),
        cw=conv_w, cb=conv_b[None, :], dtb=pad_heads(dt_bias), alog=pad_heads(a_log),
        dskip=jnp.repeat(d_skip, SSM_HEAD_DIM)[None, :], nw=ssm_norm[None, :],
        wa=w_attn_o[perm, :].astype(BF16), ws=w_ssm_o.astype(BF16), wo=w_out.astype(BF16),
        wu=w_up.astype(BF16), wd=w_down.astype(BF16))


def _prompt_layer(x, lw):
    batch, seq, d = x.shape
    assert seq % BLK == 0 and seq >= WINDOW
    x2 = x.reshape(batch * seq, d)
    p_act = _in_proj(x2, lw["norm_mix"], lw["w_p"])
    cos, sin = _rope_tables(jnp.arange(seq))
    o_attn, k_last, v_last = _attn_prompt(p_act, lw["lay"], batch, seq, cos, sin, lw["qg"], lw["kg"],
                                          lw["bd"], lw["sink_p"])
    o_ssm, tail, h_fin = _ssd_prompt(p_act, lw["lay"], batch, seq, lw["cw"], lw["cb"], lw["dtb"], lw["alog"],
                                     lw["dskip"], lw["nw"], lw["e3"])
    x1 = _merge(x2, o_attn, o_ssm, p_act, lw["lay"], lw["wa"], lw["ws"], lw["wo"])
    y = _mlp(x1, lw["norm_mlp"], lw["wu"], lw["wd"])
    return (y.reshape(batch, seq, d),
            k_last.reshape(batch, BLK, KV_HEADS, HEAD_DIM), v_last.reshape(batch, BLK, KV_HEADS, HEAD_DIM),
            tail[:, SUBLANES - (CONV_W - 1):, :],
            h_fin.reshape(batch, SSM_HEADS, SSM_HEAD_DIM, D_STATE))


def _decode_layer(x, cache_k, cache_v, conv_state, ssm_state, lw):
    m, t, d = x.shape
    w = cache_k.shape[1]
    assert t == 1 and w == WINDOW and m % DEC_ROWS == 0
    x2 = x.reshape(m, d)
    p_act = _in_proj(x2, lw["norm_mix"], lw["w_p"])
    cos, sin = _rope_tables(PAST_LEN + jnp.arange(1))
    q, k, xc, xdt, dec, conv_new = _decode_pre(
        p_act, lw["lay"], conv_state.reshape(m, (CONV_W - 1) * CONV_DIM), cos, sin, lw["qg"], lw["kg"],
        lw["bd"], lw["cw"], lw["cb"], lw["dtb"], lw["alog"], lw["e3"])
    ck = cache_k.reshape(m, w, KV_DIM)
    cv = cache_v.reshape(m, w, KV_DIM)
    o_attn = _attn_decode(q, k, p_act, lw["lay"], ck, cv, lw["sink_rows"])
    h_new, o_ssm = _ssm_decode(ssm_state.reshape(m, D_INNER, D_STATE), xdt, dec, xc, p_act, lw["lay"],
                               lw["dskip"], lw["nw"])
    x1 = _merge(x2, o_attn, o_ssm.reshape(m, D_INNER), p_act, lw["lay"], lw["wa"], lw["ws"], lw["wo"])
    y = _mlp(x1, lw["norm_mlp"], lw["wu"], lw["wd"])
    v_off = lw["lay"]["v"][0]
    v_new = p_act[:, v_off:v_off + KV_DIM]
    new_k = jnp.concatenate([ck[:, 1:], k[:, None, :]], axis=1).reshape(m, w, KV_HEADS, HEAD_DIM)
    new_v = jnp.concatenate([cv[:, 1:], v_new[:, None, :]], axis=1).reshape(m, w, KV_HEADS, HEAD_DIM)
    return (y.reshape(m, 1, d), new_k, new_v, conv_new.reshape(m, CONV_W - 1, CONV_DIM),
            h_new.reshape(m, SSM_HEADS, SSM_HEAD_DIM, D_STATE))


def kernel(x_prompt, x_sample, cache_k, cache_v, state_conv, state_ssm, norm_mix, w_in, q_norm, k_norm,
           attn_sinks, conv_w, conv_b, dt_bias, a_log, d_skip, ssm_norm, w_attn_o, w_ssm_o, w_out,
           norm_mlp, w_up, w_down):
    depth = w_in.shape[0]
    d_model = x_prompt.shape[-1]
    yp, ys = x_prompt, x_sample
    cols = [[] for _ in range(8)]
    for l in range(depth):
        lw = _prep_layer(d_model, norm_mix[l], w_in[l], q_norm[l], k_norm[l], attn_sinks[l], conv_w[l],
                         conv_b[l], dt_bias[l], a_log[l], d_skip[l], ssm_norm[l], w_attn_o[l], w_ssm_o[l],
                         w_out[l], norm_mlp[l], w_up[l], w_down[l])
        yp, kp, vp, cp, hp = _prompt_layer(yp, lw)
        ys, ks, vs, cs, hs = _decode_layer(ys, cache_k[l], cache_v[l], state_conv[l], state_ssm[l], lw)
        for lst, val in zip(cols, (kp, vp, cp, hp, ks, vs, cs, hs)):
            lst.append(val)
    return (yp, ys) + tuple(jnp.stack(c) for c in cols)
```

```python
import functools

import numpy as np
import jax
import jax.numpy as jnp
from jax import lax
from jax.experimental import pallas as pl
from jax.experimental.pallas import tpu as pltpu

F32 = jnp.float32
BF16 = jnp.bfloat16

N_HEADS = 16
KV_HEADS = 4
HEAD_DIM = 64
Q_PER_KV = N_HEADS // KV_HEADS
WINDOW = 128
ROPE_THETA = 10000.0
Q_DIM = N_HEADS * HEAD_DIM
KV_DIM = KV_HEADS * HEAD_DIM
SSM_HEAD_DIM = 64
SSM_HEADS = 32
D_INNER = SSM_HEADS * SSM_HEAD_DIM
SSM_GROUPS = 4
HEADS_PER_GROUP = SSM_HEADS // SSM_GROUPS
D_STATE = 128
BC_DIM = SSM_GROUPS * D_STATE
CONV_W = 4
CONV_DIM = D_INNER + 2 * BC_DIM
SSD_CHUNK = 128
EPS = 1e-6
PAST_LEN = 8192
LOG2E = 1.4426950408889634

LANES = 128
SUBLANES = 8
VMEM_LIMIT = 48 * 1024 * 1024

BLK = 128
assert WINDOW == BLK and SSD_CHUNK == BLK
SLAB = KV_HEADS * HEAD_DIM
GROUP_LANES = HEADS_PER_GROUP * SSM_HEAD_DIM
DT_PAD = LANES


def _layout(d_model):
    segs = [("z", D_INNER), ("xs", D_INNER), ("ga", d_model), ("gs", d_model), ("q", Q_DIM),
            ("B", BC_DIM), ("C", BC_DIM), ("k", KV_DIM), ("v", KV_DIM), ("dt", DT_PAD)]
    lay, off = {}, 0
    for name, width in segs:
        assert off % width == 0, (name, off, width)
        lay[name] = (off, width)
        off += width
    return lay, off


def _col_tile(n, cap=3072):
    units = n // LANES
    best = 1
    for d in range(1, units + 1):
        if units % d == 0 and d * LANES <= cap:
            best = d
    return best * LANES


def _row_tile(m, cap):
    assert m % SUBLANES == 0
    best = SUBLANES
    for t in range(SUBLANES, min(m, cap) + 1, SUBLANES):
        if m % t == 0:
            best = t
    return best


def _cparams(sem):
    return pltpu.CompilerParams(dimension_semantics=sem, vmem_limit_bytes=VMEM_LIMIT)


def _sigmoid(x):
    return 0.5 + 0.5 * jnp.tanh(0.5 * x)


def _silu(x):
    h = 0.5 * x
    return h + h * jnp.tanh(h)


def _softplus(x):
    return jnp.maximum(x, 0.0) + jnp.log1p(jnp.exp(-jnp.abs(x)))


def _split3(a):
    hi = a.astype(BF16)
    r1 = a - hi.astype(F32)
    mid = r1.astype(BF16)
    lo = (r1 - mid.astype(F32)).astype(BF16)
    return hi, mid, lo


def _expand_heads(a, e3):
    hi, mid, lo = _split3(a)
    return jnp.dot(jnp.concatenate([hi, mid, lo], axis=1), e3, preferred_element_type=F32)


def _head_ms(x, bd):
    sq = x * x
    hi = sq.astype(BF16)
    lo = (sq - hi.astype(F32)).astype(BF16)
    outs = []
    for s in range(x.shape[1] // SLAB):
        sl = slice(s * SLAB, (s + 1) * SLAB)
        outs.append(jnp.dot(hi[:, sl], bd, preferred_element_type=F32)
                    + jnp.dot(lo[:, sl], bd, preferred_element_type=F32))
    return outs[0] if len(outs) == 1 else jnp.concatenate(outs, axis=1)


def _head_norm_rope(x, g, cos, sin, bd):
    xn = x * lax.rsqrt(_head_ms(x, bd) + EPS) * g
    tiles = []
    for t in range(x.shape[1] // LANES):
        xt = xn[:, t * LANES:(t + 1) * LANES]
        tiles.append(xt * cos + pltpu.roll(xt, LANES // 2, 1) * sin)
    return tiles[0] if len(tiles) == 1 else jnp.concatenate(tiles, axis=1)


def _unpair(x):
    q = HEAD_DIM // 2
    lane = lax.broadcasted_iota(jnp.int32, (x.shape[0], LANES), 1)
    tiles = []
    for t in range(x.shape[1] // LANES):
        xt = x[:, t * LANES:(t + 1) * LANES]
        nat = jnp.where((lane >= q) & (lane < 2 * q), pltpu.roll(xt, LANES - q, 1), xt)
        tiles.append(jnp.where((lane >= 2 * q) & (lane < 3 * q), pltpu.roll(xt, q, 1), nat))
    return tiles[0] if len(tiles) == 1 else jnp.concatenate(tiles, axis=1)


def _in_proj_kernel(x_ref, g_ref, w_ref, o_ref):
    x = x_ref[...]
    ms = jnp.mean(x * x, axis=-1, keepdims=True)
    xn = (x * lax.rsqrt(ms + EPS) * g_ref[...]).astype(BF16)
    o_ref[...] = jnp.dot(xn, w_ref[...], preferred_element_type=F32)


def _in_proj(x, g, w):
    m, k = x.shape
    n = w.shape[1]
    tm = _row_tile(m, 512)
    tn = _col_tile(n)
    return pl.pallas_call(
        _in_proj_kernel,
        grid=(n // tn, m // tm),
        in_specs=[pl.BlockSpec((tm, k), lambda j, i: (i, 0)),
                  pl.BlockSpec((1, k), lambda j, i: (0, 0)),
                  pl.BlockSpec((k, tn), lambda j, i: (0, j))],
        out_specs=pl.BlockSpec((tm, tn), lambda j, i: (i, j)),
        out_shape=jax.ShapeDtypeStruct((m, n), F32),
        compiler_params=_cparams(("arbitrary", "arbitrary")),
        name="in_proj",
    )(x, g, w)


def _attn_prompt_kernel(q_ref, k_ref, v_ref, cos_ref, sin_ref, qg_ref, kg_ref, bd_ref, sink_ref,
                        o_ref, ko_ref, vo_ref, kbuf, vbuf, probs):
    i = pl.program_id(1)
    last = pl.num_programs(1) - 1

    @pl.when(i == 0)
    def _():
        kbuf[...] = jnp.zeros_like(kbuf)
        vbuf[...] = jnp.zeros_like(vbuf)

    cos, sin, bd = cos_ref[...], sin_ref[...], bd_ref[...]
    slot = i % 2
    lane = lax.broadcasted_iota(jnp.int32, (BLK, SLAB), 1)
    grp_k = 2 * (lane // LANES) + (lane % HEAD_DIM) // (HEAD_DIM // 2)
    grp_v = lane // HEAD_DIM
    r = lax.broadcasted_iota(jnp.int32, (BLK, 2 * BLK), 0)
    c = lax.broadcasted_iota(jnp.int32, (BLK, 2 * BLK), 1)
    key = c % BLK
    is_cur = (c // BLK) == slot
    prev_ok = jnp.where(i > 0, key, -1)
    mask = jnp.where(is_cur, (key <= r).astype(jnp.int32), (prev_ok > r + (BLK - WINDOW)).astype(jnp.int32)) > 0

    for u in range(q_ref.shape[0]):
        q = _head_norm_rope(q_ref[u], qg_ref[...], cos, sin, bd) * (HEAD_DIM ** -0.5 * LOG2E)
        k = _head_norm_rope(k_ref[u], kg_ref[...], cos, sin, bd)
        v = v_ref[u]

        @pl.when(i == last)
        def _():
            ko_ref[u] = _unpair(k)
            vo_ref[u] = v

        for p in range(KV_HEADS):
            rows = pl.ds(pl.multiple_of(p * 2 * BLK + slot * BLK, BLK), BLK)
            kbuf[u, rows, :] = jnp.where(grp_k == p, k, 0.0).astype(BF16)
            vbuf[u, rows, :] = jnp.where(grp_v == p, v, 0.0).astype(BF16)

        q_stack = jnp.concatenate([q[:, j * SLAB:(j + 1) * SLAB] for j in range(Q_PER_KV)],
                                  axis=0).astype(BF16)
        s_all = lax.dot_general(q_stack, kbuf[u], (((1,), (1,)), ((), ())), preferred_element_type=F32)
        for j in range(Q_PER_KV):
            for p in range(KV_HEADS):
                s = jnp.where(mask, s_all[j * BLK:(j + 1) * BLK, p * 2 * BLK:(p + 1) * 2 * BLK], -jnp.inf)
                sink = sink_ref[j * KV_HEADS + p] * LOG2E
                mx = jnp.maximum(jnp.max(s, axis=-1, keepdims=True), sink)
                e = jnp.exp2(s - mx)
                den = jnp.sum(e, axis=-1, keepdims=True) + jnp.exp2(sink - mx)
                probs[u, j * BLK:(j + 1) * BLK, p * 2 * BLK:(p + 1) * 2 * BLK] = (e / den).astype(BF16)
        pv = jnp.dot(probs[u], vbuf[u], preferred_element_type=F32)
        for j in range(Q_PER_KV):
            o_ref[u, :, j * SLAB:(j + 1) * SLAB] = pv[j * BLK:(j + 1) * BLK, :]


ATTN_SEQS = 2


def _attn_prompt(p_act, lay, batch, seq, cos, sin, qg, kg, bd, sinks):
    nb = seq // BLK
    u = ATTN_SEQS if batch % ATTN_SEQS == 0 else 1
    qc, kc, vc = lay["q"][0] // Q_DIM, lay["k"][0] // KV_DIM, lay["v"][0] // KV_DIM
    p3 = p_act.reshape(batch, seq, p_act.shape[1])
    return pl.pallas_call(
        _attn_prompt_kernel,
        grid=(batch // u, nb),
        in_specs=[pl.BlockSpec((u, BLK, Q_DIM), lambda b, i: (b, i, qc)),
                  pl.BlockSpec((u, BLK, KV_DIM), lambda b, i: (b, i, kc)),
                  pl.BlockSpec((u, BLK, KV_DIM), lambda b, i: (b, i, vc)),
                  pl.BlockSpec((BLK, LANES), lambda b, i: (i, 0)),
                  pl.BlockSpec((BLK, LANES), lambda b, i: (i, 0)),
                  pl.BlockSpec((1, Q_DIM), lambda b, i: (0, 0)),
                  pl.BlockSpec((1, KV_DIM), lambda b, i: (0, 0)),
                  pl.BlockSpec((SLAB, SLAB), lambda b, i: (0, 0)),
                  pl.BlockSpec(memory_space=pltpu.SMEM)],
        out_specs=[pl.BlockSpec((u, BLK, Q_DIM), lambda b, i: (b, i, 0)),
                   pl.BlockSpec((u, BLK, KV_DIM), lambda b, i: (b, 0, 0)),
                   pl.BlockSpec((u, BLK, KV_DIM), lambda b, i: (b, 0, 0))],
        out_shape=[jax.ShapeDtypeStruct((batch, seq, Q_DIM), F32),
                   jax.ShapeDtypeStruct((batch, BLK, KV_DIM), F32),
                   jax.ShapeDtypeStruct((batch, BLK, KV_DIM), F32)],
        scratch_shapes=[pltpu.VMEM((u, KV_HEADS * 2 * BLK, KV_DIM), BF16),
                        pltpu.VMEM((u, KV_HEADS * 2 * BLK, KV_DIM), BF16),
                        pltpu.VMEM((u, Q_PER_KV * BLK, KV_HEADS * 2 * BLK), BF16)],
        compiler_params=_cparams(("arbitrary", "arbitrary")),
        name="attn_prompt",
    )(p3, p3, p3, cos, sin, qg, kg, bd, sinks)


def _ssd_prompt_kernel(z_ref, xs_ref, b_ref, c_ref, dt_ref, cw_ref, cb_ref, dtb_ref, alog_ref, dskip_ref,
                       nw_ref, e3_ref, o_ref, tail_ref, hfin_ref, xpad, tails, st, ybuf):
    i = pl.program_id(1)
    last = pl.num_programs(1) - 1

    @pl.when(i == 0)
    def _():
        tails[...] = jnp.zeros_like(tails)
        st[...] = jnp.zeros_like(st)

    slot = i % 2
    xpad[0:SUBLANES, :] = tails[1 - slot]
    xpad[SUBLANES:SUBLANES + BLK, 0:D_INNER] = xs_ref[...]
    xpad[SUBLANES:SUBLANES + BLK, D_INNER:D_INNER + BC_DIM] = b_ref[...]
    xpad[SUBLANES:SUBLANES + BLK, D_INNER + BC_DIM:CONV_DIM] = c_ref[...]
    cwh = 0.5 * cw_ref[...]
    acc = 0.5 * cb_ref[...] + cwh[CONV_W - 1:CONV_W, :] * xpad[SUBLANES:SUBLANES + BLK, :]
    for t in range(1, CONV_W):
        acc = acc + cwh[CONV_W - 1 - t:CONV_W - t, :] * xpad[SUBLANES - t:SUBLANES - t + BLK, :]
    xc = acc + acc * jnp.tanh(acc)
    new_tail = xpad[BLK:BLK + SUBLANES, :]
    tail_ref[...] = new_tail
    tails[slot] = new_tail

    xs = xc[:, 0:D_INNER]
    bm = xc[:, D_INNER:D_INNER + BC_DIM].astype(BF16)
    cm = xc[:, D_INNER + BC_DIM:CONV_DIM].astype(BF16)

    e3 = e3_ref[...]
    dt = _softplus(dt_ref[...] + dtb_ref[...])
    dta = dt * (-LOG2E * jnp.exp(alog_ref[...]))
    row = lax.broadcasted_iota(jnp.int32, (BLK, BLK), 0)
    col = lax.broadcasted_iota(jnp.int32, (BLK, BLK), 1)
    causal = row >= col
    cum = jnp.dot(causal.astype(F32), dta, preferred_element_type=F32, precision=lax.Precision.HIGHEST)
    cum_t = cum.T
    ecum = jnp.exp2(cum)
    to_end = jnp.exp2(cum[BLK - 1:BLK, :] - cum) * dt
    dt_e = _expand_heads(dt, e3)
    ecum_e = _expand_heads(ecum, e3)
    to_end_e = _expand_heads(to_end, e3)
    xdt = (xs * dt_e).astype(BF16)
    xte = (xs * to_end_e).astype(BF16)
    lane = lax.broadcasted_iota(jnp.int32, (BLK, LANES), 1)
    first_head = lane < SSM_HEAD_DIM

    for g in range(SSM_GROUPS):
        gl = slice(g * GROUP_LANES, (g + 1) * GROUP_LANES)
        bg = bm[:, g * D_STATE:(g + 1) * D_STATE]
        cg = cm[:, g * D_STATE:(g + 1) * D_STATE]
        cbg = lax.dot_general(cg, bg, (((1,), (1,)), ((), ())), preferred_element_type=F32)
        st_g = st[1 - slot, :, gl]
        y_inter = jnp.dot(cg, st_g.astype(BF16), preferred_element_type=F32) * ecum_e[:, gl]
        for pr in range(HEADS_PER_GROUP // 2):
            h0 = g * HEADS_PER_GROUP + 2 * pr
            xd = xdt[:, h0 * SSM_HEAD_DIM:(h0 + 2) * SSM_HEAD_DIM]
            ys = []
            for h in (h0, h0 + 1):
                diff = cum[:, h:h + 1] - cum_t[h:h + 1, :]
                w = (jnp.exp2(jnp.where(causal, diff, -jnp.inf)) * cbg).astype(BF16)
                ys.append(jnp.dot(w, xd, preferred_element_type=F32))
            lo = pr * LANES
            ybuf[:, h0 * SSM_HEAD_DIM:(h0 + 2) * SSM_HEAD_DIM] = (
                jnp.where(first_head, ys[0], ys[1]) + y_inter[:, lo:lo + LANES])
        upd = lax.dot_general(bg, xte[:, gl], (((0,), (0,)), ((), ())), preferred_element_type=F32)
        st[slot, :, gl] = st_g * ecum_e[BLK - 1:BLK, gl] + upd

    y = ybuf[...] + dskip_ref[...] * xs
    yz = y * _silu(z_ref[...])
    ms = jnp.mean(yz * yz, axis=-1, keepdims=True)
    o_ref[...] = yz * lax.rsqrt(ms + EPS) * nw_ref[...]

    @pl.when(i == last)
    def _():
        hfin_ref[...] = st[slot].T


def _ssd_prompt(p_act, lay, batch, seq, cw, cb, dtb, alog, dskip, nw, e3):
    nb = seq // BLK
    m = batch * seq
    zc, xc = lay["z"][0] // D_INNER, lay["xs"][0] // D_INNER
    bc, cc, dc = lay["B"][0] // BC_DIM, lay["C"][0] // BC_DIM, lay["dt"][0] // DT_PAD
    const = lambda b, i: (0, 0)
    return pl.pallas_call(
        _ssd_prompt_kernel,
        grid=(batch, nb),
        in_specs=[pl.BlockSpec((BLK, D_INNER), lambda b, i: (b * nb + i, zc)),
                  pl.BlockSpec((BLK, D_INNER), lambda b, i: (b * nb + i, xc)),
                  pl.BlockSpec((BLK, BC_DIM), lambda b, i: (b * nb + i, bc)),
                  pl.BlockSpec((BLK, BC_DIM), lambda b, i: (b * nb + i, cc)),
                  pl.BlockSpec((BLK, DT_PAD), lambda b, i: (b * nb + i, dc)),
                  pl.BlockSpec((CONV_W, CONV_DIM), const),
                  pl.BlockSpec((1, CONV_DIM), const),
                  pl.BlockSpec((1, DT_PAD), const),
                  pl.BlockSpec((1, DT_PAD), const),
                  pl.BlockSpec((1, D_INNER), const),
                  pl.BlockSpec((1, D_INNER), const),
                  pl.BlockSpec((3 * DT_PAD, D_INNER), const)],
        out_specs=[pl.BlockSpec((BLK, D_INNER), lambda b, i: (b * nb + i, 0)),
                   pl.BlockSpec((None, SUBLANES, CONV_DIM), lambda b, i: (b, 0, 0)),
                   pl.BlockSpec((None, D_INNER, D_STATE), lambda b, i: (b, 0, 0))],
        out_shape=[jax.ShapeDtypeStruct((m, D_INNER), F32),
                   jax.ShapeDtypeStruct((batch, SUBLANES, CONV_DIM), F32),
                   jax.ShapeDtypeStruct((batch, D_INNER, D_STATE), F32)],
        scratch_shapes=[pltpu.VMEM((BLK + SUBLANES, CONV_DIM), F32),
                        pltpu.VMEM((2, SUBLANES, CONV_DIM), F32),
                        pltpu.VMEM((2, D_STATE, D_INNER), F32),
                        pltpu.VMEM((BLK, D_INNER), F32)],
        compiler_params=_cparams(("arbitrary", "arbitrary")),
        name="ssd_prompt",
    )(p_act, p_act, p_act, p_act, p_act, cw, cb, dtb, alog, dskip, nw, e3)


def _decode_pre_kernel(q_ref, k_ref, xs_ref, b_ref, c_ref, dt_ref, cs_ref, cos_ref, sin_ref, qg_ref, kg_ref,
                       bd_ref, cw_ref, cb_ref, dtb_ref, alog_ref, e3_ref,
                       qo_ref, ko_ref, xc_ref, xdt_ref, dec_ref, cso_ref):
    cos, sin, bd = cos_ref[...], sin_ref[...], bd_ref[...]
    qo_ref[...] = _unpair(_head_norm_rope(q_ref[...], qg_ref[...], cos, sin, bd)) * (HEAD_DIM ** -0.5)
    ko_ref[...] = _unpair(_head_norm_rope(k_ref[...], kg_ref[...], cos, sin, bd))

    segs = ((xs_ref, 0, D_INNER), (b_ref, D_INNER, BC_DIM), (c_ref, D_INNER + BC_DIM, BC_DIM))
    for ref, off, width in segs:
        new = ref[...]
        acc = cb_ref[:, off:off + width] + cw_ref[CONV_W - 1:CONV_W, off:off + width] * new
        for t in range(CONV_W - 1):
            lo = t * CONV_DIM + off
            acc = acc + cw_ref[t:t + 1, off:off + width] * cs_ref[:, lo:lo + width]
        xc_ref[:, off:off + width] = _silu(acc)
        for t in range(CONV_W - 2):
            cso_ref[:, t * CONV_DIM + off:t * CONV_DIM + off + width] = (
                cs_ref[:, (t + 1) * CONV_DIM + off:(t + 1) * CONV_DIM + off + width])
        lo = (CONV_W - 2) * CONV_DIM + off
        cso_ref[:, lo:lo + width] = new

    e3 = e3_ref[...]
    dt = _softplus(dt_ref[...] + dtb_ref[...])
    decay = jnp.exp(dt * (-jnp.exp(alog_ref[...])))
    xdt_ref[...] = xc_ref[:, 0:D_INNER] * _expand_heads(dt, e3)
    dec_ref[...] = _expand_heads(decay, e3)


def _decode_pre(p_act, lay, conv_state, cos, sin, qg, kg, bd, cw, cb, dtb, alog, e3):
    m = p_act.shape[0]
    full = lambda shape: pl.BlockSpec(shape, lambda i: (0, 0))

    def col(name):
        c = lay[name][0] // lay[name][1]
        return pl.BlockSpec((m, lay[name][1]), lambda i: (0, c))

    cs_w = (CONV_W - 1) * CONV_DIM
    return pl.pallas_call(
        _decode_pre_kernel,
        grid=(1,),
        in_specs=[col("q"), col("k"), col("xs"), col("B"), col("C"), col("dt"),
                  full((m, cs_w)), full((1, LANES)), full((1, LANES)), full((1, Q_DIM)), full((1, KV_DIM)),
                  full((SLAB, SLAB)), full((CONV_W, CONV_DIM)), full((1, CONV_DIM)), full((1, DT_PAD)),
                  full((1, DT_PAD)), full((3 * DT_PAD, D_INNER))],
        out_specs=[full((m, Q_DIM)), full((m, KV_DIM)), full((m, CONV_DIM)), full((m, D_INNER)),
                   full((m, D_INNER)), full((m, cs_w))],
        out_shape=[jax.ShapeDtypeStruct((m, Q_DIM), F32), jax.ShapeDtypeStruct((m, KV_DIM), F32),
                   jax.ShapeDtypeStruct((m, CONV_DIM), F32), jax.ShapeDtypeStruct((m, D_INNER), F32),
                   jax.ShapeDtypeStruct((m, D_INNER), F32), jax.ShapeDtypeStruct((m, cs_w), F32)],
        compiler_params=_cparams(("arbitrary",)),
        name="decode_pre",
    )(p_act, p_act, p_act, p_act, p_act, p_act, conv_state, cos, sin, qg, kg, bd, cw, cb, dtb, alog, e3)


DEC_ROWS = 8


def _attn_decode_kernel(q_ref, k_ref, v_ref, ck_ref, cv_ref, sink_ref, o_ref):
    w = ck_ref.shape[2]
    nq = N_HEADS
    r = lax.broadcasted_iota(jnp.int32, (nq, SLAB), 0)
    grp = lax.broadcasted_iota(jnp.int32, (nq, SLAB), 1) // HEAD_DIM
    own = grp == (r % KV_HEADS)
    in_window = lax.broadcasted_iota(jnp.int32, (nq, w), 1) > (w - WINDOW)
    sink = sink_ref[...][:, 0:1]
    for bl in range(DEC_ROWS):
        qrow = q_ref[bl:bl + 1, :]
        qm = jnp.zeros((nq, SLAB), F32)
        for j in range(Q_PER_KV):
            slab = jnp.broadcast_to(qrow[:, j * SLAB:(j + 1) * SLAB], (nq, SLAB))
            qm = jnp.where((r // KV_HEADS) == j, slab, qm)
        qm = jnp.where(own, qm, 0.0)
        s = jnp.dot(qm.astype(BF16), ck_ref[bl].astype(BF16), preferred_element_type=F32)
        s = jnp.where(in_window, s, -jnp.inf)
        s_new = jnp.sum(qm * k_ref[bl:bl + 1, :], axis=-1, keepdims=True)
        mx = jnp.maximum(jnp.maximum(jnp.max(s, axis=-1, keepdims=True), s_new), sink)
        e = jnp.exp(s - mx)
        e_new = jnp.exp(s_new - mx)
        den = jnp.sum(e, axis=-1, keepdims=True) + e_new + jnp.exp(sink - mx)
        pv = lax.dot_general(e.astype(BF16), cv_ref[bl].astype(BF16), (((1,), (1,)), ((), ())),
                             preferred_element_type=F32)
        pv = jnp.where(own, (pv + e_new * v_ref[bl:bl + 1, :]) / den, 0.0)
        for j in range(Q_PER_KV):
            o_ref[bl:bl + 1, j * SLAB:(j + 1) * SLAB] = jnp.sum(
                pv[j * KV_HEADS:(j + 1) * KV_HEADS, :], axis=0, keepdims=True)


def _attn_decode(q, k, p_act, lay, cache_k, cache_v, sink_rows):
    m = q.shape[0]
    w = cache_k.shape[2]
    vc = lay["v"][0] // KV_DIM
    return pl.pallas_call(
        _attn_decode_kernel,
        grid=(m // DEC_ROWS,),
        in_specs=[pl.BlockSpec((DEC_ROWS, Q_DIM), lambda i: (i, 0)),
                  pl.BlockSpec((DEC_ROWS, KV_DIM), lambda i: (i, 0)),
                  pl.BlockSpec((DEC_ROWS, KV_DIM), lambda i: (i, vc)),
                  pl.BlockSpec((DEC_ROWS, KV_DIM, w), lambda i: (i, 0, 0)),
                  pl.BlockSpec((DEC_ROWS, KV_DIM, w), lambda i: (i, 0, 0)),
                  pl.BlockSpec((N_HEADS, LANES), lambda i: (0, 0))],
        out_specs=pl.BlockSpec((DEC_ROWS, Q_DIM), lambda i: (i, 0)),
        out_shape=jax.ShapeDtypeStruct((m, Q_DIM), F32),
        compiler_params=_cparams(("arbitrary",)),
        name="attn_decode",
    )(q, k, p_act, cache_k, cache_v, sink_rows)


MM_ROWS = 16


def _ssm_decode_kernel(st_ref, xdt_ref, dec_ref, xc_ref, z_ref, dskip_ref, nw_ref, sto_ref, o_ref):
    r = lax.broadcasted_iota(jnp.int32, (MM_ROWS, D_INNER), 0)
    grp = lax.broadcasted_iota(jnp.int32, (MM_ROWS, D_INNER), 1) // GROUP_LANES
    rr = lax.broadcasted_iota(jnp.int32, (MM_ROWS, D_STATE), 0)
    ones_rows = jnp.where((rr >= SSM_GROUPS) & (rr < SSM_GROUPS + 3), 1.0, 0.0)
    bc = lambda a: jnp.broadcast_to(a, (MM_ROWS, D_INNER))
    for u in range(st_ref.shape[0]):
        h = st_ref[u]
        xdt = xdt_ref[u]
        xc = xc_ref[u]
        hi, mid, lo = _split3(dec_ref[u])
        lhs_t = jnp.where(r == grp, bc(xdt), 0.0)
        for t, piece in enumerate((hi, mid, lo)):
            lhs_t = jnp.where(r == SSM_GROUPS + t, bc(piece.astype(F32)), lhs_t)
        lhs_t = lhs_t.astype(BF16)
        b_rows = jnp.zeros((MM_ROWS, D_STATE), F32)
        c_rows = jnp.zeros((MM_ROWS, D_STATE), F32)
        for g in range(SSM_GROUPS):
            b_g = xc[:, D_INNER + g * D_STATE:D_INNER + (g + 1) * D_STATE]
            c_g = xc[:, D_INNER + BC_DIM + g * D_STATE:D_INNER + BC_DIM + (g + 1) * D_STATE]
            b_rows = jnp.where(rr == g, jnp.broadcast_to(b_g, (MM_ROWS, D_STATE)), b_rows)
            c_rows = jnp.where(rr == g, jnp.broadcast_to(c_g, (MM_ROWS, D_STATE)), c_rows)
        rhs = jnp.concatenate([b_rows, ones_rows], axis=1).astype(BF16)
        both = lax.dot_general(lhs_t, rhs, (((0,), (0,)), ((), ())), preferred_element_type=F32)
        h_new = both[:, D_STATE:] * h + both[:, :D_STATE]
        sto_ref[u] = h_new
        yg = lax.dot_general(c_rows.astype(BF16), h_new.astype(BF16), (((1,), (1,)), ((), ())),
                             preferred_element_type=F32)
        y = jnp.sum(jnp.where(r == grp, yg, 0.0), axis=0, keepdims=True)
        y = y + dskip_ref[...] * xc[:, 0:D_INNER]
        yz = y * _silu(z_ref[u])
        ms = jnp.mean(yz * yz, axis=-1, keepdims=True)
        o_ref[u] = yz * lax.rsqrt(ms + EPS) * nw_ref[...]


SSM_DEC_ROWS = 2


def _ssm_decode(state, xdt, dec, xc, p_act, lay, dskip, nw):
    m = state.shape[0]
    n = p_act.shape[1]
    u = SSM_DEC_ROWS
    assert m % u == 0
    zc = lay["z"][0] // D_INNER
    row = lambda width: pl.BlockSpec((u, 1, width), lambda b: (b, 0, 0))
    return pl.pallas_call(
        _ssm_decode_kernel,
        grid=(m // u,),
        in_specs=[pl.BlockSpec((u, D_INNER, D_STATE), lambda b: (b, 0, 0)),
                  row(D_INNER), row(D_INNER), row(CONV_DIM),
                  pl.BlockSpec((u, 1, D_INNER), lambda b: (b, 0, zc)),
                  pl.BlockSpec((1, D_INNER), lambda b: (0, 0)),
                  pl.BlockSpec((1, D_INNER), lambda b: (0, 0))],
        out_specs=[pl.BlockSpec((u, D_INNER, D_STATE), lambda b: (b, 0, 0)), row(D_INNER)],
        out_shape=[jax.ShapeDtypeStruct((m, D_INNER, D_STATE), F32),
                   jax.ShapeDtypeStruct((m, 1, D_INNER), F32)],
        compiler_params=_cparams(("arbitrary",)),
        name="ssm_decode",
    )(state, xdt.reshape(m, 1, D_INNER), dec.reshape(m, 1, D_INNER), xc.reshape(m, 1, CONV_DIM),
      p_act.reshape(m, 1, n), dskip, nw)


def _merge_kernel(x_ref, oa_ref, os_ref, ga_ref, gs_ref, wa_ref, ws_ref, wo_ref, o_ref):
    a = jnp.dot(oa_ref[...].astype(BF16), wa_ref[...], preferred_element_type=F32)
    s = jnp.dot(os_ref[...].astype(BF16), ws_ref[...], preferred_element_type=F32)
    mixed = _sigmoid(ga_ref[...]) * a + _sigmoid(gs_ref[...]) * s
    o_ref[...] = x_ref[...] + jnp.dot(mixed.astype(BF16), wo_ref[...], preferred_element_type=F32)


def _merge(x, o_attn, o_ssm, p_act, lay, wa, ws, wo):
    m, d = x.shape
    tm = _row_tile(m, 256)
    gac, gsc = lay["ga"][0] // d, lay["gs"][0] // d
    const = lambda i: (0, 0)
    return pl.pallas_call(
        _merge_kernel,
        grid=(m // tm,),
        in_specs=[pl.BlockSpec((tm, d), lambda i: (i, 0)),
                  pl.BlockSpec((tm, Q_DIM), lambda i: (i, 0)),
                  pl.BlockSpec((tm, D_INNER), lambda i: (i, 0)),
                  pl.BlockSpec((tm, d), lambda i: (i, gac)),
                  pl.BlockSpec((tm, d), lambda i: (i, gsc)),
                  pl.BlockSpec((Q_DIM, d), const),
                  pl.BlockSpec((D_INNER, d), const),
                  pl.BlockSpec((d, d), const)],
        out_specs=pl.BlockSpec((tm, d), lambda i: (i, 0)),
        out_shape=jax.ShapeDtypeStruct((m, d), F32),
        compiler_params=_cparams(("arbitrary",)),
        name="merge",
    )(x, o_attn, o_ssm, p_act, p_act, wa, ws, wo)


def _mlp_kernel(x_ref, g_ref, wu_ref, wd_ref, o_ref, *, ff_tile):
    x = x_ref[...]
    ms = jnp.mean(x * x, axis=-1, keepdims=True)
    xn = (x * lax.rsqrt(ms + EPS) * g_ref[...]).astype(BF16)
    acc = x
    for c in range(wu_ref.shape[1] // ff_tile):
        u = jnp.dot(xn, wu_ref[:, c * ff_tile:(c + 1) * ff_tile], preferred_element_type=F32)
        a = jnp.square(jnp.maximum(u, 0.0)).astype(BF16)
        acc = acc + jnp.dot(a, wd_ref[c * ff_tile:(c + 1) * ff_tile, :], preferred_element_type=F32)
    o_ref[...] = acc


def _mlp(x, g, wu, wd):
    m, d = x.shape
    ff = wu.shape[1]
    tm = _row_tile(m, 256)
    const = lambda i: (0, 0)
    return pl.pallas_call(
        functools.partial(_mlp_kernel, ff_tile=min(1024, ff)),
        grid=(m // tm,),
        in_specs=[pl.BlockSpec((tm, d), lambda i: (i, 0)),
                  pl.BlockSpec((1, d), const),
                  pl.BlockSpec((d, ff), const),
                  pl.BlockSpec((ff, d), const)],
        out_specs=pl.BlockSpec((tm, d), lambda i: (i, 0)),
        out_shape=jax.ShapeDtypeStruct((m, d), F32),
        compiler_params=_cparams(("arbitrary",)),
        name="mlp",
    )(x, g, wu, wd)


def _rope_tables(pos):
    half = HEAD_DIM // 2
    inv = ROPE_THETA ** (-jnp.arange(half, dtype=F32) / half)
    ang = pos.astype(F32)[:, None] * inv[None, :]
    cos, sin = jnp.cos(ang), jnp.sin(ang)
    return (jnp.concatenate([cos, cos, cos, cos], axis=1),
            jnp.concatenate([-sin, -sin, sin, sin], axis=1))


def _constants():
    heads = np.arange(DT_PAD)[:, None]
    cols = np.arange(D_INNER)[None, :] // SSM_HEAD_DIM
    e = (heads == cols).astype(np.float32)
    e3 = jnp.asarray(np.concatenate([e, e, e], axis=0), BF16)
    half = HEAD_DIM // 2
    a = np.arange(SLAB)
    grp = 2 * (a // LANES) + (a % HEAD_DIM) // half
    bd = jnp.asarray((grp[:, None] == grp[None, :]).astype(np.float32) / HEAD_DIM, BF16)
    perm = np.concatenate([np.arange(HEAD_DIM) + (p * Q_PER_KV + j) * HEAD_DIM
                           for j in range(Q_PER_KV) for p in range(KV_HEADS)])
    head_perm = np.array([p * Q_PER_KV + j for j in range(Q_PER_KV) for p in range(KV_HEADS)])
    t, hi, which, d = (a // LANES), (a % LANES) // HEAD_DIM, (a % HEAD_DIM) // half, a % half
    pair_k = (2 * t + which) * HEAD_DIM + hi * half + d
    pair_q = np.concatenate([perm[j * SLAB:(j + 1) * SLAB][pair_k] for j in range(Q_PER_KV)])
    return e3, bd, perm, head_perm, pair_q, pair_k


def _prep_layer(d_model, norm_mix, w_in, q_norm, k_norm, attn_sinks, conv_w, conv_b, dt_bias, a_log, d_skip,
                ssm_norm, w_attn_o, w_ssm_o, w_out, norm_mlp, w_up, w_down):
    e3, bd, perm, head_perm, pair_q, pair_k = _constants()
    o_q, o_k, o_v, o_z, o_xbc, o_dt = (0, Q_DIM, Q_DIM + KV_DIM, Q_DIM + 2 * KV_DIM,
                                       Q_DIM + 2 * KV_DIM + D_INNER, Q_DIM + 2 * KV_DIM + D_INNER + CONV_DIM)
    o_g = o_dt + SSM_HEADS
    cut = lambda lo, width: w_in[:, lo:lo + width]
    pieces = {"z": cut(o_z, D_INNER), "xs": cut(o_xbc, D_INNER), "ga": cut(o_g, d_model),
              "gs": cut(o_g + d_model, d_model), "q": cut(o_q, Q_DIM)[:, pair_q],
              "B": cut(o_xbc + D_INNER, BC_DIM), "C": cut(o_xbc + D_INNER + BC_DIM, BC_DIM),
              "k": cut(o_k, KV_DIM)[:, pair_k], "v": cut(o_v, KV_DIM),
              "dt": jnp.pad(cut(o_dt, SSM_HEADS), ((0, 0), (0, DT_PAD - SSM_HEADS)))}
    lay, _ = _layout(d_model)
    w_p = jnp.concatenate([pieces[name] for name in lay], axis=1).astype(BF16)
    pad_heads = lambda a: jnp.pad(a, (0, DT_PAD - SSM_HEADS))[None, :]
    sink_p = attn_sinks[head_perm]
    return dict(
        lay=lay, e3=e3, bd=bd, w_p=w_p,
        norm_mix=norm_mix[None, :], norm_mlp=norm_mlp[None, :],
        qg=q_norm[pair_q % HEAD_DIM][None, :], kg=k_norm[pair_k % HEAD_DIM][None, :],
        sink_p=sink_p, sink_rows=jnp.broadcast_to(sink_p[:, None], (N_HEADS, LANES)),
        cw=conv_w, cb=conv_b[None, :], dtb=pad_heads(dt_bias), alog=pad_heads(a_log),
        dskip=jnp.repeat(d_skip, SSM_HEAD_DIM)[None, :], nw=ssm_norm[None, :],
        wa=w_attn_o[perm, :].astype(BF16), ws=w_ssm_o.astype(BF16), wo=w_out.astype(BF16),
        wu=w_up.astype(BF16), wd=w_down.astype(BF16))


def _prompt_layer(x, lw):
    batch, seq, d = x.shape
    assert seq % BLK == 0 and seq >= WINDOW
    x2 = x.reshape(batch * seq, d)
    p_act = _in_proj(x2, lw["norm_mix"], lw["w_p"])
    cos, sin = _rope_tables(jnp.arange(seq))
    o_attn, k_last, v_last = _attn_prompt(p_act, lw["lay"], batch, seq, cos, sin, lw["qg"], lw["kg"],
                                          lw["bd"], lw["sink_p"])
    o_ssm, tail, h_fin = _ssd_prompt(p_act, lw["lay"], batch, seq, lw["cw"], lw["cb"], lw["dtb"], lw["alog"],
                                     lw["dskip"], lw["nw"], lw["e3"])
    x1 = _merge(x2, o_attn.reshape(batch * seq, Q_DIM), o_ssm, p_act, lw["lay"], lw["wa"], lw["ws"], lw["wo"])
    y = _mlp(x1, lw["norm_mlp"], lw["wu"], lw["wd"])
    return (y.reshape(batch, seq, d),
            k_last.reshape(batch, BLK, KV_HEADS, HEAD_DIM), v_last.reshape(batch, BLK, KV_HEADS, HEAD_DIM),
            tail[:, SUBLANES - (CONV_W - 1):, :],
            h_fin.reshape(batch, SSM_HEADS, SSM_HEAD_DIM, D_STATE))


def _decode_layer(x, cache_k, cache_v, conv_state, ssm_state, lw):
    m, t, d = x.shape
    w = cache_k.shape[1]
    assert t == 1 and w == WINDOW and m % DEC_ROWS == 0
    x2 = x.reshape(m, d)
    p_act = _in_proj(x2, lw["norm_mix"], lw["w_p"])
    cos, sin = _rope_tables(PAST_LEN + jnp.arange(1))
    q, k, xc, xdt, dec, conv_new = _decode_pre(
        p_act, lw["lay"], conv_state.reshape(m, (CONV_W - 1) * CONV_DIM), cos, sin, lw["qg"], lw["kg"],
        lw["bd"], lw["cw"], lw["cb"], lw["dtb"], lw["alog"], lw["e3"])
    ck = jnp.transpose(cache_k, (0, 2, 3, 1))
    cv = jnp.transpose(cache_v, (0, 2, 3, 1))
    o_attn = _attn_decode(q, k, p_act, lw["lay"], ck.reshape(m, KV_DIM, w), cv.reshape(m, KV_DIM, w),
                          lw["sink_rows"])
    h_new, o_ssm = _ssm_decode(ssm_state.reshape(m, D_INNER, D_STATE), xdt, dec, xc, p_act, lw["lay"],
                               lw["dskip"], lw["nw"])
    x1 = _merge(x2, o_attn, o_ssm.reshape(m, D_INNER), p_act, lw["lay"], lw["wa"], lw["ws"], lw["wo"])
    y = _mlp(x1, lw["norm_mlp"], lw["wu"], lw["wd"])
    v_off = lw["lay"]["v"][0]
    v_new = p_act[:, v_off:v_off + KV_DIM]
    shift = lambda c, new: jnp.transpose(
        jnp.concatenate([c[..., 1:], new.reshape(m, KV_HEADS, HEAD_DIM, 1)], axis=-1), (0, 3, 1, 2))
    new_k, new_v = shift(ck, k), shift(cv, v_new)
    return (y.reshape(m, 1, d), new_k, new_v, conv_new.reshape(m, CONV_W - 1, CONV_DIM),
            h_new.reshape(m, SSM_HEADS, SSM_HEAD_DIM, D_STATE))


def kernel(x_prompt, x_sample, cache_k, cache_v, state_conv, state_ssm, norm_mix, w_in, q_norm, k_norm,
           attn_sinks, conv_w, conv_b, dt_bias, a_log, d_skip, ssm_norm, w_attn_o, w_ssm_o, w_out,
           norm_mlp, w_up, w_down):
    depth = w_in.shape[0]
    d_model = x_prompt.shape[-1]
    yp, ys = x_prompt, x_sample
    cols = [[] for _ in range(8)]
    for l in range(depth):
        lw = _prep_layer(d_model, norm_mix[l], w_in[l], q_norm[l], k_norm[l], attn_sinks[l], conv_w[l],
                         conv_b[l], dt_bias[l], a_log[l], d_skip[l], ssm_norm[l], w_attn_o[l], w_ssm_o[l],
                         w_out[l], norm_mlp[l], w_up[l], w_down[l])
        yp, kp, vp, cp, hp = _prompt_layer(yp, lw)
        ys, ks, vs, cs, hs = _decode_layer(ys, cache_k[l], cache_v[l], state_conv[l], state_ssm[l], lw)
        for lst, val in zip(cols, (kp, vp, cp, hp, ks, vs, cs, hs)):
            lst.append(val)
    return (yp, ys) + tuple(jnp.stack(c) for c in cols)
```

```python
import functools

import numpy as np
import jax
import jax.numpy as jnp
from jax import lax
from jax.experimental import pallas as pl
from jax.experimental.pallas import tpu as pltpu

F32 = jnp.float32
BF16 = jnp.bfloat16

N_HEADS = 16
KV_HEADS = 4
HEAD_DIM = 64
Q_PER_KV = N_HEADS // KV_HEADS
WINDOW = 128
ROPE_THETA = 10000.0
Q_DIM = N_HEADS * HEAD_DIM
KV_DIM = KV_HEADS * HEAD_DIM
SSM_HEAD_DIM = 64
SSM_HEADS = 32
D_INNER = SSM_HEADS * SSM_HEAD_DIM
SSM_GROUPS = 4
HEADS_PER_GROUP = SSM_HEADS // SSM_GROUPS
D_STATE = 128
BC_DIM = SSM_GROUPS * D_STATE
CONV_W = 4
CONV_DIM = D_INNER + 2 * BC_DIM
SSD_CHUNK = 128
EPS = 1e-6
PAST_LEN = 8192
LOG2E = 1.4426950408889634

LANES = 128
SUBLANES = 8
VMEM_LIMIT = 48 * 1024 * 1024

BLK = 128
assert WINDOW == BLK and SSD_CHUNK == BLK
SLAB = KV_HEADS * HEAD_DIM
GROUP_LANES = HEADS_PER_GROUP * SSM_HEAD_DIM
DT_PAD = LANES


def _layout(d_model):
    segs = [("z", D_INNER), ("xs", D_INNER), ("ga", d_model), ("gs", d_model), ("q", Q_DIM),
            ("B", BC_DIM), ("C", BC_DIM), ("k", KV_DIM), ("v", KV_DIM), ("dt", DT_PAD)]
    lay, off = {}, 0
    for name, width in segs:
        assert off % width == 0, (name, off, width)
        lay[name] = (off, width)
        off += width
    return lay, off


def _col_tile(n, cap=3072):
    units = n // LANES
    best = 1
    for d in range(1, units + 1):
        if units % d == 0 and d * LANES <= cap:
            best = d
    return best * LANES


def _row_tile(m, cap):
    assert m % SUBLANES == 0
    best = SUBLANES
    for t in range(SUBLANES, min(m, cap) + 1, SUBLANES):
        if m % t == 0:
            best = t
    return best


def _cparams(sem):
    return pltpu.CompilerParams(dimension_semantics=sem, vmem_limit_bytes=VMEM_LIMIT)


def _sigmoid(x):
    return 0.5 + 0.5 * jnp.tanh(0.5 * x)


def _silu(x):
    h = 0.5 * x
    return h + h * jnp.tanh(h)


def _softplus(x):
    return jnp.maximum(x, 0.0) + jnp.log1p(jnp.exp(-jnp.abs(x)))


def _split3(a):
    hi = a.astype(BF16)
    r1 = a - hi.astype(F32)
    mid = r1.astype(BF16)
    lo = (r1 - mid.astype(F32)).astype(BF16)
    return hi, mid, lo


def _expand_heads(a, e3):
    hi, mid, lo = _split3(a)
    return jnp.dot(jnp.concatenate([hi, mid, lo], axis=1), e3, preferred_element_type=F32)


def _head_ms(x, bd):
    sq = x * x
    hi = sq.astype(BF16)
    lo = (sq - hi.astype(F32)).astype(BF16)
    outs = []
    for s in range(x.shape[1] // SLAB):
        sl = slice(s * SLAB, (s + 1) * SLAB)
        outs.append(jnp.dot(hi[:, sl], bd, preferred_element_type=F32)
                    + jnp.dot(lo[:, sl], bd, preferred_element_type=F32))
    return outs[0] if len(outs) == 1 else jnp.concatenate(outs, axis=1)


def _head_norm_rope(x, g, cos, sin, bd):
    xn = x * lax.rsqrt(_head_ms(x, bd) + EPS) * g
    tiles = []
    for t in range(x.shape[1] // LANES):
        xt = xn[:, t * LANES:(t + 1) * LANES]
        tiles.append(xt * cos + pltpu.roll(xt, LANES // 2, 1) * sin)
    return tiles[0] if len(tiles) == 1 else jnp.concatenate(tiles, axis=1)


def _unpair(x):
    q = HEAD_DIM // 2
    lane = lax.broadcasted_iota(jnp.int32, (x.shape[0], LANES), 1)
    tiles = []
    for t in range(x.shape[1] // LANES):
        xt = x[:, t * LANES:(t + 1) * LANES]
        nat = jnp.where((lane >= q) & (lane < 2 * q), pltpu.roll(xt, LANES - q, 1), xt)
        tiles.append(jnp.where((lane >= 2 * q) & (lane < 3 * q), pltpu.roll(xt, q, 1), nat))
    return tiles[0] if len(tiles) == 1 else jnp.concatenate(tiles, axis=1)


def _in_proj_kernel(x_ref, g_ref, w_ref, o_ref):
    x = x_ref[...]
    ms = jnp.mean(x * x, axis=-1, keepdims=True)
    xn = (x * lax.rsqrt(ms + EPS) * g_ref[...]).astype(BF16)
    o_ref[...] = jnp.dot(xn, w_ref[...], preferred_element_type=F32)


def _in_proj(x, g, w):
    m, k = x.shape
    n = w.shape[1]
    tm = _row_tile(m, 512)
    tn = _col_tile(n)
    return pl.pallas_call(
        _in_proj_kernel,
        grid=(n // tn, m // tm),
        in_specs=[pl.BlockSpec((tm, k), lambda j, i: (i, 0)),
                  pl.BlockSpec((1, k), lambda j, i: (0, 0)),
                  pl.BlockSpec((k, tn), lambda j, i: (0, j))],
        out_specs=pl.BlockSpec((tm, tn), lambda j, i: (i, j)),
        out_shape=jax.ShapeDtypeStruct((m, n), F32),
        compiler_params=_cparams(("arbitrary", "arbitrary")),
        name="in_proj",
    )(x, g, w)


def _attn_prompt_kernel(q_ref, k_ref, v_ref, cos_ref, sin_ref, qg_ref, kg_ref, bd_ref, sink_ref,
                        o_ref, ko_ref, vo_ref, kbuf, vbuf, probs):
    i = pl.program_id(1)
    last = pl.num_programs(1) - 1

    @pl.when(i == 0)
    def _():
        kbuf[...] = jnp.zeros_like(kbuf)
        vbuf[...] = jnp.zeros_like(vbuf)

    cos, sin, bd = cos_ref[...], sin_ref[...], bd_ref[...]
    slot = i % 2
    lane = lax.broadcasted_iota(jnp.int32, (BLK, SLAB), 1)
    grp_k = 2 * (lane // LANES) + (lane % HEAD_DIM) // (HEAD_DIM // 2)
    grp_v = lane // HEAD_DIM
    r = lax.broadcasted_iota(jnp.int32, (BLK, 2 * BLK), 0)
    c = lax.broadcasted_iota(jnp.int32, (BLK, 2 * BLK), 1)
    key = c % BLK
    is_cur = (c // BLK) == slot
    prev_ok = jnp.where(i > 0, key, -1)
    mask = jnp.where(is_cur, (key <= r).astype(jnp.int32), (prev_ok > r + (BLK - WINDOW)).astype(jnp.int32)) > 0

    for u in range(q_ref.shape[0]):
        q = _head_norm_rope(q_ref[u], qg_ref[...], cos, sin, bd) * (HEAD_DIM ** -0.5 * LOG2E)
        k = _head_norm_rope(k_ref[u], kg_ref[...], cos, sin, bd)
        v = v_ref[u]

        @pl.when(i == last)
        def _():
            ko_ref[u] = _unpair(k).T
            vo_ref[u] = v.T

        for p in range(KV_HEADS):
            rows = pl.ds(pl.multiple_of(p * 2 * BLK + slot * BLK, BLK), BLK)
            kbuf[u, rows, :] = jnp.where(grp_k == p, k, 0.0).astype(BF16)
            vbuf[u, rows, :] = jnp.where(grp_v == p, v, 0.0).astype(BF16)

        q_stack = jnp.concatenate([q[:, j * SLAB:(j + 1) * SLAB] for j in range(Q_PER_KV)],
                                  axis=0).astype(BF16)
        s_all = lax.dot_general(q_stack, kbuf[u], (((1,), (1,)), ((), ())), preferred_element_type=F32)
        for j in range(Q_PER_KV):
            for p in range(KV_HEADS):
                s = jnp.where(mask, s_all[j * BLK:(j + 1) * BLK, p * 2 * BLK:(p + 1) * 2 * BLK], -jnp.inf)
                sink = sink_ref[j * KV_HEADS + p] * LOG2E
                mx = jnp.maximum(jnp.max(s, axis=-1, keepdims=True), sink)
                e = jnp.exp2(s - mx)
                den = jnp.sum(e, axis=-1, keepdims=True) + jnp.exp2(sink - mx)
                probs[u, j * BLK:(j + 1) * BLK, p * 2 * BLK:(p + 1) * 2 * BLK] = (e / den).astype(BF16)
        pv = jnp.dot(probs[u], vbuf[u], preferred_element_type=F32)
        for j in range(Q_PER_KV):
            o_ref[u, :, j * SLAB:(j + 1) * SLAB] = pv[j * BLK:(j + 1) * BLK, :]


ATTN_SEQS = 2


def _attn_prompt(p_act, lay, batch, seq, cos, sin, qg, kg, bd, sinks):
    nb = seq // BLK
    u = ATTN_SEQS if batch % ATTN_SEQS == 0 else 1
    qc, kc, vc = lay["q"][0] // Q_DIM, lay["k"][0] // KV_DIM, lay["v"][0] // KV_DIM
    p3 = p_act.reshape(batch, seq, p_act.shape[1])
    return pl.pallas_call(
        _attn_prompt_kernel,
        grid=(batch // u, nb),
        in_specs=[pl.BlockSpec((u, BLK, Q_DIM), lambda b, i: (b, i, qc)),
                  pl.BlockSpec((u, BLK, KV_DIM), lambda b, i: (b, i, kc)),
                  pl.BlockSpec((u, BLK, KV_DIM), lambda b, i: (b, i, vc)),
                  pl.BlockSpec((BLK, LANES), lambda b, i: (i, 0)),
                  pl.BlockSpec((BLK, LANES), lambda b, i: (i, 0)),
                  pl.BlockSpec((1, Q_DIM), lambda b, i: (0, 0)),
                  pl.BlockSpec((1, KV_DIM), lambda b, i: (0, 0)),
                  pl.BlockSpec((SLAB, SLAB), lambda b, i: (0, 0)),
                  pl.BlockSpec(memory_space=pltpu.SMEM)],
        out_specs=[pl.BlockSpec((u, BLK, Q_DIM), lambda b, i: (b, i, 0)),
                   pl.BlockSpec((u, KV_DIM, BLK), lambda b, i: (b, 0, 0)),
                   pl.BlockSpec((u, KV_DIM, BLK), lambda b, i: (b, 0, 0))],
        out_shape=[jax.ShapeDtypeStruct((batch, seq, Q_DIM), F32),
                   jax.ShapeDtypeStruct((batch, KV_DIM, BLK), F32),
                   jax.ShapeDtypeStruct((batch, KV_DIM, BLK), F32)],
        scratch_shapes=[pltpu.VMEM((u, KV_HEADS * 2 * BLK, KV_DIM), BF16),
                        pltpu.VMEM((u, KV_HEADS * 2 * BLK, KV_DIM), BF16),
                        pltpu.VMEM((u, Q_PER_KV * BLK, KV_HEADS * 2 * BLK), BF16)],
        compiler_params=_cparams(("arbitrary", "arbitrary")),
        name="attn_prompt",
    )(p3, p3, p3, cos, sin, qg, kg, bd, sinks)


def _ssd_prompt_kernel(z_ref, xs_ref, b_ref, c_ref, dt_ref, cw_ref, cb_ref, dtb_ref, alog_ref, dskip_ref,
                       nw_ref, e3_ref, o_ref, tail_ref, hfin_ref, xpad, tails, st, ybuf):
    i = pl.program_id(1)
    last = pl.num_programs(1) - 1

    @pl.when(i == 0)
    def _():
        tails[...] = jnp.zeros_like(tails)
        st[...] = jnp.zeros_like(st)

    slot = i % 2
    xpad[0:SUBLANES, :] = tails[1 - slot]
    xpad[SUBLANES:SUBLANES + BLK, 0:D_INNER] = xs_ref[...]
    xpad[SUBLANES:SUBLANES + BLK, D_INNER:D_INNER + BC_DIM] = b_ref[...]
    xpad[SUBLANES:SUBLANES + BLK, D_INNER + BC_DIM:CONV_DIM] = c_ref[...]
    cwh = 0.5 * cw_ref[...]
    acc = 0.5 * cb_ref[...] + cwh[CONV_W - 1:CONV_W, :] * xpad[SUBLANES:SUBLANES + BLK, :]
    for t in range(1, CONV_W):
        acc = acc + cwh[CONV_W - 1 - t:CONV_W - t, :] * xpad[SUBLANES - t:SUBLANES - t + BLK, :]
    xc = acc + acc * jnp.tanh(acc)
    new_tail = xpad[BLK:BLK + SUBLANES, :]
    tail_ref[...] = new_tail
    tails[slot] = new_tail

    xs = xc[:, 0:D_INNER]
    bm = xc[:, D_INNER:D_INNER + BC_DIM].astype(BF16)
    cm = xc[:, D_INNER + BC_DIM:CONV_DIM].astype(BF16)

    e3 = e3_ref[...]
    dt = _softplus(dt_ref[...] + dtb_ref[...])
    dta = dt * (-LOG2E * jnp.exp(alog_ref[...]))
    row = lax.broadcasted_iota(jnp.int32, (BLK, BLK), 0)
    col = lax.broadcasted_iota(jnp.int32, (BLK, BLK), 1)
    causal = row >= col
    cum = jnp.dot(causal.astype(F32), dta, preferred_element_type=F32, precision=lax.Precision.HIGHEST)
    cum_t = cum.T
    ecum = jnp.exp2(cum)
    to_end = jnp.exp2(cum[BLK - 1:BLK, :] - cum) * dt
    dt_e = _expand_heads(dt, e3)
    ecum_e = _expand_heads(ecum, e3)
    to_end_e = _expand_heads(to_end, e3)
    xdt = (xs * dt_e).astype(BF16)
    xte = (xs * to_end_e).astype(BF16)
    lane = lax.broadcasted_iota(jnp.int32, (BLK, LANES), 1)
    first_head = lane < SSM_HEAD_DIM

    for g in range(SSM_GROUPS):
        gl = slice(g * GROUP_LANES, (g + 1) * GROUP_LANES)
        bg = bm[:, g * D_STATE:(g + 1) * D_STATE]
        cg = cm[:, g * D_STATE:(g + 1) * D_STATE]
        cbg = lax.dot_general(cg, bg, (((1,), (1,)), ((), ())), preferred_element_type=F32)
        st_g = st[1 - slot, :, gl]
        y_inter = jnp.dot(cg, st_g.astype(BF16), preferred_element_type=F32) * ecum_e[:, gl]
        for pr in range(HEADS_PER_GROUP // 2):
            h0 = g * HEADS_PER_GROUP + 2 * pr
            xd = xdt[:, h0 * SSM_HEAD_DIM:(h0 + 2) * SSM_HEAD_DIM]
            ys = []
            for h in (h0, h0 + 1):
                diff = cum[:, h:h + 1] - cum_t[h:h + 1, :]
                w = (jnp.exp2(jnp.where(causal, diff, -jnp.inf)) * cbg).astype(BF16)
                ys.append(jnp.dot(w, xd, preferred_element_type=F32))
            lo = pr * LANES
            ybuf[:, h0 * SSM_HEAD_DIM:(h0 + 2) * SSM_HEAD_DIM] = (
                jnp.where(first_head, ys[0], ys[1]) + y_inter[:, lo:lo + LANES])
        upd = lax.dot_general(bg, xte[:, gl], (((0,), (0,)), ((), ())), preferred_element_type=F32)
        st[slot, :, gl] = st_g * ecum_e[BLK - 1:BLK, gl] + upd

    y = ybuf[...] + dskip_ref[...] * xs
    yz = y * _silu(z_ref[...])
    ms = jnp.mean(yz * yz, axis=-1, keepdims=True)
    o_ref[...] = yz * lax.rsqrt(ms + EPS) * nw_ref[...]

    @pl.when(i == last)
    def _():
        hfin_ref[...] = st[slot].T


def _ssd_prompt(p_act, lay, batch, seq, cw, cb, dtb, alog, dskip, nw, e3):
    nb = seq // BLK
    m = batch * seq
    zc, xc = lay["z"][0] // D_INNER, lay["xs"][0] // D_INNER
    bc, cc, dc = lay["B"][0] // BC_DIM, lay["C"][0] // BC_DIM, lay["dt"][0] // DT_PAD
    const = lambda b, i: (0, 0)
    return pl.pallas_call(
        _ssd_prompt_kernel,
        grid=(batch, nb),
        in_specs=[pl.BlockSpec((BLK, D_INNER), lambda b, i: (b * nb + i, zc)),
                  pl.BlockSpec((BLK, D_INNER), lambda b, i: (b * nb + i, xc)),
                  pl.BlockSpec((BLK, BC_DIM), lambda b, i: (b * nb + i, bc)),
                  pl.BlockSpec((BLK, BC_DIM), lambda b, i: (b * nb + i, cc)),
                  pl.BlockSpec((BLK, DT_PAD), lambda b, i: (b * nb + i, dc)),
                  pl.BlockSpec((CONV_W, CONV_DIM), const),
                  pl.BlockSpec((1, CONV_DIM), const),
                  pl.BlockSpec((1, DT_PAD), const),
                  pl.BlockSpec((1, DT_PAD), const),
                  pl.BlockSpec((1, D_INNER), const),
                  pl.BlockSpec((1, D_INNER), const),
                  pl.BlockSpec((3 * DT_PAD, D_INNER), const)],
        out_specs=[pl.BlockSpec((BLK, D_INNER), lambda b, i: (b * nb + i, 0)),
                   pl.BlockSpec((None, SUBLANES, CONV_DIM), lambda b, i: (b, 0, 0)),
                   pl.BlockSpec((None, D_INNER, D_STATE), lambda b, i: (b, 0, 0))],
        out_shape=[jax.ShapeDtypeStruct((m, D_INNER), F32),
                   jax.ShapeDtypeStruct((batch, SUBLANES, CONV_DIM), F32),
                   jax.ShapeDtypeStruct((batch, D_INNER, D_STATE), F32)],
        scratch_shapes=[pltpu.VMEM((BLK + SUBLANES, CONV_DIM), F32),
                        pltpu.VMEM((2, SUBLANES, CONV_DIM), F32),
                        pltpu.VMEM((2, D_STATE, D_INNER), F32),
                        pltpu.VMEM((BLK, D_INNER), F32)],
        compiler_params=_cparams(("arbitrary", "arbitrary")),
        name="ssd_prompt",
    )(p_act, p_act, p_act, p_act, p_act, cw, cb, dtb, alog, dskip, nw, e3)


def _decode_pre_kernel(q_ref, k_ref, xs_ref, b_ref, c_ref, dt_ref, cs_ref, cos_ref, sin_ref, qg_ref, kg_ref,
                       bd_ref, cw_ref, cb_ref, dtb_ref, alog_ref, e3_ref,
                       qo_ref, ko_ref, xc_ref, xdt_ref, dec_ref, cso_ref):
    cos, sin, bd = cos_ref[...], sin_ref[...], bd_ref[...]
    qo_ref[...] = _unpair(_head_norm_rope(q_ref[...], qg_ref[...], cos, sin, bd)) * (HEAD_DIM ** -0.5)
    ko_ref[...] = _unpair(_head_norm_rope(k_ref[...], kg_ref[...], cos, sin, bd))

    segs = ((xs_ref, 0, D_INNER), (b_ref, D_INNER, BC_DIM), (c_ref, D_INNER + BC_DIM, BC_DIM))
    for ref, off, width in segs:
        new = ref[...]
        acc = cb_ref[:, off:off + width] + cw_ref[CONV_W - 1:CONV_W, off:off + width] * new
        for t in range(CONV_W - 1):
            lo = t * CONV_DIM + off
            acc = acc + cw_ref[t:t + 1, off:off + width] * cs_ref[:, lo:lo + width]
        xc_ref[:, off:off + width] = _silu(acc)
        for t in range(CONV_W - 2):
            cso_ref[:, t * CONV_DIM + off:t * CONV_DIM + off + width] = (
                cs_ref[:, (t + 1) * CONV_DIM + off:(t + 1) * CONV_DIM + off + width])
        lo = (CONV_W - 2) * CONV_DIM + off
        cso_ref[:, lo:lo + width] = new

    e3 = e3_ref[...]
    dt = _softplus(dt_ref[...] + dtb_ref[...])
    decay = jnp.exp(dt * (-jnp.exp(alog_ref[...])))
    xdt_ref[...] = xc_ref[:, 0:D_INNER] * _expand_heads(dt, e3)
    dec_ref[...] = _expand_heads(decay, e3)


def _decode_pre(p_act, lay, conv_state, cos, sin, qg, kg, bd, cw, cb, dtb, alog, e3):
    m = p_act.shape[0]
    full = lambda shape: pl.BlockSpec(shape, lambda i: (0, 0))

    def col(name):
        c = lay[name][0] // lay[name][1]
        return pl.BlockSpec((m, lay[name][1]), lambda i: (0, c))

    cs_w = (CONV_W - 1) * CONV_DIM
    return pl.pallas_call(
        _decode_pre_kernel,
        grid=(1,),
        in_specs=[col("q"), col("k"), col("xs"), col("B"), col("C"), col("dt"),
                  full((m, cs_w)), full((1, LANES)), full((1, LANES)), full((1, Q_DIM)), full((1, KV_DIM)),
                  full((SLAB, SLAB)), full((CONV_W, CONV_DIM)), full((1, CONV_DIM)), full((1, DT_PAD)),
                  full((1, DT_PAD)), full((3 * DT_PAD, D_INNER))],
        out_specs=[full((m, Q_DIM)), full((m, KV_DIM)), full((m, CONV_DIM)), full((m, D_INNER)),
                   full((m, D_INNER)), full((m, cs_w))],
        out_shape=[jax.ShapeDtypeStruct((m, Q_DIM), F32), jax.ShapeDtypeStruct((m, KV_DIM), F32),
                   jax.ShapeDtypeStruct((m, CONV_DIM), F32), jax.ShapeDtypeStruct((m, D_INNER), F32),
                   jax.ShapeDtypeStruct((m, D_INNER), F32), jax.ShapeDtypeStruct((m, cs_w), F32)],
        compiler_params=_cparams(("arbitrary",)),
        name="decode_pre",
    )(p_act, p_act, p_act, p_act, p_act, p_act, conv_state, cos, sin, qg, kg, bd, cw, cb, dtb, alog, e3)


DEC_ROWS = 8


def _attn_decode_kernel(q_ref, k_ref, v_ref, ck_ref, cv_ref, sink_ref, o_ref, cko_ref, cvo_ref):
    w = ck_ref.shape[2]
    nq = N_HEADS
    r = lax.broadcasted_iota(jnp.int32, (nq, SLAB), 0)
    grp = lax.broadcasted_iota(jnp.int32, (nq, SLAB), 1) // HEAD_DIM
    own = grp == (r % KV_HEADS)
    in_window = lax.broadcasted_iota(jnp.int32, (nq, w), 1) > (w - WINDOW - 1)
    newest = lax.broadcasted_iota(jnp.int32, (KV_DIM, w), 1) == w - 1
    sink = sink_ref[...][:, 0:1]
    pad = jnp.zeros((DEC_ROWS, KV_DIM), BF16)
    k_parts = jnp.concatenate(list(_split3(k_ref[...])) + [pad], axis=0)
    v_parts = jnp.concatenate(list(_split3(v_ref[...])) + [pad], axis=0)
    part_row = lax.broadcasted_iota(jnp.int32, (4 * DEC_ROWS, w), 0) % DEC_ROWS
    tdims = (((0,), (0,)), ((), ()))
    for bl in range(DEC_ROWS):
        qrow = q_ref[bl:bl + 1, :]
        qm = jnp.zeros((nq, SLAB), F32)
        for j in range(Q_PER_KV):
            slab = jnp.broadcast_to(qrow[:, j * SLAB:(j + 1) * SLAB], (nq, SLAB))
            qm = jnp.where((r // KV_HEADS) == j, slab, qm)
        qm = jnp.where(own, qm, 0.0).astype(BF16)
        pick = jnp.where(part_row == bl, 1.0, 0.0).astype(BF16)
        k_col = lax.dot_general(k_parts, pick, tdims, preferred_element_type=F32)
        v_col = lax.dot_general(v_parts, pick, tdims, preferred_element_type=F32)
        keys = jnp.where(newest, k_col, pltpu.roll(ck_ref[bl], w - 1, 1))
        vals = jnp.where(newest, v_col, pltpu.roll(cv_ref[bl], w - 1, 1))
        cko_ref[bl] = keys
        cvo_ref[bl] = vals
        s = jnp.dot(qm, keys.astype(BF16), preferred_element_type=F32)
        s = jnp.where(in_window, s, -jnp.inf)
        mx = jnp.maximum(jnp.max(s, axis=-1, keepdims=True), sink)
        e = jnp.exp(s - mx)
        den = jnp.sum(e, axis=-1, keepdims=True) + jnp.exp(sink - mx)
        pv = lax.dot_general(e.astype(BF16), vals.astype(BF16), (((1,), (1,)), ((), ())),
                             preferred_element_type=F32)
        pv = jnp.where(own, pv / den, 0.0)
        for j in range(Q_PER_KV):
            o_ref[bl:bl + 1, j * SLAB:(j + 1) * SLAB] = jnp.sum(
                pv[j * KV_HEADS:(j + 1) * KV_HEADS, :], axis=0, keepdims=True)


def _attn_decode(q, k, p_act, lay, cache_k, cache_v, sink_rows):
    m = q.shape[0]
    w = cache_k.shape[2]
    vc = lay["v"][0] // KV_DIM
    cache_spec = pl.BlockSpec((DEC_ROWS, KV_DIM, w), lambda i: (i, 0, 0))
    return pl.pallas_call(
        _attn_decode_kernel,
        grid=(m // DEC_ROWS,),
        in_specs=[pl.BlockSpec((DEC_ROWS, Q_DIM), lambda i: (i, 0)),
                  pl.BlockSpec((DEC_ROWS, KV_DIM), lambda i: (i, 0)),
                  pl.BlockSpec((DEC_ROWS, KV_DIM), lambda i: (i, vc)),
                  cache_spec, cache_spec,
                  pl.BlockSpec((N_HEADS, LANES), lambda i: (0, 0))],
        out_specs=[pl.BlockSpec((DEC_ROWS, Q_DIM), lambda i: (i, 0)), cache_spec, cache_spec],
        out_shape=[jax.ShapeDtypeStruct((m, Q_DIM), F32),
                   jax.ShapeDtypeStruct((m, KV_DIM, w), F32),
                   jax.ShapeDtypeStruct((m, KV_DIM, w), F32)],
        compiler_params=_cparams(("arbitrary",)),
        name="attn_decode",
    )(q, k, p_act, cache_k, cache_v, sink_rows)


MM_ROWS = 16


def _ssm_decode_kernel(st_ref, xdt_ref, dec_ref, xc_ref, z_ref, dskip_ref, nw_ref, sto_ref, o_ref):
    r = lax.broadcasted_iota(jnp.int32, (MM_ROWS, D_INNER), 0)
    grp = lax.broadcasted_iota(jnp.int32, (MM_ROWS, D_INNER), 1) // GROUP_LANES
    rr = lax.broadcasted_iota(jnp.int32, (MM_ROWS, D_STATE), 0)
    ones_rows = jnp.where((rr >= SSM_GROUPS) & (rr < SSM_GROUPS + 3), 1.0, 0.0)
    bc = lambda a: jnp.broadcast_to(a, (MM_ROWS, D_INNER))
    for u in range(st_ref.shape[0]):
        h = st_ref[u]
        xdt = xdt_ref[u]
        xc = xc_ref[u]
        hi, mid, lo = _split3(dec_ref[u])
        lhs_t = jnp.where(r == grp, bc(xdt), 0.0)
        for t, piece in enumerate((hi, mid, lo)):
            lhs_t = jnp.where(r == SSM_GROUPS + t, bc(piece.astype(F32)), lhs_t)
        lhs_t = lhs_t.astype(BF16)
        b_rows = jnp.zeros((MM_ROWS, D_STATE), F32)
        c_rows = jnp.zeros((MM_ROWS, D_STATE), F32)
        for g in range(SSM_GROUPS):
            b_g = xc[:, D_INNER + g * D_STATE:D_INNER + (g + 1) * D_STATE]
            c_g = xc[:, D_INNER + BC_DIM + g * D_STATE:D_INNER + BC_DIM + (g + 1) * D_STATE]
            b_rows = jnp.where(rr == g, jnp.broadcast_to(b_g, (MM_ROWS, D_STATE)), b_rows)
            c_rows = jnp.where(rr == g, jnp.broadcast_to(c_g, (MM_ROWS, D_STATE)), c_rows)
        rhs = jnp.concatenate([b_rows, ones_rows], axis=1).astype(BF16)
        both = lax.dot_general(lhs_t, rhs, (((0,), (0,)), ((), ())), preferred_element_type=F32)
        h_new = both[:, D_STATE:] * h + both[:, :D_STATE]
        sto_ref[u] = h_new
        yg = lax.dot_general(c_rows.astype(BF16), h_new.astype(BF16), (((1,), (1,)), ((), ())),
                             preferred_element_type=F32)
        y = jnp.sum(jnp.where(r == grp, yg, 0.0), axis=0, keepdims=True)
        y = y + dskip_ref[...] * xc[:, 0:D_INNER]
        yz = y * _silu(z_ref[u])
        ms = jnp.mean(yz * yz, axis=-1, keepdims=True)
        o_ref[u] = yz * lax.rsqrt(ms + EPS) * nw_ref[...]


SSM_DEC_ROWS = 2


def _ssm_decode(state, xdt, dec, xc, p_act, lay, dskip, nw):
    m = state.shape[0]
    n = p_act.shape[1]
    u = SSM_DEC_ROWS
    assert m % u == 0
    zc = lay["z"][0] // D_INNER
    row = lambda width: pl.BlockSpec((u, 1, width), lambda b: (b, 0, 0))
    return pl.pallas_call(
        _ssm_decode_kernel,
        grid=(m // u,),
        in_specs=[pl.BlockSpec((u, D_INNER, D_STATE), lambda b: (b, 0, 0)),
                  row(D_INNER), row(D_INNER), row(CONV_DIM),
                  pl.BlockSpec((u, 1, D_INNER), lambda b: (b, 0, zc)),
                  pl.BlockSpec((1, D_INNER), lambda b: (0, 0)),
                  pl.BlockSpec((1, D_INNER), lambda b: (0, 0))],
        out_specs=[pl.BlockSpec((u, D_INNER, D_STATE), lambda b: (b, 0, 0)), row(D_INNER)],
        out_shape=[jax.ShapeDtypeStruct((m, D_INNER, D_STATE), F32),
                   jax.ShapeDtypeStruct((m, 1, D_INNER), F32)],
        compiler_params=_cparams(("arbitrary",)),
        name="ssm_decode",
    )(state, xdt.reshape(m, 1, D_INNER), dec.reshape(m, 1, D_INNER), xc.reshape(m, 1, CONV_DIM),
      p_act.reshape(m, 1, n), dskip, nw)


def _merge_kernel(x_ref, oa_ref, os_ref, ga_ref, gs_ref, wa_ref, ws_ref, wo_ref, o_ref):
    a = jnp.dot(oa_ref[...].astype(BF16), wa_ref[...], preferred_element_type=F32)
    s = jnp.dot(os_ref[...].astype(BF16), ws_ref[...], preferred_element_type=F32)
    mixed = _sigmoid(ga_ref[...]) * a + _sigmoid(gs_ref[...]) * s
    o_ref[...] = x_ref[...] + jnp.dot(mixed.astype(BF16), wo_ref[...], preferred_element_type=F32)


def _merge(x, o_attn, o_ssm, p_act, lay, wa, ws, wo):
    m, d = x.shape
    tm = _row_tile(m, 256)
    gac, gsc = lay["ga"][0] // d, lay["gs"][0] // d
    const = lambda i: (0, 0)
    return pl.pallas_call(
        _merge_kernel,
        grid=(m // tm,),
        in_specs=[pl.BlockSpec((tm, d), lambda i: (i, 0)),
                  pl.BlockSpec((tm, Q_DIM), lambda i: (i, 0)),
                  pl.BlockSpec((tm, D_INNER), lambda i: (i, 0)),
                  pl.BlockSpec((tm, d), lambda i: (i, gac)),
                  pl.BlockSpec((tm, d), lambda i: (i, gsc)),
                  pl.BlockSpec((Q_DIM, d), const),
                  pl.BlockSpec((D_INNER, d), const),
                  pl.BlockSpec((d, d), const)],
        out_specs=pl.BlockSpec((tm, d), lambda i: (i, 0)),
        out_shape=jax.ShapeDtypeStruct((m, d), F32),
        compiler_params=_cparams(("arbitrary",)),
        name="merge",
    )(x, o_attn, o_ssm, p_act, p_act, wa, ws, wo)


def _mlp_kernel(x_ref, g_ref, wu_ref, wd_ref, o_ref, *, ff_tile):
    x = x_ref[...]
    ms = jnp.mean(x * x, axis=-1, keepdims=True)
    xn = (x * lax.rsqrt(ms + EPS) * g_ref[...]).astype(BF16)
    acc = x
    for c in range(wu_ref.shape[1] // ff_tile):
        u = jnp.dot(xn, wu_ref[:, c * ff_tile:(c + 1) * ff_tile], preferred_element_type=F32)
        a = jnp.square(jnp.maximum(u, 0.0)).astype(BF16)
        acc = acc + jnp.dot(a, wd_ref[c * ff_tile:(c + 1) * ff_tile, :], preferred_element_type=F32)
    o_ref[...] = acc


def _mlp(x, g, wu, wd):
    m, d = x.shape
    ff = wu.shape[1]
    tm = _row_tile(m, 256)
    const = lambda i: (0, 0)
    return pl.pallas_call(
        functools.partial(_mlp_kernel, ff_tile=min(1024, ff)),
        grid=(m // tm,),
        in_specs=[pl.BlockSpec((tm, d), lambda i: (i, 0)),
                  pl.BlockSpec((1, d), const),
                  pl.BlockSpec((d, ff), const),
                  pl.BlockSpec((ff, d), const)],
        out_specs=pl.BlockSpec((tm, d), lambda i: (i, 0)),
        out_shape=jax.ShapeDtypeStruct((m, d), F32),
        compiler_params=_cparams(("arbitrary",)),
        name="mlp",
    )(x, g, wu, wd)


def _rope_tables(pos):
    half = HEAD_DIM // 2
    inv = ROPE_THETA ** (-jnp.arange(half, dtype=F32) / half)
    ang = pos.astype(F32)[:, None] * inv[None, :]
    cos, sin = jnp.cos(ang), jnp.sin(ang)
    return (jnp.concatenate([cos, cos, cos, cos], axis=1),
            jnp.concatenate([-sin, -sin, sin, sin], axis=1))


def _constants():
    heads = np.arange(DT_PAD)[:, None]
    cols = np.arange(D_INNER)[None, :] // SSM_HEAD_DIM
    e = (heads == cols).astype(np.float32)
    e3 = jnp.asarray(np.concatenate([e, e, e], axis=0), BF16)
    a = np.arange(SLAB)
    grp = 2 * (a // LANES) + (a % HEAD_DIM) // (HEAD_DIM // 2)
    bd = jnp.asarray((grp[:, None] == grp[None, :]).astype(np.float32) / HEAD_DIM, BF16)
    return e3, bd


_HALF = HEAD_DIM // 2


def _pair_q(a):
    lead = a.shape[:-1]
    n = len(lead)
    a = a.reshape(lead + (2, 2, Q_PER_KV, 2, _HALF))
    return a.transpose(tuple(range(n)) + (n + 2, n, n + 3, n + 1, n + 4)).reshape(lead + (Q_DIM,))


def _pair_k(a):
    lead = a.shape[:-1]
    n = len(lead)
    a = a.reshape(lead + (2, 2, 2, _HALF))
    return a.transpose(tuple(range(n)) + (n, n + 2, n + 1, n + 3)).reshape(lead + (KV_DIM,))


def _prep_layer(d_model, norm_mix, w_in, q_norm, k_norm, attn_sinks, conv_w, conv_b, dt_bias, a_log, d_skip,
                ssm_norm, w_attn_o, w_ssm_o, w_out, norm_mlp, w_up, w_down):
    assert KV_HEADS == 4 and Q_PER_KV == 4
    e3, bd = _constants()
    o_q, o_k, o_v, o_z, o_xbc, o_dt = (0, Q_DIM, Q_DIM + KV_DIM, Q_DIM + 2 * KV_DIM,
                                       Q_DIM + 2 * KV_DIM + D_INNER, Q_DIM + 2 * KV_DIM + D_INNER + CONV_DIM)
    o_g = o_dt + SSM_HEADS
    cut = lambda lo, width: w_in[:, lo:lo + width]
    pieces = {"z": cut(o_z, D_INNER), "xs": cut(o_xbc, D_INNER), "ga": cut(o_g, d_model),
              "gs": cut(o_g + d_model, d_model), "q": _pair_q(cut(o_q, Q_DIM)),
              "B": cut(o_xbc + D_INNER, BC_DIM), "C": cut(o_xbc + D_INNER + BC_DIM, BC_DIM),
              "k": _pair_k(cut(o_k, KV_DIM)), "v": cut(o_v, KV_DIM),
              "dt": jnp.pad(cut(o_dt, SSM_HEADS), ((0, 0), (0, DT_PAD - SSM_HEADS)))}
    lay, _ = _layout(d_model)
    w_p = jnp.concatenate([pieces[name] for name in lay], axis=1).astype(BF16)
    pad_heads = lambda a: jnp.pad(a, (0, DT_PAD - SSM_HEADS))[None, :]
    sink_p = attn_sinks.reshape(KV_HEADS, Q_PER_KV).T.reshape(N_HEADS)
    wa = w_attn_o.reshape(KV_HEADS, Q_PER_KV, HEAD_DIM, d_model).transpose(1, 0, 2, 3).reshape(Q_DIM, d_model)
    return dict(
        lay=lay, e3=e3, bd=bd, w_p=w_p,
        norm_mix=norm_mix[None, :], norm_mlp=norm_mlp[None, :],
        qg=_pair_q(jnp.tile(q_norm, N_HEADS))[None, :], kg=_pair_k(jnp.tile(k_norm, KV_HEADS))[None, :],
        sink_p=sink_p, sink_rows=jnp.broadcast_to(sink_p[:, None], (N_HEADS, LANES)),
        cw=conv_w, cb=conv_b[None, :], dtb=pad_heads(dt_bias), alog=pad_heads(a_log),
        dskip=jnp.repeat(d_skip, SSM_HEAD_DIM)[None, :], nw=ssm_norm[None, :],
        wa=wa.astype(BF16), ws=w_ssm_o.astype(BF16), wo=w_out.astype(BF16),
        wu=w_up.astype(BF16), wd=w_down.astype(BF16))


def _prompt_layer(x, lw):
    batch, seq, d = x.shape
    assert seq % BLK == 0 and seq >= WINDOW
    x2 = x.reshape(batch * seq, d)
    p_act = _in_proj(x2, lw["norm_mix"], lw["w_p"])
    cos, sin = _rope_tables(jnp.arange(seq))
    o_attn, k_last, v_last = _attn_prompt(p_act, lw["lay"], batch, seq, cos, sin, lw["qg"], lw["kg"],
                                          lw["bd"], lw["sink_p"])
    o_ssm, tail, h_fin = _ssd_prompt(p_act, lw["lay"], batch, seq, lw["cw"], lw["cb"], lw["dtb"], lw["alog"],
                                     lw["dskip"], lw["nw"], lw["e3"])
    x1 = _merge(x2, o_attn.reshape(batch * seq, Q_DIM), o_ssm, p_act, lw["lay"], lw["wa"], lw["ws"], lw["wo"])
    y = _mlp(x1, lw["norm_mlp"], lw["wu"], lw["wd"])
    return (y.reshape(batch, seq, d),
            k_last.reshape(batch, KV_HEADS, HEAD_DIM, BLK).transpose(0, 3, 1, 2),
            v_last.reshape(batch, KV_HEADS, HEAD_DIM, BLK).transpose(0, 3, 1, 2),
            tail[:, SUBLANES - (CONV_W - 1):, :],
            h_fin.reshape(batch, SSM_HEADS, SSM_HEAD_DIM, D_STATE))


def _decode_layer(x, cache_k, cache_v, conv_state, ssm_state, lw):
    m, t, d = x.shape
    w = cache_k.shape[1]
    assert t == 1 and w == WINDOW and m % DEC_ROWS == 0
    x2 = x.reshape(m, d)
    p_act = _in_proj(x2, lw["norm_mix"], lw["w_p"])
    cos, sin = _rope_tables(PAST_LEN + jnp.arange(1))
    q, k, xc, xdt, dec, conv_new = _decode_pre(
        p_act, lw["lay"], conv_state.reshape(m, (CONV_W - 1) * CONV_DIM), cos, sin, lw["qg"], lw["kg"],
        lw["bd"], lw["cw"], lw["cb"], lw["dtb"], lw["alog"], lw["e3"])
    ck = jnp.transpose(cache_k, (0, 2, 3, 1)).reshape(m, KV_DIM, w)
    cv = jnp.transpose(cache_v, (0, 2, 3, 1)).reshape(m, KV_DIM, w)
    o_attn, ck_new, cv_new = _attn_decode(q, k, p_act, lw["lay"], ck, cv, lw["sink_rows"])
    h_new, o_ssm = _ssm_decode(ssm_state.reshape(m, D_INNER, D_STATE), xdt, dec, xc, p_act, lw["lay"],
                               lw["dskip"], lw["nw"])
    x1 = _merge(x2, o_attn, o_ssm.reshape(m, D_INNER), p_act, lw["lay"], lw["wa"], lw["ws"], lw["wo"])
    y = _mlp(x1, lw["norm_mlp"], lw["wu"], lw["wd"])
    unview = lambda c: jnp.transpose(c.reshape(m, KV_HEADS, HEAD_DIM, w), (0, 3, 1, 2))
    return (y.reshape(m, 1, d), unview(ck_new), unview(cv_new), conv_new.reshape(m, CONV_W - 1, CONV_DIM),
            h_new.reshape(m, SSM_HEADS, SSM_HEAD_DIM, D_STATE))


def kernel(x_prompt, x_sample, cache_k, cache_v, state_conv, state_ssm, norm_mix, w_in, q_norm, k_norm,
           attn_sinks, conv_w, conv_b, dt_bias, a_log, d_skip, ssm_norm, w_attn_o, w_ssm_o, w_out,
           norm_mlp, w_up, w_down):
    depth = w_in.shape[0]
    d_model = x_prompt.shape[-1]
    yp, ys = x_prompt, x_sample
    cols = [[] for _ in range(8)]
    for l in range(depth):
        lw = _prep_layer(d_model, norm_mix[l], w_in[l], q_norm[l], k_norm[l], attn_sinks[l], conv_w[l],
                         conv_b[l], dt_bias[l], a_log[l], d_skip[l], ssm_norm[l], w_attn_o[l], w_ssm_o[l],
                         w_out[l], norm_mlp[l], w_up[l], w_down[l])
        yp, kp, vp, cp, hp = _prompt_layer(yp, lw)
        ys, ks, vs, cs, hs = _decode_layer(ys, cache_k[l], cache_v[l], state_conv[l], state_ssm[l], lw)
        for lst, val in zip(cols, (kp, vp, cp, hp, ks, vs, cs, hs)):
            lst.append(val)
    return (yp, ys) + tuple(jnp.stack(c) for c in cols)
```

```python
import functools

import numpy as np
import jax
import jax.numpy as jnp
from jax import lax
from jax.experimental import pallas as pl
from jax.experimental.pallas import tpu as pltpu

F32 = jnp.float32
BF16 = jnp.bfloat16

N_HEADS = 16
KV_HEADS = 4
HEAD_DIM = 64
Q_PER_KV = N_HEADS // KV_HEADS
WINDOW = 128
ROPE_THETA = 10000.0
Q_DIM = N_HEADS * HEAD_DIM
KV_DIM = KV_HEADS * HEAD_DIM
SSM_HEAD_DIM = 64
SSM_HEADS = 32
D_INNER = SSM_HEADS * SSM_HEAD_DIM
SSM_GROUPS = 4
HEADS_PER_GROUP = SSM_HEADS // SSM_GROUPS
D_STATE = 128
BC_DIM = SSM_GROUPS * D_STATE
CONV_W = 4
CONV_DIM = D_INNER + 2 * BC_DIM
SSD_CHUNK = 128
EPS = 1e-6
PAST_LEN = 8192
LOG2E = 1.4426950408889634

LANES = 128
SUBLANES = 8
VMEM_LIMIT = 48 * 1024 * 1024

BLK = 128
assert WINDOW == BLK and SSD_CHUNK == BLK
SLAB = KV_HEADS * HEAD_DIM
GROUP_LANES = HEADS_PER_GROUP * SSM_HEAD_DIM
DT_PAD = LANES


def _layout(d_model):
    segs = [("z", D_INNER), ("xs", D_INNER), ("ga", d_model), ("gs", d_model), ("q", Q_DIM),
            ("B", BC_DIM), ("C", BC_DIM), ("k", KV_DIM), ("v", KV_DIM), ("dt", DT_PAD)]
    lay, off = {}, 0
    for name, width in segs:
        assert off % width == 0, (name, off, width)
        lay[name] = (off, width)
        off += width
    return lay, off


def _col_tile(n, cap=3072):
    units = n // LANES
    best = 1
    for d in range(1, units + 1):
        if units % d == 0 and d * LANES <= cap:
            best = d
    return best * LANES


def _row_tile(m, cap):
    assert m % SUBLANES == 0
    best = SUBLANES
    for t in range(SUBLANES, min(m, cap) + 1, SUBLANES):
        if m % t == 0:
            best = t
    return best


def _cparams(sem):
    return pltpu.CompilerParams(dimension_semantics=sem, vmem_limit_bytes=VMEM_LIMIT)


def _sigmoid(x):
    return 0.5 + 0.5 * jnp.tanh(0.5 * x)


def _silu(x):
    h = 0.5 * x
    return h + h * jnp.tanh(h)


def _softplus(x):
    return jnp.maximum(x, 0.0) + jnp.log1p(jnp.exp(-jnp.abs(x)))


def _split3(a):
    hi = a.astype(BF16)
    r1 = a - hi.astype(F32)
    mid = r1.astype(BF16)
    lo = (r1 - mid.astype(F32)).astype(BF16)
    return hi, mid, lo


def _expand_heads(a, e3):
    hi, mid, lo = _split3(a)
    return jnp.dot(jnp.concatenate([hi, mid, lo], axis=1), e3, preferred_element_type=F32)


def _head_ms(x, bd):
    sq = x * x
    hi = sq.astype(BF16)
    lo = (sq - hi.astype(F32)).astype(BF16)
    outs = []
    for s in range(x.shape[1] // SLAB):
        sl = slice(s * SLAB, (s + 1) * SLAB)
        outs.append(jnp.dot(hi[:, sl], bd, preferred_element_type=F32)
                    + jnp.dot(lo[:, sl], bd, preferred_element_type=F32))
    return outs[0] if len(outs) == 1 else jnp.concatenate(outs, axis=1)


def _head_norm_rope(x, g, cos, sin, bd):
    xn = x * lax.rsqrt(_head_ms(x, bd) + EPS) * g
    tiles = []
    for t in range(x.shape[1] // LANES):
        xt = xn[:, t * LANES:(t + 1) * LANES]
        tiles.append(xt * cos + pltpu.roll(xt, LANES // 2, 1) * sin)
    return tiles[0] if len(tiles) == 1 else jnp.concatenate(tiles, axis=1)


def _unpair(x):
    q = HEAD_DIM // 2
    lane = lax.broadcasted_iota(jnp.int32, (x.shape[0], LANES), 1)
    tiles = []
    for t in range(x.shape[1] // LANES):
        xt = x[:, t * LANES:(t + 1) * LANES]
        nat = jnp.where((lane >= q) & (lane < 2 * q), pltpu.roll(xt, LANES - q, 1), xt)
        tiles.append(jnp.where((lane >= 2 * q) & (lane < 3 * q), pltpu.roll(xt, q, 1), nat))
    return tiles[0] if len(tiles) == 1 else jnp.concatenate(tiles, axis=1)


def _in_proj_kernel(x_ref, g_ref, w_ref, o_ref):
    x = x_ref[...]
    ms = jnp.mean(x * x, axis=-1, keepdims=True)
    xn = (x * lax.rsqrt(ms + EPS) * g_ref[...]).astype(BF16)
    o_ref[...] = jnp.dot(xn, w_ref[...], preferred_element_type=F32)


def _in_proj(x, g, w):
    m, k = x.shape
    n = w.shape[1]
    tm = _row_tile(m, 512)
    tn = _col_tile(n)
    return pl.pallas_call(
        _in_proj_kernel,
        grid=(n // tn, m // tm),
        in_specs=[pl.BlockSpec((tm, k), lambda j, i: (i, 0)),
                  pl.BlockSpec((1, k), lambda j, i: (0, 0)),
                  pl.BlockSpec((k, tn), lambda j, i: (0, j))],
        out_specs=pl.BlockSpec((tm, tn), lambda j, i: (i, j)),
        out_shape=jax.ShapeDtypeStruct((m, n), F32),
        compiler_params=_cparams(("arbitrary", "arbitrary")),
        name="in_proj",
    )(x, g, w)


def _attn_prompt_kernel(q_ref, k_ref, v_ref, cos_ref, sin_ref, qg_ref, kg_ref, bd_ref, sink_ref,
                        o_ref, ko_ref, vo_ref, kbuf, vbuf, probs):
    i = pl.program_id(1)
    last = pl.num_programs(1) - 1

    @pl.when(i == 0)
    def _():
        kbuf[...] = jnp.zeros_like(kbuf)
        vbuf[...] = jnp.zeros_like(vbuf)

    cos, sin, bd = cos_ref[...], sin_ref[...], bd_ref[...]
    slot = i % 2
    lane = lax.broadcasted_iota(jnp.int32, (BLK, SLAB), 1)
    grp_k = 2 * (lane // LANES) + (lane % HEAD_DIM) // (HEAD_DIM // 2)
    grp_v = lane // HEAD_DIM
    r = lax.broadcasted_iota(jnp.int32, (BLK, 2 * BLK), 0)
    c = lax.broadcasted_iota(jnp.int32, (BLK, 2 * BLK), 1)
    key = c % BLK
    is_cur = (c // BLK) == slot
    prev_ok = jnp.where(i > 0, key, -1)
    mask = jnp.where(is_cur, (key <= r).astype(jnp.int32), (prev_ok > r + (BLK - WINDOW)).astype(jnp.int32)) > 0

    for u in range(q_ref.shape[0]):
        q = _head_norm_rope(q_ref[u], qg_ref[...], cos, sin, bd) * (HEAD_DIM ** -0.5 * LOG2E)
        k = _head_norm_rope(k_ref[u], kg_ref[...], cos, sin, bd)
        v = v_ref[u]

        @pl.when(i == last)
        def _():
            ko_ref[u] = _unpair(k).T
            vo_ref[u] = v.T

        for p in range(KV_HEADS):
            rows = pl.ds(pl.multiple_of(p * 2 * BLK + slot * BLK, BLK), BLK)
            kbuf[u, rows, :] = jnp.where(grp_k == p, k, 0.0).astype(BF16)
            vbuf[u, rows, :] = jnp.where(grp_v == p, v, 0.0).astype(BF16)

        q_stack = jnp.concatenate([q[:, j * SLAB:(j + 1) * SLAB] for j in range(Q_PER_KV)],
                                  axis=0).astype(BF16)
        s_all = lax.dot_general(q_stack, kbuf[u], (((1,), (1,)), ((), ())), preferred_element_type=F32)
        for j in range(Q_PER_KV):
            for p in range(KV_HEADS):
                s = jnp.where(mask, s_all[j * BLK:(j + 1) * BLK, p * 2 * BLK:(p + 1) * 2 * BLK], -jnp.inf)
                sink = sink_ref[j * KV_HEADS + p] * LOG2E
                mx = jnp.maximum(jnp.max(s, axis=-1, keepdims=True), sink)
                e = jnp.exp2(s - mx)
                den = jnp.sum(e, axis=-1, keepdims=True) + jnp.exp2(sink - mx)
                probs[u, j * BLK:(j + 1) * BLK, p * 2 * BLK:(p + 1) * 2 * BLK] = (e / den).astype(BF16)
        pv = jnp.dot(probs[u], vbuf[u], preferred_element_type=F32)
        for j in range(Q_PER_KV):
            o_ref[u, :, j * SLAB:(j + 1) * SLAB] = pv[j * BLK:(j + 1) * BLK, :].astype(o_ref.dtype)


ATTN_SEQS = 2


def _attn_prompt(p_act, lay, batch, seq, cos, sin, qg, kg, bd, sinks):
    nb = seq // BLK
    u = ATTN_SEQS if batch % ATTN_SEQS == 0 else 1
    qc, kc, vc = lay["q"][0] // Q_DIM, lay["k"][0] // KV_DIM, lay["v"][0] // KV_DIM
    p3 = p_act.reshape(batch, seq, p_act.shape[1])
    return pl.pallas_call(
        _attn_prompt_kernel,
        grid=(batch // u, nb),
        in_specs=[pl.BlockSpec((u, BLK, Q_DIM), lambda b, i: (b, i, qc)),
                  pl.BlockSpec((u, BLK, KV_DIM), lambda b, i: (b, i, kc)),
                  pl.BlockSpec((u, BLK, KV_DIM), lambda b, i: (b, i, vc)),
                  pl.BlockSpec((BLK, LANES), lambda b, i: (i, 0)),
                  pl.BlockSpec((BLK, LANES), lambda b, i: (i, 0)),
                  pl.BlockSpec((1, Q_DIM), lambda b, i: (0, 0)),
                  pl.BlockSpec((1, KV_DIM), lambda b, i: (0, 0)),
                  pl.BlockSpec((SLAB, SLAB), lambda b, i: (0, 0)),
                  pl.BlockSpec(memory_space=pltpu.SMEM)],
        out_specs=[pl.BlockSpec((u, BLK, Q_DIM), lambda b, i: (b, i, 0)),
                   pl.BlockSpec((u, KV_DIM, BLK), lambda b, i: (b, 0, 0)),
                   pl.BlockSpec((u, KV_DIM, BLK), lambda b, i: (b, 0, 0))],
        out_shape=[jax.ShapeDtypeStruct((batch, seq, Q_DIM), BF16),
                   jax.ShapeDtypeStruct((batch, KV_DIM, BLK), F32),
                   jax.ShapeDtypeStruct((batch, KV_DIM, BLK), F32)],
        scratch_shapes=[pltpu.VMEM((u, KV_HEADS * 2 * BLK, KV_DIM), BF16),
                        pltpu.VMEM((u, KV_HEADS * 2 * BLK, KV_DIM), BF16),
                        pltpu.VMEM((u, Q_PER_KV * BLK, KV_HEADS * 2 * BLK), BF16)],
        compiler_params=_cparams(("arbitrary", "arbitrary")),
        name="attn_prompt",
    )(p3, p3, p3, cos, sin, qg, kg, bd, sinks)


def _ssd_prompt_kernel(z_ref, xs_ref, b_ref, c_ref, dt_ref, cw_ref, cb_ref, dtb_ref, alog_ref, dskip_ref,
                       nw_ref, e3_ref, o_ref, tail_ref, hfin_ref, xpad, tails, st, ybuf):
    i = pl.program_id(1)
    last = pl.num_programs(1) - 1

    @pl.when(i == 0)
    def _():
        tails[...] = jnp.zeros_like(tails)
        st[...] = jnp.zeros_like(st)

    slot = i % 2
    xpad[0:SUBLANES, :] = tails[1 - slot]
    xpad[SUBLANES:SUBLANES + BLK, 0:D_INNER] = xs_ref[...]
    xpad[SUBLANES:SUBLANES + BLK, D_INNER:D_INNER + BC_DIM] = b_ref[...]
    xpad[SUBLANES:SUBLANES + BLK, D_INNER + BC_DIM:CONV_DIM] = c_ref[...]
    cwh = 0.5 * cw_ref[...]
    acc = 0.5 * cb_ref[...] + cwh[CONV_W - 1:CONV_W, :] * xpad[SUBLANES:SUBLANES + BLK, :]
    for t in range(1, CONV_W):
        acc = acc + cwh[CONV_W - 1 - t:CONV_W - t, :] * xpad[SUBLANES - t:SUBLANES - t + BLK, :]
    xc = acc + acc * jnp.tanh(acc)
    new_tail = xpad[BLK:BLK + SUBLANES, :]
    tail_ref[...] = new_tail
    tails[slot] = new_tail

    xs = xc[:, 0:D_INNER]
    bm = xc[:, D_INNER:D_INNER + BC_DIM].astype(BF16)
    cm = xc[:, D_INNER + BC_DIM:CONV_DIM].astype(BF16)

    e3 = e3_ref[...]
    dt = _softplus(dt_ref[...] + dtb_ref[...])
    dta = dt * (-LOG2E * jnp.exp(alog_ref[...]))
    row = lax.broadcasted_iota(jnp.int32, (BLK, BLK), 0)
    col = lax.broadcasted_iota(jnp.int32, (BLK, BLK), 1)
    causal = row >= col
    cum = jnp.dot(causal.astype(F32), dta, preferred_element_type=F32, precision=lax.Precision.HIGHEST)
    cum_t = cum.T
    ecum = jnp.exp2(cum)
    to_end = jnp.exp2(cum[BLK - 1:BLK, :] - cum) * dt
    dt_e = _expand_heads(dt, e3)
    ecum_e = _expand_heads(ecum, e3)
    to_end_e = _expand_heads(to_end, e3)
    xdt = (xs * dt_e).astype(BF16)
    xte = (xs * to_end_e).astype(BF16)
    lane = lax.broadcasted_iota(jnp.int32, (BLK, LANES), 1)
    first_head = lane < SSM_HEAD_DIM

    for g in range(SSM_GROUPS):
        gl = slice(g * GROUP_LANES, (g + 1) * GROUP_LANES)
        bg = bm[:, g * D_STATE:(g + 1) * D_STATE]
        cg = cm[:, g * D_STATE:(g + 1) * D_STATE]
        cbg = lax.dot_general(cg, bg, (((1,), (1,)), ((), ())), preferred_element_type=F32)
        st_g = st[1 - slot, :, gl]
        y_inter = jnp.dot(cg, st_g.astype(BF16), preferred_element_type=F32) * ecum_e[:, gl]
        for pr in range(HEADS_PER_GROUP // 2):
            h0 = g * HEADS_PER_GROUP + 2 * pr
            xd = xdt[:, h0 * SSM_HEAD_DIM:(h0 + 2) * SSM_HEAD_DIM]
            ys = []
            for h in (h0, h0 + 1):
                diff = cum[:, h:h + 1] - cum_t[h:h + 1, :]
                w = (jnp.exp2(jnp.where(causal, diff, -jnp.inf)) * cbg).astype(BF16)
                ys.append(jnp.dot(w, xd, preferred_element_type=F32))
            lo = pr * LANES
            ybuf[:, h0 * SSM_HEAD_DIM:(h0 + 2) * SSM_HEAD_DIM] = (
                jnp.where(first_head, ys[0], ys[1]) + y_inter[:, lo:lo + LANES])
        upd = lax.dot_general(bg, xte[:, gl], (((0,), (0,)), ((), ())), preferred_element_type=F32)
        st[slot, :, gl] = st_g * ecum_e[BLK - 1:BLK, gl] + upd

    y = ybuf[...] + dskip_ref[...] * xs
    yz = y * _silu(z_ref[...])
    ms = jnp.mean(yz * yz, axis=-1, keepdims=True)
    o_ref[...] = (yz * lax.rsqrt(ms + EPS) * nw_ref[...]).astype(o_ref.dtype)

    @pl.when(i == last)
    def _():
        hfin_ref[...] = st[slot].T


def _ssd_prompt(p_act, lay, batch, seq, cw, cb, dtb, alog, dskip, nw, e3):
    nb = seq // BLK
    m = batch * seq
    zc, xc = lay["z"][0] // D_INNER, lay["xs"][0] // D_INNER
    bc, cc, dc = lay["B"][0] // BC_DIM, lay["C"][0] // BC_DIM, lay["dt"][0] // DT_PAD
    const = lambda b, i: (0, 0)
    return pl.pallas_call(
        _ssd_prompt_kernel,
        grid=(batch, nb),
        in_specs=[pl.BlockSpec((BLK, D_INNER), lambda b, i: (b * nb + i, zc)),
                  pl.BlockSpec((BLK, D_INNER), lambda b, i: (b * nb + i, xc)),
                  pl.BlockSpec((BLK, BC_DIM), lambda b, i: (b * nb + i, bc)),
                  pl.BlockSpec((BLK, BC_DIM), lambda b, i: (b * nb + i, cc)),
                  pl.BlockSpec((BLK, DT_PAD), lambda b, i: (b * nb + i, dc)),
                  pl.BlockSpec((CONV_W, CONV_DIM), const),
                  pl.BlockSpec((1, CONV_DIM), const),
                  pl.BlockSpec((1, DT_PAD), const),
                  pl.BlockSpec((1, DT_PAD), const),
                  pl.BlockSpec((1, D_INNER), const),
                  pl.BlockSpec((1, D_INNER), const),
                  pl.BlockSpec((3 * DT_PAD, D_INNER), const)],
        out_specs=[pl.BlockSpec((BLK, D_INNER), lambda b, i: (b * nb + i, 0)),
                   pl.BlockSpec((None, SUBLANES, CONV_DIM), lambda b, i: (b, 0, 0)),
                   pl.BlockSpec((None, D_INNER, D_STATE), lambda b, i: (b, 0, 0))],
        out_shape=[jax.ShapeDtypeStruct((m, D_INNER), BF16),
                   jax.ShapeDtypeStruct((batch, SUBLANES, CONV_DIM), F32),
                   jax.ShapeDtypeStruct((batch, D_INNER, D_STATE), F32)],
        scratch_shapes=[pltpu.VMEM((BLK + SUBLANES, CONV_DIM), F32),
                        pltpu.VMEM((2, SUBLANES, CONV_DIM), F32),
                        pltpu.VMEM((2, D_STATE, D_INNER), F32),
                        pltpu.VMEM((BLK, D_INNER), F32)],
        compiler_params=_cparams(("arbitrary", "arbitrary")),
        name="ssd_prompt",
    )(p_act, p_act, p_act, p_act, p_act, cw, cb, dtb, alog, dskip, nw, e3)


def _decode_pre_kernel(q_ref, k_ref, xs_ref, b_ref, c_ref, dt_ref, cs_ref, cos_ref, sin_ref, qg_ref, kg_ref,
                       bd_ref, cw_ref, cb_ref, dtb_ref, alog_ref, e3_ref,
                       qo_ref, ko_ref, xc_ref, xdt_ref, dec_ref, cso_ref):
    cos, sin, bd = cos_ref[...], sin_ref[...], bd_ref[...]
    qo_ref[...] = _unpair(_head_norm_rope(q_ref[...], qg_ref[...], cos, sin, bd)) * (HEAD_DIM ** -0.5)
    ko_ref[...] = _unpair(_head_norm_rope(k_ref[...], kg_ref[...], cos, sin, bd))

    segs = ((xs_ref, 0, D_INNER), (b_ref, D_INNER, BC_DIM), (c_ref, D_INNER + BC_DIM, BC_DIM))
    for ref, off, width in segs:
        new = ref[...]
        acc = cb_ref[:, off:off + width] + cw_ref[CONV_W - 1:CONV_W, off:off + width] * new
        for t in range(CONV_W - 1):
            lo = t * CONV_DIM + off
            acc = acc + cw_ref[t:t + 1, off:off + width] * cs_ref[:, lo:lo + width]
        xc_ref[:, off:off + width] = _silu(acc)
        for t in range(CONV_W - 2):
            cso_ref[:, t * CONV_DIM + off:t * CONV_DIM + off + width] = (
                cs_ref[:, (t + 1) * CONV_DIM + off:(t + 1) * CONV_DIM + off + width])
        lo = (CONV_W - 2) * CONV_DIM + off
        cso_ref[:, lo:lo + width] = new

    e3 = e3_ref[...]
    dt = _softplus(dt_ref[...] + dtb_ref[...])
    decay = jnp.exp(dt * (-jnp.exp(alog_ref[...])))
    xdt_ref[...] = xc_ref[:, 0:D_INNER] * _expand_heads(dt, e3)
    dec_ref[...] = _expand_heads(decay, e3)


def _decode_pre(p_act, lay, conv_state, cos, sin, qg, kg, bd, cw, cb, dtb, alog, e3):
    m = p_act.shape[0]
    full = lambda shape: pl.BlockSpec(shape, lambda i: (0, 0))

    def col(name):
        c = lay[name][0] // lay[name][1]
        return pl.BlockSpec((m, lay[name][1]), lambda i: (0, c))

    cs_w = (CONV_W - 1) * CONV_DIM
    return pl.pallas_call(
        _decode_pre_kernel,
        grid=(1,),
        in_specs=[col("q"), col("k"), col("xs"), col("B"), col("C"), col("dt"),
                  full((m, cs_w)), full((1, LANES)), full((1, LANES)), full((1, Q_DIM)), full((1, KV_DIM)),
                  full((SLAB, SLAB)), full((CONV_W, CONV_DIM)), full((1, CONV_DIM)), full((1, DT_PAD)),
                  full((1, DT_PAD)), full((3 * DT_PAD, D_INNER))],
        out_specs=[full((m, Q_DIM)), full((m, KV_DIM)), full((m, CONV_DIM)), full((m, D_INNER)),
                   full((m, D_INNER)), full((m, cs_w))],
        out_shape=[jax.ShapeDtypeStruct((m, Q_DIM), F32), jax.ShapeDtypeStruct((m, KV_DIM), F32),
                   jax.ShapeDtypeStruct((m, CONV_DIM), F32), jax.ShapeDtypeStruct((m, D_INNER), F32),
                   jax.ShapeDtypeStruct((m, D_INNER), F32), jax.ShapeDtypeStruct((m, cs_w), F32)],
        compiler_params=_cparams(("arbitrary",)),
        name="decode_pre",
    )(p_act, p_act, p_act, p_act, p_act, p_act, conv_state, cos, sin, qg, kg, bd, cw, cb, dtb, alog, e3)


DEC_ROWS = 8


def _attn_decode_kernel(q_ref, k_ref, v_ref, ck_ref, cv_ref, sink_ref, o_ref, cko_ref, cvo_ref):
    w = ck_ref.shape[2]
    nq = N_HEADS
    r = lax.broadcasted_iota(jnp.int32, (nq, SLAB), 0)
    grp = lax.broadcasted_iota(jnp.int32, (nq, SLAB), 1) // HEAD_DIM
    own = grp == (r % KV_HEADS)
    in_window = lax.broadcasted_iota(jnp.int32, (nq, w), 1) > (w - WINDOW - 1)
    newest = lax.broadcasted_iota(jnp.int32, (KV_DIM, w), 1) == w - 1
    sink = sink_ref[...][:, 0:1]
    pad = jnp.zeros((DEC_ROWS, KV_DIM), BF16)
    k_parts = jnp.concatenate(list(_split3(k_ref[...])) + [pad], axis=0)
    v_parts = jnp.concatenate(list(_split3(v_ref[...])) + [pad], axis=0)
    part_row = lax.broadcasted_iota(jnp.int32, (4 * DEC_ROWS, w), 0) % DEC_ROWS
    tdims = (((0,), (0,)), ((), ()))
    for bl in range(DEC_ROWS):
        qrow = q_ref[bl:bl + 1, :]
        qm = jnp.zeros((nq, SLAB), F32)
        for j in range(Q_PER_KV):
            slab = jnp.broadcast_to(qrow[:, j * SLAB:(j + 1) * SLAB], (nq, SLAB))
            qm = jnp.where((r // KV_HEADS) == j, slab, qm)
        qm = jnp.where(own, qm, 0.0).astype(BF16)
        pick = jnp.where(part_row == bl, 1.0, 0.0).astype(BF16)
        k_col = lax.dot_general(k_parts, pick, tdims, preferred_element_type=F32)
        v_col = lax.dot_general(v_parts, pick, tdims, preferred_element_type=F32)
        keys = jnp.where(newest, k_col, pltpu.roll(ck_ref[bl], w - 1, 1))
        vals = jnp.where(newest, v_col, pltpu.roll(cv_ref[bl], w - 1, 1))
        cko_ref[bl] = keys
        cvo_ref[bl] = vals
        s = jnp.dot(qm, keys.astype(BF16), preferred_element_type=F32)
        s = jnp.where(in_window, s, -jnp.inf)
        mx = jnp.maximum(jnp.max(s, axis=-1, keepdims=True), sink)
        e = jnp.exp(s - mx)
        den = jnp.sum(e, axis=-1, keepdims=True) + jnp.exp(sink - mx)
        pv = lax.dot_general(e.astype(BF16), vals.astype(BF16), (((1,), (1,)), ((), ())),
                             preferred_element_type=F32)
        pv = jnp.where(own, pv / den, 0.0)
        for j in range(Q_PER_KV):
            o_ref[bl:bl + 1, j * SLAB:(j + 1) * SLAB] = jnp.sum(
                pv[j * KV_HEADS:(j + 1) * KV_HEADS, :], axis=0, keepdims=True)


def _attn_decode(q, k, p_act, lay, cache_k, cache_v, sink_rows):
    m = q.shape[0]
    w = cache_k.shape[2]
    vc = lay["v"][0] // KV_DIM
    cache_spec = pl.BlockSpec((DEC_ROWS, KV_DIM, w), lambda i: (i, 0, 0))
    return pl.pallas_call(
        _attn_decode_kernel,
        grid=(m // DEC_ROWS,),
        in_specs=[pl.BlockSpec((DEC_ROWS, Q_DIM), lambda i: (i, 0)),
                  pl.BlockSpec((DEC_ROWS, KV_DIM), lambda i: (i, 0)),
                  pl.BlockSpec((DEC_ROWS, KV_DIM), lambda i: (i, vc)),
                  cache_spec, cache_spec,
                  pl.BlockSpec((N_HEADS, LANES), lambda i: (0, 0))],
        out_specs=[pl.BlockSpec((DEC_ROWS, Q_DIM), lambda i: (i, 0)), cache_spec, cache_spec],
        out_shape=[jax.ShapeDtypeStruct((m, Q_DIM), F32),
                   jax.ShapeDtypeStruct((m, KV_DIM, w), F32),
                   jax.ShapeDtypeStruct((m, KV_DIM, w), F32)],
        compiler_params=_cparams(("arbitrary",)),
        name="attn_decode",
    )(q, k, p_act, cache_k, cache_v, sink_rows)


MM_ROWS = 16


def _ssm_decode_kernel(st_ref, xdt_ref, dec_ref, xc_ref, z_ref, dskip_ref, nw_ref, sto_ref, o_ref):
    r = lax.broadcasted_iota(jnp.int32, (MM_ROWS, D_INNER), 0)
    grp = lax.broadcasted_iota(jnp.int32, (MM_ROWS, D_INNER), 1) // GROUP_LANES
    rr = lax.broadcasted_iota(jnp.int32, (MM_ROWS, D_STATE), 0)
    ones_rows = jnp.where((rr >= SSM_GROUPS) & (rr < SSM_GROUPS + 3), 1.0, 0.0)
    bc = lambda a: jnp.broadcast_to(a, (MM_ROWS, D_INNER))
    for u in range(st_ref.shape[0]):
        h = st_ref[u]
        xdt = xdt_ref[u]
        xc = xc_ref[u]
        hi, mid, lo = _split3(dec_ref[u])
        lhs_t = jnp.where(r == grp, bc(xdt), 0.0)
        for t, piece in enumerate((hi, mid, lo)):
            lhs_t = jnp.where(r == SSM_GROUPS + t, bc(piece.astype(F32)), lhs_t)
        lhs_t = lhs_t.astype(BF16)
        b_rows = jnp.zeros((MM_ROWS, D_STATE), F32)
        c_rows = jnp.zeros((MM_ROWS, D_STATE), F32)
        for g in range(SSM_GROUPS):
            b_g = xc[:, D_INNER + g * D_STATE:D_INNER + (g + 1) * D_STATE]
            c_g = xc[:, D_INNER + BC_DIM + g * D_STATE:D_INNER + BC_DIM + (g + 1) * D_STATE]
            b_rows = jnp.where(rr == g, jnp.broadcast_to(b_g, (MM_ROWS, D_STATE)), b_rows)
            c_rows = jnp.where(rr == g, jnp.broadcast_to(c_g, (MM_ROWS, D_STATE)), c_rows)
        rhs = jnp.concatenate([b_rows, ones_rows], axis=1).astype(BF16)
        both = lax.dot_general(lhs_t, rhs, (((0,), (0,)), ((), ())), preferred_element_type=F32)
        h_new = both[:, D_STATE:] * h + both[:, :D_STATE]
        sto_ref[u] = h_new
        yg = lax.dot_general(c_rows.astype(BF16), h_new.astype(BF16), (((1,), (1,)), ((), ())),
                             preferred_element_type=F32)
        y = jnp.sum(jnp.where(r == grp, yg, 0.0), axis=0, keepdims=True)
        y = y + dskip_ref[...] * xc[:, 0:D_INNER]
        yz = y * _silu(z_ref[u])
        ms = jnp.mean(yz * yz, axis=-1, keepdims=True)
        o_ref[u] = yz * lax.rsqrt(ms + EPS) * nw_ref[...]


SSM_DEC_ROWS = 4


def _ssm_decode(state, xdt, dec, xc, p_act, lay, dskip, nw):
    m = state.shape[0]
    n = p_act.shape[1]
    u = SSM_DEC_ROWS
    assert m % u == 0
    zc = lay["z"][0] // D_INNER
    row = lambda width: pl.BlockSpec((u, 1, width), lambda b: (b, 0, 0))
    return pl.pallas_call(
        _ssm_decode_kernel,
        grid=(m // u,),
        in_specs=[pl.BlockSpec((u, D_INNER, D_STATE), lambda b: (b, 0, 0)),
                  row(D_INNER), row(D_INNER), row(CONV_DIM),
                  pl.BlockSpec((u, 1, D_INNER), lambda b: (b, 0, zc)),
                  pl.BlockSpec((1, D_INNER), lambda b: (0, 0)),
                  pl.BlockSpec((1, D_INNER), lambda b: (0, 0))],
        out_specs=[pl.BlockSpec((u, D_INNER, D_STATE), lambda b: (b, 0, 0)), row(D_INNER)],
        out_shape=[jax.ShapeDtypeStruct((m, D_INNER, D_STATE), F32),
                   jax.ShapeDtypeStruct((m, 1, D_INNER), F32)],
        compiler_params=_cparams(("arbitrary",)),
        name="ssm_decode",
    )(state, xdt.reshape(m, 1, D_INNER), dec.reshape(m, 1, D_INNER), xc.reshape(m, 1, CONV_DIM),
      p_act.reshape(m, 1, n), dskip, nw)


def _merge_mlp_kernel(x_ref, oa_ref, os_ref, ga_ref, gs_ref, wa_ref, ws_ref, wo_ref, g_ref, wu_ref, wd_ref,
                      o_ref, *, ff_tile):
    a = jnp.dot(oa_ref[...].astype(BF16), wa_ref[...], preferred_element_type=F32)
    s = jnp.dot(os_ref[...].astype(BF16), ws_ref[...], preferred_element_type=F32)
    mixed = _sigmoid(ga_ref[...]) * a + _sigmoid(gs_ref[...]) * s
    x = x_ref[...] + jnp.dot(mixed.astype(BF16), wo_ref[...], preferred_element_type=F32)
    ms = jnp.mean(x * x, axis=-1, keepdims=True)
    xn = (x * lax.rsqrt(ms + EPS) * g_ref[...]).astype(BF16)
    acc = x
    for c in range(wu_ref.shape[1] // ff_tile):
        u = jnp.dot(xn, wu_ref[:, c * ff_tile:(c + 1) * ff_tile], preferred_element_type=F32)
        act = jnp.square(jnp.maximum(u, 0.0)).astype(BF16)
        acc = acc + jnp.dot(act, wd_ref[c * ff_tile:(c + 1) * ff_tile, :], preferred_element_type=F32)
    o_ref[...] = acc


def _merge_mlp(x, o_attn, o_ssm, p_act, lay, wa, ws, wo, g, wu, wd):
    m, d = x.shape
    ff = wu.shape[1]
    tm = _row_tile(m, 256)
    gac, gsc = lay["ga"][0] // d, lay["gs"][0] // d
    weight = lambda shape: pl.BlockSpec(shape, lambda i: (0, 0), pipeline_mode=pl.Buffered(1))
    return pl.pallas_call(
        functools.partial(_merge_mlp_kernel, ff_tile=min(1024, ff)),
        grid=(m // tm,),
        in_specs=[pl.BlockSpec((tm, d), lambda i: (i, 0)),
                  pl.BlockSpec((tm, Q_DIM), lambda i: (i, 0)),
                  pl.BlockSpec((tm, D_INNER), lambda i: (i, 0)),
                  pl.BlockSpec((tm, d), lambda i: (i, gac)),
                  pl.BlockSpec((tm, d), lambda i: (i, gsc)),
                  weight((Q_DIM, d)), weight((D_INNER, d)), weight((d, d)),
                  weight((1, d)), weight((d, ff)), weight((ff, d))],
        out_specs=pl.BlockSpec((tm, d), lambda i: (i, 0)),
        out_shape=jax.ShapeDtypeStruct((m, d), F32),
        compiler_params=_cparams(("arbitrary",)),
        name="merge_mlp",
    )(x, o_attn, o_ssm, p_act, p_act, wa, ws, wo, g, wu, wd)


def _rope_tables(pos):
    half = HEAD_DIM // 2
    inv = ROPE_THETA ** (-jnp.arange(half, dtype=F32) / half)
    ang = pos.astype(F32)[:, None] * inv[None, :]
    cos, sin = jnp.cos(ang), jnp.sin(ang)
    return (jnp.concatenate([cos, cos, cos, cos], axis=1),
            jnp.concatenate([-sin, -sin, sin, sin], axis=1))


def _constants():
    heads = np.arange(DT_PAD)[:, None]
    cols = np.arange(D_INNER)[None, :] // SSM_HEAD_DIM
    e = (heads == cols).astype(np.float32)
    e3 = jnp.asarray(np.concatenate([e, e, e], axis=0), BF16)
    a = np.arange(SLAB)
    grp = 2 * (a // LANES) + (a % HEAD_DIM) // (HEAD_DIM // 2)
    bd = jnp.asarray((grp[:, None] == grp[None, :]).astype(np.float32) / HEAD_DIM, BF16)
    return e3, bd


_HALF = HEAD_DIM // 2


def _pair_q(a):
    lead = a.shape[:-1]
    n = len(lead)
    a = a.reshape(lead + (2, 2, Q_PER_KV, 2, _HALF))
    return a.transpose(tuple(range(n)) + (n + 2, n, n + 3, n + 1, n + 4)).reshape(lead + (Q_DIM,))


def _pair_k(a):
    lead = a.shape[:-1]
    n = len(lead)
    a = a.reshape(lead + (2, 2, 2, _HALF))
    return a.transpose(tuple(range(n)) + (n, n + 2, n + 1, n + 3)).reshape(lead + (KV_DIM,))


def _prep_layer(d_model, norm_mix, w_in, q_norm, k_norm, attn_sinks, conv_w, conv_b, dt_bias, a_log, d_skip,
                ssm_norm, w_attn_o, w_ssm_o, w_out, norm_mlp, w_up, w_down):
    assert KV_HEADS == 4 and Q_PER_KV == 4
    e3, bd = _constants()
    o_q, o_k, o_v, o_z, o_xbc, o_dt = (0, Q_DIM, Q_DIM + KV_DIM, Q_DIM + 2 * KV_DIM,
                                       Q_DIM + 2 * KV_DIM + D_INNER, Q_DIM + 2 * KV_DIM + D_INNER + CONV_DIM)
    o_g = o_dt + SSM_HEADS
    cut = lambda lo, width: w_in[:, lo:lo + width]
    pieces = {"z": cut(o_z, D_INNER), "xs": cut(o_xbc, D_INNER), "ga": cut(o_g, d_model),
              "gs": cut(o_g + d_model, d_model), "q": _pair_q(cut(o_q, Q_DIM)),
              "B": cut(o_xbc + D_INNER, BC_DIM), "C": cut(o_xbc + D_INNER + BC_DIM, BC_DIM),
              "k": _pair_k(cut(o_k, KV_DIM)), "v": cut(o_v, KV_DIM),
              "dt": jnp.pad(cut(o_dt, SSM_HEADS), ((0, 0), (0, DT_PAD - SSM_HEADS)))}
    lay, _ = _layout(d_model)
    w_p = jnp.concatenate([pieces[name] for name in lay], axis=1).astype(BF16)
    pad_heads = lambda a: jnp.pad(a, (0, DT_PAD - SSM_HEADS))[None, :]
    sink_p = attn_sinks.reshape(KV_HEADS, Q_PER_KV).T.reshape(N_HEADS)
    wa = w_attn_o.reshape(KV_HEADS, Q_PER_KV, HEAD_DIM, d_model).transpose(1, 0, 2, 3).reshape(Q_DIM, d_model)
    return dict(
        lay=lay, e3=e3, bd=bd, w_p=w_p,
        norm_mix=norm_mix[None, :], norm_mlp=norm_mlp[None, :],
        qg=_pair_q(jnp.tile(q_norm, N_HEADS))[None, :], kg=_pair_k(jnp.tile(k_norm, KV_HEADS))[None, :],
        sink_p=sink_p, sink_rows=jnp.broadcast_to(sink_p[:, None], (N_HEADS, LANES)),
        cw=conv_w, cb=conv_b[None, :], dtb=pad_heads(dt_bias), alog=pad_heads(a_log),
        dskip=jnp.repeat(d_skip, SSM_HEAD_DIM)[None, :], nw=ssm_norm[None, :],
        wa=wa.astype(BF16), ws=w_ssm_o.astype(BF16), wo=w_out.astype(BF16),
        wu=w_up.astype(BF16), wd=w_down.astype(BF16))


def _prompt_layer(x, lw):
    batch, seq, d = x.shape
    assert seq % BLK == 0 and seq >= WINDOW
    x2 = x.reshape(batch * seq, d)
    p_act = _in_proj(x2, lw["norm_mix"], lw["w_p"])
    cos, sin = _rope_tables(jnp.arange(seq))
    o_attn, k_last, v_last = _attn_prompt(p_act, lw["lay"], batch, seq, cos, sin, lw["qg"], lw["kg"],
                                          lw["bd"], lw["sink_p"])
    o_ssm, tail, h_fin = _ssd_prompt(p_act, lw["lay"], batch, seq, lw["cw"], lw["cb"], lw["dtb"], lw["alog"],
                                     lw["dskip"], lw["nw"], lw["e3"])
    y = _merge_mlp(x2, o_attn.reshape(batch * seq, Q_DIM), o_ssm, p_act, lw["lay"], lw["wa"], lw["ws"], lw["wo"],
                   lw["norm_mlp"], lw["wu"], lw["wd"])
    return (y.reshape(batch, seq, d),
            k_last.reshape(batch, KV_HEADS, HEAD_DIM, BLK).transpose(0, 3, 1, 2),
            v_last.reshape(batch, KV_HEADS, HEAD_DIM, BLK).transpose(0, 3, 1, 2),
            tail[:, SUBLANES - (CONV_W - 1):, :],
            h_fin.reshape(batch, SSM_HEADS, SSM_HEAD_DIM, D_STATE))


def _decode_layer(x, cache_k, cache_v, conv_state, ssm_state, lw):
    m, t, d = x.shape
    w = cache_k.shape[1]
    assert t == 1 and w == WINDOW and m % DEC_ROWS == 0
    x2 = x.reshape(m, d)
    p_act = _in_proj(x2, lw["norm_mix"], lw["w_p"])
    cos, sin = _rope_tables(PAST_LEN + jnp.arange(1))
    q, k, xc, xdt, dec, conv_new = _decode_pre(
        p_act, lw["lay"], conv_state.reshape(m, (CONV_W - 1) * CONV_DIM), cos, sin, lw["qg"], lw["kg"],
        lw["bd"], lw["cw"], lw["cb"], lw["dtb"], lw["alog"], lw["e3"])
    ck = jnp.transpose(cache_k, (0, 2, 3, 1)).reshape(m, KV_DIM, w)
    cv = jnp.transpose(cache_v, (0, 2, 3, 1)).reshape(m, KV_DIM, w)
    o_attn, ck_new, cv_new = _attn_decode(q, k, p_act, lw["lay"], ck, cv, lw["sink_rows"])
    h_new, o_ssm = _ssm_decode(ssm_state.reshape(m, D_INNER, D_STATE), xdt, dec, xc, p_act, lw["lay"],
                               lw["dskip"], lw["nw"])
    y = _merge_mlp(x2, o_attn, o_ssm.reshape(m, D_INNER), p_act, lw["lay"], lw["wa"], lw["ws"], lw["wo"],
                   lw["norm_mlp"], lw["wu"], lw["wd"])
    unview = lambda c: jnp.transpose(c.reshape(m, KV_HEADS, HEAD_DIM, w), (0, 3, 1, 2))
    return (y.reshape(m, 1, d), unview(ck_new), unview(cv_new), conv_new.reshape(m, CONV_W - 1, CONV_DIM),
            h_new.reshape(m, SSM_HEADS, SSM_HEAD_DIM, D_STATE))


def kernel(x_prompt, x_sample, cache_k, cache_v, state_conv, state_ssm, norm_mix, w_in, q_norm, k_norm,
           attn_sinks, conv_w, conv_b, dt_bias, a_log, d_skip, ssm_norm, w_attn_o, w_ssm_o, w_out,
           norm_mlp, w_up, w_down):
    depth = w_in.shape[0]
    d_model = x_prompt.shape[-1]
    yp, ys = x_prompt, x_sample
    cols = [[] for _ in range(8)]
    for l in range(depth):
        lw = _prep_layer(d_model, norm_mix[l], w_in[l], q_norm[l], k_norm[l], attn_sinks[l], conv_w[l],
                         conv_b[l], dt_bias[l], a_log[l], d_skip[l], ssm_norm[l], w_attn_o[l], w_ssm_o[l],
                         w_out[l], norm_mlp[l], w_up[l], w_down[l])
        yp, kp, vp, cp, hp = _prompt_layer(yp, lw)
        ys, ks, vs, cs, hs = _decode_layer(ys, cache_k[l], cache_v[l], state_conv[l], state_ssm[l], lw)
        for lst, val in zip(cols, (kp, vp, cp, hp, ks, vs, cs, hs)):
            lst.append(val)
    return (yp, ys) + tuple(jnp.stack(c) for c in cols)
```

```python
import functools

import numpy as np
import jax
import jax.numpy as jnp
from jax import lax
from jax.experimental import pallas as pl
from jax.experimental.pallas import tpu as pltpu

F32 = jnp.float32
BF16 = jnp.bfloat16

N_HEADS = 16
KV_HEADS = 4
HEAD_DIM = 64
Q_PER_KV = N_HEADS // KV_HEADS
WINDOW = 128
ROPE_THETA = 10000.0
Q_DIM = N_HEADS * HEAD_DIM
KV_DIM = KV_HEADS * HEAD_DIM
SSM_HEAD_DIM = 64
SSM_HEADS = 32
D_INNER = SSM_HEADS * SSM_HEAD_DIM
SSM_GROUPS = 4
HEADS_PER_GROUP = SSM_HEADS // SSM_GROUPS
D_STATE = 128
BC_DIM = SSM_GROUPS * D_STATE
CONV_W = 4
CONV_DIM = D_INNER + 2 * BC_DIM
SSD_CHUNK = 128
EPS = 1e-6
PAST_LEN = 8192
LOG2E = 1.4426950408889634

LANES = 128
SUBLANES = 8
VMEM_LIMIT = 48 * 1024 * 1024

BLK = 128
assert WINDOW == BLK and SSD_CHUNK == BLK
SLAB = KV_HEADS * HEAD_DIM
GROUP_LANES = HEADS_PER_GROUP * SSM_HEAD_DIM
DT_PAD = LANES


def _layout(d_model):
    segs = [("z", D_INNER), ("xs", D_INNER), ("ga", d_model), ("gs", d_model), ("q", Q_DIM),
            ("B", BC_DIM), ("C", BC_DIM), ("k", KV_DIM), ("v", KV_DIM), ("dt", DT_PAD)]
    lay, off = {}, 0
    for name, width in segs:
        assert off % width == 0, (name, off, width)
        lay[name] = (off, width)
        off += width
    return lay, off


def _col_tile(n, cap=3072):
    units = n // LANES
    best = 1
    for d in range(1, units + 1):
        if units % d == 0 and d * LANES <= cap:
            best = d
    return best * LANES


def _row_tile(m, cap):
    assert m % SUBLANES == 0
    best = SUBLANES
    for t in range(SUBLANES, min(m, cap) + 1, SUBLANES):
        if m % t == 0:
            best = t
    return best


def _cparams(sem):
    return pltpu.CompilerParams(dimension_semantics=sem, vmem_limit_bytes=VMEM_LIMIT)


def _sigmoid(x):
    return 0.5 + 0.5 * jnp.tanh(0.5 * x)


def _silu(x):
    h = 0.5 * x
    return h + h * jnp.tanh(h)


def _softplus(x):
    return jnp.maximum(x, 0.0) + jnp.log1p(jnp.exp(-jnp.abs(x)))


def _split3(a):
    hi = a.astype(BF16)
    r1 = a - hi.astype(F32)
    mid = r1.astype(BF16)
    lo = (r1 - mid.astype(F32)).astype(BF16)
    return hi, mid, lo


def _expand_heads(a, e3):
    hi, mid, lo = _split3(a)
    return jnp.dot(jnp.concatenate([hi, mid, lo], axis=1), e3, preferred_element_type=F32)


def _head_ms(x, bd):
    sq = x * x
    hi = sq.astype(BF16)
    lo = (sq - hi.astype(F32)).astype(BF16)
    outs = []
    for s in range(x.shape[1] // SLAB):
        sl = slice(s * SLAB, (s + 1) * SLAB)
        outs.append(jnp.dot(hi[:, sl], bd, preferred_element_type=F32)
                    + jnp.dot(lo[:, sl], bd, preferred_element_type=F32))
    return outs[0] if len(outs) == 1 else jnp.concatenate(outs, axis=1)


def _head_norm_rope(x, g, cos, sin, bd):
    xn = x * lax.rsqrt(_head_ms(x, bd) + EPS) * g
    tiles = []
    for t in range(x.shape[1] // LANES):
        xt = xn[:, t * LANES:(t + 1) * LANES]
        tiles.append(xt * cos + pltpu.roll(xt, LANES // 2, 1) * sin)
    return tiles[0] if len(tiles) == 1 else jnp.concatenate(tiles, axis=1)


def _unpair(x):
    q = HEAD_DIM // 2
    lane = lax.broadcasted_iota(jnp.int32, (x.shape[0], LANES), 1)
    tiles = []
    for t in range(x.shape[1] // LANES):
        xt = x[:, t * LANES:(t + 1) * LANES]
        nat = jnp.where((lane >= q) & (lane < 2 * q), pltpu.roll(xt, LANES - q, 1), xt)
        tiles.append(jnp.where((lane >= 2 * q) & (lane < 3 * q), pltpu.roll(xt, q, 1), nat))
    return tiles[0] if len(tiles) == 1 else jnp.concatenate(tiles, axis=1)


def _in_proj_kernel(x_ref, g_ref, w_ref, o_ref, xn_ref, *, tn):
    j = pl.program_id(1)

    @pl.when(j == 0)
    def _():
        x = x_ref[...]
        ms = jnp.mean(x * x, axis=-1, keepdims=True)
        xn_ref[...] = (x * lax.rsqrt(ms + EPS) * g_ref[...]).astype(BF16)

    for c in range(w_ref.shape[1] // tn):
        @pl.when(j == c)
        def _():
            o_ref[...] = jnp.dot(xn_ref[...], w_ref[:, c * tn:(c + 1) * tn], preferred_element_type=F32)


def _in_proj(x, g, w):
    m, k = x.shape
    n = w.shape[1]
    tm = _row_tile(m, 512)
    tn = _col_tile(n)
    return pl.pallas_call(
        functools.partial(_in_proj_kernel, tn=tn),
        grid=(m // tm, n // tn),
        in_specs=[pl.BlockSpec((tm, k), lambda i, j: (i, 0)),
                  pl.BlockSpec((1, k), lambda i, j: (0, 0)),
                  pl.BlockSpec((k, n), lambda i, j: (0, 0), pipeline_mode=pl.Buffered(1))],
        out_specs=pl.BlockSpec((tm, tn), lambda i, j: (i, j)),
        out_shape=jax.ShapeDtypeStruct((m, n), F32),
        scratch_shapes=[pltpu.VMEM((tm, k), BF16)],
        compiler_params=_cparams(("arbitrary", "arbitrary")),
        name="in_proj",
    )(x, g, w)


def _attn_prompt_kernel(q_ref, k_ref, v_ref, cos_ref, sin_ref, qg_ref, kg_ref, bd_ref, sink_ref,
                        o_ref, ko_ref, vo_ref, kbuf, vbuf, probs):
    i = pl.program_id(1)
    last = pl.num_programs(1) - 1

    @pl.when(i == 0)
    def _():
        kbuf[...] = jnp.zeros_like(kbuf)
        vbuf[...] = jnp.zeros_like(vbuf)

    cos, sin, bd = cos_ref[...], sin_ref[...], bd_ref[...]
    slot = i % 2
    lane = lax.broadcasted_iota(jnp.int32, (BLK, SLAB), 1)
    grp_k = 2 * (lane // LANES) + (lane % HEAD_DIM) // (HEAD_DIM // 2)
    grp_v = lane // HEAD_DIM
    r = lax.broadcasted_iota(jnp.int32, (BLK, 2 * BLK), 0)
    c = lax.broadcasted_iota(jnp.int32, (BLK, 2 * BLK), 1)
    key = c % BLK
    is_cur = (c // BLK) == slot
    prev_ok = jnp.where(i > 0, key, -1)
    mask = jnp.where(is_cur, (key <= r).astype(jnp.int32), (prev_ok > r + (BLK - WINDOW)).astype(jnp.int32)) > 0

    seqs = range(q_ref.shape[0])
    ks, vs, scores = [], [], []
    for u in seqs:
        q = _head_norm_rope(q_ref[u], qg_ref[...], cos, sin, bd) * (HEAD_DIM ** -0.5 * LOG2E)
        k = _head_norm_rope(k_ref[u], kg_ref[...], cos, sin, bd)
        v = v_ref[u]
        ks.append(k)
        vs.append(v)
        for p in range(KV_HEADS):
            rows = pl.ds(pl.multiple_of(p * 2 * BLK + slot * BLK, BLK), BLK)
            kbuf[u, rows, :] = jnp.where(grp_k == p, k, 0.0).astype(BF16)
            vbuf[u, rows, :] = jnp.where(grp_v == p, v, 0.0).astype(BF16)
        q_stack = jnp.concatenate([q[:, j * SLAB:(j + 1) * SLAB] for j in range(Q_PER_KV)],
                                  axis=0).astype(BF16)
        scores.append(lax.dot_general(q_stack, kbuf[u], (((1,), (1,)), ((), ())),
                                      preferred_element_type=F32))
    for u in seqs:
        for j in range(Q_PER_KV):
            for p in range(KV_HEADS):
                s = jnp.where(mask, scores[u][j * BLK:(j + 1) * BLK, p * 2 * BLK:(p + 1) * 2 * BLK], -jnp.inf)
                sink = sink_ref[j * KV_HEADS + p] * LOG2E
                mx = jnp.maximum(jnp.max(s, axis=-1, keepdims=True), sink)
                e = jnp.exp2(s - mx)
                den = jnp.sum(e, axis=-1, keepdims=True) + jnp.exp2(sink - mx)
                probs[u, j * BLK:(j + 1) * BLK, p * 2 * BLK:(p + 1) * 2 * BLK] = (e / den).astype(BF16)
    for u in seqs:
        pv = jnp.dot(probs[u], vbuf[u], preferred_element_type=F32)
        for j in range(Q_PER_KV):
            o_ref[u, :, j * SLAB:(j + 1) * SLAB] = pv[j * BLK:(j + 1) * BLK, :].astype(o_ref.dtype)

    @pl.when(i == last)
    def _():
        for u in seqs:
            ko_ref[u] = _unpair(ks[u]).T
            vo_ref[u] = vs[u].T


ATTN_SEQS = 2


def _attn_prompt(p_act, lay, batch, seq, cos, sin, qg, kg, bd, sinks):
    nb = seq // BLK
    u = ATTN_SEQS if batch % ATTN_SEQS == 0 else 1
    qc, kc, vc = lay["q"][0] // Q_DIM, lay["k"][0] // KV_DIM, lay["v"][0] // KV_DIM
    p3 = p_act.reshape(batch, seq, p_act.shape[1])
    return pl.pallas_call(
        _attn_prompt_kernel,
        grid=(batch // u, nb),
        in_specs=[pl.BlockSpec((u, BLK, Q_DIM), lambda b, i: (b, i, qc)),
                  pl.BlockSpec((u, BLK, KV_DIM), lambda b, i: (b, i, kc)),
                  pl.BlockSpec((u, BLK, KV_DIM), lambda b, i: (b, i, vc)),
                  pl.BlockSpec((BLK, LANES), lambda b, i: (i, 0)),
                  pl.BlockSpec((BLK, LANES), lambda b, i: (i, 0)),
                  pl.BlockSpec((1, Q_DIM), lambda b, i: (0, 0)),
                  pl.BlockSpec((1, KV_DIM), lambda b, i: (0, 0)),
                  pl.BlockSpec((SLAB, SLAB), lambda b, i: (0, 0)),
                  pl.BlockSpec(memory_space=pltpu.SMEM)],
        out_specs=[pl.BlockSpec((u, BLK, Q_DIM), lambda b, i: (b, i, 0)),
                   pl.BlockSpec((u, KV_DIM, BLK), lambda b, i: (b, 0, 0)),
                   pl.BlockSpec((u, KV_DIM, BLK), lambda b, i: (b, 0, 0))],
        out_shape=[jax.ShapeDtypeStruct((batch, seq, Q_DIM), BF16),
                   jax.ShapeDtypeStruct((batch, KV_DIM, BLK), F32),
                   jax.ShapeDtypeStruct((batch, KV_DIM, BLK), F32)],
        scratch_shapes=[pltpu.VMEM((u, KV_HEADS * 2 * BLK, KV_DIM), BF16),
                        pltpu.VMEM((u, KV_HEADS * 2 * BLK, KV_DIM), BF16),
                        pltpu.VMEM((u, Q_PER_KV * BLK, KV_HEADS * 2 * BLK), BF16)],
        compiler_params=_cparams(("arbitrary", "arbitrary")),
        name="attn_prompt",
    )(p3, p3, p3, cos, sin, qg, kg, bd, sinks)


def _ssd_prompt_kernel(z_ref, xs_ref, b_ref, c_ref, dt_ref, cw_ref, cb_ref, dtb_ref, alog_ref, dskip_ref,
                       nw_ref, e3_ref, o_ref, tail_ref, hfin_ref, xpad, tails, st, ybuf):
    i = pl.program_id(1)
    last = pl.num_programs(1) - 1

    @pl.when(i == 0)
    def _():
        tails[...] = jnp.zeros_like(tails)
        st[...] = jnp.zeros_like(st)

    slot = i % 2
    xpad[0:SUBLANES, :] = tails[1 - slot]
    xpad[SUBLANES:SUBLANES + BLK, 0:D_INNER] = xs_ref[...]
    xpad[SUBLANES:SUBLANES + BLK, D_INNER:D_INNER + BC_DIM] = b_ref[...]
    xpad[SUBLANES:SUBLANES + BLK, D_INNER + BC_DIM:CONV_DIM] = c_ref[...]
    cwh = 0.5 * cw_ref[...]
    acc = 0.5 * cb_ref[...] + cwh[CONV_W - 1:CONV_W, :] * xpad[SUBLANES:SUBLANES + BLK, :]
    for t in range(1, CONV_W):
        acc = acc + cwh[CONV_W - 1 - t:CONV_W - t, :] * xpad[SUBLANES - t:SUBLANES - t + BLK, :]
    xc = acc + acc * jnp.tanh(acc)
    new_tail = xpad[BLK:BLK + SUBLANES, :]
    tail_ref[...] = new_tail
    tails[slot] = new_tail

    xs = xc[:, 0:D_INNER]
    bm = xc[:, D_INNER:D_INNER + BC_DIM].astype(BF16)
    cm = xc[:, D_INNER + BC_DIM:CONV_DIM].astype(BF16)

    e3 = e3_ref[...]
    dt = _softplus(dt_ref[...] + dtb_ref[...])
    dta = dt * (-LOG2E * jnp.exp(alog_ref[...]))
    row = lax.broadcasted_iota(jnp.int32, (BLK, BLK), 0)
    col = lax.broadcasted_iota(jnp.int32, (BLK, BLK), 1)
    causal = row >= col
    cum = jnp.dot(causal.astype(F32), dta, preferred_element_type=F32, precision=lax.Precision.HIGHEST)
    cum_t = cum.T
    ecum = jnp.exp2(cum)
    to_end = jnp.exp2(cum[BLK - 1:BLK, :] - cum) * dt
    dt_e = _expand_heads(dt, e3)
    ecum_e = _expand_heads(ecum, e3)
    to_end_e = _expand_heads(to_end, e3)
    xdt = (xs * dt_e).astype(BF16)
    xte = (xs * to_end_e).astype(BF16)
    lane = lax.broadcasted_iota(jnp.int32, (BLK, LANES), 1)
    first_head = lane < SSM_HEAD_DIM

    for g in range(SSM_GROUPS):
        gl = slice(g * GROUP_LANES, (g + 1) * GROUP_LANES)
        bg = bm[:, g * D_STATE:(g + 1) * D_STATE]
        cg = cm[:, g * D_STATE:(g + 1) * D_STATE]
        cbg = lax.dot_general(cg, bg, (((1,), (1,)), ((), ())), preferred_element_type=F32)
        st_g = st[1 - slot, :, gl]
        y_inter = jnp.dot(cg, st_g.astype(BF16), preferred_element_type=F32) * ecum_e[:, gl]
        for pr in range(HEADS_PER_GROUP // 2):
            h0 = g * HEADS_PER_GROUP + 2 * pr
            xd = xdt[:, h0 * SSM_HEAD_DIM:(h0 + 2) * SSM_HEAD_DIM]
            ys = []
            for h in (h0, h0 + 1):
                diff = cum[:, h:h + 1] - cum_t[h:h + 1, :]
                w = (jnp.exp2(jnp.where(causal, diff, -jnp.inf)) * cbg).astype(BF16)
                ys.append(jnp.dot(w, xd, preferred_element_type=F32))
            lo = pr * LANES
            ybuf[:, h0 * SSM_HEAD_DIM:(h0 + 2) * SSM_HEAD_DIM] = (
                jnp.where(first_head, ys[0], ys[1]) + y_inter[:, lo:lo + LANES])
        upd = lax.dot_general(bg, xte[:, gl], (((0,), (0,)), ((), ())), preferred_element_type=F32)
        st[slot, :, gl] = st_g * ecum_e[BLK - 1:BLK, gl] + upd

    y = ybuf[...] + dskip_ref[...] * xs
    yz = y * _silu(z_ref[...])
    ms = jnp.mean(yz * yz, axis=-1, keepdims=True)
    o_ref[...] = (yz * lax.rsqrt(ms + EPS) * nw_ref[...]).astype(o_ref.dtype)

    @pl.when(i == last)
    def _():
        hfin_ref[...] = st[slot].T


def _ssd_prompt(p_act, lay, batch, seq, cw, cb, dtb, alog, dskip, nw, e3):
    nb = seq // BLK
    m = batch * seq
    zc, xc = lay["z"][0] // D_INNER, lay["xs"][0] // D_INNER
    bc, cc, dc = lay["B"][0] // BC_DIM, lay["C"][0] // BC_DIM, lay["dt"][0] // DT_PAD
    const = lambda b, i: (0, 0)
    return pl.pallas_call(
        _ssd_prompt_kernel,
        grid=(batch, nb),
        in_specs=[pl.BlockSpec((BLK, D_INNER), lambda b, i: (b * nb + i, zc)),
                  pl.BlockSpec((BLK, D_INNER), lambda b, i: (b * nb + i, xc)),
                  pl.BlockSpec((BLK, BC_DIM), lambda b, i: (b * nb + i, bc)),
                  pl.BlockSpec((BLK, BC_DIM), lambda b, i: (b * nb + i, cc)),
                  pl.BlockSpec((BLK, DT_PAD), lambda b, i: (b * nb + i, dc)),
                  pl.BlockSpec((CONV_W, CONV_DIM), const),
                  pl.BlockSpec((1, CONV_DIM), const),
                  pl.BlockSpec((1, DT_PAD), const),
                  pl.BlockSpec((1, DT_PAD), const),
                  pl.BlockSpec((1, D_INNER), const),
                  pl.BlockSpec((1, D_INNER), const),
                  pl.BlockSpec((3 * DT_PAD, D_INNER), const)],
        out_specs=[pl.BlockSpec((BLK, D_INNER), lambda b, i: (b * nb + i, 0)),
                   pl.BlockSpec((None, SUBLANES, CONV_DIM), lambda b, i: (b, 0, 0)),
                   pl.BlockSpec((None, D_INNER, D_STATE), lambda b, i: (b, 0, 0))],
        out_shape=[jax.ShapeDtypeStruct((m, D_INNER), BF16),
                   jax.ShapeDtypeStruct((batch, SUBLANES, CONV_DIM), F32),
                   jax.ShapeDtypeStruct((batch, D_INNER, D_STATE), F32)],
        scratch_shapes=[pltpu.VMEM((BLK + SUBLANES, CONV_DIM), F32),
                        pltpu.VMEM((2, SUBLANES, CONV_DIM), F32),
                        pltpu.VMEM((2, D_STATE, D_INNER), F32),
                        pltpu.VMEM((BLK, D_INNER), F32)],
        compiler_params=_cparams(("arbitrary", "arbitrary")),
        name="ssd_prompt",
    )(p_act, p_act, p_act, p_act, p_act, cw, cb, dtb, alog, dskip, nw, e3)


def _decode_pre_kernel(q_ref, k_ref, xs_ref, b_ref, c_ref, dt_ref, cs_ref, cos_ref, sin_ref, qg_ref, kg_ref,
                       bd_ref, cw_ref, cb_ref, dtb_ref, alog_ref, e3_ref,
                       qo_ref, ko_ref, xc_ref, xdt_ref, dec_ref, cso_ref):
    cos, sin, bd = cos_ref[...], sin_ref[...], bd_ref[...]
    qo_ref[...] = _unpair(_head_norm_rope(q_ref[...], qg_ref[...], cos, sin, bd)) * (HEAD_DIM ** -0.5)
    ko_ref[...] = _unpair(_head_norm_rope(k_ref[...], kg_ref[...], cos, sin, bd))

    segs = ((xs_ref, 0, D_INNER), (b_ref, D_INNER, BC_DIM), (c_ref, D_INNER + BC_DIM, BC_DIM))
    for ref, off, width in segs:
        new = ref[...]
        acc = cb_ref[:, off:off + width] + cw_ref[CONV_W - 1:CONV_W, off:off + width] * new
        for t in range(CONV_W - 1):
            lo = t * CONV_DIM + off
            acc = acc + cw_ref[t:t + 1, off:off + width] * cs_ref[:, lo:lo + width]
        xc_ref[:, off:off + width] = _silu(acc)
        for t in range(CONV_W - 2):
            cso_ref[:, t * CONV_DIM + off:t * CONV_DIM + off + width] = (
                cs_ref[:, (t + 1) * CONV_DIM + off:(t + 1) * CONV_DIM + off + width])
        lo = (CONV_W - 2) * CONV_DIM + off
        cso_ref[:, lo:lo + width] = new

    e3 = e3_ref[...]
    dt = _softplus(dt_ref[...] + dtb_ref[...])
    decay = jnp.exp(dt * (-jnp.exp(alog_ref[...])))
    xdt_ref[...] = xc_ref[:, 0:D_INNER] * _expand_heads(dt, e3)
    dec_ref[...] = _expand_heads(decay, e3)


def _decode_pre(p_act, lay, conv_state, cos, sin, qg, kg, bd, cw, cb, dtb, alog, e3):
    m = p_act.shape[0]
    full = lambda shape: pl.BlockSpec(shape, lambda i: (0, 0))

    def col(name):
        c = lay[name][0] // lay[name][1]
        return pl.BlockSpec((m, lay[name][1]), lambda i: (0, c))

    cs_w = (CONV_W - 1) * CONV_DIM
    return pl.pallas_call(
        _decode_pre_kernel,
        grid=(1,),
        in_specs=[col("q"), col("k"), col("xs"), col("B"), col("C"), col("dt"),
                  full((m, cs_w)), full((1, LANES)), full((1, LANES)), full((1, Q_DIM)), full((1, KV_DIM)),
                  full((SLAB, SLAB)), full((CONV_W, CONV_DIM)), full((1, CONV_DIM)), full((1, DT_PAD)),
                  full((1, DT_PAD)), full((3 * DT_PAD, D_INNER))],
        out_specs=[full((m, Q_DIM)), full((m, KV_DIM)), full((m, CONV_DIM)), full((m, D_INNER)),
                   full((m, D_INNER)), full((m, cs_w))],
        out_shape=[jax.ShapeDtypeStruct((m, Q_DIM), F32), jax.ShapeDtypeStruct((m, KV_DIM), F32),
                   jax.ShapeDtypeStruct((m, CONV_DIM), F32), jax.ShapeDtypeStruct((m, D_INNER), F32),
                   jax.ShapeDtypeStruct((m, D_INNER), F32), jax.ShapeDtypeStruct((m, cs_w), F32)],
        compiler_params=_cparams(("arbitrary",)),
        name="decode_pre",
    )(p_act, p_act, p_act, p_act, p_act, p_act, conv_state, cos, sin, qg, kg, bd, cw, cb, dtb, alog, e3)


DEC_ROWS = 8


def _attn_decode_kernel(q_ref, k_ref, v_ref, ck_ref, cv_ref, sink_ref, o_ref, cko_ref, cvo_ref):
    w = ck_ref.shape[2]
    nq = N_HEADS
    r = lax.broadcasted_iota(jnp.int32, (nq, SLAB), 0)
    grp = lax.broadcasted_iota(jnp.int32, (nq, SLAB), 1) // HEAD_DIM
    own = grp == (r % KV_HEADS)
    in_window = lax.broadcasted_iota(jnp.int32, (nq, w), 1) > (w - WINDOW - 1)
    newest = lax.broadcasted_iota(jnp.int32, (KV_DIM, w), 1) == w - 1
    sink = sink_ref[...][:, 0:1]
    pad = jnp.zeros((DEC_ROWS, KV_DIM), BF16)
    k_parts = jnp.concatenate(list(_split3(k_ref[...])) + [pad], axis=0)
    v_parts = jnp.concatenate(list(_split3(v_ref[...])) + [pad], axis=0)
    part_row = lax.broadcasted_iota(jnp.int32, (4 * DEC_ROWS, w), 0) % DEC_ROWS
    tdims = (((0,), (0,)), ((), ()))
    for bl in range(DEC_ROWS):
        qrow = q_ref[bl:bl + 1, :]
        qm = jnp.zeros((nq, SLAB), F32)
        for j in range(Q_PER_KV):
            slab = jnp.broadcast_to(qrow[:, j * SLAB:(j + 1) * SLAB], (nq, SLAB))
            qm = jnp.where((r // KV_HEADS) == j, slab, qm)
        qm = jnp.where(own, qm, 0.0).astype(BF16)
        pick = jnp.where(part_row == bl, 1.0, 0.0).astype(BF16)
        k_col = lax.dot_general(k_parts, pick, tdims, preferred_element_type=F32)
        v_col = lax.dot_general(v_parts, pick, tdims, preferred_element_type=F32)
        keys = jnp.where(newest, k_col, pltpu.roll(ck_ref[bl], w - 1, 1))
        vals = jnp.where(newest, v_col, pltpu.roll(cv_ref[bl], w - 1, 1))
        cko_ref[bl] = keys
        cvo_ref[bl] = vals
        s = jnp.dot(qm, keys.astype(BF16), preferred_element_type=F32)
        s = jnp.where(in_window, s, -jnp.inf)
        mx = jnp.maximum(jnp.max(s, axis=-1, keepdims=True), sink)
        e = jnp.exp(s - mx)
        den = jnp.sum(e, axis=-1, keepdims=True) + jnp.exp(sink - mx)
        pv = lax.dot_general(e.astype(BF16), vals.astype(BF16), (((1,), (1,)), ((), ())),
                             preferred_element_type=F32)
        pv = jnp.where(own, pv / den, 0.0)
        for j in range(Q_PER_KV):
            o_ref[bl:bl + 1, j * SLAB:(j + 1) * SLAB] = jnp.sum(
                pv[j * KV_HEADS:(j + 1) * KV_HEADS, :], axis=0, keepdims=True)


def _attn_decode(q, k, p_act, lay, cache_k, cache_v, sink_rows):
    m = q.shape[0]
    w = cache_k.shape[2]
    vc = lay["v"][0] // KV_DIM
    cache_spec = pl.BlockSpec((DEC_ROWS, KV_DIM, w), lambda i: (i, 0, 0))
    return pl.pallas_call(
        _attn_decode_kernel,
        grid=(m // DEC_ROWS,),
        in_specs=[pl.BlockSpec((DEC_ROWS, Q_DIM), lambda i: (i, 0)),
                  pl.BlockSpec((DEC_ROWS, KV_DIM), lambda i: (i, 0)),
                  pl.BlockSpec((DEC_ROWS, KV_DIM), lambda i: (i, vc)),
                  cache_spec, cache_spec,
                  pl.BlockSpec((N_HEADS, LANES), lambda i: (0, 0))],
        out_specs=[pl.BlockSpec((DEC_ROWS, Q_DIM), lambda i: (i, 0)), cache_spec, cache_spec],
        out_shape=[jax.ShapeDtypeStruct((m, Q_DIM), F32),
                   jax.ShapeDtypeStruct((m, KV_DIM, w), F32),
                   jax.ShapeDtypeStruct((m, KV_DIM, w), F32)],
        compiler_params=_cparams(("arbitrary",)),
        name="attn_decode",
    )(q, k, p_act, cache_k, cache_v, sink_rows)


MM_ROWS = 16


def _ssm_decode_kernel(st_ref, xdt_ref, dec_ref, xc_ref, z_ref, dskip_ref, nw_ref, sto_ref, o_ref):
    r = lax.broadcasted_iota(jnp.int32, (MM_ROWS, D_INNER), 0)
    grp = lax.broadcasted_iota(jnp.int32, (MM_ROWS, D_INNER), 1) // GROUP_LANES
    rr = lax.broadcasted_iota(jnp.int32, (MM_ROWS, D_STATE), 0)
    ones_rows = jnp.where((rr >= SSM_GROUPS) & (rr < SSM_GROUPS + 3), 1.0, 0.0)
    bc = lambda a: jnp.broadcast_to(a, (MM_ROWS, D_INNER))
    for u in range(st_ref.shape[0]):
        h = st_ref[u]
        xdt = xdt_ref[u]
        xc = xc_ref[u]
        hi, mid, lo = _split3(dec_ref[u])
        lhs_t = jnp.where(r == grp, bc(xdt), 0.0)
        for t, piece in enumerate((hi, mid, lo)):
            lhs_t = jnp.where(r == SSM_GROUPS + t, bc(piece.astype(F32)), lhs_t)
        lhs_t = lhs_t.astype(BF16)
        b_rows = jnp.zeros((MM_ROWS, D_STATE), F32)
        c_rows = jnp.zeros((MM_ROWS, D_STATE), F32)
        for g in range(SSM_GROUPS):
            b_g = xc[:, D_INNER + g * D_STATE:D_INNER + (g + 1) * D_STATE]
            c_g = xc[:, D_INNER + BC_DIM + g * D_STATE:D_INNER + BC_DIM + (g + 1) * D_STATE]
            b_rows = jnp.where(rr == g, jnp.broadcast_to(b_g, (MM_ROWS, D_STATE)), b_rows)
            c_rows = jnp.where(rr == g, jnp.broadcast_to(c_g, (MM_ROWS, D_STATE)), c_rows)
        rhs = jnp.concatenate([b_rows, ones_rows], axis=1).astype(BF16)
        both = lax.dot_general(lhs_t, rhs, (((0,), (0,)), ((), ())), preferred_element_type=F32)
        h_new = both[:, D_STATE:] * h + both[:, :D_STATE]
        sto_ref[u] = h_new
        yg = lax.dot_general(c_rows.astype(BF16), h_new.astype(BF16), (((1,), (1,)), ((), ())),
                             preferred_element_type=F32)
        y = jnp.sum(jnp.where(r == grp, yg, 0.0), axis=0, keepdims=True)
        y = y + dskip_ref[...] * xc[:, 0:D_INNER]
        yz = y * _silu(z_ref[u])
        ms = jnp.mean(yz * yz, axis=-1, keepdims=True)
        o_ref[u] = yz * lax.rsqrt(ms + EPS) * nw_ref[...]


SSM_DEC_ROWS = 4


def _ssm_decode(state, xdt, dec, xc, p_act, lay, dskip, nw):
    m = state.shape[0]
    n = p_act.shape[1]
    u = SSM_DEC_ROWS
    assert m % u == 0
    zc = lay["z"][0] // D_INNER
    row = lambda width: pl.BlockSpec((u, 1, width), lambda b: (b, 0, 0))
    return pl.pallas_call(
        _ssm_decode_kernel,
        grid=(m // u,),
        in_specs=[pl.BlockSpec((u, D_INNER, D_STATE), lambda b: (b, 0, 0)),
                  row(D_INNER), row(D_INNER), row(CONV_DIM),
                  pl.BlockSpec((u, 1, D_INNER), lambda b: (b, 0, zc)),
                  pl.BlockSpec((1, D_INNER), lambda b: (0, 0)),
                  pl.BlockSpec((1, D_INNER), lambda b: (0, 0))],
        out_specs=[pl.BlockSpec((u, D_INNER, D_STATE), lambda b: (b, 0, 0)), row(D_INNER)],
        out_shape=[jax.ShapeDtypeStruct((m, D_INNER, D_STATE), F32),
                   jax.ShapeDtypeStruct((m, 1, D_INNER), F32)],
        compiler_params=_cparams(("arbitrary",)),
        name="ssm_decode",
    )(state, xdt.reshape(m, 1, D_INNER), dec.reshape(m, 1, D_INNER), xc.reshape(m, 1, CONV_DIM),
      p_act.reshape(m, 1, n), dskip, nw)


def _merge_mlp_kernel(x_ref, oa_ref, os_ref, ga_ref, gs_ref, wa_ref, ws_ref, wo_ref, g_ref, wu_ref, wd_ref,
                      o_ref, *, ff_tile):
    a = jnp.dot(oa_ref[...].astype(BF16), wa_ref[...], preferred_element_type=F32)
    s = jnp.dot(os_ref[...].astype(BF16), ws_ref[...], preferred_element_type=F32)
    mixed = _sigmoid(ga_ref[...]) * a + _sigmoid(gs_ref[...]) * s
    x = x_ref[...] + jnp.dot(mixed.astype(BF16), wo_ref[...], preferred_element_type=F32)
    ms = jnp.mean(x * x, axis=-1, keepdims=True)
    xn = (x * lax.rsqrt(ms + EPS) * g_ref[...]).astype(BF16)
    acc = x
    for c in range(wu_ref.shape[1] // ff_tile):
        u = jnp.dot(xn, wu_ref[:, c * ff_tile:(c + 1) * ff_tile], preferred_element_type=F32)
        act = jnp.square(jnp.maximum(u, 0.0)).astype(BF16)
        acc = acc + jnp.dot(act, wd_ref[c * ff_tile:(c + 1) * ff_tile, :], preferred_element_type=F32)
    o_ref[...] = acc


def _merge_mlp(x, o_attn, o_ssm, p_act, lay, wa, ws, wo, g, wu, wd):
    m, d = x.shape
    ff = wu.shape[1]
    tm = _row_tile(m, 256)
    gac, gsc = lay["ga"][0] // d, lay["gs"][0] // d
    weight = lambda shape: pl.BlockSpec(shape, lambda i: (0, 0), pipeline_mode=pl.Buffered(1))
    return pl.pallas_call(
        functools.partial(_merge_mlp_kernel, ff_tile=min(1024, ff)),
        grid=(m // tm,),
        in_specs=[pl.BlockSpec((tm, d), lambda i: (i, 0)),
                  pl.BlockSpec((tm, Q_DIM), lambda i: (i, 0)),
                  pl.BlockSpec((tm, D_INNER), lambda i: (i, 0)),
                  pl.BlockSpec((tm, d), lambda i: (i, gac)),
                  pl.BlockSpec((tm, d), lambda i: (i, gsc)),
                  weight((Q_DIM, d)), weight((D_INNER, d)), weight((d, d)),
                  weight((1, d)), weight((d, ff)), weight((ff, d))],
        out_specs=pl.BlockSpec((tm, d), lambda i: (i, 0)),
        out_shape=jax.ShapeDtypeStruct((m, d), F32),
        compiler_params=_cparams(("arbitrary",)),
        name="merge_mlp",
    )(x, o_attn, o_ssm, p_act, p_act, wa, ws, wo, g, wu, wd)


def _rope_tables(pos):
    half = HEAD_DIM // 2
    inv = ROPE_THETA ** (-jnp.arange(half, dtype=F32) / half)
    ang = pos.astype(F32)[:, None] * inv[None, :]
    cos, sin = jnp.cos(ang), jnp.sin(ang)
    return (jnp.concatenate([cos, cos, cos, cos], axis=1),
            jnp.concatenate([-sin, -sin, sin, sin], axis=1))


def _constants():
    heads = np.arange(DT_PAD)[:, None]
    cols = np.arange(D_INNER)[None, :] // SSM_HEAD_DIM
    e = (heads == cols).astype(np.float32)
    e3 = jnp.asarray(np.concatenate([e, e, e], axis=0), BF16)
    a = np.arange(SLAB)
    grp = 2 * (a // LANES) + (a % HEAD_DIM) // (HEAD_DIM // 2)
    bd = jnp.asarray((grp[:, None] == grp[None, :]).astype(np.float32) / HEAD_DIM, BF16)
    return e3, bd


_HALF = HEAD_DIM // 2


def _pair_q(a):
    lead = a.shape[:-1]
    n = len(lead)
    a = a.reshape(lead + (2, 2, Q_PER_KV, 2, _HALF))
    return a.transpose(tuple(range(n)) + (n + 2, n, n + 3, n + 1, n + 4)).reshape(lead + (Q_DIM,))


def _pair_k(a):
    lead = a.shape[:-1]
    n = len(lead)
    a = a.reshape(lead + (2, 2, 2, _HALF))
    return a.transpose(tuple(range(n)) + (n, n + 2, n + 1, n + 3)).reshape(lead + (KV_DIM,))


def _prep_layer(d_model, norm_mix, w_in, q_norm, k_norm, attn_sinks, conv_w, conv_b, dt_bias, a_log, d_skip,
                ssm_norm, w_attn_o, w_ssm_o, w_out, norm_mlp, w_up, w_down):
    assert KV_HEADS == 4 and Q_PER_KV == 4
    e3, bd = _constants()
    o_q, o_k, o_v, o_z, o_xbc, o_dt = (0, Q_DIM, Q_DIM + KV_DIM, Q_DIM + 2 * KV_DIM,
                                       Q_DIM + 2 * KV_DIM + D_INNER, Q_DIM + 2 * KV_DIM + D_INNER + CONV_DIM)
    o_g = o_dt + SSM_HEADS
    cut = lambda lo, width: w_in[:, lo:lo + width]
    pieces = {"z": cut(o_z, D_INNER), "xs": cut(o_xbc, D_INNER), "ga": cut(o_g, d_model),
              "gs": cut(o_g + d_model, d_model), "q": _pair_q(cut(o_q, Q_DIM)),
              "B": cut(o_xbc + D_INNER, BC_DIM), "C": cut(o_xbc + D_INNER + BC_DIM, BC_DIM),
              "k": _pair_k(cut(o_k, KV_DIM)), "v": cut(o_v, KV_DIM),
              "dt": jnp.pad(cut(o_dt, SSM_HEADS), ((0, 0), (0, DT_PAD - SSM_HEADS)))}
    lay, _ = _layout(d_model)
    w_p = jnp.concatenate([pieces[name] for name in lay], axis=1).astype(BF16)
    pad_heads = lambda a: jnp.pad(a, (0, DT_PAD - SSM_HEADS))[None, :]
    sink_p = attn_sinks.reshape(KV_HEADS, Q_PER_KV).T.reshape(N_HEADS)
    wa = w_attn_o.reshape(KV_HEADS, Q_PER_KV, HEAD_DIM, d_model).transpose(1, 0, 2, 3).reshape(Q_DIM, d_model)
    return dict(
        lay=lay, e3=e3, bd=bd, w_p=w_p,
        norm_mix=norm_mix[None, :], norm_mlp=norm_mlp[None, :],
        qg=_pair_q(jnp.tile(q_norm, N_HEADS))[None, :], kg=_pair_k(jnp.tile(k_norm, KV_HEADS))[None, :],
        sink_p=sink_p, sink_rows=jnp.broadcast_to(sink_p[:, None], (N_HEADS, LANES)),
        cw=conv_w, cb=conv_b[None, :], dtb=pad_heads(dt_bias), alog=pad_heads(a_log),
        dskip=jnp.repeat(d_skip, SSM_HEAD_DIM)[None, :], nw=ssm_norm[None, :],
        wa=wa.astype(BF16), ws=w_ssm_o.astype(BF16), wo=w_out.astype(BF16),
        wu=w_up.astype(BF16), wd=w_down.astype(BF16))


def _prompt_layer(x, lw):
    batch, seq, d = x.shape
    assert seq % BLK == 0 and seq >= WINDOW
    x2 = x.reshape(batch * seq, d)
    p_act = _in_proj(x2, lw["norm_mix"], lw["w_p"])
    cos, sin = _rope_tables(jnp.arange(seq))
    o_attn, k_last, v_last = _attn_prompt(p_act, lw["lay"], batch, seq, cos, sin, lw["qg"], lw["kg"],
                                          lw["bd"], lw["sink_p"])
    o_ssm, tail, h_fin = _ssd_prompt(p_act, lw["lay"], batch, seq, lw["cw"], lw["cb"], lw["dtb"], lw["alog"],
                                     lw["dskip"], lw["nw"], lw["e3"])
    y = _merge_mlp(x2, o_attn.reshape(batch * seq, Q_DIM), o_ssm, p_act, lw["lay"], lw["wa"], lw["ws"], lw["wo"],
                   lw["norm_mlp"], lw["wu"], lw["wd"])
    return (y.reshape(batch, seq, d),
            k_last.reshape(batch, KV_HEADS, HEAD_DIM, BLK).transpose(0, 3, 1, 2),
            v_last.reshape(batch, KV_HEADS, HEAD_DIM, BLK).transpose(0, 3, 1, 2),
            tail[:, SUBLANES - (CONV_W - 1):, :],
            h_fin.reshape(batch, SSM_HEADS, SSM_HEAD_DIM, D_STATE))


def _decode_layer(x, cache_k, cache_v, conv_state, ssm_state, lw):
    m, t, d = x.shape
    w = cache_k.shape[1]
    assert t == 1 and w == WINDOW and m % DEC_ROWS == 0
    x2 = x.reshape(m, d)
    p_act = _in_proj(x2, lw["norm_mix"], lw["w_p"])
    cos, sin = _rope_tables(PAST_LEN + jnp.arange(1))
    q, k, xc, xdt, dec, conv_new = _decode_pre(
        p_act, lw["lay"], conv_state.reshape(m, (CONV_W - 1) * CONV_DIM), cos, sin, lw["qg"], lw["kg"],
        lw["bd"], lw["cw"], lw["cb"], lw["dtb"], lw["alog"], lw["e3"])
    ck = jnp.transpose(cache_k, (0, 2, 3, 1)).reshape(m, KV_DIM, w)
    cv = jnp.transpose(cache_v, (0, 2, 3, 1)).reshape(m, KV_DIM, w)
    o_attn, ck_new, cv_new = _attn_decode(q, k, p_act, lw["lay"], ck, cv, lw["sink_rows"])
    h_new, o_ssm = _ssm_decode(ssm_state.reshape(m, D_INNER, D_STATE), xdt, dec, xc, p_act, lw["lay"],
                               lw["dskip"], lw["nw"])
    y = _merge_mlp(x2, o_attn, o_ssm.reshape(m, D_INNER), p_act, lw["lay"], lw["wa"], lw["ws"], lw["wo"],
                   lw["norm_mlp"], lw["wu"], lw["wd"])
    unview = lambda c: jnp.transpose(c.reshape(m, KV_HEADS, HEAD_DIM, w), (0, 3, 1, 2))
    return (y.reshape(m, 1, d), unview(ck_new), unview(cv_new), conv_new.reshape(m, CONV_W - 1, CONV_DIM),
            h_new.reshape(m, SSM_HEADS, SSM_HEAD_DIM, D_STATE))


def kernel(x_prompt, x_sample, cache_k, cache_v, state_conv, state_ssm, norm_mix, w_in, q_norm, k_norm,
           attn_sinks, conv_w, conv_b, dt_bias, a_log, d_skip, ssm_norm, w_attn_o, w_ssm_o, w_out,
           norm_mlp, w_up, w_down):
    depth = w_in.shape[0]
    d_model = x_prompt.shape[-1]
    yp, ys = x_prompt, x_sample
    cols = [[] for _ in range(8)]
    for l in range(depth):
        lw = _prep_layer(d_model, norm_mix[l], w_in[l], q_norm[l], k_norm[l], attn_sinks[l], conv_w[l],
                         conv_b[l], dt_bias[l], a_log[l], d_skip[l], ssm_norm[l], w_attn_o[l], w_ssm_o[l],
                         w_out[l], norm_mlp[l], w_up[l], w_down[l])
        yp, kp, vp, cp, hp = _prompt_layer(yp, lw)
        ys, ks, vs, cs, hs = _decode_layer(ys, cache_k[l], cache_v[l], state_conv[l], state_ssm[l], lw)
        for lst, val in zip(cols, (kp, vp, cp, hp, ks, vs, cs, hs)):
            lst.append(val)
    return (yp, ys) + tuple(jnp.stack(c) for c in cols)
```

```python
import functools

import numpy as np
import jax
import jax.numpy as jnp
from jax import lax
from jax.experimental import pallas as pl
from jax.experimental.pallas import tpu as pltpu

F32 = jnp.float32
BF16 = jnp.bfloat16

N_HEADS = 16
KV_HEADS = 4
HEAD_DIM = 64
Q_PER_KV = N_HEADS // KV_HEADS
WINDOW = 128
ROPE_THETA = 10000.0
Q_DIM = N_HEADS * HEAD_DIM
KV_DIM = KV_HEADS * HEAD_DIM
SSM_HEAD_DIM = 64
SSM_HEADS = 32
D_INNER = SSM_HEADS * SSM_HEAD_DIM
SSM_GROUPS = 4
HEADS_PER_GROUP = SSM_HEADS // SSM_GROUPS
D_STATE = 128
BC_DIM = SSM_GROUPS * D_STATE
CONV_W = 4
CONV_DIM = D_INNER + 2 * BC_DIM
SSD_CHUNK = 128
EPS = 1e-6
PAST_LEN = 8192
LOG2E = 1.4426950408889634

LANES = 128
SUBLANES = 8
VMEM_LIMIT = 48 * 1024 * 1024

BLK = 128
assert WINDOW == BLK and SSD_CHUNK == BLK
SLAB = KV_HEADS * HEAD_DIM
GROUP_LANES = HEADS_PER_GROUP * SSM_HEAD_DIM
DT_PAD = LANES


def _layout(d_model):
    segs = [("z", D_INNER), ("xs", D_INNER), ("ga", d_model), ("gs", d_model), ("q", Q_DIM),
            ("B", BC_DIM), ("C", BC_DIM), ("k", KV_DIM), ("v", KV_DIM), ("dt", DT_PAD)]
    lay, off = {}, 0
    for name, width in segs:
        assert off % width == 0, (name, off, width)
        lay[name] = (off, width)
        off += width
    return lay, off


def _col_tile(n, cap=3072):
    units = n // LANES
    best = 1
    for d in range(1, units + 1):
        if units % d == 0 and d * LANES <= cap:
            best = d
    return best * LANES


def _row_tile(m, cap):
    assert m % SUBLANES == 0
    best = SUBLANES
    for t in range(SUBLANES, min(m, cap) + 1, SUBLANES):
        if m % t == 0:
            best = t
    return best


def _cparams(sem):
    return pltpu.CompilerParams(dimension_semantics=sem, vmem_limit_bytes=VMEM_LIMIT)


def _sigmoid(x):
    return 0.5 + 0.5 * jnp.tanh(0.5 * x)


def _silu(x):
    h = 0.5 * x
    return h + h * jnp.tanh(h)


def _softplus(x):
    return jnp.maximum(x, 0.0) + jnp.log1p(jnp.exp(-jnp.abs(x)))


def _split3(a):
    hi = a.astype(BF16)
    r1 = a - hi.astype(F32)
    mid = r1.astype(BF16)
    lo = (r1 - mid.astype(F32)).astype(BF16)
    return hi, mid, lo


def _expand_heads(a, e3):
    hi, mid, lo = _split3(a)
    return jnp.dot(jnp.concatenate([hi, mid, lo], axis=1), e3, preferred_element_type=F32)


def _head_ms(x, bd):
    sq = x * x
    hi = sq.astype(BF16)
    lo = (sq - hi.astype(F32)).astype(BF16)
    outs = []
    for s in range(x.shape[1] // SLAB):
        sl = slice(s * SLAB, (s + 1) * SLAB)
        outs.append(jnp.dot(hi[:, sl], bd, preferred_element_type=F32)
                    + jnp.dot(lo[:, sl], bd, preferred_element_type=F32))
    return outs[0] if len(outs) == 1 else jnp.concatenate(outs, axis=1)


def _head_norm_rope(x, g, cos, sin, bd):
    xn = x * lax.rsqrt(_head_ms(x, bd) + EPS) * g
    tiles = []
    for t in range(x.shape[1] // LANES):
        xt = xn[:, t * LANES:(t + 1) * LANES]
        tiles.append(xt * cos + pltpu.roll(xt, LANES // 2, 1) * sin)
    return tiles[0] if len(tiles) == 1 else jnp.concatenate(tiles, axis=1)


def _unpair(x):
    q = HEAD_DIM // 2
    lane = lax.broadcasted_iota(jnp.int32, (x.shape[0], LANES), 1)
    tiles = []
    for t in range(x.shape[1] // LANES):
        xt = x[:, t * LANES:(t + 1) * LANES]
        nat = jnp.where((lane >= q) & (lane < 2 * q), pltpu.roll(xt, LANES - q, 1), xt)
        tiles.append(jnp.where((lane >= 2 * q) & (lane < 3 * q), pltpu.roll(xt, q, 1), nat))
    return tiles[0] if len(tiles) == 1 else jnp.concatenate(tiles, axis=1)


def _in_proj_kernel(x_ref, g_ref, w_ref, o_ref, xn_ref, *, tn):
    j = pl.program_id(1)

    @pl.when(j == 0)
    def _():
        x = x_ref[...]
        ms = jnp.mean(x * x, axis=-1, keepdims=True)
        xn_ref[...] = (x * lax.rsqrt(ms + EPS) * g_ref[...]).astype(BF16)

    for c in range(w_ref.shape[1] // tn):
        @pl.when(j == c)
        def _():
            o_ref[...] = jnp.dot(xn_ref[...], w_ref[:, c * tn:(c + 1) * tn], preferred_element_type=F32)


def _in_proj(x, g, w):
    m, k = x.shape
    n = w.shape[1]
    tm = _row_tile(m, 512)
    tn = _col_tile(n)
    return pl.pallas_call(
        functools.partial(_in_proj_kernel, tn=tn),
        grid=(m // tm, n // tn),
        in_specs=[pl.BlockSpec((tm, k), lambda i, j: (i, 0)),
                  pl.BlockSpec((1, k), lambda i, j: (0, 0)),
                  pl.BlockSpec((k, n), lambda i, j: (0, 0), pipeline_mode=pl.Buffered(1))],
        out_specs=pl.BlockSpec((tm, tn), lambda i, j: (i, j)),
        out_shape=jax.ShapeDtypeStruct((m, n), F32),
        scratch_shapes=[pltpu.VMEM((tm, k), BF16)],
        compiler_params=_cparams(("arbitrary", "arbitrary")),
        name="in_proj",
    )(x, g, w)


def _attn_prompt_kernel(q_ref, k_ref, v_ref, cos_ref, sin_ref, qg_ref, kg_ref, bd_ref, sink_ref,
                        o_ref, ko_ref, vo_ref, kbuf, vbuf, probs):
    i = pl.program_id(1)
    last = pl.num_programs(1) - 1

    @pl.when(i == 0)
    def _():
        kbuf[...] = jnp.zeros_like(kbuf)
        vbuf[...] = jnp.zeros_like(vbuf)

    cos, sin, bd = cos_ref[...], sin_ref[...], bd_ref[...]
    slot = i % 2
    lane = lax.broadcasted_iota(jnp.int32, (BLK, SLAB), 1)
    grp_k = 2 * (lane // LANES) + (lane % HEAD_DIM) // (HEAD_DIM // 2)
    grp_v = lane // HEAD_DIM
    r = lax.broadcasted_iota(jnp.int32, (BLK, 2 * BLK), 0)
    c = lax.broadcasted_iota(jnp.int32, (BLK, 2 * BLK), 1)
    key = c % BLK
    is_cur = (c // BLK) == slot
    prev_ok = jnp.where(i > 0, key, -1)
    mask = jnp.where(is_cur, (key <= r).astype(jnp.int32), (prev_ok > r + (BLK - WINDOW)).astype(jnp.int32)) > 0

    seqs = range(q_ref.shape[0])
    ks, vs, scores = [], [], []
    for u in seqs:
        q = _head_norm_rope(q_ref[u], qg_ref[...], cos, sin, bd) * (HEAD_DIM ** -0.5 * LOG2E)
        k = _head_norm_rope(k_ref[u], kg_ref[...], cos, sin, bd)
        v = v_ref[u]
        ks.append(k)
        vs.append(v)
        for p in range(KV_HEADS):
            rows = pl.ds(pl.multiple_of(p * 2 * BLK + slot * BLK, BLK), BLK)
            kbuf[u, rows, :] = jnp.where(grp_k == p, k, 0.0).astype(BF16)
            vbuf[u, rows, :] = jnp.where(grp_v == p, v, 0.0).astype(BF16)
        q_stack = jnp.concatenate([q[:, j * SLAB:(j + 1) * SLAB] for j in range(Q_PER_KV)],
                                  axis=0).astype(BF16)
        scores.append(lax.dot_general(q_stack, kbuf[u], (((1,), (1,)), ((), ())),
                                      preferred_element_type=F32))
    for u in seqs:
        for j in range(Q_PER_KV):
            for p in range(KV_HEADS):
                s = jnp.where(mask, scores[u][j * BLK:(j + 1) * BLK, p * 2 * BLK:(p + 1) * 2 * BLK], -jnp.inf)
                sink = sink_ref[j * KV_HEADS + p] * LOG2E
                mx = jnp.maximum(jnp.max(s, axis=-1, keepdims=True), sink)
                e = jnp.exp2(s - mx)
                den = jnp.sum(e, axis=-1, keepdims=True) + jnp.exp2(sink - mx)
                probs[u, j * BLK:(j + 1) * BLK, p * 2 * BLK:(p + 1) * 2 * BLK] = (e / den).astype(BF16)
    for u in seqs:
        pv = jnp.dot(probs[u], vbuf[u], preferred_element_type=F32)
        for j in range(Q_PER_KV):
            o_ref[u, :, j * SLAB:(j + 1) * SLAB] = pv[j * BLK:(j + 1) * BLK, :].astype(o_ref.dtype)

    @pl.when(i == last)
    def _():
        for u in seqs:
            ko_ref[u] = _unpair(ks[u]).T
            vo_ref[u] = vs[u].T


ATTN_SEQS = 2


def _attn_prompt(p_act, lay, batch, seq, cos, sin, qg, kg, bd, sinks):
    nb = seq // BLK
    u = ATTN_SEQS if batch % ATTN_SEQS == 0 else 1
    qc, kc, vc = lay["q"][0] // Q_DIM, lay["k"][0] // KV_DIM, lay["v"][0] // KV_DIM
    p3 = p_act.reshape(batch, seq, p_act.shape[1])
    return pl.pallas_call(
        _attn_prompt_kernel,
        grid=(batch // u, nb),
        in_specs=[pl.BlockSpec((u, BLK, Q_DIM), lambda b, i: (b, i, qc)),
                  pl.BlockSpec((u, BLK, KV_DIM), lambda b, i: (b, i, kc)),
                  pl.BlockSpec((u, BLK, KV_DIM), lambda b, i: (b, i, vc)),
                  pl.BlockSpec((BLK, LANES), lambda b, i: (i, 0)),
                  pl.BlockSpec((BLK, LANES), lambda b, i: (i, 0)),
                  pl.BlockSpec((1, Q_DIM), lambda b, i: (0, 0)),
                  pl.BlockSpec((1, KV_DIM), lambda b, i: (0, 0)),
                  pl.BlockSpec((SLAB, SLAB), lambda b, i: (0, 0)),
                  pl.BlockSpec(memory_space=pltpu.SMEM)],
        out_specs=[pl.BlockSpec((u, BLK, Q_DIM), lambda b, i: (b, i, 0)),
                   pl.BlockSpec((u, KV_DIM, BLK), lambda b, i: (b, 0, 0)),
                   pl.BlockSpec((u, KV_DIM, BLK), lambda b, i: (b, 0, 0))],
        out_shape=[jax.ShapeDtypeStruct((batch, seq, Q_DIM), BF16),
                   jax.ShapeDtypeStruct((batch, KV_DIM, BLK), F32),
                   jax.ShapeDtypeStruct((batch, KV_DIM, BLK), F32)],
        scratch_shapes=[pltpu.VMEM((u, KV_HEADS * 2 * BLK, KV_DIM), BF16),
                        pltpu.VMEM((u, KV_HEADS * 2 * BLK, KV_DIM), BF16),
                        pltpu.VMEM((u, Q_PER_KV * BLK, KV_HEADS * 2 * BLK), BF16)],
        compiler_params=_cparams(("arbitrary", "arbitrary")),
        name="attn_prompt",
    )(p3, p3, p3, cos, sin, qg, kg, bd, sinks)


def _ssd_chunk(slot, z_ref, xs_ref, b_ref, c_ref, dt_ref, cw_ref, cb_ref, dtb_ref, alog_ref, dskip_ref,
               nw_ref, e3_ref, o_ref, tail_ref, xpad, tails, st, ybuf):
    xpad[0:SUBLANES, :] = tails[1 - slot]
    xpad[SUBLANES:SUBLANES + BLK, 0:D_INNER] = xs_ref[...]
    xpad[SUBLANES:SUBLANES + BLK, D_INNER:D_INNER + BC_DIM] = b_ref[...]
    xpad[SUBLANES:SUBLANES + BLK, D_INNER + BC_DIM:CONV_DIM] = c_ref[...]
    cwh = 0.5 * cw_ref[...]
    acc = 0.5 * cb_ref[...] + cwh[CONV_W - 1:CONV_W, :] * xpad[SUBLANES:SUBLANES + BLK, :]
    for t in range(1, CONV_W):
        acc = acc + cwh[CONV_W - 1 - t:CONV_W - t, :] * xpad[SUBLANES - t:SUBLANES - t + BLK, :]
    xc = acc + acc * jnp.tanh(acc)
    new_tail = xpad[BLK:BLK + SUBLANES, :]
    tail_ref[...] = new_tail
    tails[slot] = new_tail

    xs = xc[:, 0:D_INNER]
    bm = xc[:, D_INNER:D_INNER + BC_DIM].astype(BF16)
    cm = xc[:, D_INNER + BC_DIM:CONV_DIM].astype(BF16)

    e3 = e3_ref[...]
    dt = _softplus(dt_ref[...] + dtb_ref[...])
    dta = dt * (-LOG2E * jnp.exp(alog_ref[...]))
    row = lax.broadcasted_iota(jnp.int32, (BLK, BLK), 0)
    col = lax.broadcasted_iota(jnp.int32, (BLK, BLK), 1)
    causal = row >= col
    cum = jnp.dot(causal.astype(F32), dta, preferred_element_type=F32, precision=lax.Precision.HIGHEST)
    cum_t = cum.T
    ecum = jnp.exp2(cum)
    to_end = jnp.exp2(cum[BLK - 1:BLK, :] - cum) * dt
    dt_e = _expand_heads(dt, e3)
    ecum_e = _expand_heads(ecum, e3)
    to_end_e = _expand_heads(to_end, e3)
    xdt = (xs * dt_e).astype(BF16)
    xte = (xs * to_end_e).astype(BF16)
    lane = lax.broadcasted_iota(jnp.int32, (BLK, LANES), 1)
    first_head = lane < SSM_HEAD_DIM

    for g in range(SSM_GROUPS):
        gl = slice(g * GROUP_LANES, (g + 1) * GROUP_LANES)
        bg = bm[:, g * D_STATE:(g + 1) * D_STATE]
        cg = cm[:, g * D_STATE:(g + 1) * D_STATE]
        cbg = lax.dot_general(cg, bg, (((1,), (1,)), ((), ())), preferred_element_type=F32)
        st_g = st[1 - slot, :, gl]
        y_inter = jnp.dot(cg, st_g.astype(BF16), preferred_element_type=F32) * ecum_e[:, gl]
        for pr in range(HEADS_PER_GROUP // 2):
            h0 = g * HEADS_PER_GROUP + 2 * pr
            xd = xdt[:, h0 * SSM_HEAD_DIM:(h0 + 2) * SSM_HEAD_DIM]
            ys = []
            for h in (h0, h0 + 1):
                diff = cum[:, h:h + 1] - cum_t[h:h + 1, :]
                w = (jnp.exp2(jnp.where(causal, diff, -jnp.inf)) * cbg).astype(BF16)
                ys.append(jnp.dot(w, xd, preferred_element_type=F32))
            lo = pr * LANES
            ybuf[:, h0 * SSM_HEAD_DIM:(h0 + 2) * SSM_HEAD_DIM] = (
                jnp.where(first_head, ys[0], ys[1]) + y_inter[:, lo:lo + LANES])
        upd = lax.dot_general(bg, xte[:, gl], (((0,), (0,)), ((), ())), preferred_element_type=F32)
        st[slot, :, gl] = st_g * ecum_e[BLK - 1:BLK, gl] + upd

    y = ybuf[...] + dskip_ref[...] * xs
    yz = y * _silu(z_ref[...])
    ms = jnp.mean(yz * yz, axis=-1, keepdims=True)
    o_ref[...] = (yz * lax.rsqrt(ms + EPS) * nw_ref[...]).astype(o_ref.dtype)


def _ssd_prompt_kernel(z_ref, xs_ref, b_ref, c_ref, dt_ref, cw_ref, cb_ref, dtb_ref, alog_ref, dskip_ref,
                       nw_ref, e3_ref, o_ref, tail_ref, hfin_ref, xpad, tails, st, ybuf):
    i = pl.program_id(1)
    last = pl.num_programs(1) - 1

    @pl.when(i == 0)
    def _():
        tails[...] = jnp.zeros_like(tails)
        st[...] = jnp.zeros_like(st)

    slot = i % 2
    seqs = range(z_ref.shape[0])
    for u in seqs:
        _ssd_chunk(slot, z_ref.at[u], xs_ref.at[u], b_ref.at[u], c_ref.at[u], dt_ref.at[u], cw_ref, cb_ref,
                   dtb_ref, alog_ref, dskip_ref, nw_ref, e3_ref, o_ref.at[u], tail_ref.at[u], xpad.at[u],
                   tails.at[u], st.at[u], ybuf.at[u])

    @pl.when(i == last)
    def _():
        for u in seqs:
            hfin_ref[u] = st[u, slot].T


SSD_SEQS = 2


def _ssd_prompt(p_act, lay, batch, seq, cw, cb, dtb, alog, dskip, nw, e3):
    nb = seq // BLK
    u = SSD_SEQS if batch % SSD_SEQS == 0 else 1
    p3 = p_act.reshape(batch, seq, p_act.shape[1])
    zc, xc = lay["z"][0] // D_INNER, lay["xs"][0] // D_INNER
    bc, cc, dc = lay["B"][0] // BC_DIM, lay["C"][0] // BC_DIM, lay["dt"][0] // DT_PAD
    const = lambda b, i: (0, 0)
    return pl.pallas_call(
        _ssd_prompt_kernel,
        grid=(batch // u, nb),
        in_specs=[pl.BlockSpec((u, BLK, D_INNER), lambda b, i: (b, i, zc)),
                  pl.BlockSpec((u, BLK, D_INNER), lambda b, i: (b, i, xc)),
                  pl.BlockSpec((u, BLK, BC_DIM), lambda b, i: (b, i, bc)),
                  pl.BlockSpec((u, BLK, BC_DIM), lambda b, i: (b, i, cc)),
                  pl.BlockSpec((u, BLK, DT_PAD), lambda b, i: (b, i, dc)),
                  pl.BlockSpec((CONV_W, CONV_DIM), const),
                  pl.BlockSpec((1, CONV_DIM), const),
                  pl.BlockSpec((1, DT_PAD), const),
                  pl.BlockSpec((1, DT_PAD), const),
                  pl.BlockSpec((1, D_INNER), const),
                  pl.BlockSpec((1, D_INNER), const),
                  pl.BlockSpec((3 * DT_PAD, D_INNER), const)],
        out_specs=[pl.BlockSpec((u, BLK, D_INNER), lambda b, i: (b, i, 0)),
                   pl.BlockSpec((u, SUBLANES, CONV_DIM), lambda b, i: (b, 0, 0)),
                   pl.BlockSpec((u, D_INNER, D_STATE), lambda b, i: (b, 0, 0))],
        out_shape=[jax.ShapeDtypeStruct((batch, seq, D_INNER), BF16),
                   jax.ShapeDtypeStruct((batch, SUBLANES, CONV_DIM), F32),
                   jax.ShapeDtypeStruct((batch, D_INNER, D_STATE), F32)],
        scratch_shapes=[pltpu.VMEM((u, BLK + SUBLANES, CONV_DIM), F32),
                        pltpu.VMEM((u, 2, SUBLANES, CONV_DIM), F32),
                        pltpu.VMEM((u, 2, D_STATE, D_INNER), F32),
                        pltpu.VMEM((u, BLK, D_INNER), F32)],
        compiler_params=_cparams(("arbitrary", "arbitrary")),
        name="ssd_prompt",
    )(p3, p3, p3, p3, p3, cw, cb, dtb, alog, dskip, nw, e3)


def _decode_pre_kernel(q_ref, k_ref, xs_ref, b_ref, c_ref, dt_ref, cs_ref, cos_ref, sin_ref, qg_ref, kg_ref,
                       bd_ref, cw_ref, cb_ref, dtb_ref, alog_ref, e3_ref,
                       qo_ref, ko_ref, xc_ref, xdt_ref, dec_ref, cso_ref):
    cos, sin, bd = cos_ref[...], sin_ref[...], bd_ref[...]
    qo_ref[...] = _unpair(_head_norm_rope(q_ref[...], qg_ref[...], cos, sin, bd)) * (HEAD_DIM ** -0.5)
    ko_ref[...] = _unpair(_head_norm_rope(k_ref[...], kg_ref[...], cos, sin, bd))

    segs = ((xs_ref, 0, D_INNER), (b_ref, D_INNER, BC_DIM), (c_ref, D_INNER + BC_DIM, BC_DIM))
    for ref, off, width in segs:
        new = ref[...]
        acc = cb_ref[:, off:off + width] + cw_ref[CONV_W - 1:CONV_W, off:off + width] * new
        for t in range(CONV_W - 1):
            lo = t * CONV_DIM + off
            acc = acc + cw_ref[t:t + 1, off:off + width] * cs_ref[:, lo:lo + width]
        xc_ref[:, off:off + width] = _silu(acc)
        for t in range(CONV_W - 2):
            cso_ref[:, t * CONV_DIM + off:t * CONV_DIM + off + width] = (
                cs_ref[:, (t + 1) * CONV_DIM + off:(t + 1) * CONV_DIM + off + width])
        lo = (CONV_W - 2) * CONV_DIM + off
        cso_ref[:, lo:lo + width] = new

    e3 = e3_ref[...]
    dt = _softplus(dt_ref[...] + dtb_ref[...])
    decay = jnp.exp(dt * (-jnp.exp(alog_ref[...])))
    xdt_ref[...] = xc_ref[:, 0:D_INNER] * _expand_heads(dt, e3)
    dec_ref[...] = _expand_heads(decay, e3)


def _decode_pre(p_act, lay, conv_state, cos, sin, qg, kg, bd, cw, cb, dtb, alog, e3):
    m = p_act.shape[0]
    full = lambda shape: pl.BlockSpec(shape, lambda i: (0, 0))

    def col(name):
        c = lay[name][0] // lay[name][1]
        return pl.BlockSpec((m, lay[name][1]), lambda i: (0, c))

    cs_w = (CONV_W - 1) * CONV_DIM
    return pl.pallas_call(
        _decode_pre_kernel,
        grid=(1,),
        in_specs=[col("q"), col("k"), col("xs"), col("B"), col("C"), col("dt"),
                  full((m, cs_w)), full((1, LANES)), full((1, LANES)), full((1, Q_DIM)), full((1, KV_DIM)),
                  full((SLAB, SLAB)), full((CONV_W, CONV_DIM)), full((1, CONV_DIM)), full((1, DT_PAD)),
                  full((1, DT_PAD)), full((3 * DT_PAD, D_INNER))],
        out_specs=[full((m, Q_DIM)), full((m, KV_DIM)), full((m, CONV_DIM)), full((m, D_INNER)),
                   full((m, D_INNER)), full((m, cs_w))],
        out_shape=[jax.ShapeDtypeStruct((m, Q_DIM), F32), jax.ShapeDtypeStruct((m, KV_DIM), F32),
                   jax.ShapeDtypeStruct((m, CONV_DIM), F32), jax.ShapeDtypeStruct((m, D_INNER), F32),
                   jax.ShapeDtypeStruct((m, D_INNER), F32), jax.ShapeDtypeStruct((m, cs_w), F32)],
        compiler_params=_cparams(("arbitrary",)),
        name="decode_pre",
    )(p_act, p_act, p_act, p_act, p_act, p_act, conv_state, cos, sin, qg, kg, bd, cw, cb, dtb, alog, e3)


DEC_ROWS = 8


def _attn_decode_kernel(q_ref, k_ref, v_ref, ck_ref, cv_ref, sink_ref, o_ref, cko_ref, cvo_ref):
    w = ck_ref.shape[2]
    nq = N_HEADS
    r = lax.broadcasted_iota(jnp.int32, (nq, SLAB), 0)
    grp = lax.broadcasted_iota(jnp.int32, (nq, SLAB), 1) // HEAD_DIM
    own = grp == (r % KV_HEADS)
    in_window = lax.broadcasted_iota(jnp.int32, (nq, w), 1) > (w - WINDOW - 1)
    newest = lax.broadcasted_iota(jnp.int32, (KV_DIM, w), 1) == w - 1
    sink = sink_ref[...][:, 0:1]
    pad = jnp.zeros((DEC_ROWS, KV_DIM), BF16)
    k_parts = jnp.concatenate(list(_split3(k_ref[...])) + [pad], axis=0)
    v_parts = jnp.concatenate(list(_split3(v_ref[...])) + [pad], axis=0)
    part_row = lax.broadcasted_iota(jnp.int32, (4 * DEC_ROWS, w), 0) % DEC_ROWS
    tdims = (((0,), (0,)), ((), ()))
    for bl in range(DEC_ROWS):
        qrow = q_ref[bl:bl + 1, :]
        qm = jnp.zeros((nq, SLAB), F32)
        for j in range(Q_PER_KV):
            slab = jnp.broadcast_to(qrow[:, j * SLAB:(j + 1) * SLAB], (nq, SLAB))
            qm = jnp.where((r // KV_HEADS) == j, slab, qm)
        qm = jnp.where(own, qm, 0.0).astype(BF16)
        pick = jnp.where(part_row == bl, 1.0, 0.0).astype(BF16)
        k_col = lax.dot_general(k_parts, pick, tdims, preferred_element_type=F32)
        v_col = lax.dot_general(v_parts, pick, tdims, preferred_element_type=F32)
        keys = jnp.where(newest, k_col, pltpu.roll(ck_ref[bl], w - 1, 1))
        vals = jnp.where(newest, v_col, pltpu.roll(cv_ref[bl], w - 1, 1))
        cko_ref[bl] = keys
        cvo_ref[bl] = vals
        s = jnp.dot(qm, keys.astype(BF16), preferred_element_type=F32)
        s = jnp.where(in_window, s, -jnp.inf)
        mx = jnp.maximum(jnp.max(s, axis=-1, keepdims=True), sink)
        e = jnp.exp(s - mx)
        den = jnp.sum(e, axis=-1, keepdims=True) + jnp.exp(sink - mx)
        pv = lax.dot_general(e.astype(BF16), vals.astype(BF16), (((1,), (1,)), ((), ())),
                             preferred_element_type=F32)
        pv = jnp.where(own, pv / den, 0.0)
        for j in range(Q_PER_KV):
            o_ref[bl:bl + 1, j * SLAB:(j + 1) * SLAB] = jnp.sum(
                pv[j * KV_HEADS:(j + 1) * KV_HEADS, :], axis=0, keepdims=True)


def _attn_decode(q, k, p_act, lay, cache_k, cache_v, sink_rows):
    m = q.shape[0]
    w = cache_k.shape[2]
    vc = lay["v"][0] // KV_DIM
    cache_spec = pl.BlockSpec((DEC_ROWS, KV_DIM, w), lambda i: (i, 0, 0))
    return pl.pallas_call(
        _attn_decode_kernel,
        grid=(m // DEC_ROWS,),
        in_specs=[pl.BlockSpec((DEC_ROWS, Q_DIM), lambda i: (i, 0)),
                  pl.BlockSpec((DEC_ROWS, KV_DIM), lambda i: (i, 0)),
                  pl.BlockSpec((DEC_ROWS, KV_DIM), lambda i: (i, vc)),
                  cache_spec, cache_spec,
                  pl.BlockSpec((N_HEADS, LANES), lambda i: (0, 0))],
        out_specs=[pl.BlockSpec((DEC_ROWS, Q_DIM), lambda i: (i, 0)), cache_spec, cache_spec],
        out_shape=[jax.ShapeDtypeStruct((m, Q_DIM), F32),
                   jax.ShapeDtypeStruct((m, KV_DIM, w), F32),
                   jax.ShapeDtypeStruct((m, KV_DIM, w), F32)],
        compiler_params=_cparams(("arbitrary",)),
        name="attn_decode",
    )(q, k, p_act, cache_k, cache_v, sink_rows)


MM_ROWS = 16


def _ssm_decode_kernel(st_ref, xdt_ref, dec_ref, xc_ref, z_ref, dskip_ref, nw_ref, sto_ref, o_ref):
    r = lax.broadcasted_iota(jnp.int32, (MM_ROWS, D_INNER), 0)
    grp = lax.broadcasted_iota(jnp.int32, (MM_ROWS, D_INNER), 1) // GROUP_LANES
    rr = lax.broadcasted_iota(jnp.int32, (MM_ROWS, D_STATE), 0)
    ones_rows = jnp.where((rr >= SSM_GROUPS) & (rr < SSM_GROUPS + 3), 1.0, 0.0)
    bc = lambda a: jnp.broadcast_to(a, (MM_ROWS, D_INNER))
    for u in range(st_ref.shape[0]):
        h = st_ref[u]
        xdt = xdt_ref[u:u + 1, :]
        xc = xc_ref[u:u + 1, :]
        hi, mid, lo = _split3(dec_ref[u:u + 1, :])
        lhs_t = jnp.where(r == grp, bc(xdt), 0.0)
        for t, piece in enumerate((hi, mid, lo)):
            lhs_t = jnp.where(r == SSM_GROUPS + t, bc(piece.astype(F32)), lhs_t)
        lhs_t = lhs_t.astype(BF16)
        b_rows = jnp.zeros((MM_ROWS, D_STATE), F32)
        c_rows = jnp.zeros((MM_ROWS, D_STATE), F32)
        for g in range(SSM_GROUPS):
            b_g = xc[:, D_INNER + g * D_STATE:D_INNER + (g + 1) * D_STATE]
            c_g = xc[:, D_INNER + BC_DIM + g * D_STATE:D_INNER + BC_DIM + (g + 1) * D_STATE]
            b_rows = jnp.where(rr == g, jnp.broadcast_to(b_g, (MM_ROWS, D_STATE)), b_rows)
            c_rows = jnp.where(rr == g, jnp.broadcast_to(c_g, (MM_ROWS, D_STATE)), c_rows)
        rhs = jnp.concatenate([b_rows, ones_rows], axis=1).astype(BF16)
        both = lax.dot_general(lhs_t, rhs, (((0,), (0,)), ((), ())), preferred_element_type=F32)
        h_new = both[:, D_STATE:] * h + both[:, :D_STATE]
        sto_ref[u] = h_new
        yg = lax.dot_general(c_rows.astype(BF16), h_new.astype(BF16), (((1,), (1,)), ((), ())),
                             preferred_element_type=F32)
        y = jnp.sum(jnp.where(r == grp, yg, 0.0), axis=0, keepdims=True)
        y = y + dskip_ref[...] * xc[:, 0:D_INNER]
        yz = y * _silu(z_ref[u:u + 1, :])
        ms = jnp.mean(yz * yz, axis=-1, keepdims=True)
        o_ref[u:u + 1, :] = yz * lax.rsqrt(ms + EPS) * nw_ref[...]


SSM_DEC_ROWS = SUBLANES


def _ssm_decode(state, xdt, dec, xc, p_act, lay, dskip, nw):
    m = state.shape[0]
    u = SSM_DEC_ROWS
    assert m % u == 0
    zc = lay["z"][0] // D_INNER
    row = lambda width: pl.BlockSpec((u, width), lambda b: (b, 0))
    return pl.pallas_call(
        _ssm_decode_kernel,
        grid=(m // u,),
        in_specs=[pl.BlockSpec((u, D_INNER, D_STATE), lambda b: (b, 0, 0)),
                  row(D_INNER), row(D_INNER), row(CONV_DIM),
                  pl.BlockSpec((u, D_INNER), lambda b: (b, zc)),
                  pl.BlockSpec((1, D_INNER), lambda b: (0, 0)),
                  pl.BlockSpec((1, D_INNER), lambda b: (0, 0))],
        out_specs=[pl.BlockSpec((u, D_INNER, D_STATE), lambda b: (b, 0, 0)), row(D_INNER)],
        out_shape=[jax.ShapeDtypeStruct((m, D_INNER, D_STATE), F32),
                   jax.ShapeDtypeStruct((m, D_INNER), F32)],
        compiler_params=_cparams(("arbitrary",)),
        name="ssm_decode",
    )(state, xdt, dec, xc, p_act, dskip, nw)


def _merge_mlp_kernel(x_ref, oa_ref, os_ref, ga_ref, gs_ref, wa_ref, ws_ref, wo_ref, g_ref, wu_ref, wd_ref,
                      o_ref, *, ff_tile):
    a = jnp.dot(oa_ref[...].astype(BF16), wa_ref[...], preferred_element_type=F32)
    s = jnp.dot(os_ref[...].astype(BF16), ws_ref[...], preferred_element_type=F32)
    mixed = _sigmoid(ga_ref[...]) * a + _sigmoid(gs_ref[...]) * s
    x = x_ref[...] + jnp.dot(mixed.astype(BF16), wo_ref[...], preferred_element_type=F32)
    ms = jnp.mean(x * x, axis=-1, keepdims=True)
    xn = (x * lax.rsqrt(ms + EPS) * g_ref[...]).astype(BF16)
    acc = x
    for c in range(wu_ref.shape[1] // ff_tile):
        u = jnp.dot(xn, wu_ref[:, c * ff_tile:(c + 1) * ff_tile], preferred_element_type=F32)
        act = jnp.square(jnp.maximum(u, 0.0)).astype(BF16)
        acc = acc + jnp.dot(act, wd_ref[c * ff_tile:(c + 1) * ff_tile, :], preferred_element_type=F32)
    o_ref[...] = acc


def _merge_mlp(x, o_attn, o_ssm, p_act, lay, wa, ws, wo, g, wu, wd):
    m, d = x.shape
    ff = wu.shape[1]
    tm = _row_tile(m, 256)
    gac, gsc = lay["ga"][0] // d, lay["gs"][0] // d
    weight = lambda shape: pl.BlockSpec(shape, lambda i: (0, 0), pipeline_mode=pl.Buffered(1))
    return pl.pallas_call(
        functools.partial(_merge_mlp_kernel, ff_tile=min(1024, ff)),
        grid=(m // tm,),
        in_specs=[pl.BlockSpec((tm, d), lambda i: (i, 0)),
                  pl.BlockSpec((tm, Q_DIM), lambda i: (i, 0)),
                  pl.BlockSpec((tm, D_INNER), lambda i: (i, 0)),
                  pl.BlockSpec((tm, d), lambda i: (i, gac)),
                  pl.BlockSpec((tm, d), lambda i: (i, gsc)),
                  weight((Q_DIM, d)), weight((D_INNER, d)), weight((d, d)),
                  weight((1, d)), weight((d, ff)), weight((ff, d))],
        out_specs=pl.BlockSpec((tm, d), lambda i: (i, 0)),
        out_shape=jax.ShapeDtypeStruct((m, d), F32),
        compiler_params=_cparams(("arbitrary",)),
        name="merge_mlp",
    )(x, o_attn, o_ssm, p_act, p_act, wa, ws, wo, g, wu, wd)


def _rope_tables(pos):
    half = HEAD_DIM // 2
    inv = ROPE_THETA ** (-jnp.arange(half, dtype=F32) / half)
    ang = pos.astype(F32)[:, None] * inv[None, :]
    cos, sin = jnp.cos(ang), jnp.sin(ang)
    return (jnp.concatenate([cos, cos, cos, cos], axis=1),
            jnp.concatenate([-sin, -sin, sin, sin], axis=1))


def _constants():
    heads = np.arange(DT_PAD)[:, None]
    cols = np.arange(D_INNER)[None, :] // SSM_HEAD_DIM
    e = (heads == cols).astype(np.float32)
    e3 = jnp.asarray(np.concatenate([e, e, e], axis=0), BF16)
    a = np.arange(SLAB)
    grp = 2 * (a // LANES) + (a % HEAD_DIM) // (HEAD_DIM // 2)
    bd = jnp.asarray((grp[:, None] == grp[None, :]).astype(np.float32) / HEAD_DIM, BF16)
    return e3, bd


_HALF = HEAD_DIM // 2


def _pair_q(a):
    lead = a.shape[:-1]
    n = len(lead)
    a = a.reshape(lead + (2, 2, Q_PER_KV, 2, _HALF))
    return a.transpose(tuple(range(n)) + (n + 2, n, n + 3, n + 1, n + 4)).reshape(lead + (Q_DIM,))


def _pair_k(a):
    lead = a.shape[:-1]
    n = len(lead)
    a = a.reshape(lead + (2, 2, 2, _HALF))
    return a.transpose(tuple(range(n)) + (n, n + 2, n + 1, n + 3)).reshape(lead + (KV_DIM,))


def _prep_layer(d_model, norm_mix, w_in, q_norm, k_norm, attn_sinks, conv_w, conv_b, dt_bias, a_log, d_skip,
                ssm_norm, w_attn_o, w_ssm_o, w_out, norm_mlp, w_up, w_down):
    assert KV_HEADS == 4 and Q_PER_KV == 4
    e3, bd = _constants()
    o_q, o_k, o_v, o_z, o_xbc, o_dt = (0, Q_DIM, Q_DIM + KV_DIM, Q_DIM + 2 * KV_DIM,
                                       Q_DIM + 2 * KV_DIM + D_INNER, Q_DIM + 2 * KV_DIM + D_INNER + CONV_DIM)
    o_g = o_dt + SSM_HEADS
    cut = lambda lo, width: w_in[:, lo:lo + width]
    pieces = {"z": cut(o_z, D_INNER), "xs": cut(o_xbc, D_INNER), "ga": cut(o_g, d_model),
              "gs": cut(o_g + d_model, d_model), "q": _pair_q(cut(o_q, Q_DIM)),
              "B": cut(o_xbc + D_INNER, BC_DIM), "C": cut(o_xbc + D_INNER + BC_DIM, BC_DIM),
              "k": _pair_k(cut(o_k, KV_DIM)), "v": cut(o_v, KV_DIM),
              "dt": jnp.pad(cut(o_dt, SSM_HEADS), ((0, 0), (0, DT_PAD - SSM_HEADS)))}
    lay, _ = _layout(d_model)
    w_p = jnp.concatenate([pieces[name] for name in lay], axis=1).astype(BF16)
    pad_heads = lambda a: jnp.pad(a, (0, DT_PAD - SSM_HEADS))[None, :]
    sink_p = attn_sinks.reshape(KV_HEADS, Q_PER_KV).T.reshape(N_HEADS)
    wa = w_attn_o.reshape(KV_HEADS, Q_PER_KV, HEAD_DIM, d_model).transpose(1, 0, 2, 3).reshape(Q_DIM, d_model)
    return dict(
        lay=lay, e3=e3, bd=bd, w_p=w_p,
        norm_mix=norm_mix[None, :], norm_mlp=norm_mlp[None, :],
        qg=_pair_q(jnp.tile(q_norm, N_HEADS))[None, :], kg=_pair_k(jnp.tile(k_norm, KV_HEADS))[None, :],
        sink_p=sink_p, sink_rows=jnp.broadcast_to(sink_p[:, None], (N_HEADS, LANES)),
        cw=conv_w, cb=conv_b[None, :], dtb=pad_heads(dt_bias), alog=pad_heads(a_log),
        dskip=jnp.repeat(d_skip, SSM_HEAD_DIM)[None, :], nw=ssm_norm[None, :],
        wa=wa.astype(BF16), ws=w_ssm_o.astype(BF16), wo=w_out.astype(BF16),
        wu=w_up.astype(BF16), wd=w_down.astype(BF16))


def _prompt_layer(x, lw):
    batch, seq, d = x.shape
    assert seq % BLK == 0 and seq >= WINDOW
    x2 = x.reshape(batch * seq, d)
    p_act = _in_proj(x2, lw["norm_mix"], lw["w_p"])
    cos, sin = _rope_tables(jnp.arange(seq))
    o_attn, k_last, v_last = _attn_prompt(p_act, lw["lay"], batch, seq, cos, sin, lw["qg"], lw["kg"],
                                          lw["bd"], lw["sink_p"])
    o_ssm, tail, h_fin = _ssd_prompt(p_act, lw["lay"], batch, seq, lw["cw"], lw["cb"], lw["dtb"], lw["alog"],
                                     lw["dskip"], lw["nw"], lw["e3"])
    y = _merge_mlp(x2, o_attn.reshape(batch * seq, Q_DIM), o_ssm.reshape(batch * seq, D_INNER), p_act,
                   lw["lay"], lw["wa"], lw["ws"], lw["wo"],
                   lw["norm_mlp"], lw["wu"], lw["wd"])
    return (y.reshape(batch, seq, d),
            k_last.reshape(batch, KV_HEADS, HEAD_DIM, BLK).transpose(0, 3, 1, 2),
            v_last.reshape(batch, KV_HEADS, HEAD_DIM, BLK).transpose(0, 3, 1, 2),
            tail[:, SUBLANES - (CONV_W - 1):, :],
            h_fin.reshape(batch, SSM_HEADS, SSM_HEAD_DIM, D_STATE))


def _decode_layer(x, cache_k, cache_v, conv_state, ssm_state, lw):
    m, t, d = x.shape
    w = cache_k.shape[1]
    assert t == 1 and w == WINDOW and m % DEC_ROWS == 0
    x2 = x.reshape(m, d)
    p_act = _in_proj(x2, lw["norm_mix"], lw["w_p"])
    cos, sin = _rope_tables(PAST_LEN + jnp.arange(1))
    q, k, xc, xdt, dec, conv_new = _decode_pre(
        p_act, lw["lay"], conv_state.reshape(m, (CONV_W - 1) * CONV_DIM), cos, sin, lw["qg"], lw["kg"],
        lw["bd"], lw["cw"], lw["cb"], lw["dtb"], lw["alog"], lw["e3"])
    ck = jnp.transpose(cache_k, (0, 2, 3, 1)).reshape(m, KV_DIM, w)
    cv = jnp.transpose(cache_v, (0, 2, 3, 1)).reshape(m, KV_DIM, w)
    o_attn, ck_new, cv_new = _attn_decode(q, k, p_act, lw["lay"], ck, cv, lw["sink_rows"])
    h_new, o_ssm = _ssm_decode(ssm_state.reshape(m, D_INNER, D_STATE), xdt, dec, xc, p_act, lw["lay"],
                               lw["dskip"], lw["nw"])
    y = _merge_mlp(x2, o_attn, o_ssm, p_act, lw["lay"], lw["wa"], lw["ws"], lw["wo"],
                   lw["norm_mlp"], lw["wu"], lw["wd"])
    unview = lambda c: jnp.transpose(c.reshape(m, KV_HEADS, HEAD_DIM, w), (0, 3, 1, 2))
    return (y.reshape(m, 1, d), unview(ck_new), unview(cv_new), conv_new.reshape(m, CONV_W - 1, CONV_DIM),
            h_new.reshape(m, SSM_HEADS, SSM_HEAD_DIM, D_STATE))


def kernel(x_prompt, x_sample, cache_k, cache_v, state_conv, state_ssm, norm_mix, w_in, q_norm, k_norm,
           attn_sinks, conv_w, conv_b, dt_bias, a_log, d_skip, ssm_norm, w_attn_o, w_ssm_o, w_out,
           norm_mlp, w_up, w_down):
    depth = w_in.shape[0]
    d_model = x_prompt.shape[-1]
    yp, ys = x_prompt, x_sample
    cols = [[] for _ in range(8)]
    for l in range(depth):
        lw = _prep_layer(d_model, norm_mix[l], w_in[l], q_norm[l], k_norm[l], attn_sinks[l], conv_w[l],
                         conv_b[l], dt_bias[l], a_log[l], d_skip[l], ssm_norm[l], w_attn_o[l], w_ssm_o[l],
                         w_out[l], norm_mlp[l], w_up[l], w_down[l])
        yp, kp, vp, cp, hp = _prompt_layer(yp, lw)
        ys, ks, vs, cs, hs = _decode_layer(ys, cache_k[l], cache_v[l], state_conv[l], state_ssm[l], lw)
        for lst, val in zip(cols, (kp, vp, cp, hp, ks, vs, cs, hs)):
            lst.append(val)
    return (yp, ys) + tuple(jnp.stack(c) for c in cols)
```

```python
import functools

import numpy as np
import jax
import jax.numpy as jnp
from jax import lax
from jax.experimental import pallas as pl
from jax.experimental.pallas import tpu as pltpu

F32 = jnp.float32
BF16 = jnp.bfloat16

N_HEADS = 16
KV_HEADS = 4
HEAD_DIM = 64
Q_PER_KV = N_HEADS // KV_HEADS
WINDOW = 128
ROPE_THETA = 10000.0
Q_DIM = N_HEADS * HEAD_DIM
KV_DIM = KV_HEADS * HEAD_DIM
SSM_HEAD_DIM = 64
SSM_HEADS = 32
D_INNER = SSM_HEADS * SSM_HEAD_DIM
SSM_GROUPS = 4
HEADS_PER_GROUP = SSM_HEADS // SSM_GROUPS
D_STATE = 128
BC_DIM = SSM_GROUPS * D_STATE
CONV_W = 4
CONV_DIM = D_INNER + 2 * BC_DIM
SSD_CHUNK = 128
EPS = 1e-6
PAST_LEN = 8192
LOG2E = 1.4426950408889634

LANES = 128
SUBLANES = 8
VMEM_LIMIT = 48 * 1024 * 1024
WEIGHT_RESIDENT_VMEM_LIMIT = 56 * 1024 * 1024

BLK = 128
assert WINDOW == BLK and SSD_CHUNK == BLK
SLAB = KV_HEADS * HEAD_DIM
GROUP_LANES = HEADS_PER_GROUP * SSM_HEAD_DIM
DT_PAD = LANES


def _layout(d_model):
    segs = [("z", D_INNER), ("xs", D_INNER), ("ga", d_model), ("gs", d_model), ("q", Q_DIM),
            ("B", BC_DIM), ("C", BC_DIM), ("k", KV_DIM), ("v", KV_DIM), ("dt", DT_PAD)]
    lay, off = {}, 0
    for name, width in segs:
        assert off % width == 0, (name, off, width)
        lay[name] = (off, width)
        off += width
    return lay, off


def _col_tile(n, cap=3072):
    units = n // LANES
    best = 1
    for d in range(1, units + 1):
        if units % d == 0 and d * LANES <= cap:
            best = d
    return best * LANES


def _row_tile(m, cap):
    assert m % SUBLANES == 0
    best = SUBLANES
    for t in range(SUBLANES, min(m, cap) + 1, SUBLANES):
        if m % t == 0:
            best = t
    return best


def _cparams(sem, vmem_limit=VMEM_LIMIT):
    return pltpu.CompilerParams(dimension_semantics=sem, vmem_limit_bytes=vmem_limit)


def _sigmoid(x):
    return 0.5 + 0.5 * jnp.tanh(0.5 * x)


def _silu(x):
    h = 0.5 * x
    return h + h * jnp.tanh(h)


def _softplus(x):
    return jnp.maximum(x, 0.0) + jnp.log1p(jnp.exp(-jnp.abs(x)))


def _split3(a):
    hi = a.astype(BF16)
    r1 = a - hi.astype(F32)
    mid = r1.astype(BF16)
    lo = (r1 - mid.astype(F32)).astype(BF16)
    return hi, mid, lo


def _expand_heads(a, e3):
    hi, mid, lo = _split3(a)
    return jnp.dot(jnp.concatenate([hi, mid, lo], axis=1), e3, preferred_element_type=F32)


def _head_ms(x, bd):
    sq = x * x
    hi = sq.astype(BF16)
    lo = (sq - hi.astype(F32)).astype(BF16)
    outs = []
    for s in range(x.shape[1] // SLAB):
        sl = slice(s * SLAB, (s + 1) * SLAB)
        outs.append(jnp.dot(hi[:, sl], bd, preferred_element_type=F32)
                    + jnp.dot(lo[:, sl], bd, preferred_element_type=F32))
    return outs[0] if len(outs) == 1 else jnp.concatenate(outs, axis=1)


def _head_norm_rope(x, g, cos, sin, bd):
    xn = x * lax.rsqrt(_head_ms(x, bd) + EPS) * g
    tiles = []
    for t in range(x.shape[1] // LANES):
        xt = xn[:, t * LANES:(t + 1) * LANES]
        tiles.append(xt * cos + pltpu.roll(xt, LANES // 2, 1) * sin)
    return tiles[0] if len(tiles) == 1 else jnp.concatenate(tiles, axis=1)


def _unpair(x):
    q = HEAD_DIM // 2
    lane = lax.broadcasted_iota(jnp.int32, (x.shape[0], LANES), 1)
    tiles = []
    for t in range(x.shape[1] // LANES):
        xt = x[:, t * LANES:(t + 1) * LANES]
        nat = jnp.where((lane >= q) & (lane < 2 * q), pltpu.roll(xt, LANES - q, 1), xt)
        tiles.append(jnp.where((lane >= 2 * q) & (lane < 3 * q), pltpu.roll(xt, q, 1), nat))
    return tiles[0] if len(tiles) == 1 else jnp.concatenate(tiles, axis=1)


def _in_proj_kernel(x_ref, g_ref, w_ref, o_ref, xn_ref, *, tn):
    j = pl.program_id(1)

    @pl.when(j == 0)
    def _():
        x = x_ref[...]
        ms = jnp.mean(x * x, axis=-1, keepdims=True)
        xn_ref[...] = (x * lax.rsqrt(ms + EPS) * g_ref[...]).astype(BF16)

    for c in range(w_ref.shape[1] // tn):
        @pl.when(j == c)
        def _():
            o_ref[...] = jnp.dot(xn_ref[...], w_ref[:, c * tn:(c + 1) * tn], preferred_element_type=F32)


def _in_proj(x, g, w):
    m, k = x.shape
    n = w.shape[1]
    tm = _row_tile(m, 1024)
    tn = _col_tile(n)
    return pl.pallas_call(
        functools.partial(_in_proj_kernel, tn=tn),
        grid=(m // tm, n // tn),
        in_specs=[pl.BlockSpec((tm, k), lambda i, j: (i, 0)),
                  pl.BlockSpec((1, k), lambda i, j: (0, 0)),
                  pl.BlockSpec((k, n), lambda i, j: (0, 0), pipeline_mode=pl.Buffered(1))],
        out_specs=pl.BlockSpec((tm, tn), lambda i, j: (i, j)),
        out_shape=jax.ShapeDtypeStruct((m, n), F32),
        scratch_shapes=[pltpu.VMEM((tm, k), BF16)],
        compiler_params=_cparams(("arbitrary", "arbitrary"), WEIGHT_RESIDENT_VMEM_LIMIT),
        name="in_proj",
    )(x, g, w)


def _attn_prompt_kernel(q_ref, k_ref, v_ref, cos_ref, sin_ref, qg_ref, kg_ref, bd_ref, sink_ref,
                        o_ref, ko_ref, vo_ref, kbuf, vbuf, probs):
    i = pl.program_id(1)
    last = pl.num_programs(1) - 1

    @pl.when(i == 0)
    def _():
        kbuf[...] = jnp.zeros_like(kbuf)
        vbuf[...] = jnp.zeros_like(vbuf)

    cos, sin, bd = cos_ref[...], sin_ref[...], bd_ref[...]
    slot = i % 2
    lane = lax.broadcasted_iota(jnp.int32, (BLK, SLAB), 1)
    grp_k = 2 * (lane // LANES) + (lane % HEAD_DIM) // (HEAD_DIM // 2)
    grp_v = lane // HEAD_DIM
    r = lax.broadcasted_iota(jnp.int32, (BLK, 2 * BLK), 0)
    c = lax.broadcasted_iota(jnp.int32, (BLK, 2 * BLK), 1)
    key = c % BLK
    is_cur = (c // BLK) == slot
    prev_ok = jnp.where(i > 0, key, -1)
    mask = jnp.where(is_cur, (key <= r).astype(jnp.int32), (prev_ok > r + (BLK - WINDOW)).astype(jnp.int32)) > 0

    seqs = range(q_ref.shape[0])
    ks, vs, scores = [], [], []
    for u in seqs:
        q = _head_norm_rope(q_ref[u], qg_ref[...], cos, sin, bd) * (HEAD_DIM ** -0.5 * LOG2E)
        k = _head_norm_rope(k_ref[u], kg_ref[...], cos, sin, bd)
        v = v_ref[u]
        ks.append(k)
        vs.append(v)
        for p in range(KV_HEADS):
            rows = pl.ds(pl.multiple_of(p * 2 * BLK + slot * BLK, BLK), BLK)
            kbuf[u, rows, :] = jnp.where(grp_k == p, k, 0.0).astype(BF16)
            vbuf[u, rows, :] = jnp.where(grp_v == p, v, 0.0).astype(BF16)
        q_stack = jnp.concatenate([q[:, j * SLAB:(j + 1) * SLAB] for j in range(Q_PER_KV)],
                                  axis=0).astype(BF16)
        scores.append(lax.dot_general(q_stack, kbuf[u], (((1,), (1,)), ((), ())),
                                      preferred_element_type=F32))
    for u in seqs:
        for j in range(Q_PER_KV):
            for p in range(KV_HEADS):
                s = jnp.where(mask, scores[u][j * BLK:(j + 1) * BLK, p * 2 * BLK:(p + 1) * 2 * BLK], -jnp.inf)
                sink = sink_ref[j * KV_HEADS + p] * LOG2E
                mx = jnp.maximum(jnp.max(s, axis=-1, keepdims=True), sink)
                e = jnp.exp2(s - mx)
                den = jnp.sum(e, axis=-1, keepdims=True) + jnp.exp2(sink - mx)
                probs[u, j * BLK:(j + 1) * BLK, p * 2 * BLK:(p + 1) * 2 * BLK] = (e / den).astype(BF16)
    for u in seqs:
        pv = jnp.dot(probs[u], vbuf[u], preferred_element_type=F32)
        for j in range(Q_PER_KV):
            o_ref[u, :, j * SLAB:(j + 1) * SLAB] = pv[j * BLK:(j + 1) * BLK, :].astype(o_ref.dtype)

    @pl.when(i == last)
    def _():
        for u in seqs:
            ko_ref[u] = _unpair(ks[u]).T
            vo_ref[u] = vs[u].T


ATTN_SEQS = 2


def _attn_prompt(p_act, lay, batch, seq, cos, sin, qg, kg, bd, sinks):
    nb = seq // BLK
    u = ATTN_SEQS if batch % ATTN_SEQS == 0 else 1
    qc, kc, vc = lay["q"][0] // Q_DIM, lay["k"][0] // KV_DIM, lay["v"][0] // KV_DIM
    p3 = p_act.reshape(batch, seq, p_act.shape[1])
    return pl.pallas_call(
        _attn_prompt_kernel,
        grid=(batch // u, nb),
        in_specs=[pl.BlockSpec((u, BLK, Q_DIM), lambda b, i: (b, i, qc)),
                  pl.BlockSpec((u, BLK, KV_DIM), lambda b, i: (b, i, kc)),
                  pl.BlockSpec((u, BLK, KV_DIM), lambda b, i: (b, i, vc)),
                  pl.BlockSpec((BLK, LANES), lambda b, i: (i, 0)),
                  pl.BlockSpec((BLK, LANES), lambda b, i: (i, 0)),
                  pl.BlockSpec((1, Q_DIM), lambda b, i: (0, 0)),
                  pl.BlockSpec((1, KV_DIM), lambda b, i: (0, 0)),
                  pl.BlockSpec((SLAB, SLAB), lambda b, i: (0, 0)),
                  pl.BlockSpec(memory_space=pltpu.SMEM)],
        out_specs=[pl.BlockSpec((u, BLK, Q_DIM), lambda b, i: (b, i, 0)),
                   pl.BlockSpec((u, KV_DIM, BLK), lambda b, i: (b, 0, 0)),
                   pl.BlockSpec((u, KV_DIM, BLK), lambda b, i: (b, 0, 0))],
        out_shape=[jax.ShapeDtypeStruct((batch, seq, Q_DIM), BF16),
                   jax.ShapeDtypeStruct((batch, KV_DIM, BLK), F32),
                   jax.ShapeDtypeStruct((batch, KV_DIM, BLK), F32)],
        scratch_shapes=[pltpu.VMEM((u, KV_HEADS * 2 * BLK, KV_DIM), BF16),
                        pltpu.VMEM((u, KV_HEADS * 2 * BLK, KV_DIM), BF16),
                        pltpu.VMEM((u, Q_PER_KV * BLK, KV_HEADS * 2 * BLK), BF16)],
        compiler_params=_cparams(("arbitrary", "arbitrary")),
        name="attn_prompt",
    )(p3, p3, p3, cos, sin, qg, kg, bd, sinks)


def _ssd_chunk(slot, z_ref, xs_ref, b_ref, c_ref, dt_ref, cw_ref, cb_ref, dtb_ref, alog_ref, dskip_ref,
               nw_ref, e3_ref, o_ref, tail_ref, xpad, tails, st, ybuf):
    xpad[0:SUBLANES, :] = tails[1 - slot]
    xpad[SUBLANES:SUBLANES + BLK, 0:D_INNER] = xs_ref[...]
    xpad[SUBLANES:SUBLANES + BLK, D_INNER:D_INNER + BC_DIM] = b_ref[...]
    xpad[SUBLANES:SUBLANES + BLK, D_INNER + BC_DIM:CONV_DIM] = c_ref[...]
    cwh = 0.5 * cw_ref[...]
    acc = 0.5 * cb_ref[...] + cwh[CONV_W - 1:CONV_W, :] * xpad[SUBLANES:SUBLANES + BLK, :]
    for t in range(1, CONV_W):
        acc = acc + cwh[CONV_W - 1 - t:CONV_W - t, :] * xpad[SUBLANES - t:SUBLANES - t + BLK, :]
    xc = acc + acc * jnp.tanh(acc)
    new_tail = xpad[BLK:BLK + SUBLANES, :]
    tail_ref[...] = new_tail
    tails[slot] = new_tail

    xs = xc[:, 0:D_INNER]
    bm = xc[:, D_INNER:D_INNER + BC_DIM].astype(BF16)
    cm = xc[:, D_INNER + BC_DIM:CONV_DIM].astype(BF16)

    e3 = e3_ref[...]
    dt = _softplus(dt_ref[...] + dtb_ref[...])
    dta = dt * (-LOG2E * jnp.exp(alog_ref[...]))
    row = lax.broadcasted_iota(jnp.int32, (BLK, BLK), 0)
    col = lax.broadcasted_iota(jnp.int32, (BLK, BLK), 1)
    causal = row >= col
    cum = jnp.dot(causal.astype(F32), dta, preferred_element_type=F32, precision=lax.Precision.HIGHEST)
    cum_t = cum.T
    ecum = jnp.exp2(cum)
    to_end = jnp.exp2(cum[BLK - 1:BLK, :] - cum) * dt
    dt_e = _expand_heads(dt, e3)
    ecum_e = _expand_heads(ecum, e3)
    to_end_e = _expand_heads(to_end, e3)
    xdt = (xs * dt_e).astype(BF16)
    xte = (xs * to_end_e).astype(BF16)
    lane = lax.broadcasted_iota(jnp.int32, (BLK, LANES), 1)
    first_head = lane < SSM_HEAD_DIM

    for g in range(SSM_GROUPS):
        gl = slice(g * GROUP_LANES, (g + 1) * GROUP_LANES)
        bg = bm[:, g * D_STATE:(g + 1) * D_STATE]
        cg = cm[:, g * D_STATE:(g + 1) * D_STATE]
        cbg = lax.dot_general(cg, bg, (((1,), (1,)), ((), ())), preferred_element_type=F32)
        st_g = st[1 - slot, :, gl]
        y_inter = jnp.dot(cg, st_g.astype(BF16), preferred_element_type=F32) * ecum_e[:, gl]
        for pr in range(HEADS_PER_GROUP // 2):
            h0 = g * HEADS_PER_GROUP + 2 * pr
            xd = xdt[:, h0 * SSM_HEAD_DIM:(h0 + 2) * SSM_HEAD_DIM]
            ys = []
            for h in (h0, h0 + 1):
                diff = cum[:, h:h + 1] - cum_t[h:h + 1, :]
                w = (jnp.exp2(jnp.where(causal, diff, -jnp.inf)) * cbg).astype(BF16)
                ys.append(jnp.dot(w, xd, preferred_element_type=F32))
            lo = pr * LANES
            ybuf[:, h0 * SSM_HEAD_DIM:(h0 + 2) * SSM_HEAD_DIM] = (
                jnp.where(first_head, ys[0], ys[1]) + y_inter[:, lo:lo + LANES])
        upd = lax.dot_general(bg, xte[:, gl], (((0,), (0,)), ((), ())), preferred_element_type=F32)
        st[slot, :, gl] = st_g * ecum_e[BLK - 1:BLK, gl] + upd

    y = ybuf[...] + dskip_ref[...] * xs
    yz = y * _silu(z_ref[...])
    ms = jnp.mean(yz * yz, axis=-1, keepdims=True)
    o_ref[...] = (yz * lax.rsqrt(ms + EPS) * nw_ref[...]).astype(o_ref.dtype)


def _ssd_prompt_kernel(z_ref, xs_ref, b_ref, c_ref, dt_ref, cw_ref, cb_ref, dtb_ref, alog_ref, dskip_ref,
                       nw_ref, e3_ref, o_ref, tail_ref, hfin_ref, xpad, tails, st, ybuf):
    i = pl.program_id(1)
    last = pl.num_programs(1) - 1

    @pl.when(i == 0)
    def _():
        tails[...] = jnp.zeros_like(tails)
        st[...] = jnp.zeros_like(st)

    slot = i % 2
    seqs = range(z_ref.shape[0])
    for u in seqs:
        _ssd_chunk(slot, z_ref.at[u], xs_ref.at[u], b_ref.at[u], c_ref.at[u], dt_ref.at[u], cw_ref, cb_ref,
                   dtb_ref, alog_ref, dskip_ref, nw_ref, e3_ref, o_ref.at[u], tail_ref.at[u], xpad.at[u],
                   tails.at[u], st.at[u], ybuf.at[u])

    @pl.when(i == last)
    def _():
        for u in seqs:
            hfin_ref[u] = st[u, slot].T


SSD_SEQS = 2


def _ssd_prompt(p_act, lay, batch, seq, cw, cb, dtb, alog, dskip, nw, e3):
    nb = seq // BLK
    u = SSD_SEQS if batch % SSD_SEQS == 0 else 1
    p3 = p_act.reshape(batch, seq, p_act.shape[1])
    zc, xc = lay["z"][0] // D_INNER, lay["xs"][0] // D_INNER
    bc, cc, dc = lay["B"][0] // BC_DIM, lay["C"][0] // BC_DIM, lay["dt"][0] // DT_PAD
    const = lambda b, i: (0, 0)
    return pl.pallas_call(
        _ssd_prompt_kernel,
        grid=(batch // u, nb),
        in_specs=[pl.BlockSpec((u, BLK, D_INNER), lambda b, i: (b, i, zc)),
                  pl.BlockSpec((u, BLK, D_INNER), lambda b, i: (b, i, xc)),
                  pl.BlockSpec((u, BLK, BC_DIM), lambda b, i: (b, i, bc)),
                  pl.BlockSpec((u, BLK, BC_DIM), lambda b, i: (b, i, cc)),
                  pl.BlockSpec((u, BLK, DT_PAD), lambda b, i: (b, i, dc)),
                  pl.BlockSpec((CONV_W, CONV_DIM), const),
                  pl.BlockSpec((1, CONV_DIM), const),
                  pl.BlockSpec((1, DT_PAD), const),
                  pl.BlockSpec((1, DT_PAD), const),
                  pl.BlockSpec((1, D_INNER), const),
                  pl.BlockSpec((1, D_INNER), const),
                  pl.BlockSpec((3 * DT_PAD, D_INNER), const)],
        out_specs=[pl.BlockSpec((u, BLK, D_INNER), lambda b, i: (b, i, 0)),
                   pl.BlockSpec((u, SUBLANES, CONV_DIM), lambda b, i: (b, 0, 0)),
                   pl.BlockSpec((u, D_INNER, D_STATE), lambda b, i: (b, 0, 0))],
        out_shape=[jax.ShapeDtypeStruct((batch, seq, D_INNER), BF16),
                   jax.ShapeDtypeStruct((batch, SUBLANES, CONV_DIM), F32),
                   jax.ShapeDtypeStruct((batch, D_INNER, D_STATE), F32)],
        scratch_shapes=[pltpu.VMEM((u, BLK + SUBLANES, CONV_DIM), F32),
                        pltpu.VMEM((u, 2, SUBLANES, CONV_DIM), F32),
                        pltpu.VMEM((u, 2, D_STATE, D_INNER), F32),
                        pltpu.VMEM((u, BLK, D_INNER), F32)],
        compiler_params=_cparams(("arbitrary", "arbitrary")),
        name="ssd_prompt",
    )(p3, p3, p3, p3, p3, cw, cb, dtb, alog, dskip, nw, e3)


def _decode_pre_kernel(q_ref, k_ref, xs_ref, b_ref, c_ref, dt_ref, cs_ref, cos_ref, sin_ref, qg_ref, kg_ref,
                       bd_ref, cw_ref, cb_ref, dtb_ref, alog_ref, e3_ref,
                       qo_ref, ko_ref, xc_ref, xdt_ref, dec_ref, cso_ref):
    cos, sin, bd = cos_ref[...], sin_ref[...], bd_ref[...]
    qo_ref[...] = _unpair(_head_norm_rope(q_ref[...], qg_ref[...], cos, sin, bd)) * (HEAD_DIM ** -0.5)
    ko_ref[...] = _unpair(_head_norm_rope(k_ref[...], kg_ref[...], cos, sin, bd))

    segs = ((xs_ref, 0, D_INNER), (b_ref, D_INNER, BC_DIM), (c_ref, D_INNER + BC_DIM, BC_DIM))
    for ref, off, width in segs:
        new = ref[...]
        acc = cb_ref[:, off:off + width] + cw_ref[CONV_W - 1:CONV_W, off:off + width] * new
        for t in range(CONV_W - 1):
            lo = t * CONV_DIM + off
            acc = acc + cw_ref[t:t + 1, off:off + width] * cs_ref[:, lo:lo + width]
        xc_ref[:, off:off + width] = _silu(acc)
        for t in range(CONV_W - 2):
            cso_ref[:, t * CONV_DIM + off:t * CONV_DIM + off + width] = (
                cs_ref[:, (t + 1) * CONV_DIM + off:(t + 1) * CONV_DIM + off + width])
        lo = (CONV_W - 2) * CONV_DIM + off
        cso_ref[:, lo:lo + width] = new

    e3 = e3_ref[...]
    dt = _softplus(dt_ref[...] + dtb_ref[...])
    decay = jnp.exp(dt * (-jnp.exp(alog_ref[...])))
    xdt_ref[...] = xc_ref[:, 0:D_INNER] * _expand_heads(dt, e3)
    dec_ref[...] = _expand_heads(decay, e3)


def _decode_pre(p_act, lay, conv_state, cos, sin, qg, kg, bd, cw, cb, dtb, alog, e3):
    m = p_act.shape[0]
    full = lambda shape: pl.BlockSpec(shape, lambda i: (0, 0))

    def col(name):
        c = lay[name][0] // lay[name][1]
        return pl.BlockSpec((m, lay[name][1]), lambda i: (0, c))

    cs_w = (CONV_W - 1) * CONV_DIM
    return pl.pallas_call(
        _decode_pre_kernel,
        grid=(1,),
        in_specs=[col("q"), col("k"), col("xs"), col("B"), col("C"), col("dt"),
                  full((m, cs_w)), full((1, LANES)), full((1, LANES)), full((1, Q_DIM)), full((1, KV_DIM)),
                  full((SLAB, SLAB)), full((CONV_W, CONV_DIM)), full((1, CONV_DIM)), full((1, DT_PAD)),
                  full((1, DT_PAD)), full((3 * DT_PAD, D_INNER))],
        out_specs=[full((m, Q_DIM)), full((m, KV_DIM)), full((m, CONV_DIM)), full((m, D_INNER)),
                   full((m, D_INNER)), full((m, cs_w))],
        out_shape=[jax.ShapeDtypeStruct((m, Q_DIM), F32), jax.ShapeDtypeStruct((m, KV_DIM), F32),
                   jax.ShapeDtypeStruct((m, CONV_DIM), F32), jax.ShapeDtypeStruct((m, D_INNER), F32),
                   jax.ShapeDtypeStruct((m, D_INNER), F32), jax.ShapeDtypeStruct((m, cs_w), F32)],
        compiler_params=_cparams(("arbitrary",)),
        name="decode_pre",
    )(p_act, p_act, p_act, p_act, p_act, p_act, conv_state, cos, sin, qg, kg, bd, cw, cb, dtb, alog, e3)


DEC_ROWS = 8


def _attn_decode_kernel(q_ref, k_ref, v_ref, ck_ref, cv_ref, sink_ref, o_ref, cko_ref, cvo_ref):
    w = ck_ref.shape[2]
    nq = N_HEADS
    r = lax.broadcasted_iota(jnp.int32, (nq, SLAB), 0)
    grp = lax.broadcasted_iota(jnp.int32, (nq, SLAB), 1) // HEAD_DIM
    own = grp == (r % KV_HEADS)
    in_window = lax.broadcasted_iota(jnp.int32, (nq, w), 1) > (w - WINDOW - 1)
    newest = lax.broadcasted_iota(jnp.int32, (KV_DIM, w), 1) == w - 1
    sink = sink_ref[...][:, 0:1]
    pad = jnp.zeros((DEC_ROWS, KV_DIM), BF16)
    k_parts = jnp.concatenate(list(_split3(k_ref[...])) + [pad], axis=0)
    v_parts = jnp.concatenate(list(_split3(v_ref[...])) + [pad], axis=0)
    part_row = lax.broadcasted_iota(jnp.int32, (4 * DEC_ROWS, w), 0) % DEC_ROWS
    tdims = (((0,), (0,)), ((), ()))
    for bl in range(DEC_ROWS):
        qrow = q_ref[bl:bl + 1, :]
        qm = jnp.zeros((nq, SLAB), F32)
        for j in range(Q_PER_KV):
            slab = jnp.broadcast_to(qrow[:, j * SLAB:(j + 1) * SLAB], (nq, SLAB))
            qm = jnp.where((r // KV_HEADS) == j, slab, qm)
        qm = jnp.where(own, qm, 0.0).astype(BF16)
        pick = jnp.where(part_row == bl, 1.0, 0.0).astype(BF16)
        k_col = lax.dot_general(k_parts, pick, tdims, preferred_element_type=F32)
        v_col = lax.dot_general(v_parts, pick, tdims, preferred_element_type=F32)
        keys = jnp.where(newest, k_col, pltpu.roll(ck_ref[bl], w - 1, 1))
        vals = jnp.where(newest, v_col, pltpu.roll(cv_ref[bl], w - 1, 1))
        cko_ref[bl] = keys
        cvo_ref[bl] = vals
        s = jnp.dot(qm, keys.astype(BF16), preferred_element_type=F32)
        s = jnp.where(in_window, s, -jnp.inf)
        mx = jnp.maximum(jnp.max(s, axis=-1, keepdims=True), sink)
        e = jnp.exp(s - mx)
        den = jnp.sum(e, axis=-1, keepdims=True) + jnp.exp(sink - mx)
        pv = lax.dot_general(e.astype(BF16), vals.astype(BF16), (((1,), (1,)), ((), ())),
                             preferred_element_type=F32)
        pv = jnp.where(own, pv / den, 0.0)
        for j in range(Q_PER_KV):
            o_ref[bl:bl + 1, j * SLAB:(j + 1) * SLAB] = jnp.sum(
                pv[j * KV_HEADS:(j + 1) * KV_HEADS, :], axis=0, keepdims=True)


def _attn_decode(q, k, p_act, lay, cache_k, cache_v, sink_rows):
    m = q.shape[0]
    w = cache_k.shape[2]
    vc = lay["v"][0] // KV_DIM
    cache_spec = pl.BlockSpec((DEC_ROWS, KV_DIM, w), lambda i: (i, 0, 0))
    return pl.pallas_call(
        _attn_decode_kernel,
        grid=(m // DEC_ROWS,),
        in_specs=[pl.BlockSpec((DEC_ROWS, Q_DIM), lambda i: (i, 0)),
                  pl.BlockSpec((DEC_ROWS, KV_DIM), lambda i: (i, 0)),
                  pl.BlockSpec((DEC_ROWS, KV_DIM), lambda i: (i, vc)),
                  cache_spec, cache_spec,
                  pl.BlockSpec((N_HEADS, LANES), lambda i: (0, 0))],
        out_specs=[pl.BlockSpec((DEC_ROWS, Q_DIM), lambda i: (i, 0)), cache_spec, cache_spec],
        out_shape=[jax.ShapeDtypeStruct((m, Q_DIM), F32),
                   jax.ShapeDtypeStruct((m, KV_DIM, w), F32),
                   jax.ShapeDtypeStruct((m, KV_DIM, w), F32)],
        compiler_params=_cparams(("arbitrary",)),
        name="attn_decode",
    )(q, k, p_act, cache_k, cache_v, sink_rows)


MM_ROWS = 16


def _ssm_decode_kernel(st_ref, xdt_ref, dec_ref, xc_ref, z_ref, dskip_ref, nw_ref, sto_ref, o_ref):
    r = lax.broadcasted_iota(jnp.int32, (MM_ROWS, D_INNER), 0)
    grp = lax.broadcasted_iota(jnp.int32, (MM_ROWS, D_INNER), 1) // GROUP_LANES
    rr = lax.broadcasted_iota(jnp.int32, (MM_ROWS, D_STATE), 0)
    ones_rows = jnp.where((rr >= SSM_GROUPS) & (rr < SSM_GROUPS + 3), 1.0, 0.0)
    bc = lambda a: jnp.broadcast_to(a, (MM_ROWS, D_INNER))
    for u in range(st_ref.shape[0]):
        h = st_ref[u]
        xdt = xdt_ref[u:u + 1, :]
        xc = xc_ref[u:u + 1, :]
        hi, mid, lo = _split3(dec_ref[u:u + 1, :])
        lhs_t = jnp.where(r == grp, bc(xdt), 0.0)
        for t, piece in enumerate((hi, mid, lo)):
            lhs_t = jnp.where(r == SSM_GROUPS + t, bc(piece.astype(F32)), lhs_t)
        lhs_t = lhs_t.astype(BF16)
        b_rows = jnp.zeros((MM_ROWS, D_STATE), F32)
        c_rows = jnp.zeros((MM_ROWS, D_STATE), F32)
        for g in range(SSM_GROUPS):
            b_g = xc[:, D_INNER + g * D_STATE:D_INNER + (g + 1) * D_STATE]
            c_g = xc[:, D_INNER + BC_DIM + g * D_STATE:D_INNER + BC_DIM + (g + 1) * D_STATE]
            b_rows = jnp.where(rr == g, jnp.broadcast_to(b_g, (MM_ROWS, D_STATE)), b_rows)
            c_rows = jnp.where(rr == g, jnp.broadcast_to(c_g, (MM_ROWS, D_STATE)), c_rows)
        rhs = jnp.concatenate([b_rows, ones_rows], axis=1).astype(BF16)
        both = lax.dot_general(lhs_t, rhs, (((0,), (0,)), ((), ())), preferred_element_type=F32)
        h_new = both[:, D_STATE:] * h + both[:, :D_STATE]
        sto_ref[u] = h_new
        yg = lax.dot_general(c_rows.astype(BF16), h_new.astype(BF16), (((1,), (1,)), ((), ())),
                             preferred_element_type=F32)
        y = jnp.sum(jnp.where(r == grp, yg, 0.0), axis=0, keepdims=True)
        y = y + dskip_ref[...] * xc[:, 0:D_INNER]
        yz = y * _silu(z_ref[u:u + 1, :])
        ms = jnp.mean(yz * yz, axis=-1, keepdims=True)
        o_ref[u:u + 1, :] = yz * lax.rsqrt(ms + EPS) * nw_ref[...]


SSM_DEC_ROWS = SUBLANES


def _ssm_decode(state, xdt, dec, xc, p_act, lay, dskip, nw):
    m = state.shape[0]
    u = SSM_DEC_ROWS
    assert m % u == 0
    zc = lay["z"][0] // D_INNER
    row = lambda width: pl.BlockSpec((u, width), lambda b: (b, 0))
    return pl.pallas_call(
        _ssm_decode_kernel,
        grid=(m // u,),
        in_specs=[pl.BlockSpec((u, D_INNER, D_STATE), lambda b: (b, 0, 0)),
                  row(D_INNER), row(D_INNER), row(CONV_DIM),
                  pl.BlockSpec((u, D_INNER), lambda b: (b, zc)),
                  pl.BlockSpec((1, D_INNER), lambda b: (0, 0)),
                  pl.BlockSpec((1, D_INNER), lambda b: (0, 0))],
        out_specs=[pl.BlockSpec((u, D_INNER, D_STATE), lambda b: (b, 0, 0)), row(D_INNER)],
        out_shape=[jax.ShapeDtypeStruct((m, D_INNER, D_STATE), F32),
                   jax.ShapeDtypeStruct((m, D_INNER), F32)],
        compiler_params=_cparams(("arbitrary",)),
        name="ssm_decode",
    )(state, xdt, dec, xc, p_act, dskip, nw)


def _merge_mlp_kernel(x_ref, oa_ref, os_ref, ga_ref, gs_ref, wa_ref, ws_ref, wo_ref, g_ref, wu_ref, wd_ref,
                      o_ref, *, ff_tile):
    a = jnp.dot(oa_ref[...].astype(BF16), wa_ref[...], preferred_element_type=F32)
    s = jnp.dot(os_ref[...].astype(BF16), ws_ref[...], preferred_element_type=F32)
    mixed = _sigmoid(ga_ref[...]) * a + _sigmoid(gs_ref[...]) * s
    x = x_ref[...] + jnp.dot(mixed.astype(BF16), wo_ref[...], preferred_element_type=F32)
    ms = jnp.mean(x * x, axis=-1, keepdims=True)
    xn = (x * lax.rsqrt(ms + EPS) * g_ref[...]).astype(BF16)
    acc = x
    for c in range(wu_ref.shape[1] // ff_tile):
        u = jnp.dot(xn, wu_ref[:, c * ff_tile:(c + 1) * ff_tile], preferred_element_type=F32)
        act = jnp.square(jnp.maximum(u, 0.0)).astype(BF16)
        acc = acc + jnp.dot(act, wd_ref[c * ff_tile:(c + 1) * ff_tile, :], preferred_element_type=F32)
    o_ref[...] = acc


def _merge_mlp(x, o_attn, o_ssm, p_act, lay, wa, ws, wo, g, wu, wd):
    m, d = x.shape
    ff = wu.shape[1]
    tm = _row_tile(m, 512)
    gac, gsc = lay["ga"][0] // d, lay["gs"][0] // d
    weight = lambda shape: pl.BlockSpec(shape, lambda i: (0, 0), pipeline_mode=pl.Buffered(1))
    return pl.pallas_call(
        functools.partial(_merge_mlp_kernel, ff_tile=min(1024, ff)),
        grid=(m // tm,),
        in_specs=[pl.BlockSpec((tm, d), lambda i: (i, 0)),
                  pl.BlockSpec((tm, Q_DIM), lambda i: (i, 0)),
                  pl.BlockSpec((tm, D_INNER), lambda i: (i, 0)),
                  pl.BlockSpec((tm, d), lambda i: (i, gac)),
                  pl.BlockSpec((tm, d), lambda i: (i, gsc)),
                  weight((Q_DIM, d)), weight((D_INNER, d)), weight((d, d)),
                  weight((1, d)), weight((d, ff)), weight((ff, d))],
        out_specs=pl.BlockSpec((tm, d), lambda i: (i, 0)),
        out_shape=jax.ShapeDtypeStruct((m, d), F32),
        compiler_params=_cparams(("arbitrary",), WEIGHT_RESIDENT_VMEM_LIMIT),
        name="merge_mlp",
    )(x, o_attn, o_ssm, p_act, p_act, wa, ws, wo, g, wu, wd)


def _rope_tables(pos):
    half = HEAD_DIM // 2
    inv = ROPE_THETA ** (-jnp.arange(half, dtype=F32) / half)
    ang = pos.astype(F32)[:, None] * inv[None, :]
    cos, sin = jnp.cos(ang), jnp.sin(ang)
    return (jnp.concatenate([cos, cos, cos, cos], axis=1),
            jnp.concatenate([-sin, -sin, sin, sin], axis=1))


def _constants():
    heads = np.arange(DT_PAD)[:, None]
    cols = np.arange(D_INNER)[None, :] // SSM_HEAD_DIM
    e = (heads == cols).astype(np.float32)
    e3 = jnp.asarray(np.concatenate([e, e, e], axis=0), BF16)
    a = np.arange(SLAB)
    grp = 2 * (a // LANES) + (a % HEAD_DIM) // (HEAD_DIM // 2)
    bd = jnp.asarray((grp[:, None] == grp[None, :]).astype(np.float32) / HEAD_DIM, BF16)
    return e3, bd


_HALF = HEAD_DIM // 2


def _pair_q(a):
    lead = a.shape[:-1]
    n = len(lead)
    a = a.reshape(lead + (2, 2, Q_PER_KV, 2, _HALF))
    return a.transpose(tuple(range(n)) + (n + 2, n, n + 3, n + 1, n + 4)).reshape(lead + (Q_DIM,))


def _pair_k(a):
    lead = a.shape[:-1]
    n = len(lead)
    a = a.reshape(lead + (2, 2, 2, _HALF))
    return a.transpose(tuple(range(n)) + (n, n + 2, n + 1, n + 3)).reshape(lead + (KV_DIM,))


def _prep_layer(d_model, norm_mix, w_in, q_norm, k_norm, attn_sinks, conv_w, conv_b, dt_bias, a_log, d_skip,
                ssm_norm, w_attn_o, w_ssm_o, w_out, norm_mlp, w_up, w_down):
    assert KV_HEADS == 4 and Q_PER_KV == 4
    e3, bd = _constants()
    o_q, o_k, o_v, o_z, o_xbc, o_dt = (0, Q_DIM, Q_DIM + KV_DIM, Q_DIM + 2 * KV_DIM,
                                       Q_DIM + 2 * KV_DIM + D_INNER, Q_DIM + 2 * KV_DIM + D_INNER + CONV_DIM)
    o_g = o_dt + SSM_HEADS
    cut = lambda lo, width: w_in[:, lo:lo + width]
    pieces = {"z": cut(o_z, D_INNER), "xs": cut(o_xbc, D_INNER), "ga": cut(o_g, d_model),
              "gs": cut(o_g + d_model, d_model), "q": _pair_q(cut(o_q, Q_DIM)),
              "B": cut(o_xbc + D_INNER, BC_DIM), "C": cut(o_xbc + D_INNER + BC_DIM, BC_DIM),
              "k": _pair_k(cut(o_k, KV_DIM)), "v": cut(o_v, KV_DIM),
              "dt": jnp.pad(cut(o_dt, SSM_HEADS), ((0, 0), (0, DT_PAD - SSM_HEADS)))}
    lay, _ = _layout(d_model)
    w_p = jnp.concatenate([pieces[name] for name in lay], axis=1).astype(BF16)
    pad_heads = lambda a: jnp.pad(a, (0, DT_PAD - SSM_HEADS))[None, :]
    sink_p = attn_sinks.reshape(KV_HEADS, Q_PER_KV).T.reshape(N_HEADS)
    wa = w_attn_o.reshape(KV_HEADS, Q_PER_KV, HEAD_DIM, d_model).transpose(1, 0, 2, 3).reshape(Q_DIM, d_model)
    return dict(
        lay=lay, e3=e3, bd=bd, w_p=w_p,
        norm_mix=norm_mix[None, :], norm_mlp=norm_mlp[None, :],
        qg=_pair_q(jnp.tile(q_norm, N_HEADS))[None, :], kg=_pair_k(jnp.tile(k_norm, KV_HEADS))[None, :],
        sink_p=sink_p, sink_rows=jnp.broadcast_to(sink_p[:, None], (N_HEADS, LANES)),
        cw=conv_w, cb=conv_b[None, :], dtb=pad_heads(dt_bias), alog=pad_heads(a_log),
        dskip=jnp.repeat(d_skip, SSM_HEAD_DIM)[None, :], nw=ssm_norm[None, :],
        wa=wa.astype(BF16), ws=w_ssm_o.astype(BF16), wo=w_out.astype(BF16),
        wu=w_up.astype(BF16), wd=w_down.astype(BF16))


def _prompt_layer(x, lw):
    batch, seq, d = x.shape
    assert seq % BLK == 0 and seq >= WINDOW
    x2 = x.reshape(batch * seq, d)
    p_act = _in_proj(x2, lw["norm_mix"], lw["w_p"])
    cos, sin = _rope_tables(jnp.arange(seq))
    o_attn, k_last, v_last = _attn_prompt(p_act, lw["lay"], batch, seq, cos, sin, lw["qg"], lw["kg"],
                                          lw["bd"], lw["sink_p"])
    o_ssm, tail, h_fin = _ssd_prompt(p_act, lw["lay"], batch, seq, lw["cw"], lw["cb"], lw["dtb"], lw["alog"],
                                     lw["dskip"], lw["nw"], lw["e3"])
    y = _merge_mlp(x2, o_attn.reshape(batch * seq, Q_DIM), o_ssm.reshape(batch * seq, D_INNER), p_act,
                   lw["lay"], lw["wa"], lw["ws"], lw["wo"],
                   lw["norm_mlp"], lw["wu"], lw["wd"])
    return (y.reshape(batch, seq, d),
            k_last.reshape(batch, KV_HEADS, HEAD_DIM, BLK).transpose(0, 3, 1, 2),
            v_last.reshape(batch, KV_HEADS, HEAD_DIM, BLK).transpose(0, 3, 1, 2),
            tail[:, SUBLANES - (CONV_W - 1):, :],
            h_fin.reshape(batch, SSM_HEADS, SSM_HEAD_DIM, D_STATE))


def _decode_layer(x, cache_k, cache_v, conv_state, ssm_state, lw):
    m, t, d = x.shape
    w = cache_k.shape[1]
    assert t == 1 and w == WINDOW and m % DEC_ROWS == 0
    x2 = x.reshape(m, d)
    p_act = _in_proj(x2, lw["norm_mix"], lw["w_p"])
    cos, sin = _rope_tables(PAST_LEN + jnp.arange(1))
    q, k, xc, xdt, dec, conv_new = _decode_pre(
        p_act, lw["lay"], conv_state.reshape(m, (CONV_W - 1) * CONV_DIM), cos, sin, lw["qg"], lw["kg"],
        lw["bd"], lw["cw"], lw["cb"], lw["dtb"], lw["alog"], lw["e3"])
    ck = jnp.transpose(cache_k, (0, 2, 3, 1)).reshape(m, KV_DIM, w)
    cv = jnp.transpose(cache_v, (0, 2, 3, 1)).reshape(m, KV_DIM, w)
    o_attn, ck_new, cv_new = _attn_decode(q, k, p_act, lw["lay"], ck, cv, lw["sink_rows"])
    h_new, o_ssm = _ssm_decode(ssm_state.reshape(m, D_INNER, D_STATE), xdt, dec, xc, p_act, lw["lay"],
                               lw["dskip"], lw["nw"])
    y = _merge_mlp(x2, o_attn, o_ssm, p_act, lw["lay"], lw["wa"], lw["ws"], lw["wo"],
                   lw["norm_mlp"], lw["wu"], lw["wd"])
    unview = lambda c: jnp.transpose(c.reshape(m, KV_HEADS, HEAD_DIM, w), (0, 3, 1, 2))
    return (y.reshape(m, 1, d), unview(ck_new), unview(cv_new), conv_new.reshape(m, CONV_W - 1, CONV_DIM),
            h_new.reshape(m, SSM_HEADS, SSM_HEAD_DIM, D_STATE))


def kernel(x_prompt, x_sample, cache_k, cache_v, state_conv, state_ssm, norm_mix, w_in, q_norm, k_norm,
           attn_sinks, conv_w, conv_b, dt_bias, a_log, d_skip, ssm_norm, w_attn_o, w_ssm_o, w_out,
           norm_mlp, w_up, w_down):
    depth = w_in.shape[0]
    d_model = x_prompt.shape[-1]
    yp, ys = x_prompt, x_sample
    cols = [[] for _ in range(8)]
    for l in range(depth):
        lw = _prep_layer(d_model, norm_mix[l], w_in[l], q_norm[l], k_norm[l], attn_sinks[l], conv_w[l],
                         conv_b[l], dt_bias[l], a_log[l], d_skip[l], ssm_norm[l], w_attn_o[l], w_ssm_o[l],
                         w_out[l], norm_mlp[l], w_up[l], w_down[l])
        yp, kp, vp, cp, hp = _prompt_layer(yp, lw)
        ys, ks, vs, cs, hs = _decode_layer(ys, cache_k[l], cache_v[l], state_conv[l], state_ssm[l], lw)
        for lst, val in zip(cols, (kp, vp, cp, hp, ks, vs, cs, hs)):
            lst.append(val)
    return (yp, ys) + tuple(jnp.stack(c) for c in cols)
```

```python
import functools

import numpy as np
import jax
import jax.numpy as jnp
from jax import lax
from jax.experimental import pallas as pl
from jax.experimental.pallas import tpu as pltpu

F32 = jnp.float32
BF16 = jnp.bfloat16

N_HEADS = 16
KV_HEADS = 4
HEAD_DIM = 64
Q_PER_KV = N_HEADS // KV_HEADS
WINDOW = 128
ROPE_THETA = 10000.0
Q_DIM = N_HEADS * HEAD_DIM
KV_DIM = KV_HEADS * HEAD_DIM
SSM_HEAD_DIM = 64
SSM_HEADS = 32
D_INNER = SSM_HEADS * SSM_HEAD_DIM
SSM_GROUPS = 4
HEADS_PER_GROUP = SSM_HEADS // SSM_GROUPS
D_STATE = 128
BC_DIM = SSM_GROUPS * D_STATE
CONV_W = 4
CONV_DIM = D_INNER + 2 * BC_DIM
SSD_CHUNK = 128
EPS = 1e-6
PAST_LEN = 8192
LOG2E = 1.4426950408889634

LANES = 128
SUBLANES = 8
VMEM_LIMIT = 48 * 1024 * 1024
WEIGHT_RESIDENT_VMEM_LIMIT = 56 * 1024 * 1024

BLK = 128
assert WINDOW == BLK and SSD_CHUNK == BLK
SLAB = KV_HEADS * HEAD_DIM
GROUP_LANES = HEADS_PER_GROUP * SSM_HEAD_DIM
DT_PAD = LANES


def _layout(d_model):
    segs = [("z", D_INNER), ("xs", D_INNER), ("ga", d_model), ("gs", d_model), ("q", Q_DIM),
            ("B", BC_DIM), ("C", BC_DIM), ("k", KV_DIM), ("v", KV_DIM), ("dt", DT_PAD)]
    lay, off = {}, 0
    for name, width in segs:
        assert off % width == 0, (name, off, width)
        lay[name] = (off, width)
        off += width
    return lay, off


def _col_tile(n, cap=3072):
    units = n // LANES
    best = 1
    for d in range(1, units + 1):
        if units % d == 0 and d * LANES <= cap:
            best = d
    return best * LANES


def _row_tile(m, cap):
    assert m % SUBLANES == 0
    best = SUBLANES
    for t in range(SUBLANES, min(m, cap) + 1, SUBLANES):
        if m % t == 0:
            best = t
    return best


def _cparams(sem, vmem_limit=VMEM_LIMIT):
    return pltpu.CompilerParams(dimension_semantics=sem, vmem_limit_bytes=vmem_limit)


def _sigmoid(x):
    return 0.5 + 0.5 * jnp.tanh(0.5 * x)


def _silu(x):
    h = 0.5 * x
    return h + h * jnp.tanh(h)


def _softplus(x):
    return jnp.maximum(x, 0.0) + jnp.log1p(jnp.exp(-jnp.abs(x)))


def _split3(a):
    hi = a.astype(BF16)
    r1 = a - hi.astype(F32)
    mid = r1.astype(BF16)
    lo = (r1 - mid.astype(F32)).astype(BF16)
    return hi, mid, lo


def _expand_heads(a, e3):
    hi, mid, lo = _split3(a)
    return jnp.dot(jnp.concatenate([hi, mid, lo], axis=1), e3, preferred_element_type=F32)


def _head_ms(x, bd):
    sq = x * x
    hi = sq.astype(BF16)
    lo = (sq - hi.astype(F32)).astype(BF16)
    outs = []
    for s in range(x.shape[1] // SLAB):
        sl = slice(s * SLAB, (s + 1) * SLAB)
        outs.append(jnp.dot(hi[:, sl], bd, preferred_element_type=F32)
                    + jnp.dot(lo[:, sl], bd, preferred_element_type=F32))
    return outs[0] if len(outs) == 1 else jnp.concatenate(outs, axis=1)


def _head_norm_rope(x, g, cos, sin, bd):
    xn = x * lax.rsqrt(_head_ms(x, bd) + EPS) * g
    tiles = []
    for t in range(x.shape[1] // LANES):
        xt = xn[:, t * LANES:(t + 1) * LANES]
        tiles.append(xt * cos + pltpu.roll(xt, LANES // 2, 1) * sin)
    return tiles[0] if len(tiles) == 1 else jnp.concatenate(tiles, axis=1)


def _unpair(x):
    q = HEAD_DIM // 2
    lane = lax.broadcasted_iota(jnp.int32, (x.shape[0], LANES), 1)
    tiles = []
    for t in range(x.shape[1] // LANES):
        xt = x[:, t * LANES:(t + 1) * LANES]
        nat = jnp.where((lane >= q) & (lane < 2 * q), pltpu.roll(xt, LANES - q, 1), xt)
        tiles.append(jnp.where((lane >= 2 * q) & (lane < 3 * q), pltpu.roll(xt, q, 1), nat))
    return tiles[0] if len(tiles) == 1 else jnp.concatenate(tiles, axis=1)


def _in_proj_kernel(x_ref, g_ref, w_ref, o_ref, xn_ref, *, tn):
    j = pl.program_id(1)

    @pl.when(j == 0)
    def _():
        x = x_ref[...]
        ms = jnp.mean(x * x, axis=-1, keepdims=True)
        xn_ref[...] = (x * lax.rsqrt(ms + EPS) * g_ref[...]).astype(BF16)

    for c in range(w_ref.shape[1] // tn):
        @pl.when(j == c)
        def _():
            o_ref[...] = jnp.dot(xn_ref[...], w_ref[:, c * tn:(c + 1) * tn], preferred_element_type=F32)


def _in_proj(x, g, w):
    m, k = x.shape
    n = w.shape[1]
    tm = _row_tile(m, 1024)
    tn = _col_tile(n)
    return pl.pallas_call(
        functools.partial(_in_proj_kernel, tn=tn),
        grid=(m // tm, n // tn),
        in_specs=[pl.BlockSpec((tm, k), lambda i, j: (i, 0)),
                  pl.BlockSpec((1, k), lambda i, j: (0, 0)),
                  pl.BlockSpec((k, n), lambda i, j: (0, 0), pipeline_mode=pl.Buffered(1))],
        out_specs=pl.BlockSpec((tm, tn), lambda i, j: (i, j)),
        out_shape=jax.ShapeDtypeStruct((m, n), F32),
        scratch_shapes=[pltpu.VMEM((tm, k), BF16)],
        compiler_params=_cparams(("arbitrary", "arbitrary"), WEIGHT_RESIDENT_VMEM_LIMIT),
        name="in_proj",
    )(x, g, w)


def _attn_prompt_kernel(q_ref, k_ref, v_ref, cos_ref, sin_ref, qg_ref, kg_ref, bd_ref, sink_ref,
                        o_ref, ko_ref, vo_ref, kbuf, vbuf, probs):
    i = pl.program_id(1)
    last = pl.num_programs(1) - 1

    @pl.when(i == 0)
    def _():
        kbuf[...] = jnp.zeros_like(kbuf)
        vbuf[...] = jnp.zeros_like(vbuf)

    cos, sin, bd = cos_ref[...], sin_ref[...], bd_ref[...]
    slot = i % 2
    lane = lax.broadcasted_iota(jnp.int32, (BLK, SLAB), 1)
    grp_k = 2 * (lane // LANES) + (lane % HEAD_DIM) // (HEAD_DIM // 2)
    grp_v = lane // HEAD_DIM
    r = lax.broadcasted_iota(jnp.int32, (BLK, 2 * BLK), 0)
    c = lax.broadcasted_iota(jnp.int32, (BLK, 2 * BLK), 1)
    key = c % BLK
    is_cur = (c // BLK) == slot
    prev_ok = jnp.where(i > 0, key, -1)
    mask = jnp.where(is_cur, (key <= r).astype(jnp.int32), (prev_ok > r + (BLK - WINDOW)).astype(jnp.int32)) > 0

    seqs = range(q_ref.shape[0])
    ks, vs, scores = [], [], []
    for u in seqs:
        q = _head_norm_rope(q_ref[u], qg_ref[...], cos, sin, bd) * (HEAD_DIM ** -0.5 * LOG2E)
        k = _head_norm_rope(k_ref[u], kg_ref[...], cos, sin, bd)
        v = v_ref[u]
        ks.append(k)
        vs.append(v)
        for p in range(KV_HEADS):
            rows = pl.ds(pl.multiple_of(p * 2 * BLK + slot * BLK, BLK), BLK)
            kbuf[u, rows, :] = jnp.where(grp_k == p, k, 0.0).astype(BF16)
            vbuf[u, rows, :] = jnp.where(grp_v == p, v, 0.0).astype(BF16)
        q_stack = jnp.concatenate([q[:, j * SLAB:(j + 1) * SLAB] for j in range(Q_PER_KV)],
                                  axis=0).astype(BF16)
        scores.append(lax.dot_general(q_stack, kbuf[u], (((1,), (1,)), ((), ())),
                                      preferred_element_type=F32))
    for u in seqs:
        for j in range(Q_PER_KV):
            for p in range(KV_HEADS):
                s = jnp.where(mask, scores[u][j * BLK:(j + 1) * BLK, p * 2 * BLK:(p + 1) * 2 * BLK], -jnp.inf)
                sink = sink_ref[j * KV_HEADS + p] * LOG2E
                mx = jnp.maximum(jnp.max(s, axis=-1, keepdims=True), sink)
                e = jnp.exp2(s - mx)
                den = jnp.sum(e, axis=-1, keepdims=True) + jnp.exp2(sink - mx)
                probs[u, j * BLK:(j + 1) * BLK, p * 2 * BLK:(p + 1) * 2 * BLK] = (e / den).astype(BF16)
    for u in seqs:
        pv = jnp.dot(probs[u], vbuf[u], preferred_element_type=F32)
        for j in range(Q_PER_KV):
            o_ref[u, :, j * SLAB:(j + 1) * SLAB] = pv[j * BLK:(j + 1) * BLK, :].astype(o_ref.dtype)

    @pl.when(i == last)
    def _():
        for u in seqs:
            ko_ref[u] = _unpair(ks[u]).T
            vo_ref[u] = vs[u].T


ATTN_SEQS = 2


def _attn_prompt(p_act, lay, batch, seq, cos, sin, qg, kg, bd, sinks):
    nb = seq // BLK
    u = ATTN_SEQS if batch % ATTN_SEQS == 0 else 1
    qc, kc, vc = lay["q"][0] // Q_DIM, lay["k"][0] // KV_DIM, lay["v"][0] // KV_DIM
    p3 = p_act.reshape(batch, seq, p_act.shape[1])
    return pl.pallas_call(
        _attn_prompt_kernel,
        grid=(batch // u, nb),
        in_specs=[pl.BlockSpec((u, BLK, Q_DIM), lambda b, i: (b, i, qc)),
                  pl.BlockSpec((u, BLK, KV_DIM), lambda b, i: (b, i, kc)),
                  pl.BlockSpec((u, BLK, KV_DIM), lambda b, i: (b, i, vc)),
                  pl.BlockSpec((BLK, LANES), lambda b, i: (i, 0)),
                  pl.BlockSpec((BLK, LANES), lambda b, i: (i, 0)),
                  pl.BlockSpec((1, Q_DIM), lambda b, i: (0, 0)),
                  pl.BlockSpec((1, KV_DIM), lambda b, i: (0, 0)),
                  pl.BlockSpec((SLAB, SLAB), lambda b, i: (0, 0)),
                  pl.BlockSpec(memory_space=pltpu.SMEM)],
        out_specs=[pl.BlockSpec((u, BLK, Q_DIM), lambda b, i: (b, i, 0)),
                   pl.BlockSpec((u, KV_DIM, BLK), lambda b, i: (b, 0, 0)),
                   pl.BlockSpec((u, KV_DIM, BLK), lambda b, i: (b, 0, 0))],
        out_shape=[jax.ShapeDtypeStruct((batch, seq, Q_DIM), BF16),
                   jax.ShapeDtypeStruct((batch, KV_DIM, BLK), F32),
                   jax.ShapeDtypeStruct((batch, KV_DIM, BLK), F32)],
        scratch_shapes=[pltpu.VMEM((u, KV_HEADS * 2 * BLK, KV_DIM), BF16),
                        pltpu.VMEM((u, KV_HEADS * 2 * BLK, KV_DIM), BF16),
                        pltpu.VMEM((u, Q_PER_KV * BLK, KV_HEADS * 2 * BLK), BF16)],
        compiler_params=_cparams(("arbitrary", "arbitrary")),
        name="attn_prompt",
    )(p3, p3, p3, cos, sin, qg, kg, bd, sinks)


def _ssd_chunk(slot, z_ref, xs_ref, b_ref, c_ref, dt_ref, cw_ref, cb_ref, dtb_ref, alog_ref, dskip_ref,
               nw_ref, e3_ref, o_ref, tail_ref, xpad, tails, st, ybuf):
    xpad[0:SUBLANES, :] = tails[1 - slot]
    xpad[SUBLANES:SUBLANES + BLK, 0:D_INNER] = xs_ref[...]
    xpad[SUBLANES:SUBLANES + BLK, D_INNER:D_INNER + BC_DIM] = b_ref[...]
    xpad[SUBLANES:SUBLANES + BLK, D_INNER + BC_DIM:CONV_DIM] = c_ref[...]
    cwh = 0.5 * cw_ref[...]
    acc = 0.5 * cb_ref[...] + cwh[CONV_W - 1:CONV_W, :] * xpad[SUBLANES:SUBLANES + BLK, :]
    for t in range(1, CONV_W):
        acc = acc + cwh[CONV_W - 1 - t:CONV_W - t, :] * xpad[SUBLANES - t:SUBLANES - t + BLK, :]
    xc = acc + acc * jnp.tanh(acc)
    new_tail = xpad[BLK:BLK + SUBLANES, :]
    tail_ref[...] = new_tail
    tails[slot] = new_tail

    xs = xc[:, 0:D_INNER]
    bm = xc[:, D_INNER:D_INNER + BC_DIM].astype(BF16)
    cm = xc[:, D_INNER + BC_DIM:CONV_DIM].astype(BF16)

    e3 = e3_ref[...]
    dt = _softplus(dt_ref[...] + dtb_ref[...])
    dta = dt * (-LOG2E * jnp.exp(alog_ref[...]))
    row = lax.broadcasted_iota(jnp.int32, (BLK, BLK), 0)
    col = lax.broadcasted_iota(jnp.int32, (BLK, BLK), 1)
    causal = row >= col
    cum = jnp.dot(causal.astype(F32), dta, preferred_element_type=F32, precision=lax.Precision.HIGHEST)
    cum_t = cum.T
    ecum = jnp.exp2(cum)
    to_end = jnp.exp2(cum[BLK - 1:BLK, :] - cum) * dt
    dt_e = _expand_heads(dt, e3)
    ecum_e = _expand_heads(ecum, e3)
    to_end_e = _expand_heads(to_end, e3)
    xdt = (xs * dt_e).astype(BF16)
    xte = (xs * to_end_e).astype(BF16)
    lane = lax.broadcasted_iota(jnp.int32, (BLK, LANES), 1)
    first_head = lane < SSM_HEAD_DIM

    for g in range(SSM_GROUPS):
        gl = slice(g * GROUP_LANES, (g + 1) * GROUP_LANES)
        bg = bm[:, g * D_STATE:(g + 1) * D_STATE]
        cg = cm[:, g * D_STATE:(g + 1) * D_STATE]
        cbg = lax.dot_general(cg, bg, (((1,), (1,)), ((), ())), preferred_element_type=F32)
        st_g = st[1 - slot, :, gl]
        y_inter = jnp.dot(cg, st_g.astype(BF16), preferred_element_type=F32) * ecum_e[:, gl]
        for pr in range(HEADS_PER_GROUP // 2):
            h0 = g * HEADS_PER_GROUP + 2 * pr
            xd = xdt[:, h0 * SSM_HEAD_DIM:(h0 + 2) * SSM_HEAD_DIM]
            ys = []
            for h in (h0, h0 + 1):
                diff = cum[:, h:h + 1] - cum_t[h:h + 1, :]
                w = (jnp.exp2(jnp.where(causal, diff, -jnp.inf)) * cbg).astype(BF16)
                ys.append(jnp.dot(w, xd, preferred_element_type=F32))
            lo = pr * LANES
            ybuf[:, h0 * SSM_HEAD_DIM:(h0 + 2) * SSM_HEAD_DIM] = (
                jnp.where(first_head, ys[0], ys[1]) + y_inter[:, lo:lo + LANES])
        upd = lax.dot_general(bg, xte[:, gl], (((0,), (0,)), ((), ())), preferred_element_type=F32)
        st[slot, :, gl] = st_g * ecum_e[BLK - 1:BLK, gl] + upd

    y = ybuf[...] + dskip_ref[...] * xs
    yz = y * _silu(z_ref[...])
    ms = jnp.mean(yz * yz, axis=-1, keepdims=True)
    o_ref[...] = (yz * lax.rsqrt(ms + EPS) * nw_ref[...]).astype(o_ref.dtype)


def _ssd_prompt_kernel(z_ref, xs_ref, b_ref, c_ref, dt_ref, cw_ref, cb_ref, dtb_ref, alog_ref, dskip_ref,
                       nw_ref, e3_ref, o_ref, tail_ref, hfin_ref, xpad, tails, st, ybuf):
    i = pl.program_id(1)
    last = pl.num_programs(1) - 1

    @pl.when(i == 0)
    def _():
        tails[...] = jnp.zeros_like(tails)
        st[...] = jnp.zeros_like(st)

    slot = i % 2
    seqs = range(z_ref.shape[0])
    for u in seqs:
        _ssd_chunk(slot, z_ref.at[u], xs_ref.at[u], b_ref.at[u], c_ref.at[u], dt_ref.at[u], cw_ref, cb_ref,
                   dtb_ref, alog_ref, dskip_ref, nw_ref, e3_ref, o_ref.at[u], tail_ref.at[u], xpad.at[u],
                   tails.at[u], st.at[u], ybuf.at[u])

    @pl.when(i == last)
    def _():
        for u in seqs:
            hfin_ref[u] = st[u, slot].T


SSD_SEQS = 2


def _ssd_prompt(p_act, lay, batch, seq, cw, cb, dtb, alog, dskip, nw, e3):
    nb = seq // BLK
    u = SSD_SEQS if batch % SSD_SEQS == 0 else 1
    p3 = p_act.reshape(batch, seq, p_act.shape[1])
    zc, xc = lay["z"][0] // D_INNER, lay["xs"][0] // D_INNER
    bc, cc, dc = lay["B"][0] // BC_DIM, lay["C"][0] // BC_DIM, lay["dt"][0] // DT_PAD
    const = lambda b, i: (0, 0)
    return pl.pallas_call(
        _ssd_prompt_kernel,
        grid=(batch // u, nb),
        in_specs=[pl.BlockSpec((u, BLK, D_INNER), lambda b, i: (b, i, zc)),
                  pl.BlockSpec((u, BLK, D_INNER), lambda b, i: (b, i, xc)),
                  pl.BlockSpec((u, BLK, BC_DIM), lambda b, i: (b, i, bc)),
                  pl.BlockSpec((u, BLK, BC_DIM), lambda b, i: (b, i, cc)),
                  pl.BlockSpec((u, BLK, DT_PAD), lambda b, i: (b, i, dc)),
                  pl.BlockSpec((CONV_W, CONV_DIM), const),
                  pl.BlockSpec((1, CONV_DIM), const),
                  pl.BlockSpec((1, DT_PAD), const),
                  pl.BlockSpec((1, DT_PAD), const),
                  pl.BlockSpec((1, D_INNER), const),
                  pl.BlockSpec((1, D_INNER), const),
                  pl.BlockSpec((3 * DT_PAD, D_INNER), const)],
        out_specs=[pl.BlockSpec((u, BLK, D_INNER), lambda b, i: (b, i, 0)),
                   pl.BlockSpec((u, SUBLANES, CONV_DIM), lambda b, i: (b, 0, 0)),
                   pl.BlockSpec((u, D_INNER, D_STATE), lambda b, i: (b, 0, 0))],
        out_shape=[jax.ShapeDtypeStruct((batch, seq, D_INNER), BF16),
                   jax.ShapeDtypeStruct((batch, SUBLANES, CONV_DIM), F32),
                   jax.ShapeDtypeStruct((batch, D_INNER, D_STATE), F32)],
        scratch_shapes=[pltpu.VMEM((u, BLK + SUBLANES, CONV_DIM), F32),
                        pltpu.VMEM((u, 2, SUBLANES, CONV_DIM), F32),
                        pltpu.VMEM((u, 2, D_STATE, D_INNER), F32),
                        pltpu.VMEM((u, BLK, D_INNER), F32)],
        compiler_params=_cparams(("arbitrary", "arbitrary")),
        name="ssd_prompt",
    )(p3, p3, p3, p3, p3, cw, cb, dtb, alog, dskip, nw, e3)


def _decode_pre_kernel(q_ref, k_ref, xs_ref, b_ref, c_ref, dt_ref, cs_ref, cos_ref, sin_ref, qg_ref, kg_ref,
                       bd_ref, cw_ref, cb_ref, dtb_ref, alog_ref, e3_ref,
                       qo_ref, ko_ref, xc_ref, xdt_ref, dec_ref, cso_ref):
    cos, sin, bd = cos_ref[...], sin_ref[...], bd_ref[...]
    qo_ref[...] = _unpair(_head_norm_rope(q_ref[...], qg_ref[...], cos, sin, bd)) * (HEAD_DIM ** -0.5)
    ko_ref[...] = _unpair(_head_norm_rope(k_ref[...], kg_ref[...], cos, sin, bd))

    segs = ((xs_ref, 0, D_INNER), (b_ref, D_INNER, BC_DIM), (c_ref, D_INNER + BC_DIM, BC_DIM))
    for ref, off, width in segs:
        new = ref[...]
        acc = cb_ref[:, off:off + width] + cw_ref[CONV_W - 1:CONV_W, off:off + width] * new
        for t in range(CONV_W - 1):
            lo = t * CONV_DIM + off
            acc = acc + cw_ref[t:t + 1, off:off + width] * cs_ref[:, lo:lo + width]
        xc_ref[:, off:off + width] = _silu(acc)
        for t in range(CONV_W - 2):
            cso_ref[:, t * CONV_DIM + off:t * CONV_DIM + off + width] = (
                cs_ref[:, (t + 1) * CONV_DIM + off:(t + 1) * CONV_DIM + off + width])
        lo = (CONV_W - 2) * CONV_DIM + off
        cso_ref[:, lo:lo + width] = new

    e3 = e3_ref[...]
    dt = _softplus(dt_ref[...] + dtb_ref[...])
    decay = jnp.exp(dt * (-jnp.exp(alog_ref[...])))
    xdt_ref[...] = xc_ref[:, 0:D_INNER] * _expand_heads(dt, e3)
    dec_ref[...] = _expand_heads(decay, e3)


def _decode_pre(p_act, lay, conv_state, cos, sin, qg, kg, bd, cw, cb, dtb, alog, e3):
    m = p_act.shape[0]
    full = lambda shape: pl.BlockSpec(shape, lambda i: (0, 0))

    def col(name):
        c = lay[name][0] // lay[name][1]
        return pl.BlockSpec((m, lay[name][1]), lambda i: (0, c))

    cs_w = (CONV_W - 1) * CONV_DIM
    return pl.pallas_call(
        _decode_pre_kernel,
        grid=(1,),
        in_specs=[col("q"), col("k"), col("xs"), col("B"), col("C"), col("dt"),
                  full((m, cs_w)), full((1, LANES)), full((1, LANES)), full((1, Q_DIM)), full((1, KV_DIM)),
                  full((SLAB, SLAB)), full((CONV_W, CONV_DIM)), full((1, CONV_DIM)), full((1, DT_PAD)),
                  full((1, DT_PAD)), full((3 * DT_PAD, D_INNER))],
        out_specs=[full((m, Q_DIM)), full((m, KV_DIM)), full((m, CONV_DIM)), full((m, D_INNER)),
                   full((m, D_INNER)), full((m, cs_w))],
        out_shape=[jax.ShapeDtypeStruct((m, Q_DIM), F32), jax.ShapeDtypeStruct((m, KV_DIM), F32),
                   jax.ShapeDtypeStruct((m, CONV_DIM), F32), jax.ShapeDtypeStruct((m, D_INNER), F32),
                   jax.ShapeDtypeStruct((m, D_INNER), F32), jax.ShapeDtypeStruct((m, cs_w), F32)],
        compiler_params=_cparams(("arbitrary",)),
        name="decode_pre",
    )(p_act, p_act, p_act, p_act, p_act, p_act, conv_state, cos, sin, qg, kg, bd, cw, cb, dtb, alog, e3)


DEC_ROWS = 8


def _attn_decode_kernel(q_ref, k_ref, v_ref, ck_ref, cv_ref, sink_ref, o_ref, cko_ref, cvo_ref):
    w = ck_ref.shape[2]
    nq = N_HEADS
    r = lax.broadcasted_iota(jnp.int32, (nq, SLAB), 0)
    grp = lax.broadcasted_iota(jnp.int32, (nq, SLAB), 1) // HEAD_DIM
    own = grp == (r % KV_HEADS)
    in_window = lax.broadcasted_iota(jnp.int32, (nq, w), 1) > (w - WINDOW - 1)
    newest = lax.broadcasted_iota(jnp.int32, (KV_DIM, w), 1) == w - 1
    sink = sink_ref[...][:, 0:1]
    pad = jnp.zeros((DEC_ROWS, KV_DIM), BF16)
    k_parts = jnp.concatenate(list(_split3(k_ref[...])) + [pad], axis=0)
    v_parts = jnp.concatenate(list(_split3(v_ref[...])) + [pad], axis=0)
    part_row = lax.broadcasted_iota(jnp.int32, (4 * DEC_ROWS, w), 0) % DEC_ROWS
    tdims = (((0,), (0,)), ((), ()))
    for bl in range(DEC_ROWS):
        qrow = q_ref[bl:bl + 1, :]
        qm = jnp.zeros((nq, SLAB), F32)
        for j in range(Q_PER_KV):
            slab = jnp.broadcast_to(qrow[:, j * SLAB:(j + 1) * SLAB], (nq, SLAB))
            qm = jnp.where((r // KV_HEADS) == j, slab, qm)
        qm = jnp.where(own, qm, 0.0).astype(BF16)
        pick = jnp.where(part_row == bl, 1.0, 0.0).astype(BF16)
        k_col = lax.dot_general(k_parts, pick, tdims, preferred_element_type=F32)
        v_col = lax.dot_general(v_parts, pick, tdims, preferred_element_type=F32)
        keys = jnp.where(newest, k_col, pltpu.roll(ck_ref[bl], w - 1, 1))
        vals = jnp.where(newest, v_col, pltpu.roll(cv_ref[bl], w - 1, 1))
        cko_ref[bl] = keys
        cvo_ref[bl] = vals
        s = jnp.dot(qm, keys.astype(BF16), preferred_element_type=F32)
        s = jnp.where(in_window, s, -jnp.inf)
        mx = jnp.maximum(jnp.max(s, axis=-1, keepdims=True), sink)
        e = jnp.exp(s - mx)
        den = jnp.sum(e, axis=-1, keepdims=True) + jnp.exp(sink - mx)
        pv = lax.dot_general(e.astype(BF16), vals.astype(BF16), (((1,), (1,)), ((), ())),
                             preferred_element_type=F32)
        pv = jnp.where(own, pv / den, 0.0)
        for j in range(Q_PER_KV):
            o_ref[bl:bl + 1, j * SLAB:(j + 1) * SLAB] = jnp.sum(
                pv[j * KV_HEADS:(j + 1) * KV_HEADS, :], axis=0, keepdims=True)


def _attn_decode(q, k, p_act, lay, cache_k, cache_v, sink_rows):
    m = q.shape[0]
    w = cache_k.shape[2]
    vc = lay["v"][0] // KV_DIM
    cache_spec = pl.BlockSpec((DEC_ROWS, KV_DIM, w), lambda i: (i, 0, 0))
    return pl.pallas_call(
        _attn_decode_kernel,
        grid=(m // DEC_ROWS,),
        in_specs=[pl.BlockSpec((DEC_ROWS, Q_DIM), lambda i: (i, 0)),
                  pl.BlockSpec((DEC_ROWS, KV_DIM), lambda i: (i, 0)),
                  pl.BlockSpec((DEC_ROWS, KV_DIM), lambda i: (i, vc)),
                  cache_spec, cache_spec,
                  pl.BlockSpec((N_HEADS, LANES), lambda i: (0, 0))],
        out_specs=[pl.BlockSpec((DEC_ROWS, Q_DIM), lambda i: (i, 0)), cache_spec, cache_spec],
        out_shape=[jax.ShapeDtypeStruct((m, Q_DIM), F32),
                   jax.ShapeDtypeStruct((m, KV_DIM, w), F32),
                   jax.ShapeDtypeStruct((m, KV_DIM, w), F32)],
        compiler_params=_cparams(("arbitrary",)),
        name="attn_decode",
    )(q, k, p_act, cache_k, cache_v, sink_rows)


MM_ROWS = 16


def _ssm_decode_kernel(st_ref, xdt_ref, dec_ref, xc_ref, z_ref, dskip_ref, nw_ref, sto_ref, o_ref):
    r = lax.broadcasted_iota(jnp.int32, (MM_ROWS, D_INNER), 0)
    grp = lax.broadcasted_iota(jnp.int32, (MM_ROWS, D_INNER), 1) // GROUP_LANES
    rr = lax.broadcasted_iota(jnp.int32, (MM_ROWS, D_STATE), 0)
    ones_rows = jnp.where((rr >= SSM_GROUPS) & (rr < SSM_GROUPS + 3), 1.0, 0.0)
    bc = lambda a: jnp.broadcast_to(a, (MM_ROWS, D_INNER))
    for u in range(st_ref.shape[0]):
        h = st_ref[u]
        xdt = xdt_ref[u:u + 1, :]
        xc = xc_ref[u:u + 1, :]
        hi, mid, lo = _split3(dec_ref[u:u + 1, :])
        lhs_t = jnp.where(r == grp, bc(xdt), 0.0)
        for t, piece in enumerate((hi, mid, lo)):
            lhs_t = jnp.where(r == SSM_GROUPS + t, bc(piece.astype(F32)), lhs_t)
        lhs_t = lhs_t.astype(BF16)
        b_rows = jnp.zeros((MM_ROWS, D_STATE), F32)
        c_rows = jnp.zeros((MM_ROWS, D_STATE), F32)
        for g in range(SSM_GROUPS):
            b_g = xc[:, D_INNER + g * D_STATE:D_INNER + (g + 1) * D_STATE]
            c_g = xc[:, D_INNER + BC_DIM + g * D_STATE:D_INNER + BC_DIM + (g + 1) * D_STATE]
            b_rows = jnp.where(rr == g, jnp.broadcast_to(b_g, (MM_ROWS, D_STATE)), b_rows)
            c_rows = jnp.where(rr == g, jnp.broadcast_to(c_g, (MM_ROWS, D_STATE)), c_rows)
        rhs = jnp.concatenate([b_rows, ones_rows], axis=1).astype(BF16)
        both = lax.dot_general(lhs_t, rhs, (((0,), (0,)), ((), ())), preferred_element_type=F32)
        h_new = both[:, D_STATE:] * h + both[:, :D_STATE]
        sto_ref[u] = h_new
        yg = lax.dot_general(c_rows.astype(BF16), h_new.astype(BF16), (((1,), (1,)), ((), ())),
                             preferred_element_type=F32)
        y = jnp.sum(jnp.where(r == grp, yg, 0.0), axis=0, keepdims=True)
        y = y + dskip_ref[...] * xc[:, 0:D_INNER]
        yz = y * _silu(z_ref[u:u + 1, :])
        ms = jnp.mean(yz * yz, axis=-1, keepdims=True)
        o_ref[u:u + 1, :] = yz * lax.rsqrt(ms + EPS) * nw_ref[...]


SSM_DEC_ROWS = SUBLANES


def _ssm_decode(state, xdt, dec, xc, p_act, lay, dskip, nw):
    m = state.shape[0]
    u = SSM_DEC_ROWS
    assert m % u == 0
    zc = lay["z"][0] // D_INNER
    row = lambda width: pl.BlockSpec((u, width), lambda b: (b, 0))
    return pl.pallas_call(
        _ssm_decode_kernel,
        grid=(m // u,),
        in_specs=[pl.BlockSpec((u, D_INNER, D_STATE), lambda b: (b, 0, 0)),
                  row(D_INNER), row(D_INNER), row(CONV_DIM),
                  pl.BlockSpec((u, D_INNER), lambda b: (b, zc)),
                  pl.BlockSpec((1, D_INNER), lambda b: (0, 0)),
                  pl.BlockSpec((1, D_INNER), lambda b: (0, 0))],
        out_specs=[pl.BlockSpec((u, D_INNER, D_STATE), lambda b: (b, 0, 0)), row(D_INNER)],
        out_shape=[jax.ShapeDtypeStruct((m, D_INNER, D_STATE), F32),
                   jax.ShapeDtypeStruct((m, D_INNER), F32)],
        compiler_params=_cparams(("arbitrary",)),
        name="ssm_decode",
    )(state, xdt, dec, xc, p_act, dskip, nw)


def _merge_mlp_kernel(x_ref, oa_ref, os_ref, ga_ref, gs_ref, wa_ref, ws_ref, wo_ref, g_ref, wu_ref, wd_ref,
                      o_ref, *, ff_tile):
    a = jnp.dot(oa_ref[...].astype(BF16), wa_ref[...], preferred_element_type=F32)
    s = jnp.dot(os_ref[...].astype(BF16), ws_ref[...], preferred_element_type=F32)
    mixed = _sigmoid(ga_ref[...]) * a + _sigmoid(gs_ref[...]) * s
    x = x_ref[...] + jnp.dot(mixed.astype(BF16), wo_ref[...], preferred_element_type=F32)
    ms = jnp.mean(x * x, axis=-1, keepdims=True)
    xn = (x * lax.rsqrt(ms + EPS) * g_ref[...]).astype(BF16)
    acc = x
    for c in range(wu_ref.shape[1] // ff_tile):
        u = jnp.dot(xn, wu_ref[:, c * ff_tile:(c + 1) * ff_tile], preferred_element_type=F32)
        act = jnp.square(jnp.maximum(u, 0.0)).astype(BF16)
        acc = acc + jnp.dot(act, wd_ref[c * ff_tile:(c + 1) * ff_tile, :], preferred_element_type=F32)
    o_ref[...] = acc


def _merge_mlp(x, o_attn, o_ssm, p_act, lay, wa, ws, wo, g, wu, wd):
    m, d = x.shape
    ff = wu.shape[1]
    tm = _row_tile(m, 512)
    gac, gsc = lay["ga"][0] // d, lay["gs"][0] // d
    weight = lambda shape: pl.BlockSpec(shape, lambda i: (0, 0), pipeline_mode=pl.Buffered(1))
    return pl.pallas_call(
        functools.partial(_merge_mlp_kernel, ff_tile=min(1024, ff)),
        grid=(m // tm,),
        in_specs=[pl.BlockSpec((tm, d), lambda i: (i, 0)),
                  pl.BlockSpec((tm, Q_DIM), lambda i: (i, 0)),
                  pl.BlockSpec((tm, D_INNER), lambda i: (i, 0)),
                  pl.BlockSpec((tm, d), lambda i: (i, gac)),
                  pl.BlockSpec((tm, d), lambda i: (i, gsc)),
                  weight((Q_DIM, d)), weight((D_INNER, d)), weight((d, d)),
                  weight((1, d)), weight((d, ff)), weight((ff, d))],
        out_specs=pl.BlockSpec((tm, d), lambda i: (i, 0)),
        out_shape=jax.ShapeDtypeStruct((m, d), F32),
        compiler_params=_cparams(("arbitrary",), WEIGHT_RESIDENT_VMEM_LIMIT),
        name="merge_mlp",
    )(x, o_attn, o_ssm, p_act, p_act, wa, ws, wo, g, wu, wd)


def _rope_tables(pos):
    half = HEAD_DIM // 2
    inv = ROPE_THETA ** (-jnp.arange(half, dtype=F32) / half)
    ang = pos.astype(F32)[:, None] * inv[None, :]
    cos, sin = jnp.cos(ang), jnp.sin(ang)
    return (jnp.concatenate([cos, cos, cos, cos], axis=1),
            jnp.concatenate([-sin, -sin, sin, sin], axis=1))


def _constants():
    heads = np.arange(DT_PAD)[:, None]
    cols = np.arange(D_INNER)[None, :] // SSM_HEAD_DIM
    e = (heads == cols).astype(np.float32)
    e3 = jnp.asarray(np.concatenate([e, e, e], axis=0), BF16)
    a = np.arange(SLAB)
    grp = 2 * (a // LANES) + (a % HEAD_DIM) // (HEAD_DIM // 2)
    bd = jnp.asarray((grp[:, None] == grp[None, :]).astype(np.float32) / HEAD_DIM, BF16)
    return e3, bd


_HALF = HEAD_DIM // 2


def _pair_q(a):
    lead = a.shape[:-1]
    n = len(lead)
    a = a.reshape(lead + (2, 2, Q_PER_KV, 2, _HALF))
    return a.transpose(tuple(range(n)) + (n + 2, n, n + 3, n + 1, n + 4)).reshape(lead + (Q_DIM,))


def _pair_k(a):
    lead = a.shape[:-1]
    n = len(lead)
    a = a.reshape(lead + (2, 2, 2, _HALF))
    return a.transpose(tuple(range(n)) + (n, n + 2, n + 1, n + 3)).reshape(lead + (KV_DIM,))


def _w_in_plan(d_model):
    lay, _ = _layout(d_model)
    o_v = Q_DIM + KV_DIM
    o_z = o_v + KV_DIM
    o_xbc = o_z + D_INNER
    o_dt = o_xbc + CONV_DIM
    o_g = o_dt + SSM_HEADS
    src = {"z": o_z, "xs": o_xbc, "ga": o_g, "gs": o_g + d_model, "B": o_xbc + D_INNER,
           "C": o_xbc + D_INNER + BC_DIM, "v": o_v}
    return [(lay[name][0], off, lay[name][1]) for name, off in src.items()], o_dt


def _w_prep_kernel(w_ref, qk_ref, o_ref, *, plan, lay, o_dt):
    for dst, src, width in plan:
        o_ref[:, dst:dst + width] = w_ref[:, src:src + width].astype(BF16)
    o_ref[:, lay["q"][0]:lay["q"][0] + Q_DIM] = qk_ref[:, 0:Q_DIM]
    o_ref[:, lay["k"][0]:lay["k"][0] + KV_DIM] = qk_ref[:, Q_DIM:Q_DIM + KV_DIM]
    dt = w_ref[:, o_dt:o_dt + SSM_HEADS].astype(BF16)
    pad = jnp.zeros((dt.shape[0], DT_PAD - SSM_HEADS), BF16)
    o_ref[:, lay["dt"][0]:lay["dt"][0] + DT_PAD] = jnp.concatenate([dt, pad], axis=1)


def _w_prep(w_in, qk):
    d_model, n_in = w_in.shape
    lay, n = _layout(d_model)
    plan, o_dt = _w_in_plan(d_model)
    tr = _row_tile(d_model, LANES)
    return pl.pallas_call(
        functools.partial(_w_prep_kernel, plan=plan, lay=lay, o_dt=o_dt),
        grid=(d_model // tr,),
        in_specs=[pl.BlockSpec((tr, n_in), lambda i: (i, 0)),
                  pl.BlockSpec((tr, Q_DIM + KV_DIM), lambda i: (i, 0))],
        out_specs=pl.BlockSpec((tr, n), lambda i: (i, 0)),
        out_shape=jax.ShapeDtypeStruct((d_model, n), BF16),
        compiler_params=_cparams(("arbitrary",)),
        name="w_prep",
    )(w_in, qk)


def _prep_layer(d_model, norm_mix, w_in, q_norm, k_norm, attn_sinks, conv_w, conv_b, dt_bias, a_log, d_skip,
                ssm_norm, w_attn_o, w_ssm_o, w_out, norm_mlp, w_up, w_down):
    assert KV_HEADS == 4 and Q_PER_KV == 4
    e3, bd = _constants()
    lay, _ = _layout(d_model)
    qk = jnp.concatenate([_pair_q(w_in[:, 0:Q_DIM]), _pair_k(w_in[:, Q_DIM:Q_DIM + KV_DIM])], axis=1).astype(BF16)
    w_p = _w_prep(w_in, qk)
    pad_heads = lambda a: jnp.pad(a, (0, DT_PAD - SSM_HEADS))[None, :]
    sink_p = attn_sinks.reshape(KV_HEADS, Q_PER_KV).T.reshape(N_HEADS)
    wa = w_attn_o.reshape(KV_HEADS, Q_PER_KV, HEAD_DIM, d_model).transpose(1, 0, 2, 3).reshape(Q_DIM, d_model)
    return dict(
        lay=lay, e3=e3, bd=bd, w_p=w_p,
        norm_mix=norm_mix[None, :], norm_mlp=norm_mlp[None, :],
        qg=_pair_q(jnp.tile(q_norm, N_HEADS))[None, :], kg=_pair_k(jnp.tile(k_norm, KV_HEADS))[None, :],
        sink_p=sink_p, sink_rows=jnp.broadcast_to(sink_p[:, None], (N_HEADS, LANES)),
        cw=conv_w, cb=conv_b[None, :], dtb=pad_heads(dt_bias), alog=pad_heads(a_log),
        dskip=jnp.repeat(d_skip, SSM_HEAD_DIM)[None, :], nw=ssm_norm[None, :],
        wa=wa.astype(BF16), ws=w_ssm_o.astype(BF16), wo=w_out.astype(BF16),
        wu=w_up.astype(BF16), wd=w_down.astype(BF16))


def _prompt_layer(x, lw):
    batch, seq, d = x.shape
    assert seq % BLK == 0 and seq >= WINDOW
    x2 = x.reshape(batch * seq, d)
    p_act = _in_proj(x2, lw["norm_mix"], lw["w_p"])
    cos, sin = _rope_tables(jnp.arange(seq))
    o_attn, k_last, v_last = _attn_prompt(p_act, lw["lay"], batch, seq, cos, sin, lw["qg"], lw["kg"],
                                          lw["bd"], lw["sink_p"])
    o_ssm, tail, h_fin = _ssd_prompt(p_act, lw["lay"], batch, seq, lw["cw"], lw["cb"], lw["dtb"], lw["alog"],
                                     lw["dskip"], lw["nw"], lw["e3"])
    y = _merge_mlp(x2, o_attn.reshape(batch * seq, Q_DIM), o_ssm.reshape(batch * seq, D_INNER), p_act,
                   lw["lay"], lw["wa"], lw["ws"], lw["wo"],
                   lw["norm_mlp"], lw["wu"], lw["wd"])
    return (y.reshape(batch, seq, d),
            k_last.reshape(batch, KV_HEADS, HEAD_DIM, BLK).transpose(0, 3, 1, 2),
            v_last.reshape(batch, KV_HEADS, HEAD_DIM, BLK).transpose(0, 3, 1, 2),
            tail[:, SUBLANES - (CONV_W - 1):, :],
            h_fin.reshape(batch, SSM_HEADS, SSM_HEAD_DIM, D_STATE))


def _decode_layer(x, cache_k, cache_v, conv_state, ssm_state, lw):
    m, t, d = x.shape
    w = cache_k.shape[1]
    assert t == 1 and w == WINDOW and m % DEC_ROWS == 0
    x2 = x.reshape(m, d)
    p_act = _in_proj(x2, lw["norm_mix"], lw["w_p"])
    cos, sin = _rope_tables(PAST_LEN + jnp.arange(1))
    q, k, xc, xdt, dec, conv_new = _decode_pre(
        p_act, lw["lay"], conv_state.reshape(m, (CONV_W - 1) * CONV_DIM), cos, sin, lw["qg"], lw["kg"],
        lw["bd"], lw["cw"], lw["cb"], lw["dtb"], lw["alog"], lw["e3"])
    ck = jnp.transpose(cache_k, (0, 2, 3, 1)).reshape(m, KV_DIM, w)
    cv = jnp.transpose(cache_v, (0, 2, 3, 1)).reshape(m, KV_DIM, w)
    o_attn, ck_new, cv_new = _attn_decode(q, k, p_act, lw["lay"], ck, cv, lw["sink_rows"])
    h_new, o_ssm = _ssm_decode(ssm_state.reshape(m, D_INNER, D_STATE), xdt, dec, xc, p_act, lw["lay"],
                               lw["dskip"], lw["nw"])
    y = _merge_mlp(x2, o_attn, o_ssm, p_act, lw["lay"], lw["wa"], lw["ws"], lw["wo"],
                   lw["norm_mlp"], lw["wu"], lw["wd"])
    unview = lambda c: jnp.transpose(c.reshape(m, KV_HEADS, HEAD_DIM, w), (0, 3, 1, 2))
    return (y.reshape(m, 1, d), unview(ck_new), unview(cv_new), conv_new.reshape(m, CONV_W - 1, CONV_DIM),
            h_new.reshape(m, SSM_HEADS, SSM_HEAD_DIM, D_STATE))


def kernel(x_prompt, x_sample, cache_k, cache_v, state_conv, state_ssm, norm_mix, w_in, q_norm, k_norm,
           attn_sinks, conv_w, conv_b, dt_bias, a_log, d_skip, ssm_norm, w_attn_o, w_ssm_o, w_out,
           norm_mlp, w_up, w_down):
    depth = w_in.shape[0]
    d_model = x_prompt.shape[-1]
    yp, ys = x_prompt, x_sample
    cols = [[] for _ in range(8)]
    for l in range(depth):
        lw = _prep_layer(d_model, norm_mix[l], w_in[l], q_norm[l], k_norm[l], attn_sinks[l], conv_w[l],
                         conv_b[l], dt_bias[l], a_log[l], d_skip[l], ssm_norm[l], w_attn_o[l], w_ssm_o[l],
                         w_out[l], norm_mlp[l], w_up[l], w_down[l])
        yp, kp, vp, cp, hp = _prompt_layer(yp, lw)
        ys, ks, vs, cs, hs = _decode_layer(ys, cache_k[l], cache_v[l], state_conv[l], state_ssm[l], lw)
        for lst, val in zip(cols, (kp, vp, cp, hp, ks, vs, cs, hs)):
            lst.append(val)
    return (yp, ys) + tuple(jnp.stack(c) for c in cols)
```

```python
import functools

import numpy as np
import jax
import jax.numpy as jnp
from jax import lax
from jax.experimental import pallas as pl
from jax.experimental.pallas import tpu as pltpu

F32 = jnp.float32
BF16 = jnp.bfloat16

N_HEADS = 16
KV_HEADS = 4
HEAD_DIM = 64
Q_PER_KV = N_HEADS // KV_HEADS
WINDOW = 128
ROPE_THETA = 10000.0
Q_DIM = N_HEADS * HEAD_DIM
KV_DIM = KV_HEADS * HEAD_DIM
SSM_HEAD_DIM = 64
SSM_HEADS = 32
D_INNER = SSM_HEADS * SSM_HEAD_DIM
SSM_GROUPS = 4
HEADS_PER_GROUP = SSM_HEADS // SSM_GROUPS
D_STATE = 128
BC_DIM = SSM_GROUPS * D_STATE
CONV_W = 4
CONV_DIM = D_INNER + 2 * BC_DIM
SSD_CHUNK = 128
EPS = 1e-6
PAST_LEN = 8192
LOG2E = 1.4426950408889634

LANES = 128
SUBLANES = 8
VMEM_LIMIT = 48 * 1024 * 1024
WEIGHT_RESIDENT_VMEM_LIMIT = 56 * 1024 * 1024

BLK = 128
assert WINDOW == BLK and SSD_CHUNK == BLK
SLAB = KV_HEADS * HEAD_DIM
GROUP_LANES = HEADS_PER_GROUP * SSM_HEAD_DIM
DT_PAD = LANES


def _layout(d_model):
    segs = [("z", D_INNER), ("xs", D_INNER), ("ga", d_model), ("gs", d_model), ("q", Q_DIM),
            ("B", BC_DIM), ("C", BC_DIM), ("k", KV_DIM), ("v", KV_DIM), ("dt", DT_PAD)]
    lay, off = {}, 0
    for name, width in segs:
        assert off % width == 0, (name, off, width)
        lay[name] = (off, width)
        off += width
    return lay, off


def _col_tile(n, cap=3072):
    units = n // LANES
    best = 1
    for d in range(1, units + 1):
        if units % d == 0 and d * LANES <= cap:
            best = d
    return best * LANES


def _row_tile(m, cap):
    assert m % SUBLANES == 0
    best = SUBLANES
    for t in range(SUBLANES, min(m, cap) + 1, SUBLANES):
        if m % t == 0:
            best = t
    return best


def _cparams(sem, vmem_limit=VMEM_LIMIT):
    return pltpu.CompilerParams(dimension_semantics=sem, vmem_limit_bytes=vmem_limit)


def _sigmoid(x):
    return 0.5 + 0.5 * jnp.tanh(0.5 * x)


def _silu(x):
    h = 0.5 * x
    return h + h * jnp.tanh(h)


def _softplus(x):
    return jnp.maximum(x, 0.0) + jnp.log1p(jnp.exp(-jnp.abs(x)))


def _split3(a):
    hi = a.astype(BF16)
    r1 = a - hi.astype(F32)
    mid = r1.astype(BF16)
    lo = (r1 - mid.astype(F32)).astype(BF16)
    return hi, mid, lo


def _expand_heads(a, e3):
    hi, mid, lo = _split3(a)
    return jnp.dot(jnp.concatenate([hi, mid, lo], axis=1), e3, preferred_element_type=F32)


def _head_ms(x, bd):
    sq = x * x
    hi = sq.astype(BF16)
    lo = (sq - hi.astype(F32)).astype(BF16)
    outs = []
    for s in range(x.shape[1] // SLAB):
        sl = slice(s * SLAB, (s + 1) * SLAB)
        outs.append(jnp.dot(hi[:, sl], bd, preferred_element_type=F32)
                    + jnp.dot(lo[:, sl], bd, preferred_element_type=F32))
    return outs[0] if len(outs) == 1 else jnp.concatenate(outs, axis=1)


def _head_norm_rope(x, g, cos, sin, bd):
    xn = x * lax.rsqrt(_head_ms(x, bd) + EPS) * g
    tiles = []
    for t in range(x.shape[1] // LANES):
        xt = xn[:, t * LANES:(t + 1) * LANES]
        tiles.append(xt * cos + pltpu.roll(xt, LANES // 2, 1) * sin)
    return tiles[0] if len(tiles) == 1 else jnp.concatenate(tiles, axis=1)


def _unpair(x):
    q = HEAD_DIM // 2
    lane = lax.broadcasted_iota(jnp.int32, (x.shape[0], LANES), 1)
    tiles = []
    for t in range(x.shape[1] // LANES):
        xt = x[:, t * LANES:(t + 1) * LANES]
        nat = jnp.where((lane >= q) & (lane < 2 * q), pltpu.roll(xt, LANES - q, 1), xt)
        tiles.append(jnp.where((lane >= 2 * q) & (lane < 3 * q), pltpu.roll(xt, q, 1), nat))
    return tiles[0] if len(tiles) == 1 else jnp.concatenate(tiles, axis=1)


def _in_proj_kernel(x_ref, g_ref, w_ref, o_ref, xn_ref, *, tn):
    j = pl.program_id(1)

    @pl.when(j == 0)
    def _():
        x = x_ref[...]
        ms = jnp.mean(x * x, axis=-1, keepdims=True)
        xn_ref[...] = (x * lax.rsqrt(ms + EPS) * g_ref[...]).astype(BF16)

    for c in range(w_ref.shape[1] // tn):
        @pl.when(j == c)
        def _():
            o_ref[...] = jnp.dot(xn_ref[...], w_ref[:, c * tn:(c + 1) * tn], preferred_element_type=F32)


def _in_proj(x, g, w):
    m, k = x.shape
    n = w.shape[1]
    tm = _row_tile(m, 1024)
    tn = _col_tile(n)
    return pl.pallas_call(
        functools.partial(_in_proj_kernel, tn=tn),
        grid=(m // tm, n // tn),
        in_specs=[pl.BlockSpec((tm, k), lambda i, j: (i, 0)),
                  pl.BlockSpec((1, k), lambda i, j: (0, 0)),
                  pl.BlockSpec((k, n), lambda i, j: (0, 0), pipeline_mode=pl.Buffered(1))],
        out_specs=pl.BlockSpec((tm, tn), lambda i, j: (i, j)),
        out_shape=jax.ShapeDtypeStruct((m, n), F32),
        scratch_shapes=[pltpu.VMEM((tm, k), BF16)],
        compiler_params=_cparams(("arbitrary", "arbitrary"), WEIGHT_RESIDENT_VMEM_LIMIT),
        name="in_proj",
    )(x, g, w)


def _attn_prompt_kernel(q_ref, k_ref, v_ref, cos_ref, sin_ref, qg_ref, kg_ref, bd_ref, sink_ref,
                        o_ref, ko_ref, vo_ref, kbuf, vbuf, probs):
    i = pl.program_id(1)
    last = pl.num_programs(1) - 1

    @pl.when(i == 0)
    def _():
        kbuf[...] = jnp.zeros_like(kbuf)
        vbuf[...] = jnp.zeros_like(vbuf)

    cos, sin, bd = cos_ref[...], sin_ref[...], bd_ref[...]
    slot = i % 2
    lane = lax.broadcasted_iota(jnp.int32, (BLK, SLAB), 1)
    grp_k = 2 * (lane // LANES) + (lane % HEAD_DIM) // (HEAD_DIM // 2)
    grp_v = lane // HEAD_DIM
    r = lax.broadcasted_iota(jnp.int32, (BLK, BLK), 0)
    key = lax.broadcasted_iota(jnp.int32, (BLK, BLK), 1)
    causal = key <= r
    take0 = jnp.where(slot == 0, causal.astype(jnp.int32), 1 - causal.astype(jnp.int32)) > 0
    bias = jnp.where(causal, 0.0, jnp.where(i > 0, 0.0, -jnp.inf))

    seqs = range(q_ref.shape[0])
    ks, vs, scores = [], [], []
    for u in seqs:
        q = _head_norm_rope(q_ref[u], qg_ref[...], cos, sin, bd) * (HEAD_DIM ** -0.5 * LOG2E)
        k = _head_norm_rope(k_ref[u], kg_ref[...], cos, sin, bd)
        v = v_ref[u]
        ks.append(k)
        vs.append(v)
        for p in range(KV_HEADS):
            rows = pl.ds(pl.multiple_of(p * 2 * BLK + slot * BLK, BLK), BLK)
            kbuf[u, rows, :] = jnp.where(grp_k == p, k, 0.0).astype(BF16)
            vbuf[u, rows, :] = jnp.where(grp_v == p, v, 0.0).astype(BF16)
        q_stack = jnp.concatenate([q[:, j * SLAB:(j + 1) * SLAB] for j in range(Q_PER_KV)],
                                  axis=0).astype(BF16)
        scores.append(lax.dot_general(q_stack, kbuf[u], (((1,), (1,)), ((), ())),
                                      preferred_element_type=F32))
    for u in seqs:
        for j in range(Q_PER_KV):
            for p in range(KV_HEADS):
                rows = slice(j * BLK, (j + 1) * BLK)
                c0 = p * 2 * BLK
                s = jnp.where(take0, scores[u][rows, c0:c0 + BLK], scores[u][rows, c0 + BLK:c0 + 2 * BLK]) + bias
                sink = sink_ref[j * KV_HEADS + p] * LOG2E
                mx = jnp.maximum(jnp.max(s, axis=-1, keepdims=True), sink)
                e = jnp.exp2(s - mx)
                den = jnp.sum(e, axis=-1, keepdims=True) + jnp.exp2(sink - mx)
                pr = e / den
                probs[u, rows, c0:c0 + BLK] = jnp.where(take0, pr, 0.0).astype(BF16)
                probs[u, rows, c0 + BLK:c0 + 2 * BLK] = jnp.where(take0, 0.0, pr).astype(BF16)
    for u in seqs:
        pv = jnp.dot(probs[u], vbuf[u], preferred_element_type=F32)
        for j in range(Q_PER_KV):
            o_ref[u, :, j * SLAB:(j + 1) * SLAB] = pv[j * BLK:(j + 1) * BLK, :].astype(o_ref.dtype)

    @pl.when(i == last)
    def _():
        for u in seqs:
            ko_ref[u] = _unpair(ks[u]).T
            vo_ref[u] = vs[u].T


ATTN_SEQS = 2


def _attn_prompt(p_act, lay, batch, seq, cos, sin, qg, kg, bd, sinks):
    nb = seq // BLK
    u = ATTN_SEQS if batch % ATTN_SEQS == 0 else 1
    qc, kc, vc = lay["q"][0] // Q_DIM, lay["k"][0] // KV_DIM, lay["v"][0] // KV_DIM
    p3 = p_act.reshape(batch, seq, p_act.shape[1])
    return pl.pallas_call(
        _attn_prompt_kernel,
        grid=(batch // u, nb),
        in_specs=[pl.BlockSpec((u, BLK, Q_DIM), lambda b, i: (b, i, qc)),
                  pl.BlockSpec((u, BLK, KV_DIM), lambda b, i: (b, i, kc)),
                  pl.BlockSpec((u, BLK, KV_DIM), lambda b, i: (b, i, vc)),
                  pl.BlockSpec((BLK, LANES), lambda b, i: (i, 0)),
                  pl.BlockSpec((BLK, LANES), lambda b, i: (i, 0)),
                  pl.BlockSpec((1, Q_DIM), lambda b, i: (0, 0)),
                  pl.BlockSpec((1, KV_DIM), lambda b, i: (0, 0)),
                  pl.BlockSpec((SLAB, SLAB), lambda b, i: (0, 0)),
                  pl.BlockSpec(memory_space=pltpu.SMEM)],
        out_specs=[pl.BlockSpec((u, BLK, Q_DIM), lambda b, i: (b, i, 0)),
                   pl.BlockSpec((u, KV_DIM, BLK), lambda b, i: (b, 0, 0)),
                   pl.BlockSpec((u, KV_DIM, BLK), lambda b, i: (b, 0, 0))],
        out_shape=[jax.ShapeDtypeStruct((batch, seq, Q_DIM), BF16),
                   jax.ShapeDtypeStruct((batch, KV_DIM, BLK), F32),
                   jax.ShapeDtypeStruct((batch, KV_DIM, BLK), F32)],
        scratch_shapes=[pltpu.VMEM((u, KV_HEADS * 2 * BLK, KV_DIM), BF16),
                        pltpu.VMEM((u, KV_HEADS * 2 * BLK, KV_DIM), BF16),
                        pltpu.VMEM((u, Q_PER_KV * BLK, KV_HEADS * 2 * BLK), BF16)],
        compiler_params=_cparams(("arbitrary", "arbitrary")),
        name="attn_prompt",
    )(p3, p3, p3, cos, sin, qg, kg, bd, sinks)


def _ssd_chunk(slot, z_ref, xs_ref, b_ref, c_ref, dt_ref, cw_ref, cb_ref, dtb_ref, alog_ref, dskip_ref,
               nw_ref, e3_ref, o_ref, tail_ref, xpad, tails, st, ybuf):
    xpad[0:SUBLANES, :] = tails[1 - slot]
    xpad[SUBLANES:SUBLANES + BLK, 0:D_INNER] = xs_ref[...]
    xpad[SUBLANES:SUBLANES + BLK, D_INNER:D_INNER + BC_DIM] = b_ref[...]
    xpad[SUBLANES:SUBLANES + BLK, D_INNER + BC_DIM:CONV_DIM] = c_ref[...]
    cwh = 0.5 * cw_ref[...]
    acc = 0.5 * cb_ref[...] + cwh[CONV_W - 1:CONV_W, :] * xpad[SUBLANES:SUBLANES + BLK, :]
    for t in range(1, CONV_W):
        acc = acc + cwh[CONV_W - 1 - t:CONV_W - t, :] * xpad[SUBLANES - t:SUBLANES - t + BLK, :]
    xc = acc + acc * jnp.tanh(acc)
    new_tail = xpad[BLK:BLK + SUBLANES, :]
    tail_ref[...] = new_tail
    tails[slot] = new_tail

    xs = xc[:, 0:D_INNER]
    bm = xc[:, D_INNER:D_INNER + BC_DIM].astype(BF16)
    cm = xc[:, D_INNER + BC_DIM:CONV_DIM].astype(BF16)

    e3 = e3_ref[...]
    dt = _softplus(dt_ref[...] + dtb_ref[...])
    dta = dt * (-LOG2E * jnp.exp(alog_ref[...]))
    row = lax.broadcasted_iota(jnp.int32, (BLK, BLK), 0)
    col = lax.broadcasted_iota(jnp.int32, (BLK, BLK), 1)
    causal = row >= col
    cum = jnp.dot(causal.astype(F32), dta, preferred_element_type=F32, precision=lax.Precision.HIGHEST)
    cum_t = cum.T
    ecum = jnp.exp2(cum)
    to_end = jnp.exp2(cum[BLK - 1:BLK, :] - cum) * dt
    dt_e = _expand_heads(dt, e3)
    ecum_e = _expand_heads(ecum, e3)
    to_end_e = _expand_heads(to_end, e3)
    xdt = (xs * dt_e).astype(BF16)
    xte = (xs * to_end_e).astype(BF16)
    lane = lax.broadcasted_iota(jnp.int32, (BLK, LANES), 1)
    first_head = lane < SSM_HEAD_DIM

    for g in range(SSM_GROUPS):
        gl = slice(g * GROUP_LANES, (g + 1) * GROUP_LANES)
        bg = bm[:, g * D_STATE:(g + 1) * D_STATE]
        cg = cm[:, g * D_STATE:(g + 1) * D_STATE]
        cbg = lax.dot_general(cg, bg, (((1,), (1,)), ((), ())), preferred_element_type=F32)
        st_g = st[1 - slot, :, gl]
        y_inter = jnp.dot(cg, st_g.astype(BF16), preferred_element_type=F32) * ecum_e[:, gl]
        for pr in range(HEADS_PER_GROUP // 2):
            h0 = g * HEADS_PER_GROUP + 2 * pr
            xd = xdt[:, h0 * SSM_HEAD_DIM:(h0 + 2) * SSM_HEAD_DIM]
            ys = []
            for h in (h0, h0 + 1):
                diff = cum[:, h:h + 1] - cum_t[h:h + 1, :]
                w = (jnp.exp2(jnp.where(causal, diff, -jnp.inf)) * cbg).astype(BF16)
                ys.append(jnp.dot(w, xd, preferred_element_type=F32))
            lo = pr * LANES
            ybuf[:, h0 * SSM_HEAD_DIM:(h0 + 2) * SSM_HEAD_DIM] = (
                jnp.where(first_head, ys[0], ys[1]) + y_inter[:, lo:lo + LANES])
        upd = lax.dot_general(bg, xte[:, gl], (((0,), (0,)), ((), ())), preferred_element_type=F32)
        st[slot, :, gl] = st_g * ecum_e[BLK - 1:BLK, gl] + upd

    y = ybuf[...] + dskip_ref[...] * xs
    yz = y * _silu(z_ref[...])
    ms = jnp.mean(yz * yz, axis=-1, keepdims=True)
    o_ref[...] = (yz * lax.rsqrt(ms + EPS) * nw_ref[...]).astype(o_ref.dtype)


def _ssd_prompt_kernel(z_ref, xs_ref, b_ref, c_ref, dt_ref, cw_ref, cb_ref, dtb_ref, alog_ref, dskip_ref,
                       nw_ref, e3_ref, o_ref, tail_ref, hfin_ref, xpad, tails, st, ybuf):
    i = pl.program_id(1)
    last = pl.num_programs(1) - 1

    @pl.when(i == 0)
    def _():
        tails[...] = jnp.zeros_like(tails)
        st[...] = jnp.zeros_like(st)

    slot = i % 2
    seqs = range(z_ref.shape[0])
    for u in seqs:
        _ssd_chunk(slot, z_ref.at[u], xs_ref.at[u], b_ref.at[u], c_ref.at[u], dt_ref.at[u], cw_ref, cb_ref,
                   dtb_ref, alog_ref, dskip_ref, nw_ref, e3_ref, o_ref.at[u], tail_ref.at[u], xpad.at[u],
                   tails.at[u], st.at[u], ybuf.at[u])

    @pl.when(i == last)
    def _():
        for u in seqs:
            hfin_ref[u] = st[u, slot].T


SSD_SEQS = 2


def _ssd_prompt(p_act, lay, batch, seq, cw, cb, dtb, alog, dskip, nw, e3):
    nb = seq // BLK
    u = SSD_SEQS if batch % SSD_SEQS == 0 else 1
    p3 = p_act.reshape(batch, seq, p_act.shape[1])
    zc, xc = lay["z"][0] // D_INNER, lay["xs"][0] // D_INNER
    bc, cc, dc = lay["B"][0] // BC_DIM, lay["C"][0] // BC_DIM, lay["dt"][0] // DT_PAD
    const = lambda b, i: (0, 0)
    return pl.pallas_call(
        _ssd_prompt_kernel,
        grid=(batch // u, nb),
        in_specs=[pl.BlockSpec((u, BLK, D_INNER), lambda b, i: (b, i, zc)),
                  pl.BlockSpec((u, BLK, D_INNER), lambda b, i: (b, i, xc)),
                  pl.BlockSpec((u, BLK, BC_DIM), lambda b, i: (b, i, bc)),
                  pl.BlockSpec((u, BLK, BC_DIM), lambda b, i: (b, i, cc)),
                  pl.BlockSpec((u, BLK, DT_PAD), lambda b, i: (b, i, dc)),
                  pl.BlockSpec((CONV_W, CONV_DIM), const),
                  pl.BlockSpec((1, CONV_DIM), const),
                  pl.BlockSpec((1, DT_PAD), const),
                  pl.BlockSpec((1, DT_PAD), const),
                  pl.BlockSpec((1, D_INNER), const),
                  pl.BlockSpec((1, D_INNER), const),
                  pl.BlockSpec((3 * DT_PAD, D_INNER), const)],
        out_specs=[pl.BlockSpec((u, BLK, D_INNER), lambda b, i: (b, i, 0)),
                   pl.BlockSpec((u, SUBLANES, CONV_DIM), lambda b, i: (b, 0, 0)),
                   pl.BlockSpec((u, D_INNER, D_STATE), lambda b, i: (b, 0, 0))],
        out_shape=[jax.ShapeDtypeStruct((batch, seq, D_INNER), BF16),
                   jax.ShapeDtypeStruct((batch, SUBLANES, CONV_DIM), F32),
                   jax.ShapeDtypeStruct((batch, D_INNER, D_STATE), F32)],
        scratch_shapes=[pltpu.VMEM((u, BLK + SUBLANES, CONV_DIM), F32),
                        pltpu.VMEM((u, 2, SUBLANES, CONV_DIM), F32),
                        pltpu.VMEM((u, 2, D_STATE, D_INNER), F32),
                        pltpu.VMEM((u, BLK, D_INNER), F32)],
        compiler_params=_cparams(("arbitrary", "arbitrary")),
        name="ssd_prompt",
    )(p3, p3, p3, p3, p3, cw, cb, dtb, alog, dskip, nw, e3)


def _decode_pre_kernel(q_ref, k_ref, xs_ref, b_ref, c_ref, dt_ref, cs_ref, cos_ref, sin_ref, qg_ref, kg_ref,
                       bd_ref, cw_ref, cb_ref, dtb_ref, alog_ref, e3_ref,
                       qo_ref, ko_ref, xc_ref, xdt_ref, dec_ref, cso_ref):
    cos, sin, bd = cos_ref[...], sin_ref[...], bd_ref[...]
    qo_ref[...] = _unpair(_head_norm_rope(q_ref[...], qg_ref[...], cos, sin, bd)) * (HEAD_DIM ** -0.5)
    ko_ref[...] = _unpair(_head_norm_rope(k_ref[...], kg_ref[...], cos, sin, bd))

    segs = ((xs_ref, 0, D_INNER), (b_ref, D_INNER, BC_DIM), (c_ref, D_INNER + BC_DIM, BC_DIM))
    for ref, off, width in segs:
        new = ref[...]
        acc = cb_ref[:, off:off + width] + cw_ref[CONV_W - 1:CONV_W, off:off + width] * new
        for t in range(CONV_W - 1):
            lo = t * CONV_DIM + off
            acc = acc + cw_ref[t:t + 1, off:off + width] * cs_ref[:, lo:lo + width]
        xc_ref[:, off:off + width] = _silu(acc)
        for t in range(CONV_W - 2):
            cso_ref[:, t * CONV_DIM + off:t * CONV_DIM + off + width] = (
                cs_ref[:, (t + 1) * CONV_DIM + off:(t + 1) * CONV_DIM + off + width])
        lo = (CONV_W - 2) * CONV_DIM + off
        cso_ref[:, lo:lo + width] = new

    e3 = e3_ref[...]
    dt = _softplus(dt_ref[...] + dtb_ref[...])
    decay = jnp.exp(dt * (-jnp.exp(alog_ref[...])))
    xdt_ref[...] = xc_ref[:, 0:D_INNER] * _expand_heads(dt, e3)
    dec_ref[...] = _expand_heads(decay, e3)


def _decode_pre(p_act, lay, conv_state, cos, sin, qg, kg, bd, cw, cb, dtb, alog, e3):
    m = p_act.shape[0]
    full = lambda shape: pl.BlockSpec(shape, lambda i: (0, 0))

    def col(name):
        c = lay[name][0] // lay[name][1]
        return pl.BlockSpec((m, lay[name][1]), lambda i: (0, c))

    cs_w = (CONV_W - 1) * CONV_DIM
    return pl.pallas_call(
        _decode_pre_kernel,
        grid=(1,),
        in_specs=[col("q"), col("k"), col("xs"), col("B"), col("C"), col("dt"),
                  full((m, cs_w)), full((1, LANES)), full((1, LANES)), full((1, Q_DIM)), full((1, KV_DIM)),
                  full((SLAB, SLAB)), full((CONV_W, CONV_DIM)), full((1, CONV_DIM)), full((1, DT_PAD)),
                  full((1, DT_PAD)), full((3 * DT_PAD, D_INNER))],
        out_specs=[full((m, Q_DIM)), full((m, KV_DIM)), full((m, CONV_DIM)), full((m, D_INNER)),
                   full((m, D_INNER)), full((m, cs_w))],
        out_shape=[jax.ShapeDtypeStruct((m, Q_DIM), F32), jax.ShapeDtypeStruct((m, KV_DIM), F32),
                   jax.ShapeDtypeStruct((m, CONV_DIM), F32), jax.ShapeDtypeStruct((m, D_INNER), F32),
                   jax.ShapeDtypeStruct((m, D_INNER), F32), jax.ShapeDtypeStruct((m, cs_w), F32)],
        compiler_params=_cparams(("arbitrary",)),
        name="decode_pre",
    )(p_act, p_act, p_act, p_act, p_act, p_act, conv_state, cos, sin, qg, kg, bd, cw, cb, dtb, alog, e3)


DEC_ROWS = 8


def _attn_decode_kernel(q_ref, k_ref, v_ref, ck_ref, cv_ref, sink_ref, o_ref, cko_ref, cvo_ref):
    w = ck_ref.shape[2]
    nq = N_HEADS
    r = lax.broadcasted_iota(jnp.int32, (nq, SLAB), 0)
    grp = lax.broadcasted_iota(jnp.int32, (nq, SLAB), 1) // HEAD_DIM
    own = grp == (r % KV_HEADS)
    in_window = lax.broadcasted_iota(jnp.int32, (nq, w), 1) > (w - WINDOW - 1)
    newest = lax.broadcasted_iota(jnp.int32, (KV_DIM, w), 1) == w - 1
    sink = sink_ref[...][:, 0:1]
    pad = jnp.zeros((DEC_ROWS, KV_DIM), BF16)
    k_parts = jnp.concatenate(list(_split3(k_ref[...])) + [pad], axis=0)
    v_parts = jnp.concatenate(list(_split3(v_ref[...])) + [pad], axis=0)
    part_row = lax.broadcasted_iota(jnp.int32, (4 * DEC_ROWS, w), 0) % DEC_ROWS
    tdims = (((0,), (0,)), ((), ()))
    for bl in range(DEC_ROWS):
        qrow = q_ref[bl:bl + 1, :]
        qm = jnp.zeros((nq, SLAB), F32)
        for j in range(Q_PER_KV):
            slab = jnp.broadcast_to(qrow[:, j * SLAB:(j + 1) * SLAB], (nq, SLAB))
            qm = jnp.where((r // KV_HEADS) == j, slab, qm)
        qm = jnp.where(own, qm, 0.0).astype(BF16)
        pick = jnp.where(part_row == bl, 1.0, 0.0).astype(BF16)
        k_col = lax.dot_general(k_parts, pick, tdims, preferred_element_type=F32)
        v_col = lax.dot_general(v_parts, pick, tdims, preferred_element_type=F32)
        keys = jnp.where(newest, k_col, pltpu.roll(ck_ref[bl], w - 1, 1))
        vals = jnp.where(newest, v_col, pltpu.roll(cv_ref[bl], w - 1, 1))
        cko_ref[bl] = keys
        cvo_ref[bl] = vals
        s = jnp.dot(qm, keys.astype(BF16), preferred_element_type=F32)
        s = jnp.where(in_window, s, -jnp.inf)
        mx = jnp.maximum(jnp.max(s, axis=-1, keepdims=True), sink)
        e = jnp.exp(s - mx)
        den = jnp.sum(e, axis=-1, keepdims=True) + jnp.exp(sink - mx)
        pv = lax.dot_general(e.astype(BF16), vals.astype(BF16), (((1,), (1,)), ((), ())),
                             preferred_element_type=F32)
        pv = jnp.where(own, pv / den, 0.0)
        for j in range(Q_PER_KV):
            o_ref[bl:bl + 1, j * SLAB:(j + 1) * SLAB] = jnp.sum(
                pv[j * KV_HEADS:(j + 1) * KV_HEADS, :], axis=0, keepdims=True)


def _attn_decode(q, k, p_act, lay, cache_k, cache_v, sink_rows):
    m = q.shape[0]
    w = cache_k.shape[2]
    vc = lay["v"][0] // KV_DIM
    cache_spec = pl.BlockSpec((DEC_ROWS, KV_DIM, w), lambda i: (i, 0, 0))
    return pl.pallas_call(
        _attn_decode_kernel,
        grid=(m // DEC_ROWS,),
        in_specs=[pl.BlockSpec((DEC_ROWS, Q_DIM), lambda i: (i, 0)),
                  pl.BlockSpec((DEC_ROWS, KV_DIM), lambda i: (i, 0)),
                  pl.BlockSpec((DEC_ROWS, KV_DIM), lambda i: (i, vc)),
                  cache_spec, cache_spec,
                  pl.BlockSpec((N_HEADS, LANES), lambda i: (0, 0))],
        out_specs=[pl.BlockSpec((DEC_ROWS, Q_DIM), lambda i: (i, 0)), cache_spec, cache_spec],
        out_shape=[jax.ShapeDtypeStruct((m, Q_DIM), F32),
                   jax.ShapeDtypeStruct((m, KV_DIM, w), F32),
                   jax.ShapeDtypeStruct((m, KV_DIM, w), F32)],
        compiler_params=_cparams(("arbitrary",)),
        name="attn_decode",
    )(q, k, p_act, cache_k, cache_v, sink_rows)


MM_ROWS = 16


def _ssm_decode_kernel(st_ref, xdt_ref, dec_ref, xc_ref, z_ref, dskip_ref, nw_ref, sto_ref, o_ref):
    r = lax.broadcasted_iota(jnp.int32, (MM_ROWS, D_INNER), 0)
    grp = lax.broadcasted_iota(jnp.int32, (MM_ROWS, D_INNER), 1) // GROUP_LANES
    rr = lax.broadcasted_iota(jnp.int32, (MM_ROWS, D_STATE), 0)
    ones_rows = jnp.where((rr >= SSM_GROUPS) & (rr < SSM_GROUPS + 3), 1.0, 0.0)
    bc = lambda a: jnp.broadcast_to(a, (MM_ROWS, D_INNER))
    for u in range(st_ref.shape[0]):
        h = st_ref[u]
        xdt = xdt_ref[u:u + 1, :]
        xc = xc_ref[u:u + 1, :]
        hi, mid, lo = _split3(dec_ref[u:u + 1, :])
        lhs_t = jnp.where(r == grp, bc(xdt), 0.0)
        for t, piece in enumerate((hi, mid, lo)):
            lhs_t = jnp.where(r == SSM_GROUPS + t, bc(piece.astype(F32)), lhs_t)
        lhs_t = lhs_t.astype(BF16)
        b_rows = jnp.zeros((MM_ROWS, D_STATE), F32)
        c_rows = jnp.zeros((MM_ROWS, D_STATE), F32)
        for g in range(SSM_GROUPS):
            b_g = xc[:, D_INNER + g * D_STATE:D_INNER + (g + 1) * D_STATE]
            c_g = xc[:, D_INNER + BC_DIM + g * D_STATE:D_INNER + BC_DIM + (g + 1) * D_STATE]
            b_rows = jnp.where(rr == g, jnp.broadcast_to(b_g, (MM_ROWS, D_STATE)), b_rows)
            c_rows = jnp.where(rr == g, jnp.broadcast_to(c_g, (MM_ROWS, D_STATE)), c_rows)
        rhs = jnp.concatenate([b_rows, ones_rows], axis=1).astype(BF16)
        both = lax.dot_general(lhs_t, rhs, (((0,), (0,)), ((), ())), preferred_element_type=F32)
        h_new = both[:, D_STATE:] * h + both[:, :D_STATE]
        sto_ref[u] = h_new
        yg = lax.dot_general(c_rows.astype(BF16), h_new.astype(BF16), (((1,), (1,)), ((), ())),
                             preferred_element_type=F32)
        y = jnp.sum(jnp.where(r == grp, yg, 0.0), axis=0, keepdims=True)
        y = y + dskip_ref[...] * xc[:, 0:D_INNER]
        yz = y * _silu(z_ref[u:u + 1, :])
        ms = jnp.mean(yz * yz, axis=-1, keepdims=True)
        o_ref[u:u + 1, :] = yz * lax.rsqrt(ms + EPS) * nw_ref[...]


SSM_DEC_ROWS = SUBLANES


def _ssm_decode(state, xdt, dec, xc, p_act, lay, dskip, nw):
    m = state.shape[0]
    u = SSM_DEC_ROWS
    assert m % u == 0
    zc = lay["z"][0] // D_INNER
    row = lambda width: pl.BlockSpec((u, width), lambda b: (b, 0))
    return pl.pallas_call(
        _ssm_decode_kernel,
        grid=(m // u,),
        in_specs=[pl.BlockSpec((u, D_INNER, D_STATE), lambda b: (b, 0, 0)),
                  row(D_INNER), row(D_INNER), row(CONV_DIM),
                  pl.BlockSpec((u, D_INNER), lambda b: (b, zc)),
                  pl.BlockSpec((1, D_INNER), lambda b: (0, 0)),
                  pl.BlockSpec((1, D_INNER), lambda b: (0, 0))],
        out_specs=[pl.BlockSpec((u, D_INNER, D_STATE), lambda b: (b, 0, 0)), row(D_INNER)],
        out_shape=[jax.ShapeDtypeStruct((m, D_INNER, D_STATE), F32),
                   jax.ShapeDtypeStruct((m, D_INNER), F32)],
        compiler_params=_cparams(("arbitrary",)),
        name="ssm_decode",
    )(state, xdt, dec, xc, p_act, dskip, nw)


def _merge_mlp_kernel(x_ref, oa_ref, os_ref, ga_ref, gs_ref, wa_ref, ws_ref, wo_ref, g_ref, wu_ref, wd_ref,
                      o_ref, *, ff_tile):
    a = jnp.dot(oa_ref[...].astype(BF16), wa_ref[...], preferred_element_type=F32)
    s = jnp.dot(os_ref[...].astype(BF16), ws_ref[...], preferred_element_type=F32)
    mixed = _sigmoid(ga_ref[...]) * a + _sigmoid(gs_ref[...]) * s
    x = x_ref[...] + jnp.dot(mixed.astype(BF16), wo_ref[...], preferred_element_type=F32)
    ms = jnp.mean(x * x, axis=-1, keepdims=True)
    xn = (x * lax.rsqrt(ms + EPS) * g_ref[...]).astype(BF16)
    acc = x
    for c in range(wu_ref.shape[1] // ff_tile):
        u = jnp.dot(xn, wu_ref[:, c * ff_tile:(c + 1) * ff_tile], preferred_element_type=F32)
        act = jnp.square(jnp.maximum(u, 0.0)).astype(BF16)
        acc = acc + jnp.dot(act, wd_ref[c * ff_tile:(c + 1) * ff_tile, :], preferred_element_type=F32)
    o_ref[...] = acc


def _merge_mlp(x, o_attn, o_ssm, p_act, lay, wa, ws, wo, g, wu, wd):
    m, d = x.shape
    ff = wu.shape[1]
    tm = _row_tile(m, 512)
    gac, gsc = lay["ga"][0] // d, lay["gs"][0] // d
    weight = lambda shape: pl.BlockSpec(shape, lambda i: (0, 0), pipeline_mode=pl.Buffered(1))
    return pl.pallas_call(
        functools.partial(_merge_mlp_kernel, ff_tile=min(1024, ff)),
        grid=(m // tm,),
        in_specs=[pl.BlockSpec((tm, d), lambda i: (i, 0)),
                  pl.BlockSpec((tm, Q_DIM), lambda i: (i, 0)),
                  pl.BlockSpec((tm, D_INNER), lambda i: (i, 0)),
                  pl.BlockSpec((tm, d), lambda i: (i, gac)),
                  pl.BlockSpec((tm, d), lambda i: (i, gsc)),
                  weight((Q_DIM, d)), weight((D_INNER, d)), weight((d, d)),
                  weight((1, d)), weight((d, ff)), weight((ff, d))],
        out_specs=pl.BlockSpec((tm, d), lambda i: (i, 0)),
        out_shape=jax.ShapeDtypeStruct((m, d), F32),
        compiler_params=_cparams(("arbitrary",), WEIGHT_RESIDENT_VMEM_LIMIT),
        name="merge_mlp",
    )(x, o_attn, o_ssm, p_act, p_act, wa, ws, wo, g, wu, wd)


def _rope_tables(pos):
    half = HEAD_DIM // 2
    inv = ROPE_THETA ** (-jnp.arange(half, dtype=F32) / half)
    ang = pos.astype(F32)[:, None] * inv[None, :]
    cos, sin = jnp.cos(ang), jnp.sin(ang)
    return (jnp.concatenate([cos, cos, cos, cos], axis=1),
            jnp.concatenate([-sin, -sin, sin, sin], axis=1))


def _constants():
    heads = np.arange(DT_PAD)[:, None]
    cols = np.arange(D_INNER)[None, :] // SSM_HEAD_DIM
    e = (heads == cols).astype(np.float32)
    e3 = jnp.asarray(np.concatenate([e, e, e], axis=0), BF16)
    a = np.arange(SLAB)
    grp = 2 * (a // LANES) + (a % HEAD_DIM) // (HEAD_DIM // 2)
    bd = jnp.asarray((grp[:, None] == grp[None, :]).astype(np.float32) / HEAD_DIM, BF16)
    return e3, bd


_HALF = HEAD_DIM // 2


def _pair_q(a):
    lead = a.shape[:-1]
    n = len(lead)
    a = a.reshape(lead + (2, 2, Q_PER_KV, 2, _HALF))
    return a.transpose(tuple(range(n)) + (n + 2, n, n + 3, n + 1, n + 4)).reshape(lead + (Q_DIM,))


def _pair_k(a):
    lead = a.shape[:-1]
    n = len(lead)
    a = a.reshape(lead + (2, 2, 2, _HALF))
    return a.transpose(tuple(range(n)) + (n, n + 2, n + 1, n + 3)).reshape(lead + (KV_DIM,))


def _prep_layer(d_model, norm_mix, w_in, q_norm, k_norm, attn_sinks, conv_w, conv_b, dt_bias, a_log, d_skip,
                ssm_norm, w_attn_o, w_ssm_o, w_out, norm_mlp, w_up, w_down):
    assert KV_HEADS == 4 and Q_PER_KV == 4
    e3, bd = _constants()
    o_q, o_k, o_v, o_z, o_xbc, o_dt = (0, Q_DIM, Q_DIM + KV_DIM, Q_DIM + 2 * KV_DIM,
                                       Q_DIM + 2 * KV_DIM + D_INNER, Q_DIM + 2 * KV_DIM + D_INNER + CONV_DIM)
    o_g = o_dt + SSM_HEADS
    cut = lambda lo, width: w_in[:, lo:lo + width]
    pieces = {"z": cut(o_z, D_INNER), "xs": cut(o_xbc, D_INNER), "ga": cut(o_g, d_model),
              "gs": cut(o_g + d_model, d_model), "q": _pair_q(cut(o_q, Q_DIM)),
              "B": cut(o_xbc + D_INNER, BC_DIM), "C": cut(o_xbc + D_INNER + BC_DIM, BC_DIM),
              "k": _pair_k(cut(o_k, KV_DIM)), "v": cut(o_v, KV_DIM),
              "dt": jnp.pad(cut(o_dt, SSM_HEADS), ((0, 0), (0, DT_PAD - SSM_HEADS)))}
    lay, _ = _layout(d_model)
    w_p = jnp.concatenate([pieces[name] for name in lay], axis=1).astype(BF16)
    pad_heads = lambda a: jnp.pad(a, (0, DT_PAD - SSM_HEADS))[None, :]
    sink_p = attn_sinks.reshape(KV_HEADS, Q_PER_KV).T.reshape(N_HEADS)
    wa = w_attn_o.reshape(KV_HEADS, Q_PER_KV, HEAD_DIM, d_model).transpose(1, 0, 2, 3).reshape(Q_DIM, d_model)
    return dict(
        lay=lay, e3=e3, bd=bd, w_p=w_p,
        norm_mix=norm_mix[None, :], norm_mlp=norm_mlp[None, :],
        qg=_pair_q(jnp.tile(q_norm, N_HEADS))[None, :], kg=_pair_k(jnp.tile(k_norm, KV_HEADS))[None, :],
        sink_p=sink_p, sink_rows=jnp.broadcast_to(sink_p[:, None], (N_HEADS, LANES)),
        cw=conv_w, cb=conv_b[None, :], dtb=pad_heads(dt_bias), alog=pad_heads(a_log),
        dskip=jnp.repeat(d_skip, SSM_HEAD_DIM)[None, :], nw=ssm_norm[None, :],
        wa=wa.astype(BF16), ws=w_ssm_o.astype(BF16), wo=w_out.astype(BF16),
        wu=w_up.astype(BF16), wd=w_down.astype(BF16))


def _prompt_layer(x, lw):
    batch, seq, d = x.shape
    assert seq % BLK == 0 and seq >= WINDOW
    x2 = x.reshape(batch * seq, d)
    p_act = _in_proj(x2, lw["norm_mix"], lw["w_p"])
    cos, sin = _rope_tables(jnp.arange(seq))
    o_attn, k_last, v_last = _attn_prompt(p_act, lw["lay"], batch, seq, cos, sin, lw["qg"], lw["kg"],
                                          lw["bd"], lw["sink_p"])
    o_ssm, tail, h_fin = _ssd_prompt(p_act, lw["lay"], batch, seq, lw["cw"], lw["cb"], lw["dtb"], lw["alog"],
                                     lw["dskip"], lw["nw"], lw["e3"])
    y = _merge_mlp(x2, o_attn.reshape(batch * seq, Q_DIM), o_ssm.reshape(batch * seq, D_INNER), p_act,
                   lw["lay"], lw["wa"], lw["ws"], lw["wo"],
                   lw["norm_mlp"], lw["wu"], lw["wd"])
    return (y.reshape(batch, seq, d),
            k_last.reshape(batch, KV_HEADS, HEAD_DIM, BLK).transpose(0, 3, 1, 2),
            v_last.reshape(batch, KV_HEADS, HEAD_DIM, BLK).transpose(0, 3, 1, 2),
            tail[:, SUBLANES - (CONV_W - 1):, :],
            h_fin.reshape(batch, SSM_HEADS, SSM_HEAD_DIM, D_STATE))


def _decode_layer(x, cache_k, cache_v, conv_state, ssm_state, lw):
    m, t, d = x.shape
    w = cache_k.shape[1]
    assert t == 1 and w == WINDOW and m % DEC_ROWS == 0
    x2 = x.reshape(m, d)
    p_act = _in_proj(x2, lw["norm_mix"], lw["w_p"])
    cos, sin = _rope_tables(PAST_LEN + jnp.arange(1))
    q, k, xc, xdt, dec, conv_new = _decode_pre(
        p_act, lw["lay"], conv_state.reshape(m, (CONV_W - 1) * CONV_DIM), cos, sin, lw["qg"], lw["kg"],
        lw["bd"], lw["cw"], lw["cb"], lw["dtb"], lw["alog"], lw["e3"])
    ck = jnp.transpose(cache_k, (0, 2, 3, 1)).reshape(m, KV_DIM, w)
    cv = jnp.transpose(cache_v, (0, 2, 3, 1)).reshape(m, KV_DIM, w)
    o_attn, ck_new, cv_new = _attn_decode(q, k, p_act, lw["lay"], ck, cv, lw["sink_rows"])
    h_new, o_ssm = _ssm_decode(ssm_state.reshape(m, D_INNER, D_STATE), xdt, dec, xc, p_act, lw["lay"],
                               lw["dskip"], lw["nw"])
    y = _merge_mlp(x2, o_attn, o_ssm, p_act, lw["lay"], lw["wa"], lw["ws"], lw["wo"],
                   lw["norm_mlp"], lw["wu"], lw["wd"])
    unview = lambda c: jnp.transpose(c.reshape(m, KV_HEADS, HEAD_DIM, w), (0, 3, 1, 2))
    return (y.reshape(m, 1, d), unview(ck_new), unview(cv_new), conv_new.reshape(m, CONV_W - 1, CONV_DIM),
            h_new.reshape(m, SSM_HEADS, SSM_HEAD_DIM, D_STATE))


def kernel(x_prompt, x_sample, cache_k, cache_v, state_conv, state_ssm, norm_mix, w_in, q_norm, k_norm,
           attn_sinks, conv_w, conv_b, dt_bias, a_log, d_skip, ssm_norm, w_attn_o, w_ssm_o, w_out,
           norm_mlp, w_up, w_down):
    depth = w_in.shape[0]
    d_model = x_prompt.shape[-1]
    yp, ys = x_prompt, x_sample
    cols = [[] for _ in range(8)]
    for l in range(depth):
        lw = _prep_layer(d_model, norm_mix[l], w_in[l], q_norm[l], k_norm[l], attn_sinks[l], conv_w[l],
                         conv_b[l], dt_bias[l], a_log[l], d_skip[l], ssm_norm[l], w_attn_o[l], w_ssm_o[l],
                         w_out[l], norm_mlp[l], w_up[l], w_down[l])
        yp, kp, vp, cp, hp = _prompt_layer(yp, lw)
        ys, ks, vs, cs, hs = _decode_layer(ys, cache_k[l], cache_v[l], state_conv[l], state_ssm[l], lw)
        for lst, val in zip(cols, (kp, vp, cp, hp, ks, vs, cs, hs)):
            lst.append(val)
    return (yp, ys) + tuple(jnp.stack(c) for c in cols)
```

```python
import functools

import numpy as np
import jax
import jax.numpy as jnp
from jax import lax
from jax.experimental import pallas as pl
from jax.experimental.pallas import tpu as pltpu

F32 = jnp.float32
BF16 = jnp.bfloat16

N_HEADS = 16
KV_HEADS = 4
HEAD_DIM = 64
Q_PER_KV = N_HEADS // KV_HEADS
WINDOW = 128
ROPE_THETA = 10000.0
Q_DIM = N_HEADS * HEAD_DIM
KV_DIM = KV_HEADS * HEAD_DIM
SSM_HEAD_DIM = 64
SSM_HEADS = 32
D_INNER = SSM_HEADS * SSM_HEAD_DIM
SSM_GROUPS = 4
HEADS_PER_GROUP = SSM_HEADS // SSM_GROUPS
D_STATE = 128
BC_DIM = SSM_GROUPS * D_STATE
CONV_W = 4
CONV_DIM = D_INNER + 2 * BC_DIM
SSD_CHUNK = 128
EPS = 1e-6
PAST_LEN = 8192
LOG2E = 1.4426950408889634

LANES = 128
SUBLANES = 8
VMEM_LIMIT = 48 * 1024 * 1024
WEIGHT_RESIDENT_VMEM_LIMIT = 56 * 1024 * 1024

BLK = 128
assert WINDOW == BLK and SSD_CHUNK == BLK
SLAB = KV_HEADS * HEAD_DIM
GROUP_LANES = HEADS_PER_GROUP * SSM_HEAD_DIM
DT_PAD = LANES


def _layout(d_model):
    segs = [("z", D_INNER), ("xs", D_INNER), ("ga", d_model), ("gs", d_model), ("q", Q_DIM),
            ("B", BC_DIM), ("C", BC_DIM), ("k", KV_DIM), ("v", KV_DIM), ("dt", DT_PAD)]
    lay, off = {}, 0
    for name, width in segs:
        assert off % width == 0, (name, off, width)
        lay[name] = (off, width)
        off += width
    return lay, off


def _col_tile(n, cap=3072):
    units = n // LANES
    best = 1
    for d in range(1, units + 1):
        if units % d == 0 and d * LANES <= cap:
            best = d
    return best * LANES


def _row_tile(m, cap):
    assert m % SUBLANES == 0
    best = SUBLANES
    for t in range(SUBLANES, min(m, cap) + 1, SUBLANES):
        if m % t == 0:
            best = t
    return best


def _cparams(sem, vmem_limit=VMEM_LIMIT):
    return pltpu.CompilerParams(dimension_semantics=sem, vmem_limit_bytes=vmem_limit)


def _sigmoid(x):
    return 0.5 + 0.5 * jnp.tanh(0.5 * x)


def _silu(x):
    h = 0.5 * x
    return h + h * jnp.tanh(h)


def _softplus(x):
    return jnp.maximum(x, 0.0) + jnp.log1p(jnp.exp(-jnp.abs(x)))


def _split3(a):
    hi = a.astype(BF16)
    r1 = a - hi.astype(F32)
    mid = r1.astype(BF16)
    lo = (r1 - mid.astype(F32)).astype(BF16)
    return hi, mid, lo


def _expand_heads(a, e3):
    hi, mid, lo = _split3(a)
    return jnp.dot(jnp.concatenate([hi, mid, lo], axis=1), e3, preferred_element_type=F32)


def _head_ms(x, bd):
    sq = x * x
    hi = sq.astype(BF16)
    lo = (sq - hi.astype(F32)).astype(BF16)
    outs = []
    for s in range(x.shape[1] // SLAB):
        sl = slice(s * SLAB, (s + 1) * SLAB)
        outs.append(jnp.dot(hi[:, sl], bd, preferred_element_type=F32)
                    + jnp.dot(lo[:, sl], bd, preferred_element_type=F32))
    return outs[0] if len(outs) == 1 else jnp.concatenate(outs, axis=1)


def _head_norm_rope(x, g, cos, sin, bd):
    xn = x * lax.rsqrt(_head_ms(x, bd) + EPS) * g
    tiles = []
    for t in range(x.shape[1] // LANES):
        xt = xn[:, t * LANES:(t + 1) * LANES]
        tiles.append(xt * cos + pltpu.roll(xt, LANES // 2, 1) * sin)
    return tiles[0] if len(tiles) == 1 else jnp.concatenate(tiles, axis=1)


def _unpair(x):
    q = HEAD_DIM // 2
    lane = lax.broadcasted_iota(jnp.int32, (x.shape[0], LANES), 1)
    tiles = []
    for t in range(x.shape[1] // LANES):
        xt = x[:, t * LANES:(t + 1) * LANES]
        nat = jnp.where((lane >= q) & (lane < 2 * q), pltpu.roll(xt, LANES - q, 1), xt)
        tiles.append(jnp.where((lane >= 2 * q) & (lane < 3 * q), pltpu.roll(xt, q, 1), nat))
    return tiles[0] if len(tiles) == 1 else jnp.concatenate(tiles, axis=1)


def _in_proj_kernel(x_ref, g_ref, w_ref, o_ref, xn_ref, *, tn):
    j = pl.program_id(1)

    @pl.when(j == 0)
    def _():
        x = x_ref[...]
        ms = jnp.mean(x * x, axis=-1, keepdims=True)
        xn_ref[...] = (x * lax.rsqrt(ms + EPS) * g_ref[...]).astype(BF16)

    for c in range(w_ref.shape[1] // tn):
        @pl.when(j == c)
        def _():
            o_ref[...] = jnp.dot(xn_ref[...], w_ref[:, c * tn:(c + 1) * tn], preferred_element_type=F32)


def _in_proj(x, g, w):
    m, k = x.shape
    n = w.shape[1]
    tm = _row_tile(m, 1024)
    tn = _col_tile(n)
    return pl.pallas_call(
        functools.partial(_in_proj_kernel, tn=tn),
        grid=(m // tm, n // tn),
        in_specs=[pl.BlockSpec((tm, k), lambda i, j: (i, 0)),
                  pl.BlockSpec((1, k), lambda i, j: (0, 0)),
                  pl.BlockSpec((k, n), lambda i, j: (0, 0), pipeline_mode=pl.Buffered(1))],
        out_specs=pl.BlockSpec((tm, tn), lambda i, j: (i, j)),
        out_shape=jax.ShapeDtypeStruct((m, n), F32),
        scratch_shapes=[pltpu.VMEM((tm, k), BF16)],
        compiler_params=_cparams(("arbitrary", "arbitrary"), WEIGHT_RESIDENT_VMEM_LIMIT),
        name="in_proj",
    )(x, g, w)


def _attn_prompt_kernel(q_ref, k_ref, v_ref, cos_ref, sin_ref, qg_ref, kg_ref, bd_ref, sink_ref,
                        o_ref, ko_ref, vo_ref, kbuf, vbuf, probs):
    i = pl.program_id(1)
    last = pl.num_programs(1) - 1

    @pl.when(i == 0)
    def _():
        kbuf[...] = jnp.zeros_like(kbuf)
        vbuf[...] = jnp.zeros_like(vbuf)

    cos, sin, bd = cos_ref[...], sin_ref[...], bd_ref[...]
    slot = i % 2
    lane = lax.broadcasted_iota(jnp.int32, (BLK, SLAB), 1)
    grp_k = 2 * (lane // LANES) + (lane % HEAD_DIM) // (HEAD_DIM // 2)
    grp_v = lane // HEAD_DIM
    r = lax.broadcasted_iota(jnp.int32, (BLK, BLK), 0)
    key = lax.broadcasted_iota(jnp.int32, (BLK, BLK), 1)
    causal = key <= r
    take0 = jnp.where(slot == 0, causal.astype(jnp.int32), 1 - causal.astype(jnp.int32)) > 0
    bias = jnp.where(causal, 0.0, jnp.where(i > 0, 0.0, -jnp.inf))

    seqs = range(q_ref.shape[0])
    ks, vs, scores = [], [], []
    for u in seqs:
        q = _head_norm_rope(q_ref[u], qg_ref[...], cos, sin, bd) * (HEAD_DIM ** -0.5 * LOG2E)
        k = _head_norm_rope(k_ref[u], kg_ref[...], cos, sin, bd)
        v = v_ref[u]
        ks.append(k)
        vs.append(v)
        for p in range(KV_HEADS):
            rows = pl.ds(pl.multiple_of(p * 2 * BLK + slot * BLK, BLK), BLK)
            kbuf[u, rows, :] = jnp.where(grp_k == p, k, 0.0).astype(BF16)
            vbuf[u, rows, :] = jnp.where(grp_v == p, v, 0.0).astype(BF16)
        q_stack = jnp.concatenate([q[:, j * SLAB:(j + 1) * SLAB] for j in range(Q_PER_KV)],
                                  axis=0).astype(BF16)
        scores.append(lax.dot_general(q_stack, kbuf[u], (((1,), (1,)), ((), ())),
                                      preferred_element_type=F32))
    for u in seqs:
        for j in range(Q_PER_KV):
            for p in range(KV_HEADS):
                rows = slice(j * BLK, (j + 1) * BLK)
                c0 = p * 2 * BLK
                s = jnp.where(take0, scores[u][rows, c0:c0 + BLK], scores[u][rows, c0 + BLK:c0 + 2 * BLK]) + bias
                sink = sink_ref[j * KV_HEADS + p] * LOG2E
                mx = jnp.maximum(jnp.max(s, axis=-1, keepdims=True), sink)
                e = jnp.exp2(s - mx)
                den = jnp.sum(e, axis=-1, keepdims=True) + jnp.exp2(sink - mx)
                pr = e / den
                probs[u, rows, c0:c0 + BLK] = jnp.where(take0, pr, 0.0).astype(BF16)
                probs[u, rows, c0 + BLK:c0 + 2 * BLK] = jnp.where(take0, 0.0, pr).astype(BF16)
    for u in seqs:
        pv = jnp.dot(probs[u], vbuf[u], preferred_element_type=F32)
        for j in range(Q_PER_KV):
            o_ref[u, :, j * SLAB:(j + 1) * SLAB] = pv[j * BLK:(j + 1) * BLK, :].astype(o_ref.dtype)

    @pl.when(i == last)
    def _():
        for u in seqs:
            ko_ref[u] = _unpair(ks[u]).T
            vo_ref[u] = vs[u].T


ATTN_SEQS = 2


def _attn_prompt(p_act, lay, batch, seq, cos, sin, qg, kg, bd, sinks):
    nb = seq // BLK
    u = ATTN_SEQS if batch % ATTN_SEQS == 0 else 1
    qc, kc, vc = lay["q"][0] // Q_DIM, lay["k"][0] // KV_DIM, lay["v"][0] // KV_DIM
    p3 = p_act.reshape(batch, seq, p_act.shape[1])
    return pl.pallas_call(
        _attn_prompt_kernel,
        grid=(batch // u, nb),
        in_specs=[pl.BlockSpec((u, BLK, Q_DIM), lambda b, i: (b, i, qc)),
                  pl.BlockSpec((u, BLK, KV_DIM), lambda b, i: (b, i, kc)),
                  pl.BlockSpec((u, BLK, KV_DIM), lambda b, i: (b, i, vc)),
                  pl.BlockSpec((BLK, LANES), lambda b, i: (i, 0)),
                  pl.BlockSpec((BLK, LANES), lambda b, i: (i, 0)),
                  pl.BlockSpec((1, Q_DIM), lambda b, i: (0, 0)),
                  pl.BlockSpec((1, KV_DIM), lambda b, i: (0, 0)),
                  pl.BlockSpec((SLAB, SLAB), lambda b, i: (0, 0)),
                  pl.BlockSpec(memory_space=pltpu.SMEM)],
        out_specs=[pl.BlockSpec((u, BLK, Q_DIM), lambda b, i: (b, i, 0)),
                   pl.BlockSpec((u, KV_DIM, BLK), lambda b, i: (b, 0, 0)),
                   pl.BlockSpec((u, KV_DIM, BLK), lambda b, i: (b, 0, 0))],
        out_shape=[jax.ShapeDtypeStruct((batch, seq, Q_DIM), BF16),
                   jax.ShapeDtypeStruct((batch, KV_DIM, BLK), F32),
                   jax.ShapeDtypeStruct((batch, KV_DIM, BLK), F32)],
        scratch_shapes=[pltpu.VMEM((u, KV_HEADS * 2 * BLK, KV_DIM), BF16),
                        pltpu.VMEM((u, KV_HEADS * 2 * BLK, KV_DIM), BF16),
                        pltpu.VMEM((u, Q_PER_KV * BLK, KV_HEADS * 2 * BLK), BF16)],
        compiler_params=_cparams(("arbitrary", "arbitrary")),
        name="attn_prompt",
    )(p3, p3, p3, cos, sin, qg, kg, bd, sinks)


def _ssd_chunk(slot, z_ref, xs_ref, b_ref, c_ref, dt_ref, cw_ref, cb_ref, dtb_ref, alog_ref, dskip_ref,
               nw_ref, e3_ref, o_ref, tail_ref, xpad, tails, st, ybuf):
    xpad[0:SUBLANES, :] = tails[1 - slot]
    xpad[SUBLANES:SUBLANES + BLK, 0:D_INNER] = xs_ref[...]
    xpad[SUBLANES:SUBLANES + BLK, D_INNER:D_INNER + BC_DIM] = b_ref[...]
    xpad[SUBLANES:SUBLANES + BLK, D_INNER + BC_DIM:CONV_DIM] = c_ref[...]
    cwh = 0.5 * cw_ref[...]
    acc = 0.5 * cb_ref[...] + cwh[CONV_W - 1:CONV_W, :] * xpad[SUBLANES:SUBLANES + BLK, :]
    for t in range(1, CONV_W):
        acc = acc + cwh[CONV_W - 1 - t:CONV_W - t, :] * xpad[SUBLANES - t:SUBLANES - t + BLK, :]
    xc = acc + acc * jnp.tanh(acc)
    new_tail = xpad[BLK:BLK + SUBLANES, :]
    tail_ref[...] = new_tail
    tails[slot] = new_tail

    xs = xc[:, 0:D_INNER]
    bm = xc[:, D_INNER:D_INNER + BC_DIM].astype(BF16)
    cm = xc[:, D_INNER + BC_DIM:CONV_DIM].astype(BF16)

    e3 = e3_ref[...]
    dt = _softplus(dt_ref[...] + dtb_ref[...])
    dta = dt * (-LOG2E * jnp.exp(alog_ref[...]))
    row = lax.broadcasted_iota(jnp.int32, (BLK, BLK), 0)
    col = lax.broadcasted_iota(jnp.int32, (BLK, BLK), 1)
    causal = row >= col
    cum = jnp.dot(causal.astype(F32), dta, preferred_element_type=F32, precision=lax.Precision.HIGHEST)
    cum_t = cum.T
    ecum = jnp.exp2(cum)
    to_end = jnp.exp2(cum[BLK - 1:BLK, :] - cum) * dt
    dt_e = _expand_heads(dt, e3)
    ecum_e = _expand_heads(ecum, e3)
    to_end_e = _expand_heads(to_end, e3)
    xdt = (xs * dt_e).astype(BF16)
    xte = (xs * to_end_e).astype(BF16)
    lane = lax.broadcasted_iota(jnp.int32, (BLK, LANES), 1)
    first_head = lane < SSM_HEAD_DIM

    for g in range(SSM_GROUPS):
        gl = slice(g * GROUP_LANES, (g + 1) * GROUP_LANES)
        bg = bm[:, g * D_STATE:(g + 1) * D_STATE]
        cg = cm[:, g * D_STATE:(g + 1) * D_STATE]
        cbg = lax.dot_general(cg, bg, (((1,), (1,)), ((), ())), preferred_element_type=F32)
        st_g = st[1 - slot, :, gl]
        y_inter = jnp.dot(cg, st_g.astype(BF16), preferred_element_type=F32) * ecum_e[:, gl]
        for pr in range(HEADS_PER_GROUP // 2):
            h0 = g * HEADS_PER_GROUP + 2 * pr
            xd = xdt[:, h0 * SSM_HEAD_DIM:(h0 + 2) * SSM_HEAD_DIM]
            ys = []
            for h in (h0, h0 + 1):
                diff = cum[:, h:h + 1] - cum_t[h:h + 1, :]
                w = (jnp.exp2(jnp.where(causal, diff, -jnp.inf)) * cbg).astype(BF16)
                ys.append(jnp.dot(w, xd, preferred_element_type=F32))
            lo = pr * LANES
            ybuf[:, h0 * SSM_HEAD_DIM:(h0 + 2) * SSM_HEAD_DIM] = (
                jnp.where(first_head, ys[0], ys[1]) + y_inter[:, lo:lo + LANES])
        upd = lax.dot_general(bg, xte[:, gl], (((0,), (0,)), ((), ())), preferred_element_type=F32)
        st[slot, :, gl] = st_g * ecum_e[BLK - 1:BLK, gl] + upd

    y = ybuf[...] + dskip_ref[...] * xs
    yz = y * _silu(z_ref[...])
    ms = jnp.mean(yz * yz, axis=-1, keepdims=True)
    o_ref[...] = (yz * lax.rsqrt(ms + EPS) * nw_ref[...]).astype(o_ref.dtype)


def _ssd_prompt_kernel(z_ref, xs_ref, b_ref, c_ref, dt_ref, cw_ref, cb_ref, dtb_ref, alog_ref, dskip_ref,
                       nw_ref, e3_ref, o_ref, tail_ref, hfin_ref, xpad, tails, st, ybuf):
    i = pl.program_id(1)
    last = pl.num_programs(1) - 1

    @pl.when(i == 0)
    def _():
        tails[...] = jnp.zeros_like(tails)
        st[...] = jnp.zeros_like(st)

    slot = i % 2
    seqs = range(z_ref.shape[0])
    for u in seqs:
        _ssd_chunk(slot, z_ref.at[u], xs_ref.at[u], b_ref.at[u], c_ref.at[u], dt_ref.at[u], cw_ref, cb_ref,
                   dtb_ref, alog_ref, dskip_ref, nw_ref, e3_ref, o_ref.at[u], tail_ref.at[u], xpad.at[u],
                   tails.at[u], st.at[u], ybuf.at[u])

    @pl.when(i == last)
    def _():
        for u in seqs:
            hfin_ref[u] = st[u, slot].T


SSD_SEQS = 2


def _ssd_prompt(p_act, lay, batch, seq, cw, cb, dtb, alog, dskip, nw, e3):
    nb = seq // BLK
    u = SSD_SEQS if batch % SSD_SEQS == 0 else 1
    p3 = p_act.reshape(batch, seq, p_act.shape[1])
    zc, xc = lay["z"][0] // D_INNER, lay["xs"][0] // D_INNER
    bc, cc, dc = lay["B"][0] // BC_DIM, lay["C"][0] // BC_DIM, lay["dt"][0] // DT_PAD
    const = lambda b, i: (0, 0)
    return pl.pallas_call(
        _ssd_prompt_kernel,
        grid=(batch // u, nb),
        in_specs=[pl.BlockSpec((u, BLK, D_INNER), lambda b, i: (b, i, zc)),
                  pl.BlockSpec((u, BLK, D_INNER), lambda b, i: (b, i, xc)),
                  pl.BlockSpec((u, BLK, BC_DIM), lambda b, i: (b, i, bc)),
                  pl.BlockSpec((u, BLK, BC_DIM), lambda b, i: (b, i, cc)),
                  pl.BlockSpec((u, BLK, DT_PAD), lambda b, i: (b, i, dc)),
                  pl.BlockSpec((CONV_W, CONV_DIM), const),
                  pl.BlockSpec((1, CONV_DIM), const),
                  pl.BlockSpec((1, DT_PAD), const),
                  pl.BlockSpec((1, DT_PAD), const),
                  pl.BlockSpec((1, D_INNER), const),
                  pl.BlockSpec((1, D_INNER), const),
                  pl.BlockSpec((3 * DT_PAD, D_INNER), const)],
        out_specs=[pl.BlockSpec((u, BLK, D_INNER), lambda b, i: (b, i, 0)),
                   pl.BlockSpec((u, SUBLANES, CONV_DIM), lambda b, i: (b, 0, 0)),
                   pl.BlockSpec((u, D_INNER, D_STATE), lambda b, i: (b, 0, 0))],
        out_shape=[jax.ShapeDtypeStruct((batch, seq, D_INNER), BF16),
                   jax.ShapeDtypeStruct((batch, SUBLANES, CONV_DIM), F32),
                   jax.ShapeDtypeStruct((batch, D_INNER, D_STATE), F32)],
        scratch_shapes=[pltpu.VMEM((u, BLK + SUBLANES, CONV_DIM), F32),
                        pltpu.VMEM((u, 2, SUBLANES, CONV_DIM), F32),
                        pltpu.VMEM((u, 2, D_STATE, D_INNER), F32),
                        pltpu.VMEM((u, BLK, D_INNER), F32)],
        compiler_params=_cparams(("arbitrary", "arbitrary")),
        name="ssd_prompt",
    )(p3, p3, p3, p3, p3, cw, cb, dtb, alog, dskip, nw, e3)


def _decode_pre_kernel(q_ref, k_ref, xs_ref, b_ref, c_ref, dt_ref, cs_ref, cos_ref, sin_ref, qg_ref, kg_ref,
                       bd_ref, cw_ref, cb_ref, dtb_ref, alog_ref, e3_ref,
                       qo_ref, ko_ref, xc_ref, xdt_ref, dec_ref, cso_ref):
    cos, sin, bd = cos_ref[...], sin_ref[...], bd_ref[...]
    qo_ref[...] = _unpair(_head_norm_rope(q_ref[...], qg_ref[...], cos, sin, bd)) * (HEAD_DIM ** -0.5)
    ko_ref[...] = _unpair(_head_norm_rope(k_ref[...], kg_ref[...], cos, sin, bd))

    segs = ((xs_ref, 0, D_INNER), (b_ref, D_INNER, BC_DIM), (c_ref, D_INNER + BC_DIM, BC_DIM))
    for ref, off, width in segs:
        new = ref[...]
        acc = cb_ref[:, off:off + width] + cw_ref[CONV_W - 1:CONV_W, off:off + width] * new
        for t in range(CONV_W - 1):
            lo = t * CONV_DIM + off
            acc = acc + cw_ref[t:t + 1, off:off + width] * cs_ref[:, lo:lo + width]
        xc_ref[:, off:off + width] = _silu(acc)
        for t in range(CONV_W - 2):
            cso_ref[:, t * CONV_DIM + off:t * CONV_DIM + off + width] = (
                cs_ref[:, (t + 1) * CONV_DIM + off:(t + 1) * CONV_DIM + off + width])
        lo = (CONV_W - 2) * CONV_DIM + off
        cso_ref[:, lo:lo + width] = new

    e3 = e3_ref[...]
    dt = _softplus(dt_ref[...] + dtb_ref[...])
    decay = jnp.exp(dt * (-jnp.exp(alog_ref[...])))
    xdt_ref[...] = xc_ref[:, 0:D_INNER] * _expand_heads(dt, e3)
    dec_ref[...] = _expand_heads(decay, e3)


def _decode_pre(p_act, lay, conv_state, cos, sin, qg, kg, bd, cw, cb, dtb, alog, e3):
    m = p_act.shape[0]
    full = lambda shape: pl.BlockSpec(shape, lambda i: (0, 0))

    def col(name):
        c = lay[name][0] // lay[name][1]
        return pl.BlockSpec((m, lay[name][1]), lambda i: (0, c))

    cs_w = (CONV_W - 1) * CONV_DIM
    return pl.pallas_call(
        _decode_pre_kernel,
        grid=(1,),
        in_specs=[col("q"), col("k"), col("xs"), col("B"), col("C"), col("dt"),
                  full((m, cs_w)), full((1, LANES)), full((1, LANES)), full((1, Q_DIM)), full((1, KV_DIM)),
                  full((SLAB, SLAB)), full((CONV_W, CONV_DIM)), full((1, CONV_DIM)), full((1, DT_PAD)),
                  full((1, DT_PAD)), full((3 * DT_PAD, D_INNER))],
        out_specs=[full((m, Q_DIM)), full((m, KV_DIM)), full((m, CONV_DIM)), full((m, D_INNER)),
                   full((m, D_INNER)), full((m, cs_w))],
        out_shape=[jax.ShapeDtypeStruct((m, Q_DIM), F32), jax.ShapeDtypeStruct((m, KV_DIM), F32),
                   jax.ShapeDtypeStruct((m, CONV_DIM), F32), jax.ShapeDtypeStruct((m, D_INNER), F32),
                   jax.ShapeDtypeStruct((m, D_INNER), F32), jax.ShapeDtypeStruct((m, cs_w), F32)],
        compiler_params=_cparams(("arbitrary",)),
        name="decode_pre",
    )(p_act, p_act, p_act, p_act, p_act, p_act, conv_state, cos, sin, qg, kg, bd, cw, cb, dtb, alog, e3)


DEC_ROWS = 8


def _attn_decode_kernel(q_ref, k_ref, v_ref, ck_ref, cv_ref, sink_ref, o_ref, cko_ref, cvo_ref):
    w = ck_ref.shape[2]
    nq = N_HEADS
    r = lax.broadcasted_iota(jnp.int32, (nq, SLAB), 0)
    grp = lax.broadcasted_iota(jnp.int32, (nq, SLAB), 1) // HEAD_DIM
    own = grp == (r % KV_HEADS)
    in_window = lax.broadcasted_iota(jnp.int32, (nq, w), 1) > (w - WINDOW - 1)
    newest = lax.broadcasted_iota(jnp.int32, (KV_DIM, w), 1) == w - 1
    sink = sink_ref[...][:, 0:1]
    pad = jnp.zeros((DEC_ROWS, KV_DIM), BF16)
    k_parts = jnp.concatenate(list(_split3(k_ref[...])) + [pad], axis=0)
    v_parts = jnp.concatenate(list(_split3(v_ref[...])) + [pad], axis=0)
    part_row = lax.broadcasted_iota(jnp.int32, (4 * DEC_ROWS, w), 0) % DEC_ROWS
    tdims = (((0,), (0,)), ((), ()))
    for bl in range(DEC_ROWS):
        qrow = q_ref[bl:bl + 1, :]
        qm = jnp.zeros((nq, SLAB), F32)
        for j in range(Q_PER_KV):
            slab = jnp.broadcast_to(qrow[:, j * SLAB:(j + 1) * SLAB], (nq, SLAB))
            qm = jnp.where((r // KV_HEADS) == j, slab, qm)
        qm = jnp.where(own, qm, 0.0).astype(BF16)
        pick = jnp.where(part_row == bl, 1.0, 0.0).astype(BF16)
        k_col = lax.dot_general(k_parts, pick, tdims, preferred_element_type=F32)
        v_col = lax.dot_general(v_parts, pick, tdims, preferred_element_type=F32)
        keys = jnp.where(newest, k_col, pltpu.roll(ck_ref[bl], w - 1, 1))
        vals = jnp.where(newest, v_col, pltpu.roll(cv_ref[bl], w - 1, 1))
        cko_ref[bl] = keys
        cvo_ref[bl] = vals
        s = jnp.dot(qm, keys.astype(BF16), preferred_element_type=F32)
        s = jnp.where(in_window, s, -jnp.inf)
        mx = jnp.maximum(jnp.max(s, axis=-1, keepdims=True), sink)
        e = jnp.exp(s - mx)
        den = jnp.sum(e, axis=-1, keepdims=True) + jnp.exp(sink - mx)
        pv = lax.dot_general(e.astype(BF16), vals.astype(BF16), (((1,), (1,)), ((), ())),
                             preferred_element_type=F32)
        pv = jnp.where(own, pv / den, 0.0)
        for j in range(Q_PER_KV):
            o_ref[bl:bl + 1, j * SLAB:(j + 1) * SLAB] = jnp.sum(
                pv[j * KV_HEADS:(j + 1) * KV_HEADS, :], axis=0, keepdims=True)


def _attn_decode(q, k, p_act, lay, cache_k, cache_v, sink_rows):
    m = q.shape[0]
    w = cache_k.shape[2]
    vc = lay["v"][0] // KV_DIM
    cache_spec = pl.BlockSpec((DEC_ROWS, KV_DIM, w), lambda i: (i, 0, 0))
    return pl.pallas_call(
        _attn_decode_kernel,
        grid=(m // DEC_ROWS,),
        in_specs=[pl.BlockSpec((DEC_ROWS, Q_DIM), lambda i: (i, 0)),
                  pl.BlockSpec((DEC_ROWS, KV_DIM), lambda i: (i, 0)),
                  pl.BlockSpec((DEC_ROWS, KV_DIM), lambda i: (i, vc)),
                  cache_spec, cache_spec,
                  pl.BlockSpec((N_HEADS, LANES), lambda i: (0, 0))],
        out_specs=[pl.BlockSpec((DEC_ROWS, Q_DIM), lambda i: (i, 0)), cache_spec, cache_spec],
        out_shape=[jax.ShapeDtypeStruct((m, Q_DIM), F32),
                   jax.ShapeDtypeStruct((m, KV_DIM, w), F32),
                   jax.ShapeDtypeStruct((m, KV_DIM, w), F32)],
        compiler_params=_cparams(("arbitrary",)),
        name="attn_decode",
    )(q, k, p_act, cache_k, cache_v, sink_rows)


MM_ROWS = 16


def _ssm_decode_kernel(st_ref, xdt_ref, dec_ref, xc_ref, z_ref, dskip_ref, nw_ref, sto_ref, o_ref):
    rows = st_ref.shape[0]
    r = lax.broadcasted_iota(jnp.int32, (MM_ROWS, D_INNER), 0)
    grp = lax.broadcasted_iota(jnp.int32, (MM_ROWS, D_INNER), 1) // GROUP_LANES
    rr = lax.broadcasted_iota(jnp.int32, (MM_ROWS, D_STATE), 0)
    ones_rows = jnp.where((rr >= SSM_GROUPS) & (rr < SSM_GROUPS + 3), 1.0, 0.0)
    bc = lambda a: jnp.broadcast_to(a, (MM_ROWS, D_INNER))
    new_states = []
    for u in range(rows):
        h = st_ref[u]
        xdt = xdt_ref[u:u + 1, :]
        xc = xc_ref[u:u + 1, :]
        hi, mid, lo = _split3(dec_ref[u:u + 1, :])
        lhs_t = jnp.where(r == grp, bc(xdt), 0.0)
        for t, piece in enumerate((hi, mid, lo)):
            lhs_t = jnp.where(r == SSM_GROUPS + t, bc(piece.astype(F32)), lhs_t)
        lhs_t = lhs_t.astype(BF16)
        b_rows = jnp.zeros((MM_ROWS, D_STATE), F32)
        for g in range(SSM_GROUPS):
            b_g = xc[:, D_INNER + g * D_STATE:D_INNER + (g + 1) * D_STATE]
            b_rows = jnp.where(rr == g, jnp.broadcast_to(b_g, (MM_ROWS, D_STATE)), b_rows)
        rhs = jnp.concatenate([b_rows, ones_rows], axis=1).astype(BF16)
        both = lax.dot_general(lhs_t, rhs, (((0,), (0,)), ((), ())), preferred_element_type=F32)
        h_new = both[:, D_STATE:] * h + both[:, :D_STATE]
        sto_ref[u] = h_new
        new_states.append(h_new.astype(BF16))
    states = jnp.concatenate(new_states, axis=1)
    urow = lax.broadcasted_iota(jnp.int32, (rows, rows * D_STATE), 0)
    ublk = lax.broadcasted_iota(jnp.int32, (rows, rows * D_STATE), 1) // D_STATE
    c_sel = []
    for g in range(SSM_GROUPS):
        c_g = xc_ref[:, D_INNER + BC_DIM + g * D_STATE:D_INNER + BC_DIM + (g + 1) * D_STATE]
        c_sel.append(jnp.where(urow == ublk, jnp.concatenate([c_g] * rows, axis=1), 0.0))
    c_all = jnp.concatenate(c_sel, axis=0).astype(BF16)
    yt = lax.dot_general(states, c_all, (((1,), (1,)), ((), ())), preferred_element_type=F32).T
    lane_grp = lax.broadcasted_iota(jnp.int32, (rows, D_INNER), 1) // GROUP_LANES
    y = jnp.zeros((rows, D_INNER), F32)
    for g in range(SSM_GROUPS):
        y = jnp.where(lane_grp == g, yt[g * rows:(g + 1) * rows, :], y)
    y = y + dskip_ref[...] * xc_ref[:, 0:D_INNER]
    yz = y * _silu(z_ref[...])
    ms = jnp.mean(yz * yz, axis=-1, keepdims=True)
    o_ref[...] = yz * lax.rsqrt(ms + EPS) * nw_ref[...]


SSM_DEC_ROWS = SUBLANES


def _ssm_decode(state, xdt, dec, xc, p_act, lay, dskip, nw):
    m = state.shape[0]
    u = SSM_DEC_ROWS
    assert m % u == 0
    zc = lay["z"][0] // D_INNER
    row = lambda width: pl.BlockSpec((u, width), lambda b: (b, 0))
    return pl.pallas_call(
        _ssm_decode_kernel,
        grid=(m // u,),
        in_specs=[pl.BlockSpec((u, D_INNER, D_STATE), lambda b: (b, 0, 0)),
                  row(D_INNER), row(D_INNER), row(CONV_DIM),
                  pl.BlockSpec((u, D_INNER), lambda b: (b, zc)),
                  pl.BlockSpec((1, D_INNER), lambda b: (0, 0)),
                  pl.BlockSpec((1, D_INNER), lambda b: (0, 0))],
        out_specs=[pl.BlockSpec((u, D_INNER, D_STATE), lambda b: (b, 0, 0)), row(D_INNER)],
        out_shape=[jax.ShapeDtypeStruct((m, D_INNER, D_STATE), F32),
                   jax.ShapeDtypeStruct((m, D_INNER), F32)],
        compiler_params=_cparams(("arbitrary",)),
        name="ssm_decode",
    )(state, xdt, dec, xc, p_act, dskip, nw)


def _merge_mlp_kernel(x_ref, oa_ref, os_ref, ga_ref, gs_ref, wa_ref, ws_ref, wo_ref, g_ref, wu_ref, wd_ref,
                      o_ref, *, ff_tile):
    a = jnp.dot(oa_ref[...].astype(BF16), wa_ref[...], preferred_element_type=F32)
    s = jnp.dot(os_ref[...].astype(BF16), ws_ref[...], preferred_element_type=F32)
    mixed = _sigmoid(ga_ref[...]) * a + _sigmoid(gs_ref[...]) * s
    x = x_ref[...] + jnp.dot(mixed.astype(BF16), wo_ref[...], preferred_element_type=F32)
    ms = jnp.mean(x * x, axis=-1, keepdims=True)
    xn = (x * lax.rsqrt(ms + EPS) * g_ref[...]).astype(BF16)
    acc = x
    for c in range(wu_ref.shape[1] // ff_tile):
        u = jnp.dot(xn, wu_ref[:, c * ff_tile:(c + 1) * ff_tile], preferred_element_type=F32)
        act = jnp.square(jnp.maximum(u, 0.0)).astype(BF16)
        acc = acc + jnp.dot(act, wd_ref[c * ff_tile:(c + 1) * ff_tile, :], preferred_element_type=F32)
    o_ref[...] = acc


def _merge_mlp(x, o_attn, o_ssm, p_act, lay, wa, ws, wo, g, wu, wd):
    m, d = x.shape
    ff = wu.shape[1]
    tm = _row_tile(m, 512)
    gac, gsc = lay["ga"][0] // d, lay["gs"][0] // d
    weight = lambda shape: pl.BlockSpec(shape, lambda i: (0, 0), pipeline_mode=pl.Buffered(1))
    return pl.pallas_call(
        functools.partial(_merge_mlp_kernel, ff_tile=min(1024, ff)),
        grid=(m // tm,),
        in_specs=[pl.BlockSpec((tm, d), lambda i: (i, 0)),
                  pl.BlockSpec((tm, Q_DIM), lambda i: (i, 0)),
                  pl.BlockSpec((tm, D_INNER), lambda i: (i, 0)),
                  pl.BlockSpec((tm, d), lambda i: (i, gac)),
                  pl.BlockSpec((tm, d), lambda i: (i, gsc)),
                  weight((Q_DIM, d)), weight((D_INNER, d)), weight((d, d)),
                  weight((1, d)), weight((d, ff)), weight((ff, d))],
        out_specs=pl.BlockSpec((tm, d), lambda i: (i, 0)),
        out_shape=jax.ShapeDtypeStruct((m, d), F32),
        compiler_params=_cparams(("arbitrary",), WEIGHT_RESIDENT_VMEM_LIMIT),
        name="merge_mlp",
    )(x, o_attn, o_ssm, p_act, p_act, wa, ws, wo, g, wu, wd)


def _rope_tables(pos):
    half = HEAD_DIM // 2
    inv = ROPE_THETA ** (-jnp.arange(half, dtype=F32) / half)
    ang = pos.astype(F32)[:, None] * inv[None, :]
    cos, sin = jnp.cos(ang), jnp.sin(ang)
    return (jnp.concatenate([cos, cos, cos, cos], axis=1),
            jnp.concatenate([-sin, -sin, sin, sin], axis=1))


def _constants():
    heads = np.arange(DT_PAD)[:, None]
    cols = np.arange(D_INNER)[None, :] // SSM_HEAD_DIM
    e = (heads == cols).astype(np.float32)
    e3 = jnp.asarray(np.concatenate([e, e, e], axis=0), BF16)
    a = np.arange(SLAB)
    grp = 2 * (a // LANES) + (a % HEAD_DIM) // (HEAD_DIM // 2)
    bd = jnp.asarray((grp[:, None] == grp[None, :]).astype(np.float32) / HEAD_DIM, BF16)
    return e3, bd


_HALF = HEAD_DIM // 2


def _pair_q(a):
    lead = a.shape[:-1]
    n = len(lead)
    a = a.reshape(lead + (2, 2, Q_PER_KV, 2, _HALF))
    return a.transpose(tuple(range(n)) + (n + 2, n, n + 3, n + 1, n + 4)).reshape(lead + (Q_DIM,))


def _pair_k(a):
    lead = a.shape[:-1]
    n = len(lead)
    a = a.reshape(lead + (2, 2, 2, _HALF))
    return a.transpose(tuple(range(n)) + (n, n + 2, n + 1, n + 3)).reshape(lead + (KV_DIM,))


def _prep_layer(d_model, norm_mix, w_in, q_norm, k_norm, attn_sinks, conv_w, conv_b, dt_bias, a_log, d_skip,
                ssm_norm, w_attn_o, w_ssm_o, w_out, norm_mlp, w_up, w_down):
    assert KV_HEADS == 4 and Q_PER_KV == 4
    e3, bd = _constants()
    o_q, o_k, o_v, o_z, o_xbc, o_dt = (0, Q_DIM, Q_DIM + KV_DIM, Q_DIM + 2 * KV_DIM,
                                       Q_DIM + 2 * KV_DIM + D_INNER, Q_DIM + 2 * KV_DIM + D_INNER + CONV_DIM)
    o_g = o_dt + SSM_HEADS
    cut = lambda lo, width: w_in[:, lo:lo + width]
    pieces = {"z": cut(o_z, D_INNER), "xs": cut(o_xbc, D_INNER), "ga": cut(o_g, d_model),
              "gs": cut(o_g + d_model, d_model), "q": _pair_q(cut(o_q, Q_DIM)),
              "B": cut(o_xbc + D_INNER, BC_DIM), "C": cut(o_xbc + D_INNER + BC_DIM, BC_DIM),
              "k": _pair_k(cut(o_k, KV_DIM)), "v": cut(o_v, KV_DIM),
              "dt": jnp.pad(cut(o_dt, SSM_HEADS), ((0, 0), (0, DT_PAD - SSM_HEADS)))}
    lay, _ = _layout(d_model)
    w_p = jnp.concatenate([pieces[name] for name in lay], axis=1).astype(BF16)
    pad_heads = lambda a: jnp.pad(a, (0, DT_PAD - SSM_HEADS))[None, :]
    sink_p = attn_sinks.reshape(KV_HEADS, Q_PER_KV).T.reshape(N_HEADS)
    wa = w_attn_o.reshape(KV_HEADS, Q_PER_KV, HEAD_DIM, d_model).transpose(1, 0, 2, 3).reshape(Q_DIM, d_model)
    return dict(
        lay=lay, e3=e3, bd=bd, w_p=w_p,
        norm_mix=norm_mix[None, :], norm_mlp=norm_mlp[None, :],
        qg=_pair_q(jnp.tile(q_norm, N_HEADS))[None, :], kg=_pair_k(jnp.tile(k_norm, KV_HEADS))[None, :],
        sink_p=sink_p, sink_rows=jnp.broadcast_to(sink_p[:, None], (N_HEADS, LANES)),
        cw=conv_w, cb=conv_b[None, :], dtb=pad_heads(dt_bias), alog=pad_heads(a_log),
        dskip=jnp.repeat(d_skip, SSM_HEAD_DIM)[None, :], nw=ssm_norm[None, :],
        wa=wa.astype(BF16), ws=w_ssm_o.astype(BF16), wo=w_out.astype(BF16),
        wu=w_up.astype(BF16), wd=w_down.astype(BF16))


def _prompt_layer(x, lw):
    batch, seq, d = x.shape
    assert seq % BLK == 0 and seq >= WINDOW
    x2 = x.reshape(batch * seq, d)
    p_act = _in_proj(x2, lw["norm_mix"], lw["w_p"])
    cos, sin = _rope_tables(jnp.arange(seq))
    o_attn, k_last, v_last = _attn_prompt(p_act, lw["lay"], batch, seq, cos, sin, lw["qg"], lw["kg"],
                                          lw["bd"], lw["sink_p"])
    o_ssm, tail, h_fin = _ssd_prompt(p_act, lw["lay"], batch, seq, lw["cw"], lw["cb"], lw["dtb"], lw["alog"],
                                     lw["dskip"], lw["nw"], lw["e3"])
    y = _merge_mlp(x2, o_attn.reshape(batch * seq, Q_DIM), o_ssm.reshape(batch * seq, D_INNER), p_act,
                   lw["lay"], lw["wa"], lw["ws"], lw["wo"],
                   lw["norm_mlp"], lw["wu"], lw["wd"])
    return (y.reshape(batch, seq, d),
            k_last.reshape(batch, KV_HEADS, HEAD_DIM, BLK).transpose(0, 3, 1, 2),
            v_last.reshape(batch, KV_HEADS, HEAD_DIM, BLK).transpose(0, 3, 1, 2),
            tail[:, SUBLANES - (CONV_W - 1):, :],
            h_fin.reshape(batch, SSM_HEADS, SSM_HEAD_DIM, D_STATE))


def _decode_layer(x, cache_k, cache_v, conv_state, ssm_state, lw):
    m, t, d = x.shape
    w = cache_k.shape[1]
    assert t == 1 and w == WINDOW and m % DEC_ROWS == 0
    x2 = x.reshape(m, d)
    p_act = _in_proj(x2, lw["norm_mix"], lw["w_p"])
    cos, sin = _rope_tables(PAST_LEN + jnp.arange(1))
    q, k, xc, xdt, dec, conv_new = _decode_pre(
        p_act, lw["lay"], conv_state.reshape(m, (CONV_W - 1) * CONV_DIM), cos, sin, lw["qg"], lw["kg"],
        lw["bd"], lw["cw"], lw["cb"], lw["dtb"], lw["alog"], lw["e3"])
    ck = jnp.transpose(cache_k, (0, 2, 3, 1)).reshape(m, KV_DIM, w)
    cv = jnp.transpose(cache_v, (0, 2, 3, 1)).reshape(m, KV_DIM, w)
    o_attn, ck_new, cv_new = _attn_decode(q, k, p_act, lw["lay"], ck, cv, lw["sink_rows"])
    h_new, o_ssm = _ssm_decode(ssm_state.reshape(m, D_INNER, D_STATE), xdt, dec, xc, p_act, lw["lay"],
                               lw["dskip"], lw["nw"])
    y = _merge_mlp(x2, o_attn, o_ssm, p_act, lw["lay"], lw["wa"], lw["ws"], lw["wo"],
                   lw["norm_mlp"], lw["wu"], lw["wd"])
    unview = lambda c: jnp.transpose(c.reshape(m, KV_HEADS, HEAD_DIM, w), (0, 3, 1, 2))
    return (y.reshape(m, 1, d), unview(ck_new), unview(cv_new), conv_new.reshape(m, CONV_W - 1, CONV_DIM),
            h_new.reshape(m, SSM_HEADS, SSM_HEAD_DIM, D_STATE))


def kernel(x_prompt, x_sample, cache_k, cache_v, state_conv, state_ssm, norm_mix, w_in, q_norm, k_norm,
           attn_sinks, conv_w, conv_b, dt_bias, a_log, d_skip, ssm_norm, w_attn_o, w_ssm_o, w_out,
           norm_mlp, w_up, w_down):
    depth = w_in.shape[0]
    d_model = x_prompt.shape[-1]
    yp, ys = x_prompt, x_sample
    cols = [[] for _ in range(8)]
    for l in range(depth):
        lw = _prep_layer(d_model, norm_mix[l], w_in[l], q_norm[l], k_norm[l], attn_sinks[l], conv_w[l],
                         conv_b[l], dt_bias[l], a_log[l], d_skip[l], ssm_norm[l], w_attn_o[l], w_ssm_o[l],
                         w_out[l], norm_mlp[l], w_up[l], w_down[l])
        yp, kp, vp, cp, hp = _prompt_layer(yp, lw)
        ys, ks, vs, cs, hs = _decode_layer(ys, cache_k[l], cache_v[l], state_conv[l], state_ssm[l], lw)
        for lst, val in zip(cols, (kp, vp, cp, hp, ks, vs, cs, hs)):
            lst.append(val)
    return (yp, ys) + tuple(jnp.stack(c) for c in cols)
```

```python
import functools

import numpy as np
import jax
import jax.numpy as jnp
from jax import lax
from jax.experimental import pallas as pl
from jax.experimental.pallas import tpu as pltpu

F32 = jnp.float32
BF16 = jnp.bfloat16

N_HEADS = 16
KV_HEADS = 4
HEAD_DIM = 64
Q_PER_KV = N_HEADS // KV_HEADS
WINDOW = 128
ROPE_THETA = 10000.0
Q_DIM = N_HEADS * HEAD_DIM
KV_DIM = KV_HEADS * HEAD_DIM
SSM_HEAD_DIM = 64
SSM_HEADS = 32
D_INNER = SSM_HEADS * SSM_HEAD_DIM
SSM_GROUPS = 4
HEADS_PER_GROUP = SSM_HEADS // SSM_GROUPS
D_STATE = 128
BC_DIM = SSM_GROUPS * D_STATE
CONV_W = 4
CONV_DIM = D_INNER + 2 * BC_DIM
SSD_CHUNK = 128
EPS = 1e-6
PAST_LEN = 8192
LOG2E = 1.4426950408889634

LANES = 128
SUBLANES = 8
VMEM_LIMIT = 48 * 1024 * 1024
WEIGHT_RESIDENT_VMEM_LIMIT = 56 * 1024 * 1024

BLK = 128
assert WINDOW == BLK and SSD_CHUNK == BLK
SLAB = KV_HEADS * HEAD_DIM
GROUP_LANES = HEADS_PER_GROUP * SSM_HEAD_DIM
DT_PAD = LANES


def _layout(d_model):
    segs = [("z", D_INNER), ("xs", D_INNER), ("ga", d_model), ("gs", d_model), ("q", Q_DIM),
            ("B", BC_DIM), ("C", BC_DIM), ("k", KV_DIM), ("v", KV_DIM), ("dt", DT_PAD)]
    lay, off = {}, 0
    for name, width in segs:
        assert off % width == 0, (name, off, width)
        lay[name] = (off, width)
        off += width
    return lay, off


def _col_tile(n, cap=3072):
    units = n // LANES
    best = 1
    for d in range(1, units + 1):
        if units % d == 0 and d * LANES <= cap:
            best = d
    return best * LANES


def _row_tile(m, cap):
    assert m % SUBLANES == 0
    best = SUBLANES
    for t in range(SUBLANES, min(m, cap) + 1, SUBLANES):
        if m % t == 0:
            best = t
    return best


def _cparams(sem, vmem_limit=VMEM_LIMIT):
    return pltpu.CompilerParams(dimension_semantics=sem, vmem_limit_bytes=vmem_limit)


def _sigmoid(x):
    return 0.5 + 0.5 * jnp.tanh(0.5 * x)


def _silu(x):
    h = 0.5 * x
    return h + h * jnp.tanh(h)


def _softplus(x):
    return jnp.maximum(x, 0.0) + jnp.log1p(jnp.exp(-jnp.abs(x)))


def _split3(a):
    hi = a.astype(BF16)
    r1 = a - hi.astype(F32)
    mid = r1.astype(BF16)
    lo = (r1 - mid.astype(F32)).astype(BF16)
    return hi, mid, lo


def _expand_heads(a, e3):
    hi, mid, lo = _split3(a)
    return jnp.dot(jnp.concatenate([hi, mid, lo], axis=1), e3, preferred_element_type=F32)


def _head_ms(x, bd):
    sq = x * x
    hi = sq.astype(BF16)
    lo = (sq - hi.astype(F32)).astype(BF16)
    outs = []
    for s in range(x.shape[1] // SLAB):
        sl = slice(s * SLAB, (s + 1) * SLAB)
        outs.append(jnp.dot(hi[:, sl], bd, preferred_element_type=F32)
                    + jnp.dot(lo[:, sl], bd, preferred_element_type=F32))
    return outs[0] if len(outs) == 1 else jnp.concatenate(outs, axis=1)


def _head_norm_rope(x, g, cos, sin, bd):
    xn = x * lax.rsqrt(_head_ms(x, bd) + EPS) * g
    tiles = []
    for t in range(x.shape[1] // LANES):
        xt = xn[:, t * LANES:(t + 1) * LANES]
        tiles.append(xt * cos + pltpu.roll(xt, LANES // 2, 1) * sin)
    return tiles[0] if len(tiles) == 1 else jnp.concatenate(tiles, axis=1)


def _unpair(x):
    q = HEAD_DIM // 2
    lane = lax.broadcasted_iota(jnp.int32, (x.shape[0], LANES), 1)
    tiles = []
    for t in range(x.shape[1] // LANES):
        xt = x[:, t * LANES:(t + 1) * LANES]
        nat = jnp.where((lane >= q) & (lane < 2 * q), pltpu.roll(xt, LANES - q, 1), xt)
        tiles.append(jnp.where((lane >= 2 * q) & (lane < 3 * q), pltpu.roll(xt, q, 1), nat))
    return tiles[0] if len(tiles) == 1 else jnp.concatenate(tiles, axis=1)


def _in_proj_kernel(x_ref, g_ref, w_ref, o_ref, xn_ref, *, tn):
    j = pl.program_id(1)

    @pl.when(j == 0)
    def _():
        x = x_ref[...]
        ms = jnp.mean(x * x, axis=-1, keepdims=True)
        xn_ref[...] = (x * lax.rsqrt(ms + EPS) * g_ref[...]).astype(BF16)

    for c in range(w_ref.shape[1] // tn):
        @pl.when(j == c)
        def _():
            o_ref[...] = jnp.dot(xn_ref[...], w_ref[:, c * tn:(c + 1) * tn], preferred_element_type=F32)


def _in_proj_stream_kernel(x_ref, g_ref, w_ref, o_ref, xn_ref):
    @pl.when(pl.program_id(0) == 0)
    def _():
        x = x_ref[...]
        ms = jnp.mean(x * x, axis=-1, keepdims=True)
        xn_ref[...] = (x * lax.rsqrt(ms + EPS) * g_ref[...]).astype(BF16)

    o_ref[...] = jnp.dot(xn_ref[...], w_ref[...], preferred_element_type=F32)


def _in_proj(x, g, w):
    m, k = x.shape
    n = w.shape[1]
    tm = _row_tile(m, 1024)
    tn = _col_tile(n)
    if m == tm:
        return pl.pallas_call(
            _in_proj_stream_kernel,
            grid=(n // tn,),
            in_specs=[pl.BlockSpec((m, k), lambda j: (0, 0)),
                      pl.BlockSpec((1, k), lambda j: (0, 0)),
                      pl.BlockSpec((k, tn), lambda j: (0, j))],
            out_specs=pl.BlockSpec((m, tn), lambda j: (0, j)),
            out_shape=jax.ShapeDtypeStruct((m, n), F32),
            scratch_shapes=[pltpu.VMEM((m, k), BF16)],
            compiler_params=_cparams(("arbitrary",)),
            name="in_proj_stream",
        )(x, g, w)
    return pl.pallas_call(
        functools.partial(_in_proj_kernel, tn=tn),
        grid=(m // tm, n // tn),
        in_specs=[pl.BlockSpec((tm, k), lambda i, j: (i, 0)),
                  pl.BlockSpec((1, k), lambda i, j: (0, 0)),
                  pl.BlockSpec((k, n), lambda i, j: (0, 0), pipeline_mode=pl.Buffered(1))],
        out_specs=pl.BlockSpec((tm, tn), lambda i, j: (i, j)),
        out_shape=jax.ShapeDtypeStruct((m, n), F32),
        scratch_shapes=[pltpu.VMEM((tm, k), BF16)],
        compiler_params=_cparams(("arbitrary", "arbitrary"), WEIGHT_RESIDENT_VMEM_LIMIT),
        name="in_proj",
    )(x, g, w)


def _attn_prompt_kernel(q_ref, k_ref, v_ref, cos_ref, sin_ref, qg_ref, kg_ref, bd_ref, sink_ref,
                        o_ref, ko_ref, vo_ref, kbuf, vbuf, probs):
    i = pl.program_id(1)
    last = pl.num_programs(1) - 1

    @pl.when(i == 0)
    def _():
        kbuf[...] = jnp.zeros_like(kbuf)
        vbuf[...] = jnp.zeros_like(vbuf)

    cos, sin, bd = cos_ref[...], sin_ref[...], bd_ref[...]
    slot = i % 2
    lane = lax.broadcasted_iota(jnp.int32, (BLK, SLAB), 1)
    grp_k = 2 * (lane // LANES) + (lane % HEAD_DIM) // (HEAD_DIM // 2)
    grp_v = lane // HEAD_DIM
    r = lax.broadcasted_iota(jnp.int32, (BLK, BLK), 0)
    key = lax.broadcasted_iota(jnp.int32, (BLK, BLK), 1)
    causal = key <= r
    take0 = jnp.where(slot == 0, causal.astype(jnp.int32), 1 - causal.astype(jnp.int32)) > 0
    bias = jnp.where(causal, 0.0, jnp.where(i > 0, 0.0, -jnp.inf))

    seqs = range(q_ref.shape[0])
    ks, vs, scores = [], [], []
    for u in seqs:
        q = _head_norm_rope(q_ref[u], qg_ref[...], cos, sin, bd) * (HEAD_DIM ** -0.5 * LOG2E)
        k = _head_norm_rope(k_ref[u], kg_ref[...], cos, sin, bd)
        v = v_ref[u]
        ks.append(k)
        vs.append(v)
        for p in range(KV_HEADS):
            rows = pl.ds(pl.multiple_of(p * 2 * BLK + slot * BLK, BLK), BLK)
            kbuf[u, rows, :] = jnp.where(grp_k == p, k, 0.0).astype(BF16)
            vbuf[u, rows, :] = jnp.where(grp_v == p, v, 0.0).astype(BF16)
        q_stack = jnp.concatenate([q[:, j * SLAB:(j + 1) * SLAB] for j in range(Q_PER_KV)],
                                  axis=0).astype(BF16)
        scores.append(lax.dot_general(q_stack, kbuf[u], (((1,), (1,)), ((), ())),
                                      preferred_element_type=F32))
    for u in seqs:
        for j in range(Q_PER_KV):
            for p in range(KV_HEADS):
                rows = slice(j * BLK, (j + 1) * BLK)
                c0 = p * 2 * BLK
                s = jnp.where(take0, scores[u][rows, c0:c0 + BLK], scores[u][rows, c0 + BLK:c0 + 2 * BLK]) + bias
                sink = sink_ref[j * KV_HEADS + p] * LOG2E
                mx = jnp.maximum(jnp.max(s, axis=-1, keepdims=True), sink)
                e = jnp.exp2(s - mx)
                den = jnp.sum(e, axis=-1, keepdims=True) + jnp.exp2(sink - mx)
                pr = e / den
                probs[u, rows, c0:c0 + BLK] = jnp.where(take0, pr, 0.0).astype(BF16)
                probs[u, rows, c0 + BLK:c0 + 2 * BLK] = jnp.where(take0, 0.0, pr).astype(BF16)
    for u in seqs:
        pv = jnp.dot(probs[u], vbuf[u], preferred_element_type=F32)
        for j in range(Q_PER_KV):
            o_ref[u, :, j * SLAB:(j + 1) * SLAB] = pv[j * BLK:(j + 1) * BLK, :].astype(o_ref.dtype)

    @pl.when(i == last)
    def _():
        for u in seqs:
            ko_ref[u] = _unpair(ks[u]).T
            vo_ref[u] = vs[u].T


ATTN_SEQS = 2


def _attn_prompt(p_act, lay, batch, seq, cos, sin, qg, kg, bd, sinks):
    nb = seq // BLK
    u = ATTN_SEQS if batch % ATTN_SEQS == 0 else 1
    qc, kc, vc = lay["q"][0] // Q_DIM, lay["k"][0] // KV_DIM, lay["v"][0] // KV_DIM
    p3 = p_act.reshape(batch, seq, p_act.shape[1])
    return pl.pallas_call(
        _attn_prompt_kernel,
        grid=(batch // u, nb),
        in_specs=[pl.BlockSpec((u, BLK, Q_DIM), lambda b, i: (b, i, qc)),
                  pl.BlockSpec((u, BLK, KV_DIM), lambda b, i: (b, i, kc)),
                  pl.BlockSpec((u, BLK, KV_DIM), lambda b, i: (b, i, vc)),
                  pl.BlockSpec((BLK, LANES), lambda b, i: (i, 0)),
                  pl.BlockSpec((BLK, LANES), lambda b, i: (i, 0)),
                  pl.BlockSpec((1, Q_DIM), lambda b, i: (0, 0)),
                  pl.BlockSpec((1, KV_DIM), lambda b, i: (0, 0)),
                  pl.BlockSpec((SLAB, SLAB), lambda b, i: (0, 0)),
                  pl.BlockSpec(memory_space=pltpu.SMEM)],
        out_specs=[pl.BlockSpec((u, BLK, Q_DIM), lambda b, i: (b, i, 0)),
                   pl.BlockSpec((u, KV_DIM, BLK), lambda b, i: (b, 0, 0)),
                   pl.BlockSpec((u, KV_DIM, BLK), lambda b, i: (b, 0, 0))],
        out_shape=[jax.ShapeDtypeStruct((batch, seq, Q_DIM), BF16),
                   jax.ShapeDtypeStruct((batch, KV_DIM, BLK), F32),
                   jax.ShapeDtypeStruct((batch, KV_DIM, BLK), F32)],
        scratch_shapes=[pltpu.VMEM((u, KV_HEADS * 2 * BLK, KV_DIM), BF16),
                        pltpu.VMEM((u, KV_HEADS * 2 * BLK, KV_DIM), BF16),
                        pltpu.VMEM((u, Q_PER_KV * BLK, KV_HEADS * 2 * BLK), BF16)],
        compiler_params=_cparams(("arbitrary", "arbitrary")),
        name="attn_prompt",
    )(p3, p3, p3, cos, sin, qg, kg, bd, sinks)


def _ssd_chunk(slot, z_ref, xs_ref, b_ref, c_ref, dt_ref, cw_ref, cb_ref, dtb_ref, alog_ref, dskip_ref,
               nw_ref, e3_ref, o_ref, tail_ref, xpad, tails, st, ybuf):
    xpad[0:SUBLANES, :] = tails[1 - slot]
    xpad[SUBLANES:SUBLANES + BLK, 0:D_INNER] = xs_ref[...]
    xpad[SUBLANES:SUBLANES + BLK, D_INNER:D_INNER + BC_DIM] = b_ref[...]
    xpad[SUBLANES:SUBLANES + BLK, D_INNER + BC_DIM:CONV_DIM] = c_ref[...]
    cwh = 0.5 * cw_ref[...]
    acc = 0.5 * cb_ref[...] + cwh[CONV_W - 1:CONV_W, :] * xpad[SUBLANES:SUBLANES + BLK, :]
    for t in range(1, CONV_W):
        acc = acc + cwh[CONV_W - 1 - t:CONV_W - t, :] * xpad[SUBLANES - t:SUBLANES - t + BLK, :]
    xc = acc + acc * jnp.tanh(acc)
    new_tail = xpad[BLK:BLK + SUBLANES, :]
    tail_ref[...] = new_tail
    tails[slot] = new_tail

    xs = xc[:, 0:D_INNER]
    bm = xc[:, D_INNER:D_INNER + BC_DIM].astype(BF16)
    cm = xc[:, D_INNER + BC_DIM:CONV_DIM].astype(BF16)

    e3 = e3_ref[...]
    dt = _softplus(dt_ref[...] + dtb_ref[...])
    dta = dt * (-LOG2E * jnp.exp(alog_ref[...]))
    row = lax.broadcasted_iota(jnp.int32, (BLK, BLK), 0)
    col = lax.broadcasted_iota(jnp.int32, (BLK, BLK), 1)
    causal = row >= col
    cum = jnp.dot(causal.astype(F32), dta, preferred_element_type=F32, precision=lax.Precision.HIGHEST)
    cum_t = cum.T
    ecum = jnp.exp2(cum)
    to_end = jnp.exp2(cum[BLK - 1:BLK, :] - cum) * dt
    dt_e = _expand_heads(dt, e3)
    ecum_e = _expand_heads(ecum, e3)
    to_end_e = _expand_heads(to_end, e3)
    xdt = (xs * dt_e).astype(BF16)
    xte = (xs * to_end_e).astype(BF16)
    lane = lax.broadcasted_iota(jnp.int32, (BLK, LANES), 1)
    first_head = lane < SSM_HEAD_DIM

    for g in range(SSM_GROUPS):
        gl = slice(g * GROUP_LANES, (g + 1) * GROUP_LANES)
        bg = bm[:, g * D_STATE:(g + 1) * D_STATE]
        cg = cm[:, g * D_STATE:(g + 1) * D_STATE]
        cbg = lax.dot_general(cg, bg, (((1,), (1,)), ((), ())), preferred_element_type=F32)
        st_g = st[1 - slot, :, gl]
        y_inter = jnp.dot(cg, st_g.astype(BF16), preferred_element_type=F32) * ecum_e[:, gl]
        for pr in range(HEADS_PER_GROUP // 2):
            h0 = g * HEADS_PER_GROUP + 2 * pr
            xd = xdt[:, h0 * SSM_HEAD_DIM:(h0 + 2) * SSM_HEAD_DIM]
            ys = []
            for h in (h0, h0 + 1):
                diff = cum[:, h:h + 1] - cum_t[h:h + 1, :]
                w = (jnp.exp2(jnp.where(causal, diff, -jnp.inf)) * cbg).astype(BF16)
                ys.append(jnp.dot(w, xd, preferred_element_type=F32))
            lo = pr * LANES
            ybuf[:, h0 * SSM_HEAD_DIM:(h0 + 2) * SSM_HEAD_DIM] = (
                jnp.where(first_head, ys[0], ys[1]) + y_inter[:, lo:lo + LANES])
        upd = lax.dot_general(bg, xte[:, gl], (((0,), (0,)), ((), ())), preferred_element_type=F32)
        st[slot, :, gl] = st_g * ecum_e[BLK - 1:BLK, gl] + upd

    y = ybuf[...] + dskip_ref[...] * xs
    yz = y * _silu(z_ref[...])
    ms = jnp.mean(yz * yz, axis=-1, keepdims=True)
    o_ref[...] = (yz * lax.rsqrt(ms + EPS) * nw_ref[...]).astype(o_ref.dtype)


def _ssd_prompt_kernel(z_ref, xs_ref, b_ref, c_ref, dt_ref, cw_ref, cb_ref, dtb_ref, alog_ref, dskip_ref,
                       nw_ref, e3_ref, o_ref, tail_ref, hfin_ref, xpad, tails, st, ybuf):
    i = pl.program_id(1)
    last = pl.num_programs(1) - 1

    @pl.when(i == 0)
    def _():
        tails[...] = jnp.zeros_like(tails)
        st[...] = jnp.zeros_like(st)

    slot = i % 2
    seqs = range(z_ref.shape[0])
    for u in seqs:
        _ssd_chunk(slot, z_ref.at[u], xs_ref.at[u], b_ref.at[u], c_ref.at[u], dt_ref.at[u], cw_ref, cb_ref,
                   dtb_ref, alog_ref, dskip_ref, nw_ref, e3_ref, o_ref.at[u], tail_ref.at[u], xpad.at[u],
                   tails.at[u], st.at[u], ybuf.at[u])

    @pl.when(i == last)
    def _():
        for u in seqs:
            hfin_ref[u] = st[u, slot].T


SSD_SEQS = 2


def _ssd_prompt(p_act, lay, batch, seq, cw, cb, dtb, alog, dskip, nw, e3):
    nb = seq // BLK
    u = SSD_SEQS if batch % SSD_SEQS == 0 else 1
    p3 = p_act.reshape(batch, seq, p_act.shape[1])
    zc, xc = lay["z"][0] // D_INNER, lay["xs"][0] // D_INNER
    bc, cc, dc = lay["B"][0] // BC_DIM, lay["C"][0] // BC_DIM, lay["dt"][0] // DT_PAD
    const = lambda b, i: (0, 0)
    return pl.pallas_call(
        _ssd_prompt_kernel,
        grid=(batch // u, nb),
        in_specs=[pl.BlockSpec((u, BLK, D_INNER), lambda b, i: (b, i, zc)),
                  pl.BlockSpec((u, BLK, D_INNER), lambda b, i: (b, i, xc)),
                  pl.BlockSpec((u, BLK, BC_DIM), lambda b, i: (b, i, bc)),
                  pl.BlockSpec((u, BLK, BC_DIM), lambda b, i: (b, i, cc)),
                  pl.BlockSpec((u, BLK, DT_PAD), lambda b, i: (b, i, dc)),
                  pl.BlockSpec((CONV_W, CONV_DIM), const),
                  pl.BlockSpec((1, CONV_DIM), const),
                  pl.BlockSpec((1, DT_PAD), const),
                  pl.BlockSpec((1, DT_PAD), const),
                  pl.BlockSpec((1, D_INNER), const),
                  pl.BlockSpec((1, D_INNER), const),
                  pl.BlockSpec((3 * DT_PAD, D_INNER), const)],
        out_specs=[pl.BlockSpec((u, BLK, D_INNER), lambda b, i: (b, i, 0)),
                   pl.BlockSpec((u, SUBLANES, CONV_DIM), lambda b, i: (b, 0, 0)),
                   pl.BlockSpec((u, D_INNER, D_STATE), lambda b, i: (b, 0, 0))],
        out_shape=[jax.ShapeDtypeStruct((batch, seq, D_INNER), BF16),
                   jax.ShapeDtypeStruct((batch, SUBLANES, CONV_DIM), F32),
                   jax.ShapeDtypeStruct((batch, D_INNER, D_STATE), F32)],
        scratch_shapes=[pltpu.VMEM((u, BLK + SUBLANES, CONV_DIM), F32),
                        pltpu.VMEM((u, 2, SUBLANES, CONV_DIM), F32),
                        pltpu.VMEM((u, 2, D_STATE, D_INNER), F32),
                        pltpu.VMEM((u, BLK, D_INNER), F32)],
        compiler_params=_cparams(("arbitrary", "arbitrary")),
        name="ssd_prompt",
    )(p3, p3, p3, p3, p3, cw, cb, dtb, alog, dskip, nw, e3)


def _decode_pre_kernel(q_ref, k_ref, xs_ref, b_ref, c_ref, dt_ref, cs_ref, cos_ref, sin_ref, qg_ref, kg_ref,
                       bd_ref, cw_ref, cb_ref, dtb_ref, alog_ref, e3_ref,
                       qo_ref, ko_ref, xc_ref, xdt_ref, dec_ref, cso_ref):
    cos, sin, bd = cos_ref[...], sin_ref[...], bd_ref[...]
    qo_ref[...] = _unpair(_head_norm_rope(q_ref[...], qg_ref[...], cos, sin, bd)) * (HEAD_DIM ** -0.5)
    ko_ref[...] = _unpair(_head_norm_rope(k_ref[...], kg_ref[...], cos, sin, bd))

    segs = ((xs_ref, 0, D_INNER), (b_ref, D_INNER, BC_DIM), (c_ref, D_INNER + BC_DIM, BC_DIM))
    for ref, off, width in segs:
        new = ref[...]
        acc = cb_ref[:, off:off + width] + cw_ref[CONV_W - 1:CONV_W, off:off + width] * new
        for t in range(CONV_W - 1):
            lo = t * CONV_DIM + off
            acc = acc + cw_ref[t:t + 1, off:off + width] * cs_ref[:, lo:lo + width]
        xc_ref[:, off:off + width] = _silu(acc)
        for t in range(CONV_W - 2):
            cso_ref[:, t * CONV_DIM + off:t * CONV_DIM + off + width] = (
                cs_ref[:, (t + 1) * CONV_DIM + off:(t + 1) * CONV_DIM + off + width])
        lo = (CONV_W - 2) * CONV_DIM + off
        cso_ref[:, lo:lo + width] = new

    e3 = e3_ref[...]
    dt = _softplus(dt_ref[...] + dtb_ref[...])
    decay = jnp.exp(dt * (-jnp.exp(alog_ref[...])))
    xdt_ref[...] = xc_ref[:, 0:D_INNER] * _expand_heads(dt, e3)
    dec_ref[...] = _expand_heads(decay, e3)


def _decode_pre(p_act, lay, conv_state, cos, sin, qg, kg, bd, cw, cb, dtb, alog, e3):
    m = p_act.shape[0]
    full = lambda shape: pl.BlockSpec(shape, lambda i: (0, 0))

    def col(name):
        c = lay[name][0] // lay[name][1]
        return pl.BlockSpec((m, lay[name][1]), lambda i: (0, c))

    cs_w = (CONV_W - 1) * CONV_DIM
    return pl.pallas_call(
        _decode_pre_kernel,
        grid=(1,),
        in_specs=[col("q"), col("k"), col("xs"), col("B"), col("C"), col("dt"),
                  full((m, cs_w)), full((1, LANES)), full((1, LANES)), full((1, Q_DIM)), full((1, KV_DIM)),
                  full((SLAB, SLAB)), full((CONV_W, CONV_DIM)), full((1, CONV_DIM)), full((1, DT_PAD)),
                  full((1, DT_PAD)), full((3 * DT_PAD, D_INNER))],
        out_specs=[full((m, Q_DIM)), full((m, KV_DIM)), full((m, CONV_DIM)), full((m, D_INNER)),
                   full((m, D_INNER)), full((m, cs_w))],
        out_shape=[jax.ShapeDtypeStruct((m, Q_DIM), F32), jax.ShapeDtypeStruct((m, KV_DIM), F32),
                   jax.ShapeDtypeStruct((m, CONV_DIM), F32), jax.ShapeDtypeStruct((m, D_INNER), F32),
                   jax.ShapeDtypeStruct((m, D_INNER), F32), jax.ShapeDtypeStruct((m, cs_w), F32)],
        compiler_params=_cparams(("arbitrary",)),
        name="decode_pre",
    )(p_act, p_act, p_act, p_act, p_act, p_act, conv_state, cos, sin, qg, kg, bd, cw, cb, dtb, alog, e3)


DEC_ROWS = 8


def _attn_decode_kernel(q_ref, k_ref, v_ref, ck_ref, cv_ref, sink_ref, o_ref, cko_ref, cvo_ref):
    w = ck_ref.shape[2]
    nq = N_HEADS
    r = lax.broadcasted_iota(jnp.int32, (nq, SLAB), 0)
    grp = lax.broadcasted_iota(jnp.int32, (nq, SLAB), 1) // HEAD_DIM
    own = grp == (r % KV_HEADS)
    in_window = lax.broadcasted_iota(jnp.int32, (nq, w), 1) > (w - WINDOW - 1)
    newest = lax.broadcasted_iota(jnp.int32, (KV_DIM, w), 1) == w - 1
    sink = sink_ref[...][:, 0:1]
    pad = jnp.zeros((DEC_ROWS, KV_DIM), BF16)
    k_parts = jnp.concatenate(list(_split3(k_ref[...])) + [pad], axis=0)
    v_parts = jnp.concatenate(list(_split3(v_ref[...])) + [pad], axis=0)
    part_row = lax.broadcasted_iota(jnp.int32, (4 * DEC_ROWS, w), 0) % DEC_ROWS
    tdims = (((0,), (0,)), ((), ()))
    for bl in range(DEC_ROWS):
        qrow = q_ref[bl:bl + 1, :]
        qm = jnp.zeros((nq, SLAB), F32)
        for j in range(Q_PER_KV):
            slab = jnp.broadcast_to(qrow[:, j * SLAB:(j + 1) * SLAB], (nq, SLAB))
            qm = jnp.where((r // KV_HEADS) == j, slab, qm)
        qm = jnp.where(own, qm, 0.0).astype(BF16)
        pick = jnp.where(part_row == bl, 1.0, 0.0).astype(BF16)
        k_col = lax.dot_general(k_parts, pick, tdims, preferred_element_type=F32)
        v_col = lax.dot_general(v_parts, pick, tdims, preferred_element_type=F32)
        keys = jnp.where(newest, k_col, pltpu.roll(ck_ref[bl], w - 1, 1))
        vals = jnp.where(newest, v_col, pltpu.roll(cv_ref[bl], w - 1, 1))
        cko_ref[bl] = keys
        cvo_ref[bl] = vals
        s = jnp.dot(qm, keys.astype(BF16), preferred_element_type=F32)
        s = jnp.where(in_window, s, -jnp.inf)
        mx = jnp.maximum(jnp.max(s, axis=-1, keepdims=True), sink)
        e = jnp.exp(s - mx)
        den = jnp.sum(e, axis=-1, keepdims=True) + jnp.exp(sink - mx)
        pv = lax.dot_general(e.astype(BF16), vals.astype(BF16), (((1,), (1,)), ((), ())),
                             preferred_element_type=F32)
        pv = jnp.where(own, pv / den, 0.0)
        for j in range(Q_PER_KV):
            o_ref[bl:bl + 1, j * SLAB:(j + 1) * SLAB] = jnp.sum(
                pv[j * KV_HEADS:(j + 1) * KV_HEADS, :], axis=0, keepdims=True)


def _attn_decode(q, k, p_act, lay, cache_k, cache_v, sink_rows):
    m = q.shape[0]
    w = cache_k.shape[2]
    vc = lay["v"][0] // KV_DIM
    cache_spec = pl.BlockSpec((DEC_ROWS, KV_DIM, w), lambda i: (i, 0, 0))
    return pl.pallas_call(
        _attn_decode_kernel,
        grid=(m // DEC_ROWS,),
        in_specs=[pl.BlockSpec((DEC_ROWS, Q_DIM), lambda i: (i, 0)),
                  pl.BlockSpec((DEC_ROWS, KV_DIM), lambda i: (i, 0)),
                  pl.BlockSpec((DEC_ROWS, KV_DIM), lambda i: (i, vc)),
                  cache_spec, cache_spec,
                  pl.BlockSpec((N_HEADS, LANES), lambda i: (0, 0))],
        out_specs=[pl.BlockSpec((DEC_ROWS, Q_DIM), lambda i: (i, 0)), cache_spec, cache_spec],
        out_shape=[jax.ShapeDtypeStruct((m, Q_DIM), F32),
                   jax.ShapeDtypeStruct((m, KV_DIM, w), F32),
                   jax.ShapeDtypeStruct((m, KV_DIM, w), F32)],
        compiler_params=_cparams(("arbitrary",)),
        name="attn_decode",
    )(q, k, p_act, cache_k, cache_v, sink_rows)


MM_ROWS = 16


def _ssm_decode_kernel(st_ref, xdt_ref, dec_ref, xc_ref, z_ref, dskip_ref, nw_ref, sto_ref, o_ref):
    rows = st_ref.shape[0]
    r = lax.broadcasted_iota(jnp.int32, (MM_ROWS, D_INNER), 0)
    grp = lax.broadcasted_iota(jnp.int32, (MM_ROWS, D_INNER), 1) // GROUP_LANES
    rr = lax.broadcasted_iota(jnp.int32, (MM_ROWS, D_STATE), 0)
    ones_rows = jnp.where((rr >= SSM_GROUPS) & (rr < SSM_GROUPS + 3), 1.0, 0.0)
    bc = lambda a: jnp.broadcast_to(a, (MM_ROWS, D_INNER))
    new_states = []
    for u in range(rows):
        h = st_ref[u]
        xdt = xdt_ref[u:u + 1, :]
        xc = xc_ref[u:u + 1, :]
        hi, mid, lo = _split3(dec_ref[u:u + 1, :])
        lhs_t = jnp.where(r == grp, bc(xdt), 0.0)
        for t, piece in enumerate((hi, mid, lo)):
            lhs_t = jnp.where(r == SSM_GROUPS + t, bc(piece.astype(F32)), lhs_t)
        lhs_t = lhs_t.astype(BF16)
        b_rows = jnp.zeros((MM_ROWS, D_STATE), F32)
        for g in range(SSM_GROUPS):
            b_g = xc[:, D_INNER + g * D_STATE:D_INNER + (g + 1) * D_STATE]
            b_rows = jnp.where(rr == g, jnp.broadcast_to(b_g, (MM_ROWS, D_STATE)), b_rows)
        rhs = jnp.concatenate([b_rows, ones_rows], axis=1).astype(BF16)
        both = lax.dot_general(lhs_t, rhs, (((0,), (0,)), ((), ())), preferred_element_type=F32)
        h_new = both[:, D_STATE:] * h + both[:, :D_STATE]
        sto_ref[u] = h_new
        new_states.append(h_new.astype(BF16))
    states = jnp.concatenate(new_states, axis=1)
    urow = lax.broadcasted_iota(jnp.int32, (rows, rows * D_STATE), 0)
    ublk = lax.broadcasted_iota(jnp.int32, (rows, rows * D_STATE), 1) // D_STATE
    c_sel = []
    for g in range(SSM_GROUPS):
        c_g = xc_ref[:, D_INNER + BC_DIM + g * D_STATE:D_INNER + BC_DIM + (g + 1) * D_STATE]
        c_sel.append(jnp.where(urow == ublk, jnp.concatenate([c_g] * rows, axis=1), 0.0))
    c_all = jnp.concatenate(c_sel, axis=0).astype(BF16)
    yt = lax.dot_general(states, c_all, (((1,), (1,)), ((), ())), preferred_element_type=F32).T
    lane_grp = lax.broadcasted_iota(jnp.int32, (rows, D_INNER), 1) // GROUP_LANES
    y = jnp.zeros((rows, D_INNER), F32)
    for g in range(SSM_GROUPS):
        y = jnp.where(lane_grp == g, yt[g * rows:(g + 1) * rows, :], y)
    y = y + dskip_ref[...] * xc_ref[:, 0:D_INNER]
    yz = y * _silu(z_ref[...])
    ms = jnp.mean(yz * yz, axis=-1, keepdims=True)
    o_ref[...] = yz * lax.rsqrt(ms + EPS) * nw_ref[...]


SSM_DEC_ROWS = SUBLANES


def _ssm_decode(state, xdt, dec, xc, p_act, lay, dskip, nw):
    m = state.shape[0]
    u = SSM_DEC_ROWS
    assert m % u == 0
    zc = lay["z"][0] // D_INNER
    row = lambda width: pl.BlockSpec((u, width), lambda b: (b, 0))
    return pl.pallas_call(
        _ssm_decode_kernel,
        grid=(m // u,),
        in_specs=[pl.BlockSpec((u, D_INNER, D_STATE), lambda b: (b, 0, 0)),
                  row(D_INNER), row(D_INNER), row(CONV_DIM),
                  pl.BlockSpec((u, D_INNER), lambda b: (b, zc)),
                  pl.BlockSpec((1, D_INNER), lambda b: (0, 0)),
                  pl.BlockSpec((1, D_INNER), lambda b: (0, 0))],
        out_specs=[pl.BlockSpec((u, D_INNER, D_STATE), lambda b: (b, 0, 0)), row(D_INNER)],
        out_shape=[jax.ShapeDtypeStruct((m, D_INNER, D_STATE), F32),
                   jax.ShapeDtypeStruct((m, D_INNER), F32)],
        compiler_params=_cparams(("arbitrary",)),
        name="ssm_decode",
    )(state, xdt, dec, xc, p_act, dskip, nw)


def _merge_mlp_kernel(x_ref, oa_ref, os_ref, ga_ref, gs_ref, wa_ref, ws_ref, wo_ref, g_ref, wu_ref, wd_ref,
                      o_ref, *, ff_tile):
    a = jnp.dot(oa_ref[...].astype(BF16), wa_ref[...], preferred_element_type=F32)
    s = jnp.dot(os_ref[...].astype(BF16), ws_ref[...], preferred_element_type=F32)
    mixed = _sigmoid(ga_ref[...]) * a + _sigmoid(gs_ref[...]) * s
    x = x_ref[...] + jnp.dot(mixed.astype(BF16), wo_ref[...], preferred_element_type=F32)
    ms = jnp.mean(x * x, axis=-1, keepdims=True)
    xn = (x * lax.rsqrt(ms + EPS) * g_ref[...]).astype(BF16)
    acc = x
    for c in range(wu_ref.shape[1] // ff_tile):
        u = jnp.dot(xn, wu_ref[:, c * ff_tile:(c + 1) * ff_tile], preferred_element_type=F32)
        act = jnp.square(jnp.maximum(u, 0.0)).astype(BF16)
        acc = acc + jnp.dot(act, wd_ref[c * ff_tile:(c + 1) * ff_tile, :], preferred_element_type=F32)
    o_ref[...] = acc


def _merge_mlp_stream_kernel(x_ref, oa_ref, os_ref, ga_ref, gs_ref, wa_ref, ws_ref, wo_ref, g_ref, wu_ref, wd_ref,
                             o_ref, xn_ref):
    @pl.when(pl.program_id(0) == 0)
    def _():
        a = jnp.dot(oa_ref[...].astype(BF16), wa_ref[...], preferred_element_type=F32)
        s = jnp.dot(os_ref[...].astype(BF16), ws_ref[...], preferred_element_type=F32)
        mixed = _sigmoid(ga_ref[...]) * a + _sigmoid(gs_ref[...]) * s
        x = x_ref[...] + jnp.dot(mixed.astype(BF16), wo_ref[...], preferred_element_type=F32)
        ms = jnp.mean(x * x, axis=-1, keepdims=True)
        xn_ref[...] = (x * lax.rsqrt(ms + EPS) * g_ref[...]).astype(BF16)
        o_ref[...] = x

    u = jnp.dot(xn_ref[...], wu_ref[...], preferred_element_type=F32)
    act = jnp.square(jnp.maximum(u, 0.0)).astype(BF16)
    o_ref[...] += jnp.dot(act, wd_ref[...], preferred_element_type=F32)


def _merge_mlp(x, o_attn, o_ssm, p_act, lay, wa, ws, wo, g, wu, wd):
    m, d = x.shape
    ff = wu.shape[1]
    tm = _row_tile(m, 512)
    gac, gsc = lay["ga"][0] // d, lay["gs"][0] // d
    if m == tm:
        ft = min(1024, ff)
        const = lambda c: (0, 0)
        return pl.pallas_call(
            _merge_mlp_stream_kernel,
            grid=(ff // ft,),
            in_specs=[pl.BlockSpec((m, d), const), pl.BlockSpec((m, Q_DIM), const),
                      pl.BlockSpec((m, D_INNER), const),
                      pl.BlockSpec((m, d), lambda c: (0, gac)), pl.BlockSpec((m, d), lambda c: (0, gsc)),
                      pl.BlockSpec((Q_DIM, d), const), pl.BlockSpec((D_INNER, d), const),
                      pl.BlockSpec((d, d), const), pl.BlockSpec((1, d), const),
                      pl.BlockSpec((d, ft), lambda c: (0, c)), pl.BlockSpec((ft, d), lambda c: (c, 0))],
            out_specs=pl.BlockSpec((m, d), const),
            out_shape=jax.ShapeDtypeStruct((m, d), F32),
            scratch_shapes=[pltpu.VMEM((m, d), BF16)],
            compiler_params=_cparams(("arbitrary",)),
            name="merge_mlp_stream",
        )(x, o_attn, o_ssm, p_act, p_act, wa, ws, wo, g, wu, wd)
    weight = lambda shape: pl.BlockSpec(shape, lambda i: (0, 0), pipeline_mode=pl.Buffered(1))
    return pl.pallas_call(
        functools.partial(_merge_mlp_kernel, ff_tile=min(1024, ff)),
        grid=(m // tm,),
        in_specs=[pl.BlockSpec((tm, d), lambda i: (i, 0)),
                  pl.BlockSpec((tm, Q_DIM), lambda i: (i, 0)),
                  pl.BlockSpec((tm, D_INNER), lambda i: (i, 0)),
                  pl.BlockSpec((tm, d), lambda i: (i, gac)),
                  pl.BlockSpec((tm, d), lambda i: (i, gsc)),
                  weight((Q_DIM, d)), weight((D_INNER, d)), weight((d, d)),
                  weight((1, d)), weight((d, ff)), weight((ff, d))],
        out_specs=pl.BlockSpec((tm, d), lambda i: (i, 0)),
        out_shape=jax.ShapeDtypeStruct((m, d), F32),
        compiler_params=_cparams(("arbitrary",), WEIGHT_RESIDENT_VMEM_LIMIT),
        name="merge_mlp",
    )(x, o_attn, o_ssm, p_act, p_act, wa, ws, wo, g, wu, wd)


def _rope_tables(pos):
    half = HEAD_DIM // 2
    inv = ROPE_THETA ** (-jnp.arange(half, dtype=F32) / half)
    ang = pos.astype(F32)[:, None] * inv[None, :]
    cos, sin = jnp.cos(ang), jnp.sin(ang)
    return (jnp.concatenate([cos, cos, cos, cos], axis=1),
            jnp.concatenate([-sin, -sin, sin, sin], axis=1))


def _constants():
    heads = np.arange(DT_PAD)[:, None]
    cols = np.arange(D_INNER)[None, :] // SSM_HEAD_DIM
    e = (heads == cols).astype(np.float32)
    e3 = jnp.asarray(np.concatenate([e, e, e], axis=0), BF16)
    a = np.arange(SLAB)
    grp = 2 * (a // LANES) + (a % HEAD_DIM) // (HEAD_DIM // 2)
    bd = jnp.asarray((grp[:, None] == grp[None, :]).astype(np.float32) / HEAD_DIM, BF16)
    return e3, bd


_HALF = HEAD_DIM // 2


def _pair_q(a):
    lead = a.shape[:-1]
    n = len(lead)
    a = a.reshape(lead + (2, 2, Q_PER_KV, 2, _HALF))
    return a.transpose(tuple(range(n)) + (n + 2, n, n + 3, n + 1, n + 4)).reshape(lead + (Q_DIM,))


def _pair_k(a):
    lead = a.shape[:-1]
    n = len(lead)
    a = a.reshape(lead + (2, 2, 2, _HALF))
    return a.transpose(tuple(range(n)) + (n, n + 2, n + 1, n + 3)).reshape(lead + (KV_DIM,))


def _prep_layer(d_model, norm_mix, w_in, q_norm, k_norm, attn_sinks, conv_w, conv_b, dt_bias, a_log, d_skip,
                ssm_norm, w_attn_o, w_ssm_o, w_out, norm_mlp, w_up, w_down):
    assert KV_HEADS == 4 and Q_PER_KV == 4
    e3, bd = _constants()
    o_q, o_k, o_v, o_z, o_xbc, o_dt = (0, Q_DIM, Q_DIM + KV_DIM, Q_DIM + 2 * KV_DIM,
                                       Q_DIM + 2 * KV_DIM + D_INNER, Q_DIM + 2 * KV_DIM + D_INNER + CONV_DIM)
    o_g = o_dt + SSM_HEADS
    cut = lambda lo, width: w_in[:, lo:lo + width]
    pieces = {"z": cut(o_z, D_INNER), "xs": cut(o_xbc, D_INNER), "ga": cut(o_g, d_model),
              "gs": cut(o_g + d_model, d_model), "q": _pair_q(cut(o_q, Q_DIM)),
              "B": cut(o_xbc + D_INNER, BC_DIM), "C": cut(o_xbc + D_INNER + BC_DIM, BC_DIM),
              "k": _pair_k(cut(o_k, KV_DIM)), "v": cut(o_v, KV_DIM),
              "dt": jnp.pad(cut(o_dt, SSM_HEADS), ((0, 0), (0, DT_PAD - SSM_HEADS)))}
    lay, _ = _layout(d_model)
    w_p = jnp.concatenate([pieces[name] for name in lay], axis=1).astype(BF16)
    pad_heads = lambda a: jnp.pad(a, (0, DT_PAD - SSM_HEADS))[None, :]
    sink_p = attn_sinks.reshape(KV_HEADS, Q_PER_KV).T.reshape(N_HEADS)
    wa = w_attn_o.reshape(KV_HEADS, Q_PER_KV, HEAD_DIM, d_model).transpose(1, 0, 2, 3).reshape(Q_DIM, d_model)
    return dict(
        lay=lay, e3=e3, bd=bd, w_p=w_p,
        norm_mix=norm_mix[None, :], norm_mlp=norm_mlp[None, :],
        qg=_pair_q(jnp.tile(q_norm, N_HEADS))[None, :], kg=_pair_k(jnp.tile(k_norm, KV_HEADS))[None, :],
        sink_p=sink_p, sink_rows=jnp.broadcast_to(sink_p[:, None], (N_HEADS, LANES)),
        cw=conv_w, cb=conv_b[None, :], dtb=pad_heads(dt_bias), alog=pad_heads(a_log),
        dskip=jnp.repeat(d_skip, SSM_HEAD_DIM)[None, :], nw=ssm_norm[None, :],
        wa=wa.astype(BF16), ws=w_ssm_o.astype(BF16), wo=w_out.astype(BF16),
        wu=w_up.astype(BF16), wd=w_down.astype(BF16))


def _prompt_layer(x, lw):
    batch, seq, d = x.shape
    assert seq % BLK == 0 and seq >= WINDOW
    x2 = x.reshape(batch * seq, d)
    p_act = _in_proj(x2, lw["norm_mix"], lw["w_p"])
    cos, sin = _rope_tables(jnp.arange(seq))
    o_attn, k_last, v_last = _attn_prompt(p_act, lw["lay"], batch, seq, cos, sin, lw["qg"], lw["kg"],
                                          lw["bd"], lw["sink_p"])
    o_ssm, tail, h_fin = _ssd_prompt(p_act, lw["lay"], batch, seq, lw["cw"], lw["cb"], lw["dtb"], lw["alog"],
                                     lw["dskip"], lw["nw"], lw["e3"])
    y = _merge_mlp(x2, o_attn.reshape(batch * seq, Q_DIM), o_ssm.reshape(batch * seq, D_INNER), p_act,
                   lw["lay"], lw["wa"], lw["ws"], lw["wo"],
                   lw["norm_mlp"], lw["wu"], lw["wd"])
    return (y.reshape(batch, seq, d),
            k_last.reshape(batch, KV_HEADS, HEAD_DIM, BLK).transpose(0, 3, 1, 2),
            v_last.reshape(batch, KV_HEADS, HEAD_DIM, BLK).transpose(0, 3, 1, 2),
            tail[:, SUBLANES - (CONV_W - 1):, :],
            h_fin.reshape(batch, SSM_HEADS, SSM_HEAD_DIM, D_STATE))


def _decode_layer(x, cache_k, cache_v, conv_state, ssm_state, lw):
    m, t, d = x.shape
    w = cache_k.shape[1]
    assert t == 1 and w == WINDOW and m % DEC_ROWS == 0
    x2 = x.reshape(m, d)
    p_act = _in_proj(x2, lw["norm_mix"], lw["w_p"])
    cos, sin = _rope_tables(PAST_LEN + jnp.arange(1))
    q, k, xc, xdt, dec, conv_new = _decode_pre(
        p_act, lw["lay"], conv_state.reshape(m, (CONV_W - 1) * CONV_DIM), cos, sin, lw["qg"], lw["kg"],
        lw["bd"], lw["cw"], lw["cb"], lw["dtb"], lw["alog"], lw["e3"])
    ck = jnp.transpose(cache_k, (0, 2, 3, 1)).reshape(m, KV_DIM, w)
    cv = jnp.transpose(cache_v, (0, 2, 3, 1)).reshape(m, KV_DIM, w)
    o_attn, ck_new, cv_new = _attn_decode(q, k, p_act, lw["lay"], ck, cv, lw["sink_rows"])
    h_new, o_ssm = _ssm_decode(ssm_state.reshape(m, D_INNER, D_STATE), xdt, dec, xc, p_act, lw["lay"],
                               lw["dskip"], lw["nw"])
    y = _merge_mlp(x2, o_attn, o_ssm, p_act, lw["lay"], lw["wa"], lw["ws"], lw["wo"],
                   lw["norm_mlp"], lw["wu"], lw["wd"])
    unview = lambda c: jnp.transpose(c.reshape(m, KV_HEADS, HEAD_DIM, w), (0, 3, 1, 2))
    return (y.reshape(m, 1, d), unview(ck_new), unview(cv_new), conv_new.reshape(m, CONV_W - 1, CONV_DIM),
            h_new.reshape(m, SSM_HEADS, SSM_HEAD_DIM, D_STATE))


def kernel(x_prompt, x_sample, cache_k, cache_v, state_conv, state_ssm, norm_mix, w_in, q_norm, k_norm,
           attn_sinks, conv_w, conv_b, dt_bias, a_log, d_skip, ssm_norm, w_attn_o, w_ssm_o, w_out,
           norm_mlp, w_up, w_down):
    depth = w_in.shape[0]
    d_model = x_prompt.shape[-1]
    yp, ys = x_prompt, x_sample
    cols = [[] for _ in range(8)]
    for l in range(depth):
        lw = _prep_layer(d_model, norm_mix[l], w_in[l], q_norm[l], k_norm[l], attn_sinks[l], conv_w[l],
                         conv_b[l], dt_bias[l], a_log[l], d_skip[l], ssm_norm[l], w_attn_o[l], w_ssm_o[l],
                         w_out[l], norm_mlp[l], w_up[l], w_down[l])
        yp, kp, vp, cp, hp = _prompt_layer(yp, lw)
        ys, ks, vs, cs, hs = _decode_layer(ys, cache_k[l], cache_v[l], state_conv[l], state_ssm[l], lw)
        for lst, val in zip(cols, (kp, vp, cp, hp, ks, vs, cs, hs)):
            lst.append(val)
    return (yp, ys) + tuple(jnp.stack(c) for c in cols)
```

```python
import functools

import numpy as np
import jax
import jax.numpy as jnp
from jax import lax
from jax.experimental import pallas as pl
from jax.experimental.pallas import tpu as pltpu

F32 = jnp.float32
BF16 = jnp.bfloat16

N_HEADS = 16
KV_HEADS = 4
HEAD_DIM = 64
Q_PER_KV = N_HEADS // KV_HEADS
WINDOW = 128
ROPE_THETA = 10000.0
Q_DIM = N_HEADS * HEAD_DIM
KV_DIM = KV_HEADS * HEAD_DIM
SSM_HEAD_DIM = 64
SSM_HEADS = 32
D_INNER = SSM_HEADS * SSM_HEAD_DIM
SSM_GROUPS = 4
HEADS_PER_GROUP = SSM_HEADS // SSM_GROUPS
D_STATE = 128
BC_DIM = SSM_GROUPS * D_STATE
CONV_W = 4
CONV_DIM = D_INNER + 2 * BC_DIM
SSD_CHUNK = 128
EPS = 1e-6
PAST_LEN = 8192
LOG2E = 1.4426950408889634

LANES = 128
SUBLANES = 8
VMEM_LIMIT = 48 * 1024 * 1024
WEIGHT_RESIDENT_VMEM_LIMIT = 56 * 1024 * 1024

BLK = 128
assert WINDOW == BLK and SSD_CHUNK == BLK
SLAB = KV_HEADS * HEAD_DIM
GROUP_LANES = HEADS_PER_GROUP * SSM_HEAD_DIM
DT_PAD = LANES


def _layout(d_model):
    segs = [("z", D_INNER), ("xs", D_INNER), ("ga", d_model), ("gs", d_model), ("q", Q_DIM),
            ("B", BC_DIM), ("C", BC_DIM), ("k", KV_DIM), ("v", KV_DIM), ("dt", DT_PAD)]
    lay, off = {}, 0
    for name, width in segs:
        assert off % width == 0, (name, off, width)
        lay[name] = (off, width)
        off += width
    return lay, off


def _col_tile(n, cap=3072):
    units = n // LANES
    best = 1
    for d in range(1, units + 1):
        if units % d == 0 and d * LANES <= cap:
            best = d
    return best * LANES


def _row_tile(m, cap):
    assert m % SUBLANES == 0
    best = SUBLANES
    for t in range(SUBLANES, min(m, cap) + 1, SUBLANES):
        if m % t == 0:
            best = t
    return best


def _cparams(sem, vmem_limit=VMEM_LIMIT):
    return pltpu.CompilerParams(dimension_semantics=sem, vmem_limit_bytes=vmem_limit)


def _sigmoid(x):
    return 0.5 + 0.5 * jnp.tanh(0.5 * x)


def _silu(x):
    h = 0.5 * x
    return h + h * jnp.tanh(h)


def _softplus(x):
    return jnp.maximum(x, 0.0) + jnp.log1p(jnp.exp(-jnp.abs(x)))


def _split3(a):
    hi = a.astype(BF16)
    r1 = a - hi.astype(F32)
    mid = r1.astype(BF16)
    lo = (r1 - mid.astype(F32)).astype(BF16)
    return hi, mid, lo


def _expand_heads(a, e3):
    hi, mid, lo = _split3(a)
    return jnp.dot(jnp.concatenate([hi, mid, lo], axis=1), e3, preferred_element_type=F32)


def _head_ms(x, bd):
    sq = x * x
    hi = sq.astype(BF16)
    lo = (sq - hi.astype(F32)).astype(BF16)
    outs = []
    for s in range(x.shape[1] // SLAB):
        sl = slice(s * SLAB, (s + 1) * SLAB)
        outs.append(jnp.dot(hi[:, sl], bd, preferred_element_type=F32)
                    + jnp.dot(lo[:, sl], bd, preferred_element_type=F32))
    return outs[0] if len(outs) == 1 else jnp.concatenate(outs, axis=1)


def _head_norm_rope(x, g, cos, sin, bd):
    xn = x * lax.rsqrt(_head_ms(x, bd) + EPS) * g
    tiles = []
    for t in range(x.shape[1] // LANES):
        xt = xn[:, t * LANES:(t + 1) * LANES]
        tiles.append(xt * cos + pltpu.roll(xt, LANES // 2, 1) * sin)
    return tiles[0] if len(tiles) == 1 else jnp.concatenate(tiles, axis=1)


def _unpair(x):
    q = HEAD_DIM // 2
    lane = lax.broadcasted_iota(jnp.int32, (x.shape[0], LANES), 1)
    tiles = []
    for t in range(x.shape[1] // LANES):
        xt = x[:, t * LANES:(t + 1) * LANES]
        nat = jnp.where((lane >= q) & (lane < 2 * q), pltpu.roll(xt, LANES - q, 1), xt)
        tiles.append(jnp.where((lane >= 2 * q) & (lane < 3 * q), pltpu.roll(xt, q, 1), nat))
    return tiles[0] if len(tiles) == 1 else jnp.concatenate(tiles, axis=1)


def _in_proj_kernel(x_ref, g_ref, w_ref, o_ref, xn_ref, *, tn):
    j = pl.program_id(1)

    @pl.when(j == 0)
    def _():
        x = x_ref[...]
        ms = jnp.mean(x * x, axis=-1, keepdims=True)
        xn_ref[...] = (x * lax.rsqrt(ms + EPS) * g_ref[...]).astype(BF16)

    for c in range(w_ref.shape[1] // tn):
        @pl.when(j == c)
        def _():
            o_ref[...] = jnp.dot(xn_ref[...], w_ref[:, c * tn:(c + 1) * tn], preferred_element_type=F32)


def _in_proj_stream_kernel(x_ref, g_ref, w_ref, o_ref, xn_ref):
    @pl.when(pl.program_id(0) == 0)
    def _():
        x = x_ref[...]
        ms = jnp.mean(x * x, axis=-1, keepdims=True)
        xn_ref[...] = (x * lax.rsqrt(ms + EPS) * g_ref[...]).astype(BF16)

    o_ref[...] = jnp.dot(xn_ref[...], w_ref[...], preferred_element_type=F32)


def _in_proj(x, g, w):
    m, k = x.shape
    n = w.shape[1]
    tm = _row_tile(m, 1024)
    tn = _col_tile(n)
    if m == tm:
        return pl.pallas_call(
            _in_proj_stream_kernel,
            grid=(n // tn,),
            in_specs=[pl.BlockSpec((m, k), lambda j: (0, 0)),
                      pl.BlockSpec((1, k), lambda j: (0, 0)),
                      pl.BlockSpec((k, tn), lambda j: (0, j))],
            out_specs=pl.BlockSpec((m, tn), lambda j: (0, j)),
            out_shape=jax.ShapeDtypeStruct((m, n), F32),
            scratch_shapes=[pltpu.VMEM((m, k), BF16)],
            compiler_params=_cparams(("arbitrary",)),
            name="in_proj_stream",
        )(x, g, w)
    return pl.pallas_call(
        functools.partial(_in_proj_kernel, tn=tn),
        grid=(m // tm, n // tn),
        in_specs=[pl.BlockSpec((tm, k), lambda i, j: (i, 0)),
                  pl.BlockSpec((1, k), lambda i, j: (0, 0)),
                  pl.BlockSpec((k, n), lambda i, j: (0, 0), pipeline_mode=pl.Buffered(1))],
        out_specs=pl.BlockSpec((tm, tn), lambda i, j: (i, j)),
        out_shape=jax.ShapeDtypeStruct((m, n), F32),
        scratch_shapes=[pltpu.VMEM((tm, k), BF16)],
        compiler_params=_cparams(("arbitrary", "arbitrary"), WEIGHT_RESIDENT_VMEM_LIMIT),
        name="in_proj",
    )(x, g, w)


def _attn_prompt_kernel(q_ref, k_ref, v_ref, cos_ref, sin_ref, qg_ref, kg_ref, bd_ref, sink_ref,
                        o_ref, ko_ref, vo_ref, kbuf, vbuf, probs):
    i = pl.program_id(1)
    last = pl.num_programs(1) - 1

    @pl.when(i == 0)
    def _():
        kbuf[...] = jnp.zeros_like(kbuf)
        vbuf[...] = jnp.zeros_like(vbuf)

    cos, sin, bd = cos_ref[...], sin_ref[...], bd_ref[...]
    slot = i % 2
    lane = lax.broadcasted_iota(jnp.int32, (BLK, SLAB), 1)
    grp_k = 2 * (lane // LANES) + (lane % HEAD_DIM) // (HEAD_DIM // 2)
    grp_v = lane // HEAD_DIM
    r = lax.broadcasted_iota(jnp.int32, (BLK, BLK), 0)
    key = lax.broadcasted_iota(jnp.int32, (BLK, BLK), 1)
    causal = key <= r
    take0 = jnp.where(slot == 0, causal.astype(jnp.int32), 1 - causal.astype(jnp.int32)) > 0
    bias = jnp.where(causal, 0.0, jnp.where(i > 0, 0.0, -jnp.inf))

    seqs = range(q_ref.shape[0])
    ks, vs, scores = [], [], []
    for u in seqs:
        q = _head_norm_rope(q_ref[u], qg_ref[...], cos, sin, bd) * (HEAD_DIM ** -0.5 * LOG2E)
        k = _head_norm_rope(k_ref[u], kg_ref[...], cos, sin, bd)
        v = v_ref[u]
        ks.append(k)
        vs.append(v)
        for p in range(KV_HEADS):
            rows = pl.ds(pl.multiple_of(p * 2 * BLK + slot * BLK, BLK), BLK)
            kbuf[u, rows, :] = jnp.where(grp_k == p, k, 0.0).astype(BF16)
            vbuf[u, rows, :] = jnp.where(grp_v == p, v, 0.0).astype(BF16)
        q_stack = jnp.concatenate([q[:, j * SLAB:(j + 1) * SLAB] for j in range(Q_PER_KV)],
                                  axis=0).astype(BF16)
        scores.append(lax.dot_general(q_stack, kbuf[u], (((1,), (1,)), ((), ())),
                                      preferred_element_type=F32))
    for u in seqs:
        for j in range(Q_PER_KV):
            for p in range(KV_HEADS):
                rows = slice(j * BLK, (j + 1) * BLK)
                c0 = p * 2 * BLK
                s = jnp.where(take0, scores[u][rows, c0:c0 + BLK], scores[u][rows, c0 + BLK:c0 + 2 * BLK]) + bias
                sink = sink_ref[j * KV_HEADS + p] * LOG2E
                mx = jnp.maximum(jnp.max(s, axis=-1, keepdims=True), sink)
                e = jnp.exp2(s - mx)
                den = jnp.sum(e, axis=-1, keepdims=True) + jnp.exp2(sink - mx)
                pr = e / den
                probs[u, rows, c0:c0 + BLK] = jnp.where(take0, pr, 0.0).astype(BF16)
                probs[u, rows, c0 + BLK:c0 + 2 * BLK] = jnp.where(take0, 0.0, pr).astype(BF16)
    for u in seqs:
        pv = jnp.dot(probs[u], vbuf[u], preferred_element_type=F32)
        for j in range(Q_PER_KV):
            o_ref[u, :, j * SLAB:(j + 1) * SLAB] = pv[j * BLK:(j + 1) * BLK, :].astype(o_ref.dtype)

    @pl.when(i == last)
    def _():
        for u in seqs:
            ko_ref[u] = _unpair(ks[u]).T
            vo_ref[u] = vs[u].T


ATTN_SEQS = 2


def _attn_prompt(p_act, lay, batch, seq, cos, sin, qg, kg, bd, sinks):
    nb = seq // BLK
    u = ATTN_SEQS if batch % ATTN_SEQS == 0 else 1
    qc, kc, vc = lay["q"][0] // Q_DIM, lay["k"][0] // KV_DIM, lay["v"][0] // KV_DIM
    p3 = p_act.reshape(batch, seq, p_act.shape[1])
    return pl.pallas_call(
        _attn_prompt_kernel,
        grid=(batch // u, nb),
        in_specs=[pl.BlockSpec((u, BLK, Q_DIM), lambda b, i: (b, i, qc)),
                  pl.BlockSpec((u, BLK, KV_DIM), lambda b, i: (b, i, kc)),
                  pl.BlockSpec((u, BLK, KV_DIM), lambda b, i: (b, i, vc)),
                  pl.BlockSpec((BLK, LANES), lambda b, i: (i, 0)),
                  pl.BlockSpec((BLK, LANES), lambda b, i: (i, 0)),
                  pl.BlockSpec((1, Q_DIM), lambda b, i: (0, 0)),
                  pl.BlockSpec((1, KV_DIM), lambda b, i: (0, 0)),
                  pl.BlockSpec((SLAB, SLAB), lambda b, i: (0, 0)),
                  pl.BlockSpec(memory_space=pltpu.SMEM)],
        out_specs=[pl.BlockSpec((u, BLK, Q_DIM), lambda b, i: (b, i, 0)),
                   pl.BlockSpec((u, KV_DIM, BLK), lambda b, i: (b, 0, 0)),
                   pl.BlockSpec((u, KV_DIM, BLK), lambda b, i: (b, 0, 0))],
        out_shape=[jax.ShapeDtypeStruct((batch, seq, Q_DIM), BF16),
                   jax.ShapeDtypeStruct((batch, KV_DIM, BLK), F32),
                   jax.ShapeDtypeStruct((batch, KV_DIM, BLK), F32)],
        scratch_shapes=[pltpu.VMEM((u, KV_HEADS * 2 * BLK, KV_DIM), BF16),
                        pltpu.VMEM((u, KV_HEADS * 2 * BLK, KV_DIM), BF16),
                        pltpu.VMEM((u, Q_PER_KV * BLK, KV_HEADS * 2 * BLK), BF16)],
        compiler_params=_cparams(("arbitrary", "arbitrary")),
        name="attn_prompt",
    )(p3, p3, p3, cos, sin, qg, kg, bd, sinks)


def _ssd_chunk(slot, z_ref, xs_ref, b_ref, c_ref, dt_ref, cw_ref, cb_ref, dtb_ref, alog_ref, dskip_ref,
               nw_ref, e3_ref, o_ref, tail_ref, xpad, tails, st, ybuf):
    xpad[0:SUBLANES, :] = tails[1 - slot]
    xpad[SUBLANES:SUBLANES + BLK, 0:D_INNER] = xs_ref[...]
    xpad[SUBLANES:SUBLANES + BLK, D_INNER:D_INNER + BC_DIM] = b_ref[...]
    xpad[SUBLANES:SUBLANES + BLK, D_INNER + BC_DIM:CONV_DIM] = c_ref[...]
    cwh = 0.5 * cw_ref[...]
    acc = 0.5 * cb_ref[...] + cwh[CONV_W - 1:CONV_W, :] * xpad[SUBLANES:SUBLANES + BLK, :]
    for t in range(1, CONV_W):
        acc = acc + cwh[CONV_W - 1 - t:CONV_W - t, :] * xpad[SUBLANES - t:SUBLANES - t + BLK, :]
    xc = acc + acc * jnp.tanh(acc)
    new_tail = xpad[BLK:BLK + SUBLANES, :]
    tail_ref[...] = new_tail
    tails[slot] = new_tail

    xs = xc[:, 0:D_INNER]
    bm = xc[:, D_INNER:D_INNER + BC_DIM].astype(BF16)
    cm = xc[:, D_INNER + BC_DIM:CONV_DIM].astype(BF16)

    e3 = e3_ref[...]
    dt = _softplus(dt_ref[...] + dtb_ref[...])
    dta = dt * (-LOG2E * jnp.exp(alog_ref[...]))
    row = lax.broadcasted_iota(jnp.int32, (BLK, BLK), 0)
    col = lax.broadcasted_iota(jnp.int32, (BLK, BLK), 1)
    causal = row >= col
    cum = jnp.dot(causal.astype(F32), dta, preferred_element_type=F32, precision=lax.Precision.HIGHEST)
    cum_t = cum.T
    ecum = jnp.exp2(cum)
    to_end = jnp.exp2(cum[BLK - 1:BLK, :] - cum) * dt
    dt_e = _expand_heads(dt, e3)
    ecum_e = _expand_heads(ecum, e3)
    to_end_e = _expand_heads(to_end, e3)
    xdt = (xs * dt_e).astype(BF16)
    xte = (xs * to_end_e).astype(BF16)
    lane = lax.broadcasted_iota(jnp.int32, (BLK, LANES), 1)
    first_head = lane < SSM_HEAD_DIM

    for g in range(SSM_GROUPS):
        gl = slice(g * GROUP_LANES, (g + 1) * GROUP_LANES)
        bg = bm[:, g * D_STATE:(g + 1) * D_STATE]
        cg = cm[:, g * D_STATE:(g + 1) * D_STATE]
        cbg = lax.dot_general(cg, bg, (((1,), (1,)), ((), ())), preferred_element_type=F32)
        st_g = st[1 - slot, :, gl]
        y_inter = jnp.dot(cg, st_g.astype(BF16), preferred_element_type=F32) * ecum_e[:, gl]
        for pr in range(HEADS_PER_GROUP // 2):
            h0 = g * HEADS_PER_GROUP + 2 * pr
            xd = xdt[:, h0 * SSM_HEAD_DIM:(h0 + 2) * SSM_HEAD_DIM]
            ys = []
            for h in (h0, h0 + 1):
                diff = cum[:, h:h + 1] - cum_t[h:h + 1, :]
                w = (jnp.exp2(jnp.where(causal, diff, -jnp.inf)) * cbg).astype(BF16)
                ys.append(jnp.dot(w, xd, preferred_element_type=F32))
            lo = pr * LANES
            ybuf[:, h0 * SSM_HEAD_DIM:(h0 + 2) * SSM_HEAD_DIM] = (
                jnp.where(first_head, ys[0], ys[1]) + y_inter[:, lo:lo + LANES])
        upd = lax.dot_general(bg, xte[:, gl], (((0,), (0,)), ((), ())), preferred_element_type=F32)
        st[slot, :, gl] = st_g * ecum_e[BLK - 1:BLK, gl] + upd

    y = ybuf[...] + dskip_ref[...] * xs
    yz = y * _silu(z_ref[...])
    ms = jnp.mean(yz * yz, axis=-1, keepdims=True)
    o_ref[...] = (yz * lax.rsqrt(ms + EPS) * nw_ref[...]).astype(o_ref.dtype)


def _ssd_prompt_kernel(z_ref, xs_ref, b_ref, c_ref, dt_ref, cw_ref, cb_ref, dtb_ref, alog_ref, dskip_ref,
                       nw_ref, e3_ref, o_ref, tail_ref, hfin_ref, xpad, tails, st, ybuf):
    i = pl.program_id(1)
    last = pl.num_programs(1) - 1

    @pl.when(i == 0)
    def _():
        tails[...] = jnp.zeros_like(tails)
        st[...] = jnp.zeros_like(st)

    slot = i % 2
    seqs = range(z_ref.shape[0])
    for u in seqs:
        _ssd_chunk(slot, z_ref.at[u], xs_ref.at[u], b_ref.at[u], c_ref.at[u], dt_ref.at[u], cw_ref, cb_ref,
                   dtb_ref, alog_ref, dskip_ref, nw_ref, e3_ref, o_ref.at[u], tail_ref.at[u], xpad.at[u],
                   tails.at[u], st.at[u], ybuf.at[u])

    @pl.when(i == last)
    def _():
        for u in seqs:
            hfin_ref[u] = st[u, slot].T


SSD_SEQS = 2


def _ssd_prompt(p_act, lay, batch, seq, cw, cb, dtb, alog, dskip, nw, e3):
    nb = seq // BLK
    u = SSD_SEQS if batch % SSD_SEQS == 0 else 1
    p3 = p_act.reshape(batch, seq, p_act.shape[1])
    zc, xc = lay["z"][0] // D_INNER, lay["xs"][0] // D_INNER
    bc, cc, dc = lay["B"][0] // BC_DIM, lay["C"][0] // BC_DIM, lay["dt"][0] // DT_PAD
    const = lambda b, i: (0, 0)
    return pl.pallas_call(
        _ssd_prompt_kernel,
        grid=(batch // u, nb),
        in_specs=[pl.BlockSpec((u, BLK, D_INNER), lambda b, i: (b, i, zc)),
                  pl.BlockSpec((u, BLK, D_INNER), lambda b, i: (b, i, xc)),
                  pl.BlockSpec((u, BLK, BC_DIM), lambda b, i: (b, i, bc)),
                  pl.BlockSpec((u, BLK, BC_DIM), lambda b, i: (b, i, cc)),
                  pl.BlockSpec((u, BLK, DT_PAD), lambda b, i: (b, i, dc)),
                  pl.BlockSpec((CONV_W, CONV_DIM), const),
                  pl.BlockSpec((1, CONV_DIM), const),
                  pl.BlockSpec((1, DT_PAD), const),
                  pl.BlockSpec((1, DT_PAD), const),
                  pl.BlockSpec((1, D_INNER), const),
                  pl.BlockSpec((1, D_INNER), const),
                  pl.BlockSpec((3 * DT_PAD, D_INNER), const)],
        out_specs=[pl.BlockSpec((u, BLK, D_INNER), lambda b, i: (b, i, 0)),
                   pl.BlockSpec((u, SUBLANES, CONV_DIM), lambda b, i: (b, 0, 0)),
                   pl.BlockSpec((u, D_INNER, D_STATE), lambda b, i: (b, 0, 0))],
        out_shape=[jax.ShapeDtypeStruct((batch, seq, D_INNER), BF16),
                   jax.ShapeDtypeStruct((batch, SUBLANES, CONV_DIM), F32),
                   jax.ShapeDtypeStruct((batch, D_INNER, D_STATE), F32)],
        scratch_shapes=[pltpu.VMEM((u, BLK + SUBLANES, CONV_DIM), F32),
                        pltpu.VMEM((u, 2, SUBLANES, CONV_DIM), F32),
                        pltpu.VMEM((u, 2, D_STATE, D_INNER), F32),
                        pltpu.VMEM((u, BLK, D_INNER), F32)],
        compiler_params=_cparams(("arbitrary", "arbitrary")),
        name="ssd_prompt",
    )(p3, p3, p3, p3, p3, cw, cb, dtb, alog, dskip, nw, e3)


def _decode_pre_kernel(q_ref, k_ref, xs_ref, b_ref, c_ref, dt_ref, cs_ref, cos_ref, sin_ref, qg_ref, kg_ref,
                       bd_ref, cw_ref, cb_ref, dtb_ref, alog_ref, e3_ref,
                       qo_ref, ko_ref, xc_ref, xdt_ref, dec_ref, cso_ref):
    cos, sin, bd = cos_ref[...], sin_ref[...], bd_ref[...]
    qo_ref[...] = _unpair(_head_norm_rope(q_ref[...], qg_ref[...], cos, sin, bd)) * (HEAD_DIM ** -0.5)
    ko_ref[...] = _unpair(_head_norm_rope(k_ref[...], kg_ref[...], cos, sin, bd))

    segs = ((xs_ref, 0, D_INNER), (b_ref, D_INNER, BC_DIM), (c_ref, D_INNER + BC_DIM, BC_DIM))
    for ref, off, width in segs:
        new = ref[...]
        acc = cb_ref[:, off:off + width] + cw_ref[CONV_W - 1:CONV_W, off:off + width] * new
        for t in range(CONV_W - 1):
            lo = t * CONV_DIM + off
            acc = acc + cw_ref[t:t + 1, off:off + width] * cs_ref[:, lo:lo + width]
        xc_ref[:, off:off + width] = _silu(acc)
        for t in range(CONV_W - 2):
            cso_ref[:, t * CONV_DIM + off:t * CONV_DIM + off + width] = (
                cs_ref[:, (t + 1) * CONV_DIM + off:(t + 1) * CONV_DIM + off + width])
        lo = (CONV_W - 2) * CONV_DIM + off
        cso_ref[:, lo:lo + width] = new

    e3 = e3_ref[...]
    dt = _softplus(dt_ref[...] + dtb_ref[...])
    decay = jnp.exp(dt * (-jnp.exp(alog_ref[...])))
    xdt_ref[...] = xc_ref[:, 0:D_INNER] * _expand_heads(dt, e3)
    dec_ref[...] = _expand_heads(decay, e3)


def _decode_pre(p_act, lay, conv_state, cos, sin, qg, kg, bd, cw, cb, dtb, alog, e3):
    m = p_act.shape[0]
    full = lambda shape: pl.BlockSpec(shape, lambda i: (0, 0))

    def col(name):
        c = lay[name][0] // lay[name][1]
        return pl.BlockSpec((m, lay[name][1]), lambda i: (0, c))

    cs_w = (CONV_W - 1) * CONV_DIM
    return pl.pallas_call(
        _decode_pre_kernel,
        grid=(1,),
        in_specs=[col("q"), col("k"), col("xs"), col("B"), col("C"), col("dt"),
                  full((m, cs_w)), full((1, LANES)), full((1, LANES)), full((1, Q_DIM)), full((1, KV_DIM)),
                  full((SLAB, SLAB)), full((CONV_W, CONV_DIM)), full((1, CONV_DIM)), full((1, DT_PAD)),
                  full((1, DT_PAD)), full((3 * DT_PAD, D_INNER))],
        out_specs=[full((m, Q_DIM)), full((m, KV_DIM)), full((m, CONV_DIM)), full((m, D_INNER)),
                   full((m, D_INNER)), full((m, cs_w))],
        out_shape=[jax.ShapeDtypeStruct((m, Q_DIM), F32), jax.ShapeDtypeStruct((m, KV_DIM), F32),
                   jax.ShapeDtypeStruct((m, CONV_DIM), F32), jax.ShapeDtypeStruct((m, D_INNER), F32),
                   jax.ShapeDtypeStruct((m, D_INNER), F32), jax.ShapeDtypeStruct((m, cs_w), F32)],
        compiler_params=_cparams(("arbitrary",)),
        name="decode_pre",
    )(p_act, p_act, p_act, p_act, p_act, p_act, conv_state, cos, sin, qg, kg, bd, cw, cb, dtb, alog, e3)


DEC_ROWS = 8


def _attn_decode_kernel(q_ref, k_ref, v_ref, ck_ref, cv_ref, sink_ref, o_ref, cko_ref, cvo_ref):
    w = ck_ref.shape[2]
    nq = N_HEADS
    r = lax.broadcasted_iota(jnp.int32, (nq, SLAB), 0)
    grp = lax.broadcasted_iota(jnp.int32, (nq, SLAB), 1) // HEAD_DIM
    own = grp == (r % KV_HEADS)
    in_window = lax.broadcasted_iota(jnp.int32, (nq, w), 1) > (w - WINDOW - 1)
    newest = lax.broadcasted_iota(jnp.int32, (KV_DIM, w), 1) == w - 1
    sink = sink_ref[...][:, 0:1]
    pad = jnp.zeros((DEC_ROWS, KV_DIM), BF16)
    k_parts = jnp.concatenate(list(_split3(k_ref[...])) + [pad], axis=0)
    v_parts = jnp.concatenate(list(_split3(v_ref[...])) + [pad], axis=0)
    part_row = lax.broadcasted_iota(jnp.int32, (4 * DEC_ROWS, w), 0) % DEC_ROWS
    tdims = (((0,), (0,)), ((), ()))
    for bl in range(DEC_ROWS):
        qrow = q_ref[bl:bl + 1, :]
        qm = jnp.zeros((nq, SLAB), F32)
        for j in range(Q_PER_KV):
            slab = jnp.broadcast_to(qrow[:, j * SLAB:(j + 1) * SLAB], (nq, SLAB))
            qm = jnp.where((r // KV_HEADS) == j, slab, qm)
        qm = jnp.where(own, qm, 0.0).astype(BF16)
        pick = jnp.where(part_row == bl, 1.0, 0.0).astype(BF16)
        k_col = lax.dot_general(k_parts, pick, tdims, preferred_element_type=F32)
        v_col = lax.dot_general(v_parts, pick, tdims, preferred_element_type=F32)
        keys = jnp.where(newest, k_col, pltpu.roll(ck_ref[bl], w - 1, 1))
        vals = jnp.where(newest, v_col, pltpu.roll(cv_ref[bl], w - 1, 1))
        cko_ref[bl] = keys
        cvo_ref[bl] = vals
        s = jnp.dot(qm, keys.astype(BF16), preferred_element_type=F32)
        s = jnp.where(in_window, s, -jnp.inf)
        mx = jnp.maximum(jnp.max(s, axis=-1, keepdims=True), sink)
        e = jnp.exp(s - mx)
        den = jnp.sum(e, axis=-1, keepdims=True) + jnp.exp(sink - mx)
        pv = lax.dot_general(e.astype(BF16), vals.astype(BF16), (((1,), (1,)), ((), ())),
                             preferred_element_type=F32)
        pv = jnp.where(own, pv / den, 0.0)
        for j in range(Q_PER_KV):
            o_ref[bl:bl + 1, j * SLAB:(j + 1) * SLAB] = jnp.sum(
                pv[j * KV_HEADS:(j + 1) * KV_HEADS, :], axis=0, keepdims=True)


def _attn_decode(q, k, p_act, lay, cache_k, cache_v, sink_rows):
    m = q.shape[0]
    w = cache_k.shape[2]
    vc = lay["v"][0] // KV_DIM
    cache_spec = pl.BlockSpec((DEC_ROWS, KV_DIM, w), lambda i: (i, 0, 0))
    return pl.pallas_call(
        _attn_decode_kernel,
        grid=(m // DEC_ROWS,),
        in_specs=[pl.BlockSpec((DEC_ROWS, Q_DIM), lambda i: (i, 0)),
                  pl.BlockSpec((DEC_ROWS, KV_DIM), lambda i: (i, 0)),
                  pl.BlockSpec((DEC_ROWS, KV_DIM), lambda i: (i, vc)),
                  cache_spec, cache_spec,
                  pl.BlockSpec((N_HEADS, LANES), lambda i: (0, 0))],
        out_specs=[pl.BlockSpec((DEC_ROWS, Q_DIM), lambda i: (i, 0)), cache_spec, cache_spec],
        out_shape=[jax.ShapeDtypeStruct((m, Q_DIM), F32),
                   jax.ShapeDtypeStruct((m, KV_DIM, w), F32),
                   jax.ShapeDtypeStruct((m, KV_DIM, w), F32)],
        compiler_params=_cparams(("arbitrary",)),
        name="attn_decode",
    )(q, k, p_act, cache_k, cache_v, sink_rows)


MM_ROWS = 16


def _ssm_decode_kernel(st_ref, xdt_ref, dec_ref, xc_ref, z_ref, dskip_ref, nw_ref, sto_ref, o_ref):
    rows = st_ref.shape[0]
    r = lax.broadcasted_iota(jnp.int32, (MM_ROWS, D_INNER), 0)
    grp = lax.broadcasted_iota(jnp.int32, (MM_ROWS, D_INNER), 1) // GROUP_LANES
    rr = lax.broadcasted_iota(jnp.int32, (MM_ROWS, D_STATE), 0)
    ones_rows = jnp.where((rr >= SSM_GROUPS) & (rr < SSM_GROUPS + 3), 1.0, 0.0)
    bc = lambda a: jnp.broadcast_to(a, (MM_ROWS, D_INNER))
    new_states = []
    for u in range(rows):
        h = st_ref[u]
        xdt = xdt_ref[u:u + 1, :]
        xc = xc_ref[u:u + 1, :]
        hi, mid, lo = _split3(dec_ref[u:u + 1, :])
        lhs_t = jnp.where(r == grp, bc(xdt), 0.0)
        for t, piece in enumerate((hi, mid, lo)):
            lhs_t = jnp.where(r == SSM_GROUPS + t, bc(piece.astype(F32)), lhs_t)
        lhs_t = lhs_t.astype(BF16)
        b_rows = jnp.zeros((MM_ROWS, D_STATE), F32)
        for g in range(SSM_GROUPS):
            b_g = xc[:, D_INNER + g * D_STATE:D_INNER + (g + 1) * D_STATE]
            b_rows = jnp.where(rr == g, jnp.broadcast_to(b_g, (MM_ROWS, D_STATE)), b_rows)
        rhs = jnp.concatenate([b_rows, ones_rows], axis=1).astype(BF16)
        both = lax.dot_general(lhs_t, rhs, (((0,), (0,)), ((), ())), preferred_element_type=F32)
        h_new = both[:, D_STATE:] * h + both[:, :D_STATE]
        sto_ref[u] = h_new
        new_states.append(h_new.astype(BF16))
    states = jnp.concatenate(new_states, axis=1)
    urow = lax.broadcasted_iota(jnp.int32, (rows, rows * D_STATE), 0)
    ublk = lax.broadcasted_iota(jnp.int32, (rows, rows * D_STATE), 1) // D_STATE
    c_sel = []
    for g in range(SSM_GROUPS):
        c_g = xc_ref[:, D_INNER + BC_DIM + g * D_STATE:D_INNER + BC_DIM + (g + 1) * D_STATE]
        c_sel.append(jnp.where(urow == ublk, jnp.concatenate([c_g] * rows, axis=1), 0.0))
    c_all = jnp.concatenate(c_sel, axis=0).astype(BF16)
    yt = lax.dot_general(states, c_all, (((1,), (1,)), ((), ())), preferred_element_type=F32).T
    lane_grp = lax.broadcasted_iota(jnp.int32, (rows, D_INNER), 1) // GROUP_LANES
    y = jnp.zeros((rows, D_INNER), F32)
    for g in range(SSM_GROUPS):
        y = jnp.where(lane_grp == g, yt[g * rows:(g + 1) * rows, :], y)
    y = y + dskip_ref[...] * xc_ref[:, 0:D_INNER]
    yz = y * _silu(z_ref[...])
    ms = jnp.mean(yz * yz, axis=-1, keepdims=True)
    o_ref[...] = yz * lax.rsqrt(ms + EPS) * nw_ref[...]


SSM_DEC_ROWS = SUBLANES


def _ssm_decode(state, xdt, dec, xc, p_act, lay, dskip, nw):
    m = state.shape[0]
    u = SSM_DEC_ROWS
    assert m % u == 0
    zc = lay["z"][0] // D_INNER
    row = lambda width: pl.BlockSpec((u, width), lambda b: (b, 0))
    return pl.pallas_call(
        _ssm_decode_kernel,
        grid=(m // u,),
        in_specs=[pl.BlockSpec((u, D_INNER, D_STATE), lambda b: (b, 0, 0)),
                  row(D_INNER), row(D_INNER), row(CONV_DIM),
                  pl.BlockSpec((u, D_INNER), lambda b: (b, zc)),
                  pl.BlockSpec((1, D_INNER), lambda b: (0, 0)),
                  pl.BlockSpec((1, D_INNER), lambda b: (0, 0))],
        out_specs=[pl.BlockSpec((u, D_INNER, D_STATE), lambda b: (b, 0, 0)), row(D_INNER)],
        out_shape=[jax.ShapeDtypeStruct((m, D_INNER, D_STATE), F32),
                   jax.ShapeDtypeStruct((m, D_INNER), F32)],
        compiler_params=_cparams(("arbitrary",)),
        name="ssm_decode",
    )(state, xdt, dec, xc, p_act, dskip, nw)


def _merge_mlp_kernel(x_ref, oa_ref, os_ref, ga_ref, gs_ref, wa_ref, ws_ref, wo_ref, g_ref, wu_ref, wd_ref,
                      o_ref, *, ff_tile):
    a = jnp.dot(oa_ref[...].astype(BF16), wa_ref[...], preferred_element_type=F32)
    s = jnp.dot(os_ref[...].astype(BF16), ws_ref[...], preferred_element_type=F32)
    mixed = _sigmoid(ga_ref[...]) * a + _sigmoid(gs_ref[...]) * s
    x = x_ref[...] + jnp.dot(mixed.astype(BF16), wo_ref[...], preferred_element_type=F32)
    ms = jnp.mean(x * x, axis=-1, keepdims=True)
    xn = (x * lax.rsqrt(ms + EPS) * g_ref[...]).astype(BF16)
    acc = x
    for c in range(wu_ref.shape[1] // ff_tile):
        u = jnp.dot(xn, wu_ref[:, c * ff_tile:(c + 1) * ff_tile], preferred_element_type=F32)
        act = jnp.square(jnp.maximum(u, 0.0)).astype(BF16)
        acc = acc + jnp.dot(act, wd_ref[c * ff_tile:(c + 1) * ff_tile, :], preferred_element_type=F32)
    o_ref[...] = acc


def _merge_mlp_stream_kernel(x_ref, oa_ref, os_ref, ga_ref, gs_ref, wa_ref, ws_ref, wo_ref, g_ref, wu_ref, wd_ref,
                             o_ref, xn_ref):
    @pl.when(pl.program_id(0) == 0)
    def _():
        a = jnp.dot(oa_ref[...].astype(BF16), wa_ref[...], preferred_element_type=F32)
        s = jnp.dot(os_ref[...].astype(BF16), ws_ref[...], preferred_element_type=F32)
        mixed = _sigmoid(ga_ref[...]) * a + _sigmoid(gs_ref[...]) * s
        x = x_ref[...] + jnp.dot(mixed.astype(BF16), wo_ref[...], preferred_element_type=F32)
        ms = jnp.mean(x * x, axis=-1, keepdims=True)
        xn_ref[...] = (x * lax.rsqrt(ms + EPS) * g_ref[...]).astype(BF16)
        o_ref[...] = x

    u = jnp.dot(xn_ref[...], wu_ref[...], preferred_element_type=F32)
    act = jnp.square(jnp.maximum(u, 0.0)).astype(BF16)
    o_ref[...] += jnp.dot(act, wd_ref[...], preferred_element_type=F32)


def _merge_mlp(x, o_attn, o_ssm, p_act, lay, wa, ws, wo, g, wu, wd):
    m, d = x.shape
    ff = wu.shape[1]
    tm = _row_tile(m, 512)
    gac, gsc = lay["ga"][0] // d, lay["gs"][0] // d
    if m == tm:
        ft = min(1024, ff)
        const = lambda c: (0, 0)
        return pl.pallas_call(
            _merge_mlp_stream_kernel,
            grid=(ff // ft,),
            in_specs=[pl.BlockSpec((m, d), const), pl.BlockSpec((m, Q_DIM), const),
                      pl.BlockSpec((m, D_INNER), const),
                      pl.BlockSpec((m, d), lambda c: (0, gac)), pl.BlockSpec((m, d), lambda c: (0, gsc)),
                      pl.BlockSpec((Q_DIM, d), const), pl.BlockSpec((D_INNER, d), const),
                      pl.BlockSpec((d, d), const), pl.BlockSpec((1, d), const),
                      pl.BlockSpec((d, ft), lambda c: (0, c)), pl.BlockSpec((ft, d), lambda c: (c, 0))],
            out_specs=pl.BlockSpec((m, d), const),
            out_shape=jax.ShapeDtypeStruct((m, d), F32),
            scratch_shapes=[pltpu.VMEM((m, d), BF16)],
            compiler_params=_cparams(("arbitrary",)),
            name="merge_mlp_stream",
        )(x, o_attn, o_ssm, p_act, p_act, wa, ws, wo, g, wu, wd)
    weight = lambda shape: pl.BlockSpec(shape, lambda i: (0, 0), pipeline_mode=pl.Buffered(1))
    return pl.pallas_call(
        functools.partial(_merge_mlp_kernel, ff_tile=min(1024, ff)),
        grid=(m // tm,),
        in_specs=[pl.BlockSpec((tm, d), lambda i: (i, 0)),
                  pl.BlockSpec((tm, Q_DIM), lambda i: (i, 0)),
                  pl.BlockSpec((tm, D_INNER), lambda i: (i, 0)),
                  pl.BlockSpec((tm, d), lambda i: (i, gac)),
                  pl.BlockSpec((tm, d), lambda i: (i, gsc)),
                  weight((Q_DIM, d)), weight((D_INNER, d)), weight((d, d)),
                  weight((1, d)), weight((d, ff)), weight((ff, d))],
        out_specs=pl.BlockSpec((tm, d), lambda i: (i, 0)),
        out_shape=jax.ShapeDtypeStruct((m, d), F32),
        compiler_params=_cparams(("arbitrary",), WEIGHT_RESIDENT_VMEM_LIMIT),
        name="merge_mlp",
    )(x, o_attn, o_ssm, p_act, p_act, wa, ws, wo, g, wu, wd)


def _rope_tables(pos):
    half = HEAD_DIM // 2
    inv = ROPE_THETA ** (-jnp.arange(half, dtype=F32) / half)
    ang = pos.astype(F32)[:, None] * inv[None, :]
    cos, sin = jnp.cos(ang), jnp.sin(ang)
    return (jnp.concatenate([cos, cos, cos, cos], axis=1),
            jnp.concatenate([-sin, -sin, sin, sin], axis=1))


def _constants():
    heads = np.arange(DT_PAD)[:, None]
    cols = np.arange(D_INNER)[None, :] // SSM_HEAD_DIM
    e = (heads == cols).astype(np.float32)
    e3 = jnp.asarray(np.concatenate([e, e, e], axis=0), BF16)
    a = np.arange(SLAB)
    grp = 2 * (a // LANES) + (a % HEAD_DIM) // (HEAD_DIM // 2)
    bd = jnp.asarray((grp[:, None] == grp[None, :]).astype(np.float32) / HEAD_DIM, BF16)
    return e3, bd


_HALF = HEAD_DIM // 2


def _pair_q(a):
    lead = a.shape[:-1]
    n = len(lead)
    a = a.reshape(lead + (2, 2, Q_PER_KV, 2, _HALF))
    return a.transpose(tuple(range(n)) + (n + 2, n, n + 3, n + 1, n + 4)).reshape(lead + (Q_DIM,))


def _pair_k(a):
    lead = a.shape[:-1]
    n = len(lead)
    a = a.reshape(lead + (2, 2, 2, _HALF))
    return a.transpose(tuple(range(n)) + (n, n + 2, n + 1, n + 3)).reshape(lead + (KV_DIM,))


def _w_in_plan(d_model):
    lay, _ = _layout(d_model)
    o_v = Q_DIM + KV_DIM
    o_z = o_v + KV_DIM
    o_xbc = o_z + D_INNER
    o_dt = o_xbc + CONV_DIM
    o_g = o_dt + SSM_HEADS
    src = {"z": o_z, "xs": o_xbc, "ga": o_g, "gs": o_g + d_model, "B": o_xbc + D_INNER,
           "C": o_xbc + D_INNER + BC_DIM, "v": o_v}
    return [(lay[name][0], off, lay[name][1]) for name, off in src.items()], o_dt


def _w_prep_kernel(w_ref, qk_ref, o_ref, *, plan, lay, o_dt):
    for dst, src, width in plan:
        o_ref[:, dst:dst + width] = w_ref[:, src:src + width].astype(BF16)
    o_ref[:, lay["q"][0]:lay["q"][0] + Q_DIM] = qk_ref[:, 0:Q_DIM]
    o_ref[:, lay["k"][0]:lay["k"][0] + KV_DIM] = qk_ref[:, Q_DIM:Q_DIM + KV_DIM]
    dt = w_ref[:, o_dt:o_dt + SSM_HEADS].astype(BF16)
    pad = jnp.zeros((dt.shape[0], DT_PAD - SSM_HEADS), BF16)
    o_ref[:, lay["dt"][0]:lay["dt"][0] + DT_PAD] = jnp.concatenate([dt, pad], axis=1)


def _w_prep(w_in, qk):
    d_model, n_in = w_in.shape
    lay, n = _layout(d_model)
    plan, o_dt = _w_in_plan(d_model)
    tr = _row_tile(d_model, LANES)
    return pl.pallas_call(
        functools.partial(_w_prep_kernel, plan=plan, lay=lay, o_dt=o_dt),
        grid=(d_model // tr,),
        in_specs=[pl.BlockSpec((tr, n_in), lambda i: (i, 0)),
                  pl.BlockSpec((tr, Q_DIM + KV_DIM), lambda i: (i, 0))],
        out_specs=pl.BlockSpec((tr, n), lambda i: (i, 0)),
        out_shape=jax.ShapeDtypeStruct((d_model, n), BF16),
        compiler_params=_cparams(("arbitrary",)),
        name="w_prep",
    )(w_in, qk)


def _prep_layer(d_model, norm_mix, w_in, q_norm, k_norm, attn_sinks, conv_w, conv_b, dt_bias, a_log, d_skip,
                ssm_norm, w_attn_o, w_ssm_o, w_out, norm_mlp, w_up, w_down):
    assert KV_HEADS == 4 and Q_PER_KV == 4
    e3, bd = _constants()
    lay, _ = _layout(d_model)
    wb = w_in.astype(BF16)
    qk = jnp.concatenate([_pair_q(wb[:, 0:Q_DIM]), _pair_k(wb[:, Q_DIM:Q_DIM + KV_DIM])], axis=1)
    w_p = _w_prep(wb, qk)
    pad_heads = lambda a: jnp.pad(a, (0, DT_PAD - SSM_HEADS))[None, :]
    sink_p = attn_sinks.reshape(KV_HEADS, Q_PER_KV).T.reshape(N_HEADS)
    wa = w_attn_o.reshape(KV_HEADS, Q_PER_KV, HEAD_DIM, d_model).transpose(1, 0, 2, 3).reshape(Q_DIM, d_model)
    return dict(
        lay=lay, e3=e3, bd=bd, w_p=w_p,
        norm_mix=norm_mix[None, :], norm_mlp=norm_mlp[None, :],
        qg=_pair_q(jnp.tile(q_norm, N_HEADS))[None, :], kg=_pair_k(jnp.tile(k_norm, KV_HEADS))[None, :],
        sink_p=sink_p, sink_rows=jnp.broadcast_to(sink_p[:, None], (N_HEADS, LANES)),
        cw=conv_w, cb=conv_b[None, :], dtb=pad_heads(dt_bias), alog=pad_heads(a_log),
        dskip=jnp.repeat(d_skip, SSM_HEAD_DIM)[None, :], nw=ssm_norm[None, :],
        wa=wa.astype(BF16), ws=w_ssm_o.astype(BF16), wo=w_out.astype(BF16),
        wu=w_up.astype(BF16), wd=w_down.astype(BF16))


def _prompt_layer(x, lw):
    batch, seq, d = x.shape
    assert seq % BLK == 0 and seq >= WINDOW
    x2 = x.reshape(batch * seq, d)
    p_act = _in_proj(x2, lw["norm_mix"], lw["w_p"])
    cos, sin = _rope_tables(jnp.arange(seq))
    o_attn, k_last, v_last = _attn_prompt(p_act, lw["lay"], batch, seq, cos, sin, lw["qg"], lw["kg"],
                                          lw["bd"], lw["sink_p"])
    o_ssm, tail, h_fin = _ssd_prompt(p_act, lw["lay"], batch, seq, lw["cw"], lw["cb"], lw["dtb"], lw["alog"],
                                     lw["dskip"], lw["nw"], lw["e3"])
    y = _merge_mlp(x2, o_attn.reshape(batch * seq, Q_DIM), o_ssm.reshape(batch * seq, D_INNER), p_act,
                   lw["lay"], lw["wa"], lw["ws"], lw["wo"],
                   lw["norm_mlp"], lw["wu"], lw["wd"])
    return (y.reshape(batch, seq, d),
            k_last.reshape(batch, KV_HEADS, HEAD_DIM, BLK).transpose(0, 3, 1, 2),
            v_last.reshape(batch, KV_HEADS, HEAD_DIM, BLK).transpose(0, 3, 1, 2),
            tail[:, SUBLANES - (CONV_W - 1):, :],
            h_fin.reshape(batch, SSM_HEADS, SSM_HEAD_DIM, D_STATE))


def _decode_layer(x, cache_k, cache_v, conv_state, ssm_state, lw):
    m, t, d = x.shape
    w = cache_k.shape[1]
    assert t == 1 and w == WINDOW and m % DEC_ROWS == 0
    x2 = x.reshape(m, d)
    p_act = _in_proj(x2, lw["norm_mix"], lw["w_p"])
    cos, sin = _rope_tables(PAST_LEN + jnp.arange(1))
    q, k, xc, xdt, dec, conv_new = _decode_pre(
        p_act, lw["lay"], conv_state.reshape(m, (CONV_W - 1) * CONV_DIM), cos, sin, lw["qg"], lw["kg"],
        lw["bd"], lw["cw"], lw["cb"], lw["dtb"], lw["alog"], lw["e3"])
    ck = jnp.transpose(cache_k, (0, 2, 3, 1)).reshape(m, KV_DIM, w)
    cv = jnp.transpose(cache_v, (0, 2, 3, 1)).reshape(m, KV_DIM, w)
    o_attn, ck_new, cv_new = _attn_decode(q, k, p_act, lw["lay"], ck, cv, lw["sink_rows"])
    h_new, o_ssm = _ssm_decode(ssm_state.reshape(m, D_INNER, D_STATE), xdt, dec, xc, p_act, lw["lay"],
                               lw["dskip"], lw["nw"])
    y = _merge_mlp(x2, o_attn, o_ssm, p_act, lw["lay"], lw["wa"], lw["ws"], lw["wo"],
                   lw["norm_mlp"], lw["wu"], lw["wd"])
    unview = lambda c: jnp.transpose(c.reshape(m, KV_HEADS, HEAD_DIM, w), (0, 3, 1, 2))
    return (y.reshape(m, 1, d), unview(ck_new), unview(cv_new), conv_new.reshape(m, CONV_W - 1, CONV_DIM),
            h_new.reshape(m, SSM_HEADS, SSM_HEAD_DIM, D_STATE))


def kernel(x_prompt, x_sample, cache_k, cache_v, state_conv, state_ssm, norm_mix, w_in, q_norm, k_norm,
           attn_sinks, conv_w, conv_b, dt_bias, a_log, d_skip, ssm_norm, w_attn_o, w_ssm_o, w_out,
           norm_mlp, w_up, w_down):
    depth = w_in.shape[0]
    d_model = x_prompt.shape[-1]
    yp, ys = x_prompt, x_sample
    cols = [[] for _ in range(8)]
    for l in range(depth):
        lw = _prep_layer(d_model, norm_mix[l], w_in[l], q_norm[l], k_norm[l], attn_sinks[l], conv_w[l],
                         conv_b[l], dt_bias[l], a_log[l], d_skip[l], ssm_norm[l], w_attn_o[l], w_ssm_o[l],
                         w_out[l], norm_mlp[l], w_up[l], w_down[l])
        yp, kp, vp, cp, hp = _prompt_layer(yp, lw)
        ys, ks, vs, cs, hs = _decode_layer(ys, cache_k[l], cache_v[l], state_conv[l], state_ssm[l], lw)
        for lst, val in zip(cols, (kp, vp, cp, hp, ks, vs, cs, hs)):
            lst.append(val)
    return (yp, ys) + tuple(jnp.stack(c) for c in cols)
```

```python
import functools

import numpy as np
import jax
import jax.numpy as jnp
from jax import lax
from jax.experimental import pallas as pl
from jax.experimental.pallas import tpu as pltpu

F32 = jnp.float32
BF16 = jnp.bfloat16

N_HEADS = 16
KV_HEADS = 4
HEAD_DIM = 64
Q_PER_KV = N_HEADS // KV_HEADS
WINDOW = 128
ROPE_THETA = 10000.0
Q_DIM = N_HEADS * HEAD_DIM
KV_DIM = KV_HEADS * HEAD_DIM
SSM_HEAD_DIM = 64
SSM_HEADS = 32
D_INNER = SSM_HEADS * SSM_HEAD_DIM
SSM_GROUPS = 4
HEADS_PER_GROUP = SSM_HEADS // SSM_GROUPS
D_STATE = 128
BC_DIM = SSM_GROUPS * D_STATE
CONV_W = 4
CONV_DIM = D_INNER + 2 * BC_DIM
SSD_CHUNK = 128
EPS = 1e-6
PAST_LEN = 8192
LOG2E = 1.4426950408889634

LANES = 128
SUBLANES = 8
VMEM_LIMIT = 48 * 1024 * 1024
WEIGHT_RESIDENT_VMEM_LIMIT = 56 * 1024 * 1024

BLK = 128
assert WINDOW == BLK and SSD_CHUNK == BLK
SLAB = KV_HEADS * HEAD_DIM
GROUP_LANES = HEADS_PER_GROUP * SSM_HEAD_DIM
DT_PAD = LANES


def _layout(d_model):
    segs = [("z", D_INNER), ("xs", D_INNER), ("ga", d_model), ("gs", d_model), ("q", Q_DIM),
            ("B", BC_DIM), ("C", BC_DIM), ("k", KV_DIM), ("v", KV_DIM), ("dt", DT_PAD)]
    lay, off = {}, 0
    for name, width in segs:
        assert off % width == 0, (name, off, width)
        lay[name] = (off, width)
        off += width
    return lay, off


def _col_tile(n, cap=3072):
    units = n // LANES
    best = 1
    for d in range(1, units + 1):
        if units % d == 0 and d * LANES <= cap:
            best = d
    return best * LANES


def _row_tile(m, cap):
    assert m % SUBLANES == 0
    best = SUBLANES
    for t in range(SUBLANES, min(m, cap) + 1, SUBLANES):
        if m % t == 0:
            best = t
    return best


def _cparams(sem, vmem_limit=VMEM_LIMIT):
    return pltpu.CompilerParams(dimension_semantics=sem, vmem_limit_bytes=vmem_limit)


def _sigmoid(x):
    return 0.5 + 0.5 * jnp.tanh(0.5 * x)


def _silu(x):
    h = 0.5 * x
    return h + h * jnp.tanh(h)


def _softplus(x):
    return jnp.maximum(x, 0.0) + jnp.log1p(jnp.exp(-jnp.abs(x)))


def _split3(a):
    hi = a.astype(BF16)
    r1 = a - hi.astype(F32)
    mid = r1.astype(BF16)
    lo = (r1 - mid.astype(F32)).astype(BF16)
    return hi, mid, lo


def _expand_heads(a, e3):
    hi, mid, lo = _split3(a)
    return jnp.dot(jnp.concatenate([hi, mid, lo], axis=1), e3, preferred_element_type=F32)


def _head_ms(x, bd):
    sq = x * x
    hi = sq.astype(BF16)
    lo = (sq - hi.astype(F32)).astype(BF16)
    outs = []
    for s in range(x.shape[1] // SLAB):
        sl = slice(s * SLAB, (s + 1) * SLAB)
        outs.append(jnp.dot(hi[:, sl], bd, preferred_element_type=F32)
                    + jnp.dot(lo[:, sl], bd, preferred_element_type=F32))
    return outs[0] if len(outs) == 1 else jnp.concatenate(outs, axis=1)


def _head_norm_rope(x, g, cos, sin, bd):
    xn = x * lax.rsqrt(_head_ms(x, bd) + EPS) * g
    tiles = []
    for t in range(x.shape[1] // LANES):
        xt = xn[:, t * LANES:(t + 1) * LANES]
        tiles.append(xt * cos + pltpu.roll(xt, LANES // 2, 1) * sin)
    return tiles[0] if len(tiles) == 1 else jnp.concatenate(tiles, axis=1)


def _unpair(x):
    q = HEAD_DIM // 2
    lane = lax.broadcasted_iota(jnp.int32, (x.shape[0], LANES), 1)
    tiles = []
    for t in range(x.shape[1] // LANES):
        xt = x[:, t * LANES:(t + 1) * LANES]
        nat = jnp.where((lane >= q) & (lane < 2 * q), pltpu.roll(xt, LANES - q, 1), xt)
        tiles.append(jnp.where((lane >= 2 * q) & (lane < 3 * q), pltpu.roll(xt, q, 1), nat))
    return tiles[0] if len(tiles) == 1 else jnp.concatenate(tiles, axis=1)


def _in_proj_kernel(x_ref, g_ref, w_ref, o_ref, xn_ref, *, tn):
    j = pl.program_id(1)

    @pl.when(j == 0)
    def _():
        x = x_ref[...]
        ms = jnp.mean(x * x, axis=-1, keepdims=True)
        xn_ref[...] = (x * lax.rsqrt(ms + EPS) * g_ref[...]).astype(BF16)

    for c in range(w_ref.shape[1] // tn):
        @pl.when(j == c)
        def _():
            o_ref[...] = jnp.dot(xn_ref[...], w_ref[:, c * tn:(c + 1) * tn], preferred_element_type=F32)


def _in_proj_stream_kernel(x_ref, g_ref, w_ref, o_ref, xn_ref):
    @pl.when(pl.program_id(0) == 0)
    def _():
        x = x_ref[...]
        ms = jnp.mean(x * x, axis=-1, keepdims=True)
        xn_ref[...] = (x * lax.rsqrt(ms + EPS) * g_ref[...]).astype(BF16)

    o_ref[...] = jnp.dot(xn_ref[...], w_ref[...], preferred_element_type=F32)


def _in_proj(x, g, w):
    m, k = x.shape
    n = w.shape[1]
    tm = _row_tile(m, 1024)
    tn = _col_tile(n)
    if m == tm:
        return pl.pallas_call(
            _in_proj_stream_kernel,
            grid=(n // tn,),
            in_specs=[pl.BlockSpec((m, k), lambda j: (0, 0)),
                      pl.BlockSpec((1, k), lambda j: (0, 0)),
                      pl.BlockSpec((k, tn), lambda j: (0, j))],
            out_specs=pl.BlockSpec((m, tn), lambda j: (0, j)),
            out_shape=jax.ShapeDtypeStruct((m, n), F32),
            scratch_shapes=[pltpu.VMEM((m, k), BF16)],
            compiler_params=_cparams(("arbitrary",)),
            name="in_proj_stream",
        )(x, g, w)
    return pl.pallas_call(
        functools.partial(_in_proj_kernel, tn=tn),
        grid=(m // tm, n // tn),
        in_specs=[pl.BlockSpec((tm, k), lambda i, j: (i, 0)),
                  pl.BlockSpec((1, k), lambda i, j: (0, 0)),
                  pl.BlockSpec((k, n), lambda i, j: (0, 0), pipeline_mode=pl.Buffered(1))],
        out_specs=pl.BlockSpec((tm, tn), lambda i, j: (i, j)),
        out_shape=jax.ShapeDtypeStruct((m, n), F32),
        scratch_shapes=[pltpu.VMEM((tm, k), BF16)],
        compiler_params=_cparams(("arbitrary", "arbitrary"), WEIGHT_RESIDENT_VMEM_LIMIT),
        name="in_proj",
    )(x, g, w)


def _attn_prompt_kernel(q_ref, k_ref, v_ref, cos_ref, sin_ref, qg_ref, kg_ref, bd_ref, sink_ref,
                        o_ref, ko_ref, vo_ref, kbuf, vbuf, probs):
    i = pl.program_id(1)
    last = pl.num_programs(1) - 1

    @pl.when(i == 0)
    def _():
        kbuf[...] = jnp.zeros_like(kbuf)
        vbuf[...] = jnp.zeros_like(vbuf)

    cos, sin, bd = cos_ref[...], sin_ref[...], bd_ref[...]
    slot = i % 2
    lane = lax.broadcasted_iota(jnp.int32, (BLK, SLAB), 1)
    grp_k = 2 * (lane // LANES) + (lane % HEAD_DIM) // (HEAD_DIM // 2)
    grp_v = lane // HEAD_DIM
    r = lax.broadcasted_iota(jnp.int32, (BLK, BLK), 0)
    key = lax.broadcasted_iota(jnp.int32, (BLK, BLK), 1)
    causal = key <= r
    take0 = jnp.where(slot == 0, causal.astype(jnp.int32), 1 - causal.astype(jnp.int32)) > 0
    bias = jnp.where(causal, 0.0, jnp.where(i > 0, 0.0, -jnp.inf))

    seqs = range(q_ref.shape[0])
    ks, vs, scores = [], [], []
    for u in seqs:
        q = _head_norm_rope(q_ref[u], qg_ref[...], cos, sin, bd) * (HEAD_DIM ** -0.5 * LOG2E)
        k = _head_norm_rope(k_ref[u], kg_ref[...], cos, sin, bd)
        v = v_ref[u]
        ks.append(k)
        vs.append(v)
        for p in range(KV_HEADS):
            rows = pl.ds(pl.multiple_of(p * 2 * BLK + slot * BLK, BLK), BLK)
            kbuf[u, rows, :] = jnp.where(grp_k == p, k, 0.0).astype(BF16)
            vbuf[u, rows, :] = jnp.where(grp_v == p, v, 0.0).astype(BF16)
        q_stack = jnp.concatenate([q[:, j * SLAB:(j + 1) * SLAB] for j in range(Q_PER_KV)],
                                  axis=0).astype(BF16)
        scores.append(lax.dot_general(q_stack, kbuf[u], (((1,), (1,)), ((), ())),
                                      preferred_element_type=F32))
    for u in seqs:
        for j in range(Q_PER_KV):
            for p in range(KV_HEADS):
                rows = slice(j * BLK, (j + 1) * BLK)
                c0 = p * 2 * BLK
                s = jnp.where(take0, scores[u][rows, c0:c0 + BLK], scores[u][rows, c0 + BLK:c0 + 2 * BLK]) + bias
                sink = sink_ref[j * KV_HEADS + p] * LOG2E
                mx = jnp.maximum(jnp.max(s, axis=-1, keepdims=True), sink)
                e = jnp.exp2(s - mx)
                den = jnp.sum(e, axis=-1, keepdims=True) + jnp.exp2(sink - mx)
                pr = e / den
                probs[u, rows, c0:c0 + BLK] = jnp.where(take0, pr, 0.0).astype(BF16)
                probs[u, rows, c0 + BLK:c0 + 2 * BLK] = jnp.where(take0, 0.0, pr).astype(BF16)
    for u in seqs:
        pv = jnp.dot(probs[u], vbuf[u], preferred_element_type=F32)
        for j in range(Q_PER_KV):
            o_ref[u, :, j * SLAB:(j + 1) * SLAB] = pv[j * BLK:(j + 1) * BLK, :].astype(o_ref.dtype)

    @pl.when(i == last)
    def _():
        for u in seqs:
            ko_ref[u] = _unpair(ks[u]).T
            vo_ref[u] = vs[u].T


ATTN_SEQS = 2


def _attn_prompt(p_act, lay, batch, seq, cos, sin, qg, kg, bd, sinks):
    nb = seq // BLK
    u = ATTN_SEQS if batch % ATTN_SEQS == 0 else 1
    qc, kc, vc = lay["q"][0] // Q_DIM, lay["k"][0] // KV_DIM, lay["v"][0] // KV_DIM
    p3 = p_act.reshape(batch, seq, p_act.shape[1])
    return pl.pallas_call(
        _attn_prompt_kernel,
        grid=(batch // u, nb),
        in_specs=[pl.BlockSpec((u, BLK, Q_DIM), lambda b, i: (b, i, qc)),
                  pl.BlockSpec((u, BLK, KV_DIM), lambda b, i: (b, i, kc)),
                  pl.BlockSpec((u, BLK, KV_DIM), lambda b, i: (b, i, vc)),
                  pl.BlockSpec((BLK, LANES), lambda b, i: (i, 0)),
                  pl.BlockSpec((BLK, LANES), lambda b, i: (i, 0)),
                  pl.BlockSpec((1, Q_DIM), lambda b, i: (0, 0)),
                  pl.BlockSpec((1, KV_DIM), lambda b, i: (0, 0)),
                  pl.BlockSpec((SLAB, SLAB), lambda b, i: (0, 0)),
                  pl.BlockSpec(memory_space=pltpu.SMEM)],
        out_specs=[pl.BlockSpec((u, BLK, Q_DIM), lambda b, i: (b, i, 0)),
                   pl.BlockSpec((u, KV_DIM, BLK), lambda b, i: (b, 0, 0)),
                   pl.BlockSpec((u, KV_DIM, BLK), lambda b, i: (b, 0, 0))],
        out_shape=[jax.ShapeDtypeStruct((batch, seq, Q_DIM), BF16),
                   jax.ShapeDtypeStruct((batch, KV_DIM, BLK), F32),
                   jax.ShapeDtypeStruct((batch, KV_DIM, BLK), F32)],
        scratch_shapes=[pltpu.VMEM((u, KV_HEADS * 2 * BLK, KV_DIM), BF16),
                        pltpu.VMEM((u, KV_HEADS * 2 * BLK, KV_DIM), BF16),
                        pltpu.VMEM((u, Q_PER_KV * BLK, KV_HEADS * 2 * BLK), BF16)],
        compiler_params=_cparams(("arbitrary", "arbitrary")),
        name="attn_prompt",
    )(p3, p3, p3, cos, sin, qg, kg, bd, sinks)


def _ssd_chunk(slot, z_ref, xs_ref, b_ref, c_ref, dt_ref, cw_ref, cb_ref, dtb_ref, alog_ref, dskip_ref,
               nw_ref, e3_ref, o_ref, tail_ref, xpad, tails, st, ybuf):
    xpad[0:SUBLANES, :] = tails[1 - slot]
    xpad[SUBLANES:SUBLANES + BLK, 0:D_INNER] = xs_ref[...]
    xpad[SUBLANES:SUBLANES + BLK, D_INNER:D_INNER + BC_DIM] = b_ref[...]
    xpad[SUBLANES:SUBLANES + BLK, D_INNER + BC_DIM:CONV_DIM] = c_ref[...]
    cwh = 0.5 * cw_ref[...]
    acc = 0.5 * cb_ref[...] + cwh[CONV_W - 1:CONV_W, :] * xpad[SUBLANES:SUBLANES + BLK, :]
    for t in range(1, CONV_W):
        acc = acc + cwh[CONV_W - 1 - t:CONV_W - t, :] * xpad[SUBLANES - t:SUBLANES - t + BLK, :]
    xc = acc + acc * jnp.tanh(acc)
    new_tail = xpad[BLK:BLK + SUBLANES, :]
    tail_ref[...] = new_tail
    tails[slot] = new_tail

    xs = xc[:, 0:D_INNER]
    bm = xc[:, D_INNER:D_INNER + BC_DIM].astype(BF16)
    cm = xc[:, D_INNER + BC_DIM:CONV_DIM].astype(BF16)

    e3 = e3_ref[...]
    dt = _softplus(dt_ref[...] + dtb_ref[...])
    dta = dt * (-LOG2E * jnp.exp(alog_ref[...]))
    row = lax.broadcasted_iota(jnp.int32, (BLK, BLK), 0)
    col = lax.broadcasted_iota(jnp.int32, (BLK, BLK), 1)
    causal = row >= col
    cum = jnp.dot(causal.astype(F32), dta, preferred_element_type=F32, precision=lax.Precision.HIGHEST)
    cum_t = cum.T
    ecum = jnp.exp2(cum)
    to_end = jnp.exp2(cum[BLK - 1:BLK, :] - cum) * dt
    dt_e = _expand_heads(dt, e3)
    ecum_e = _expand_heads(ecum, e3)
    to_end_e = _expand_heads(to_end, e3)
    xdt = (xs * dt_e).astype(BF16)
    xte = (xs * to_end_e).astype(BF16)
    lane = lax.broadcasted_iota(jnp.int32, (BLK, LANES), 1)
    first_head = lane < SSM_HEAD_DIM

    for g in range(SSM_GROUPS):
        gl = slice(g * GROUP_LANES, (g + 1) * GROUP_LANES)
        bg = bm[:, g * D_STATE:(g + 1) * D_STATE]
        cg = cm[:, g * D_STATE:(g + 1) * D_STATE]
        cbg = lax.dot_general(cg, bg, (((1,), (1,)), ((), ())), preferred_element_type=F32)
        st_g = st[1 - slot, :, gl]
        y_inter = jnp.dot(cg, st_g.astype(BF16), preferred_element_type=F32) * ecum_e[:, gl]
        for pr in range(HEADS_PER_GROUP // 2):
            h0 = g * HEADS_PER_GROUP + 2 * pr
            xd = xdt[:, h0 * SSM_HEAD_DIM:(h0 + 2) * SSM_HEAD_DIM]
            ys = []
            for h in (h0, h0 + 1):
                diff = cum[:, h:h + 1] - cum_t[h:h + 1, :]
                w = (jnp.exp2(jnp.where(causal, diff, -jnp.inf)) * cbg).astype(BF16)
                ys.append(jnp.dot(w, xd, preferred_element_type=F32))
            lo = pr * LANES
            ybuf[:, h0 * SSM_HEAD_DIM:(h0 + 2) * SSM_HEAD_DIM] = (
                jnp.where(first_head, ys[0], ys[1]) + y_inter[:, lo:lo + LANES])
        upd = lax.dot_general(bg, xte[:, gl], (((0,), (0,)), ((), ())), preferred_element_type=F32)
        st[slot, :, gl] = st_g * ecum_e[BLK - 1:BLK, gl] + upd

    y = ybuf[...] + dskip_ref[...] * xs
    yz = y * _silu(z_ref[...])
    ms = jnp.mean(yz * yz, axis=-1, keepdims=True)
    o_ref[...] = (yz * lax.rsqrt(ms + EPS) * nw_ref[...]).astype(o_ref.dtype)


def _ssd_prompt_kernel(z_ref, xs_ref, b_ref, c_ref, dt_ref, cw_ref, cb_ref, dtb_ref, alog_ref, dskip_ref,
                       nw_ref, e3_ref, o_ref, tail_ref, hfin_ref, xpad, tails, st, ybuf):
    i = pl.program_id(1)
    last = pl.num_programs(1) - 1

    @pl.when(i == 0)
    def _():
        tails[...] = jnp.zeros_like(tails)
        st[...] = jnp.zeros_like(st)

    slot = i % 2
    seqs = range(z_ref.shape[0])
    for u in seqs:
        _ssd_chunk(slot, z_ref.at[u], xs_ref.at[u], b_ref.at[u], c_ref.at[u], dt_ref.at[u], cw_ref, cb_ref,
                   dtb_ref, alog_ref, dskip_ref, nw_ref, e3_ref, o_ref.at[u], tail_ref.at[u], xpad.at[u],
                   tails.at[u], st.at[u], ybuf.at[u])

    @pl.when(i == last)
    def _():
        for u in seqs:
            hfin_ref[u] = st[u, slot].T


SSD_SEQS = 2


def _ssd_prompt(p_act, lay, batch, seq, cw, cb, dtb, alog, dskip, nw, e3):
    nb = seq // BLK
    u = SSD_SEQS if batch % SSD_SEQS == 0 else 1
    p3 = p_act.reshape(batch, seq, p_act.shape[1])
    zc, xc = lay["z"][0] // D_INNER, lay["xs"][0] // D_INNER
    bc, cc, dc = lay["B"][0] // BC_DIM, lay["C"][0] // BC_DIM, lay["dt"][0] // DT_PAD
    const = lambda b, i: (0, 0)
    return pl.pallas_call(
        _ssd_prompt_kernel,
        grid=(batch // u, nb),
        in_specs=[pl.BlockSpec((u, BLK, D_INNER), lambda b, i: (b, i, zc)),
                  pl.BlockSpec((u, BLK, D_INNER), lambda b, i: (b, i, xc)),
                  pl.BlockSpec((u, BLK, BC_DIM), lambda b, i: (b, i, bc)),
                  pl.BlockSpec((u, BLK, BC_DIM), lambda b, i: (b, i, cc)),
                  pl.BlockSpec((u, BLK, DT_PAD), lambda b, i: (b, i, dc)),
                  pl.BlockSpec((CONV_W, CONV_DIM), const),
                  pl.BlockSpec((1, CONV_DIM), const),
                  pl.BlockSpec((1, DT_PAD), const),
                  pl.BlockSpec((1, DT_PAD), const),
                  pl.BlockSpec((1, D_INNER), const),
                  pl.BlockSpec((1, D_INNER), const),
                  pl.BlockSpec((3 * DT_PAD, D_INNER), const)],
        out_specs=[pl.BlockSpec((u, BLK, D_INNER), lambda b, i: (b, i, 0)),
                   pl.BlockSpec((u, SUBLANES, CONV_DIM), lambda b, i: (b, 0, 0)),
                   pl.BlockSpec((u, D_INNER, D_STATE), lambda b, i: (b, 0, 0))],
        out_shape=[jax.ShapeDtypeStruct((batch, seq, D_INNER), BF16),
                   jax.ShapeDtypeStruct((batch, SUBLANES, CONV_DIM), F32),
                   jax.ShapeDtypeStruct((batch, D_INNER, D_STATE), F32)],
        scratch_shapes=[pltpu.VMEM((u, BLK + SUBLANES, CONV_DIM), F32),
                        pltpu.VMEM((u, 2, SUBLANES, CONV_DIM), F32),
                        pltpu.VMEM((u, 2, D_STATE, D_INNER), F32),
                        pltpu.VMEM((u, BLK, D_INNER), F32)],
        compiler_params=_cparams(("arbitrary", "arbitrary")),
        name="ssd_prompt",
    )(p3, p3, p3, p3, p3, cw, cb, dtb, alog, dskip, nw, e3)


def _decode_pre_kernel(q_ref, k_ref, xs_ref, b_ref, c_ref, dt_ref, cs_ref, cos_ref, sin_ref, qg_ref, kg_ref,
                       bd_ref, cw_ref, cb_ref, dtb_ref, alog_ref, e3_ref,
                       qo_ref, ko_ref, xc_ref, xdt_ref, dec_ref, cso_ref):
    cos, sin, bd = cos_ref[...], sin_ref[...], bd_ref[...]
    qo_ref[...] = _unpair(_head_norm_rope(q_ref[...], qg_ref[...], cos, sin, bd)) * (HEAD_DIM ** -0.5)
    ko_ref[...] = _unpair(_head_norm_rope(k_ref[...], kg_ref[...], cos, sin, bd))

    taps = [cs_ref[t].T for t in range(CONV_W - 1)]
    segs = ((xs_ref, 0, D_INNER), (b_ref, D_INNER, BC_DIM), (c_ref, D_INNER + BC_DIM, BC_DIM))
    for ref, off, width in segs:
        new = ref[...]
        acc = cb_ref[:, off:off + width] + cw_ref[CONV_W - 1:CONV_W, off:off + width] * new
        for t in range(CONV_W - 1):
            acc = acc + cw_ref[t:t + 1, off:off + width] * taps[t][:, off:off + width]
        xc_ref[:, off:off + width] = _silu(acc)
        for t in range(CONV_W - 2):
            cso_ref[:, t * CONV_DIM + off:t * CONV_DIM + off + width] = taps[t + 1][:, off:off + width]
        lo = (CONV_W - 2) * CONV_DIM + off
        cso_ref[:, lo:lo + width] = new

    e3 = e3_ref[...]
    dt = _softplus(dt_ref[...] + dtb_ref[...])
    decay = jnp.exp(dt * (-jnp.exp(alog_ref[...])))
    xdt_ref[...] = xc_ref[:, 0:D_INNER] * _expand_heads(dt, e3)
    dec_ref[...] = _expand_heads(decay, e3)


def _decode_pre(p_act, lay, conv_state, cos, sin, qg, kg, bd, cw, cb, dtb, alog, e3):
    m = p_act.shape[0]
    full = lambda shape: pl.BlockSpec(shape, lambda i: (0, 0))

    def col(name):
        c = lay[name][0] // lay[name][1]
        return pl.BlockSpec((m, lay[name][1]), lambda i: (0, c))

    cs_w = (CONV_W - 1) * CONV_DIM
    return pl.pallas_call(
        _decode_pre_kernel,
        grid=(1,),
        in_specs=[col("q"), col("k"), col("xs"), col("B"), col("C"), col("dt"),
                  pl.BlockSpec((CONV_W - 1, CONV_DIM, m), lambda i: (0, 0, 0)),
                  full((1, LANES)), full((1, LANES)), full((1, Q_DIM)), full((1, KV_DIM)),
                  full((SLAB, SLAB)), full((CONV_W, CONV_DIM)), full((1, CONV_DIM)), full((1, DT_PAD)),
                  full((1, DT_PAD)), full((3 * DT_PAD, D_INNER))],
        out_specs=[full((m, Q_DIM)), full((m, KV_DIM)), full((m, CONV_DIM)), full((m, D_INNER)),
                   full((m, D_INNER)), full((m, cs_w))],
        out_shape=[jax.ShapeDtypeStruct((m, Q_DIM), F32), jax.ShapeDtypeStruct((m, KV_DIM), F32),
                   jax.ShapeDtypeStruct((m, CONV_DIM), F32), jax.ShapeDtypeStruct((m, D_INNER), F32),
                   jax.ShapeDtypeStruct((m, D_INNER), F32), jax.ShapeDtypeStruct((m, cs_w), F32)],
        compiler_params=_cparams(("arbitrary",)),
        name="decode_pre",
    )(p_act, p_act, p_act, p_act, p_act, p_act, conv_state, cos, sin, qg, kg, bd, cw, cb, dtb, alog, e3)


DEC_ROWS = 8


def _attn_decode_kernel(q_ref, k_ref, v_ref, ck_ref, cv_ref, sink_ref, o_ref, cko_ref, cvo_ref):
    w = ck_ref.shape[2]
    nq = N_HEADS
    r = lax.broadcasted_iota(jnp.int32, (nq, SLAB), 0)
    grp = lax.broadcasted_iota(jnp.int32, (nq, SLAB), 1) // HEAD_DIM
    own = grp == (r % KV_HEADS)
    in_window = lax.broadcasted_iota(jnp.int32, (nq, w), 1) > (w - WINDOW - 1)
    newest = lax.broadcasted_iota(jnp.int32, (KV_DIM, w), 1) == w - 1
    sink = sink_ref[...][:, 0:1]
    pad = jnp.zeros((DEC_ROWS, KV_DIM), BF16)
    k_parts = jnp.concatenate(list(_split3(k_ref[...])) + [pad], axis=0)
    v_parts = jnp.concatenate(list(_split3(v_ref[...])) + [pad], axis=0)
    part_row = lax.broadcasted_iota(jnp.int32, (4 * DEC_ROWS, w), 0) % DEC_ROWS
    tdims = (((0,), (0,)), ((), ()))
    for bl in range(DEC_ROWS):
        qrow = q_ref[bl:bl + 1, :]
        qm = jnp.zeros((nq, SLAB), F32)
        for j in range(Q_PER_KV):
            slab = jnp.broadcast_to(qrow[:, j * SLAB:(j + 1) * SLAB], (nq, SLAB))
            qm = jnp.where((r // KV_HEADS) == j, slab, qm)
        qm = jnp.where(own, qm, 0.0).astype(BF16)
        pick = jnp.where(part_row == bl, 1.0, 0.0).astype(BF16)
        k_col = lax.dot_general(k_parts, pick, tdims, preferred_element_type=F32)
        v_col = lax.dot_general(v_parts, pick, tdims, preferred_element_type=F32)
        keys = jnp.where(newest, k_col, pltpu.roll(ck_ref[bl], w - 1, 1))
        vals = jnp.where(newest, v_col, pltpu.roll(cv_ref[bl], w - 1, 1))
        cko_ref[bl] = keys
        cvo_ref[bl] = vals
        s = jnp.dot(qm, keys.astype(BF16), preferred_element_type=F32)
        s = jnp.where(in_window, s, -jnp.inf)
        mx = jnp.maximum(jnp.max(s, axis=-1, keepdims=True), sink)
        e = jnp.exp(s - mx)
        den = jnp.sum(e, axis=-1, keepdims=True) + jnp.exp(sink - mx)
        pv = lax.dot_general(e.astype(BF16), vals.astype(BF16), (((1,), (1,)), ((), ())),
                             preferred_element_type=F32)
        pv = jnp.where(own, pv / den, 0.0)
        for j in range(Q_PER_KV):
            o_ref[bl:bl + 1, j * SLAB:(j + 1) * SLAB] = jnp.sum(
                pv[j * KV_HEADS:(j + 1) * KV_HEADS, :], axis=0, keepdims=True)


def _attn_decode(q, k, p_act, lay, cache_k, cache_v, sink_rows):
    m = q.shape[0]
    w = cache_k.shape[2]
    vc = lay["v"][0] // KV_DIM
    cache_spec = pl.BlockSpec((DEC_ROWS, KV_DIM, w), lambda i: (i, 0, 0))
    return pl.pallas_call(
        _attn_decode_kernel,
        grid=(m // DEC_ROWS,),
        in_specs=[pl.BlockSpec((DEC_ROWS, Q_DIM), lambda i: (i, 0)),
                  pl.BlockSpec((DEC_ROWS, KV_DIM), lambda i: (i, 0)),
                  pl.BlockSpec((DEC_ROWS, KV_DIM), lambda i: (i, vc)),
                  cache_spec, cache_spec,
                  pl.BlockSpec((N_HEADS, LANES), lambda i: (0, 0))],
        out_specs=[pl.BlockSpec((DEC_ROWS, Q_DIM), lambda i: (i, 0)), cache_spec, cache_spec],
        out_shape=[jax.ShapeDtypeStruct((m, Q_DIM), F32),
                   jax.ShapeDtypeStruct((m, KV_DIM, w), F32),
                   jax.ShapeDtypeStruct((m, KV_DIM, w), F32)],
        compiler_params=_cparams(("arbitrary",)),
        name="attn_decode",
    )(q, k, p_act, cache_k, cache_v, sink_rows)


MM_ROWS = 16


def _ssm_decode_kernel(st_ref, xdt_ref, dec_ref, xc_ref, z_ref, dskip_ref, nw_ref, sto_ref, o_ref):
    rows = st_ref.shape[0]
    r = lax.broadcasted_iota(jnp.int32, (MM_ROWS, D_INNER), 0)
    grp = lax.broadcasted_iota(jnp.int32, (MM_ROWS, D_INNER), 1) // GROUP_LANES
    rr = lax.broadcasted_iota(jnp.int32, (MM_ROWS, D_STATE), 0)
    ones_rows = jnp.where((rr >= SSM_GROUPS) & (rr < SSM_GROUPS + 3), 1.0, 0.0)
    bc = lambda a: jnp.broadcast_to(a, (MM_ROWS, D_INNER))
    new_states = []
    for u in range(rows):
        h = st_ref[u]
        xdt = xdt_ref[u:u + 1, :]
        xc = xc_ref[u:u + 1, :]
        hi, mid, lo = _split3(dec_ref[u:u + 1, :])
        lhs_t = jnp.where(r == grp, bc(xdt), 0.0)
        for t, piece in enumerate((hi, mid, lo)):
            lhs_t = jnp.where(r == SSM_GROUPS + t, bc(piece.astype(F32)), lhs_t)
        lhs_t = lhs_t.astype(BF16)
        b_rows = jnp.zeros((MM_ROWS, D_STATE), F32)
        for g in range(SSM_GROUPS):
            b_g = xc[:, D_INNER + g * D_STATE:D_INNER + (g + 1) * D_STATE]
            b_rows = jnp.where(rr == g, jnp.broadcast_to(b_g, (MM_ROWS, D_STATE)), b_rows)
        rhs = jnp.concatenate([b_rows, ones_rows], axis=1).astype(BF16)
        both = lax.dot_general(lhs_t, rhs, (((0,), (0,)), ((), ())), preferred_element_type=F32)
        h_new = both[:, D_STATE:] * h + both[:, :D_STATE]
        sto_ref[u] = h_new
        new_states.append(h_new.astype(BF16))
    states = jnp.concatenate(new_states, axis=1)
    urow = lax.broadcasted_iota(jnp.int32, (rows, rows * D_STATE), 0)
    ublk = lax.broadcasted_iota(jnp.int32, (rows, rows * D_STATE), 1) // D_STATE
    c_sel = []
    for g in range(SSM_GROUPS):
        c_g = xc_ref[:, D_INNER + BC_DIM + g * D_STATE:D_INNER + BC_DIM + (g + 1) * D_STATE]
        c_sel.append(jnp.where(urow == ublk, jnp.concatenate([c_g] * rows, axis=1), 0.0))
    c_all = jnp.concatenate(c_sel, axis=0).astype(BF16)
    yt = lax.dot_general(states, c_all, (((1,), (1,)), ((), ())), preferred_element_type=F32).T
    lane_grp = lax.broadcasted_iota(jnp.int32, (rows, D_INNER), 1) // GROUP_LANES
    y = jnp.zeros((rows, D_INNER), F32)
    for g in range(SSM_GROUPS):
        y = jnp.where(lane_grp == g, yt[g * rows:(g + 1) * rows, :], y)
    y = y + dskip_ref[...] * xc_ref[:, 0:D_INNER]
    yz = y * _silu(z_ref[...])
    ms = jnp.mean(yz * yz, axis=-1, keepdims=True)
    o_ref[...] = yz * lax.rsqrt(ms + EPS) * nw_ref[...]


SSM_DEC_ROWS = SUBLANES


def _ssm_decode(state, xdt, dec, xc, p_act, lay, dskip, nw):
    m = state.shape[0]
    u = SSM_DEC_ROWS
    assert m % u == 0
    zc = lay["z"][0] // D_INNER
    row = lambda width: pl.BlockSpec((u, width), lambda b: (b, 0))
    return pl.pallas_call(
        _ssm_decode_kernel,
        grid=(m // u,),
        in_specs=[pl.BlockSpec((u, D_INNER, D_STATE), lambda b: (b, 0, 0)),
                  row(D_INNER), row(D_INNER), row(CONV_DIM),
                  pl.BlockSpec((u, D_INNER), lambda b: (b, zc)),
                  pl.BlockSpec((1, D_INNER), lambda b: (0, 0)),
                  pl.BlockSpec((1, D_INNER), lambda b: (0, 0))],
        out_specs=[pl.BlockSpec((u, D_INNER, D_STATE), lambda b: (b, 0, 0)), row(D_INNER)],
        out_shape=[jax.ShapeDtypeStruct((m, D_INNER, D_STATE), F32),
                   jax.ShapeDtypeStruct((m, D_INNER), F32)],
        compiler_params=_cparams(("arbitrary",)),
        name="ssm_decode",
    )(state, xdt, dec, xc, p_act, dskip, nw)


def _merge_mlp_kernel(x_ref, oa_ref, os_ref, ga_ref, gs_ref, wa_ref, ws_ref, wo_ref, g_ref, wu_ref, wd_ref,
                      o_ref, *, ff_tile):
    a = jnp.dot(oa_ref[...].astype(BF16), wa_ref[...], preferred_element_type=F32)
    s = jnp.dot(os_ref[...].astype(BF16), ws_ref[...], preferred_element_type=F32)
    mixed = _sigmoid(ga_ref[...]) * a + _sigmoid(gs_ref[...]) * s
    x = x_ref[...] + jnp.dot(mixed.astype(BF16), wo_ref[...], preferred_element_type=F32)
    ms = jnp.mean(x * x, axis=-1, keepdims=True)
    xn = (x * lax.rsqrt(ms + EPS) * g_ref[...]).astype(BF16)
    acc = x
    for c in range(wu_ref.shape[1] // ff_tile):
        u = jnp.dot(xn, wu_ref[:, c * ff_tile:(c + 1) * ff_tile], preferred_element_type=F32)
        act = jnp.square(jnp.maximum(u, 0.0)).astype(BF16)
        acc = acc + jnp.dot(act, wd_ref[c * ff_tile:(c + 1) * ff_tile, :], preferred_element_type=F32)
    o_ref[...] = acc


def _merge_mlp_stream_kernel(x_ref, oa_ref, os_ref, ga_ref, gs_ref, wa_ref, ws_ref, wo_ref, g_ref, wu_ref, wd_ref,
                             o_ref, xn_ref):
    @pl.when(pl.program_id(0) == 0)
    def _():
        a = jnp.dot(oa_ref[...].astype(BF16), wa_ref[...], preferred_element_type=F32)
        s = jnp.dot(os_ref[...].astype(BF16), ws_ref[...], preferred_element_type=F32)
        mixed = _sigmoid(ga_ref[...]) * a + _sigmoid(gs_ref[...]) * s
        x = x_ref[...] + jnp.dot(mixed.astype(BF16), wo_ref[...], preferred_element_type=F32)
        ms = jnp.mean(x * x, axis=-1, keepdims=True)
        xn_ref[...] = (x * lax.rsqrt(ms + EPS) * g_ref[...]).astype(BF16)
        o_ref[...] = x

    u = jnp.dot(xn_ref[...], wu_ref[...], preferred_element_type=F32)
    act = jnp.square(jnp.maximum(u, 0.0)).astype(BF16)
    o_ref[...] += jnp.dot(act, wd_ref[...], preferred_element_type=F32)


def _merge_mlp(x, o_attn, o_ssm, p_act, lay, wa, ws, wo, g, wu, wd):
    m, d = x.shape
    ff = wu.shape[1]
    tm = _row_tile(m, 512)
    gac, gsc = lay["ga"][0] // d, lay["gs"][0] // d
    if m == tm:
        ft = min(1024, ff)
        const = lambda c: (0, 0)
        return pl.pallas_call(
            _merge_mlp_stream_kernel,
            grid=(ff // ft,),
            in_specs=[pl.BlockSpec((m, d), const), pl.BlockSpec((m, Q_DIM), const),
                      pl.BlockSpec((m, D_INNER), const),
                      pl.BlockSpec((m, d), lambda c: (0, gac)), pl.BlockSpec((m, d), lambda c: (0, gsc)),
                      pl.BlockSpec((Q_DIM, d), const), pl.BlockSpec((D_INNER, d), const),
                      pl.BlockSpec((d, d), const), pl.BlockSpec((1, d), const),
                      pl.BlockSpec((d, ft), lambda c: (0, c)), pl.BlockSpec((ft, d), lambda c: (c, 0))],
            out_specs=pl.BlockSpec((m, d), const),
            out_shape=jax.ShapeDtypeStruct((m, d), F32),
            scratch_shapes=[pltpu.VMEM((m, d), BF16)],
            compiler_params=_cparams(("arbitrary",)),
            name="merge_mlp_stream",
        )(x, o_attn, o_ssm, p_act, p_act, wa, ws, wo, g, wu, wd)
    weight = lambda shape: pl.BlockSpec(shape, lambda i: (0, 0), pipeline_mode=pl.Buffered(1))
    return pl.pallas_call(
        functools.partial(_merge_mlp_kernel, ff_tile=min(1024, ff)),
        grid=(m // tm,),
        in_specs=[pl.BlockSpec((tm, d), lambda i: (i, 0)),
                  pl.BlockSpec((tm, Q_DIM), lambda i: (i, 0)),
                  pl.BlockSpec((tm, D_INNER), lambda i: (i, 0)),
                  pl.BlockSpec((tm, d), lambda i: (i, gac)),
                  pl.BlockSpec((tm, d), lambda i: (i, gsc)),
                  weight((Q_DIM, d)), weight((D_INNER, d)), weight((d, d)),
                  weight((1, d)), weight((d, ff)), weight((ff, d))],
        out_specs=pl.BlockSpec((tm, d), lambda i: (i, 0)),
        out_shape=jax.ShapeDtypeStruct((m, d), F32),
        compiler_params=_cparams(("arbitrary",), WEIGHT_RESIDENT_VMEM_LIMIT),
        name="merge_mlp",
    )(x, o_attn, o_ssm, p_act, p_act, wa, ws, wo, g, wu, wd)


def _rope_tables(pos):
    half = HEAD_DIM // 2
    inv = ROPE_THETA ** (-jnp.arange(half, dtype=F32) / half)
    ang = pos.astype(F32)[:, None] * inv[None, :]
    cos, sin = jnp.cos(ang), jnp.sin(ang)
    return (jnp.concatenate([cos, cos, cos, cos], axis=1),
            jnp.concatenate([-sin, -sin, sin, sin], axis=1))


def _constants():
    heads = np.arange(DT_PAD)[:, None]
    cols = np.arange(D_INNER)[None, :] // SSM_HEAD_DIM
    e = (heads == cols).astype(np.float32)
    e3 = jnp.asarray(np.concatenate([e, e, e], axis=0), BF16)
    a = np.arange(SLAB)
    grp = 2 * (a // LANES) + (a % HEAD_DIM) // (HEAD_DIM // 2)
    bd = jnp.asarray((grp[:, None] == grp[None, :]).astype(np.float32) / HEAD_DIM, BF16)
    return e3, bd


_HALF = HEAD_DIM // 2


def _pair_q(a):
    lead = a.shape[:-1]
    n = len(lead)
    a = a.reshape(lead + (2, 2, Q_PER_KV, 2, _HALF))
    return a.transpose(tuple(range(n)) + (n + 2, n, n + 3, n + 1, n + 4)).reshape(lead + (Q_DIM,))


def _pair_k(a):
    lead = a.shape[:-1]
    n = len(lead)
    a = a.reshape(lead + (2, 2, 2, _HALF))
    return a.transpose(tuple(range(n)) + (n, n + 2, n + 1, n + 3)).reshape(lead + (KV_DIM,))


def _w_in_plan(d_model):
    lay, _ = _layout(d_model)
    o_v = Q_DIM + KV_DIM
    o_z = o_v + KV_DIM
    o_xbc = o_z + D_INNER
    o_dt = o_xbc + CONV_DIM
    o_g = o_dt + SSM_HEADS
    src = {"z": o_z, "xs": o_xbc, "ga": o_g, "gs": o_g + d_model, "B": o_xbc + D_INNER,
           "C": o_xbc + D_INNER + BC_DIM, "v": o_v}
    return [(lay[name][0], off, lay[name][1]) for name, off in src.items()], o_dt


def _w_prep_kernel(w_ref, qk_ref, o_ref, *, plan, lay, o_dt):
    for dst, src, width in plan:
        o_ref[:, dst:dst + width] = w_ref[:, src:src + width].astype(BF16)
    o_ref[:, lay["q"][0]:lay["q"][0] + Q_DIM] = qk_ref[:, 0:Q_DIM]
    o_ref[:, lay["k"][0]:lay["k"][0] + KV_DIM] = qk_ref[:, Q_DIM:Q_DIM + KV_DIM]
    dt = w_ref[:, o_dt:o_dt + SSM_HEADS].astype(BF16)
    pad = jnp.zeros((dt.shape[0], DT_PAD - SSM_HEADS), BF16)
    o_ref[:, lay["dt"][0]:lay["dt"][0] + DT_PAD] = jnp.concatenate([dt, pad], axis=1)


def _w_prep(w_in, qk):
    d_model, n_in = w_in.shape
    lay, n = _layout(d_model)
    plan, o_dt = _w_in_plan(d_model)
    tr = _row_tile(d_model, LANES)
    return pl.pallas_call(
        functools.partial(_w_prep_kernel, plan=plan, lay=lay, o_dt=o_dt),
        grid=(d_model // tr,),
        in_specs=[pl.BlockSpec((tr, n_in), lambda i: (i, 0)),
                  pl.BlockSpec((tr, Q_DIM + KV_DIM), lambda i: (i, 0))],
        out_specs=pl.BlockSpec((tr, n), lambda i: (i, 0)),
        out_shape=jax.ShapeDtypeStruct((d_model, n), BF16),
        compiler_params=_cparams(("arbitrary",)),
        name="w_prep",
    )(w_in, qk)


def _prep_layer(d_model, norm_mix, w_in, q_norm, k_norm, attn_sinks, conv_w, conv_b, dt_bias, a_log, d_skip,
                ssm_norm, w_attn_o, w_ssm_o, w_out, norm_mlp, w_up, w_down):
    assert KV_HEADS == 4 and Q_PER_KV == 4
    e3, bd = _constants()
    lay, _ = _layout(d_model)
    wb = w_in.astype(BF16)
    qk = jnp.concatenate([_pair_q(wb[:, 0:Q_DIM]), _pair_k(wb[:, Q_DIM:Q_DIM + KV_DIM])], axis=1)
    w_p = _w_prep(wb, qk)
    pad_heads = lambda a: jnp.pad(a, (0, DT_PAD - SSM_HEADS))[None, :]
    sink_p = attn_sinks.reshape(KV_HEADS, Q_PER_KV).T.reshape(N_HEADS)
    wa = w_attn_o.reshape(KV_HEADS, Q_PER_KV, HEAD_DIM, d_model).transpose(1, 0, 2, 3).reshape(Q_DIM, d_model)
    return dict(
        lay=lay, e3=e3, bd=bd, w_p=w_p,
        norm_mix=norm_mix[None, :], norm_mlp=norm_mlp[None, :],
        qg=_pair_q(jnp.tile(q_norm, N_HEADS))[None, :], kg=_pair_k(jnp.tile(k_norm, KV_HEADS))[None, :],
        sink_p=sink_p, sink_rows=jnp.broadcast_to(sink_p[:, None], (N_HEADS, LANES)),
        cw=conv_w, cb=conv_b[None, :], dtb=pad_heads(dt_bias), alog=pad_heads(a_log),
        dskip=jnp.repeat(d_skip, SSM_HEAD_DIM)[None, :], nw=ssm_norm[None, :],
        wa=wa.astype(BF16), ws=w_ssm_o.astype(BF16), wo=w_out.astype(BF16),
        wu=w_up.astype(BF16), wd=w_down.astype(BF16))


def _prompt_layer(x, lw):
    batch, seq, d = x.shape
    assert seq % BLK == 0 and seq >= WINDOW
    x2 = x.reshape(batch * seq, d)
    p_act = _in_proj(x2, lw["norm_mix"], lw["w_p"])
    cos, sin = _rope_tables(jnp.arange(seq))
    o_attn, k_last, v_last = _attn_prompt(p_act, lw["lay"], batch, seq, cos, sin, lw["qg"], lw["kg"],
                                          lw["bd"], lw["sink_p"])
    o_ssm, tail, h_fin = _ssd_prompt(p_act, lw["lay"], batch, seq, lw["cw"], lw["cb"], lw["dtb"], lw["alog"],
                                     lw["dskip"], lw["nw"], lw["e3"])
    y = _merge_mlp(x2, o_attn.reshape(batch * seq, Q_DIM), o_ssm.reshape(batch * seq, D_INNER), p_act,
                   lw["lay"], lw["wa"], lw["ws"], lw["wo"],
                   lw["norm_mlp"], lw["wu"], lw["wd"])
    return (y.reshape(batch, seq, d),
            k_last.reshape(batch, KV_HEADS, HEAD_DIM, BLK).transpose(0, 3, 1, 2),
            v_last.reshape(batch, KV_HEADS, HEAD_DIM, BLK).transpose(0, 3, 1, 2),
            tail[:, SUBLANES - (CONV_W - 1):, :],
            h_fin.reshape(batch, SSM_HEADS, SSM_HEAD_DIM, D_STATE))


def _decode_layer(x, cache_k, cache_v, conv_state, ssm_state, lw):
    m, t, d = x.shape
    w = cache_k.shape[1]
    assert t == 1 and w == WINDOW and m % DEC_ROWS == 0
    x2 = x.reshape(m, d)
    p_act = _in_proj(x2, lw["norm_mix"], lw["w_p"])
    cos, sin = _rope_tables(PAST_LEN + jnp.arange(1))
    q, k, xc, xdt, dec, conv_new = _decode_pre(
        p_act, lw["lay"], jnp.transpose(conv_state, (1, 2, 0)), cos, sin, lw["qg"], lw["kg"],
        lw["bd"], lw["cw"], lw["cb"], lw["dtb"], lw["alog"], lw["e3"])
    ck = jnp.transpose(cache_k, (0, 2, 3, 1)).reshape(m, KV_DIM, w)
    cv = jnp.transpose(cache_v, (0, 2, 3, 1)).reshape(m, KV_DIM, w)
    o_attn, ck_new, cv_new = _attn_decode(q, k, p_act, lw["lay"], ck, cv, lw["sink_rows"])
    h_new, o_ssm = _ssm_decode(ssm_state.reshape(m, D_INNER, D_STATE), xdt, dec, xc, p_act, lw["lay"],
                               lw["dskip"], lw["nw"])
    y = _merge_mlp(x2, o_attn, o_ssm, p_act, lw["lay"], lw["wa"], lw["ws"], lw["wo"],
                   lw["norm_mlp"], lw["wu"], lw["wd"])
    unview = lambda c: jnp.transpose(c.reshape(m, KV_HEADS, HEAD_DIM, w), (0, 3, 1, 2))
    return (y.reshape(m, 1, d), unview(ck_new), unview(cv_new), conv_new.reshape(m, CONV_W - 1, CONV_DIM),
            h_new.reshape(m, SSM_HEADS, SSM_HEAD_DIM, D_STATE))


def kernel(x_prompt, x_sample, cache_k, cache_v, state_conv, state_ssm, norm_mix, w_in, q_norm, k_norm,
           attn_sinks, conv_w, conv_b, dt_bias, a_log, d_skip, ssm_norm, w_attn_o, w_ssm_o, w_out,
           norm_mlp, w_up, w_down):
    depth = w_in.shape[0]
    d_model = x_prompt.shape[-1]
    yp, ys = x_prompt, x_sample
    cols = [[] for _ in range(8)]
    for l in range(depth):
        lw = _prep_layer(d_model, norm_mix[l], w_in[l], q_norm[l], k_norm[l], attn_sinks[l], conv_w[l],
                         conv_b[l], dt_bias[l], a_log[l], d_skip[l], ssm_norm[l], w_attn_o[l], w_ssm_o[l],
                         w_out[l], norm_mlp[l], w_up[l], w_down[l])
        yp, kp, vp, cp, hp = _prompt_layer(yp, lw)
        ys, ks, vs, cs, hs = _decode_layer(ys, cache_k[l], cache_v[l], state_conv[l], state_ssm[l], lw)
        for lst, val in zip(cols, (kp, vp, cp, hp, ks, vs, cs, hs)):
            lst.append(val)
    return (yp, ys) + tuple(jnp.stack(c) for c in cols)
```
